```python
import jax, jax.numpy as jnp
from jax import lax
import numpy as np

D_MODEL = 1024
BATCH = 8
SEQ = 8192
DEPTH = 4

MLA_HEADS = 8
MLA_NOPE = 64
MLA_ROPE = 32
MLA_V = 64
Q_LORA = 384
KV_LORA = 256
MLA_WIDTH = MLA_HEADS * MLA_V

DIL_PAIRS = ((128, 1), (512, 4), (2048, 16))
DIL_GROUPS = 3
DIL_HEADS = 8
DIL_HD = 64
DIL_WIDTH = DIL_HEADS * DIL_HD
ROT_DIM = DIL_HD // 4

MIX_WIDTH = MLA_WIDTH + DIL_WIDTH
ROPE_THETA = 500000.0
Q_BLOCK = 128
EPS = 1e-6

IN_SPLITS = (Q_LORA, KV_LORA, MLA_ROPE, MLA_WIDTH, 3 * DIL_GROUPS * DIL_WIDTH, DIL_WIDTH)
IN_WIDTH = Q_LORA + KV_LORA + MLA_ROPE + MLA_WIDTH + 3 * DIL_GROUPS * DIL_WIDTH + DIL_WIDTH

kernel_name = "hymba_mla_dilated_window_encoder"


def rms_norm(x, g):
    xf = x.astype(jnp.float32)
    y = xf * lax.rsqrt(jnp.mean(xf * xf, axis=-1, keepdims=True) + EPS)
    return (y * g.astype(jnp.float32)).astype(x.dtype)


def rope_tables(seq, dim):
    inv = 1.0 / (ROPE_THETA ** (jnp.arange(0, dim, 2, dtype=jnp.float32) / dim))
    ang = jnp.arange(seq, dtype=jnp.float32)[:, None] * inv[None, :]
    return jnp.cos(ang), jnp.sin(ang)


def apply_rope(x, cos, sin):
    xf = x.astype(jnp.float32)
    x1, x2 = jnp.split(xf, 2, axis=-1)
    c = cos[:, None, :]
    s = sin[:, None, :]
    return jnp.concatenate([x1 * c - x2 * s, x1 * s + x2 * c], axis=-1).astype(x.dtype)


def partial_rope(x, cos, sin):
    return jnp.concatenate([apply_rope(x[..., :ROT_DIM], cos, sin), x[..., ROT_DIM:]], axis=-1)


def mla_attention(c_q, c_kv, k_r, q_norm_g, kv_norm_g, w_uq, w_ukv, cos, sin):
    B, S, _ = c_q.shape
    q = (rms_norm(c_q, q_norm_g) @ w_uq).reshape(B, S, MLA_HEADS, MLA_NOPE + MLA_ROPE)
    q_nope = q[..., :MLA_NOPE]
    q_rope = apply_rope(q[..., MLA_NOPE:], cos, sin)
    kv = (rms_norm(c_kv, kv_norm_g) @ w_ukv).reshape(B, S, MLA_HEADS, MLA_NOPE + MLA_V)
    k_nope = kv[..., :MLA_NOPE]
    v = kv[..., MLA_NOPE:]
    k_rope = apply_rope(k_r[:, :, None, :], cos, sin)[:, :, 0, :]
    scale = (MLA_NOPE + MLA_ROPE) ** -0.5
    nb = S // Q_BLOCK
    qn_b = q_nope.reshape(B, nb, Q_BLOCK, MLA_HEADS, MLA_NOPE).swapaxes(0, 1)
    qr_b = q_rope.reshape(B, nb, Q_BLOCK, MLA_HEADS, MLA_ROPE).swapaxes(0, 1)

    def block(args):
        qn, qr = args
        s = (jnp.einsum('bqhd,bkhd->bhqk', qn, k_nope).astype(jnp.float32)
             + jnp.einsum('bqhr,bkr->bhqk', qr, k_rope).astype(jnp.float32)) * scale
        p = jax.nn.softmax(s, axis=-1)
        return jnp.einsum('bhqk,bkhd->bqhd', p.astype(v.dtype), v)

    o = lax.map(block, (qn_b, qr_b))
    return o.swapaxes(0, 1).reshape(B, S, MLA_WIDTH)


def to_strided(t, d):
    B, S = t.shape[:2]
    rest = t.shape[2:]
    return t.reshape(B, S // d, d, *rest).swapaxes(1, 2).reshape(B * d, S // d, *rest)


def from_strided(t, B, d):
    L = t.shape[1]
    rest = t.shape[2:]
    return t.reshape(B, d, L, *rest).swapaxes(1, 2).reshape(B, L * d, *rest)


def banded_attention(q, k, v, half):
    N, L, H, Dh = q.shape
    nb = -(-L // half)
    Lp = nb * half
    pad = Lp - L
    qp = jnp.pad(q, ((0, 0), (0, pad), (0, 0), (0, 0))).reshape(N, nb, half, H, Dh)

    def key_windows(t):
        tp = jnp.pad(t, ((0, 0), (half, pad + half), (0, 0), (0, 0))).reshape(N, nb + 2, half, H, Dh)
        return jnp.concatenate([tp[:, :-2], tp[:, 1:-1], tp[:, 2:]], axis=2)

    kw = key_windows(k)
    vw = key_windows(v)
    qpos = jnp.arange(Lp).reshape(nb, half)
    kpos = (jnp.arange(nb)[:, None] - 1) * half + jnp.arange(3 * half)[None, :]
    valid = ((jnp.abs(qpos[:, :, None] - kpos[:, None, :]) <= half)
             & (kpos[:, None, :] >= 0) & (kpos[:, None, :] < L))
    s = jnp.einsum('nbqhd,nbkhd->nbhqk', qp, kw).astype(jnp.float32) * (Dh ** -0.5)
    s = jnp.where(valid[None, :, None], s, -jnp.inf)
    m = jnp.max(s, axis=-1, keepdims=True)
    e = jnp.exp(s - m)
    l = jnp.sum(e, axis=-1, keepdims=True)
    o = jnp.einsum('nbhqk,nbkhd->nbqhd', (e / l).astype(v.dtype), vw)
    lse = (m + jnp.log(l))[..., 0]
    o = o.reshape(N, Lp, H, Dh)[:, :L]
    lse = lse.swapaxes(2, 3).reshape(N, Lp, H)[:, :L]
    return o, lse


def dilated_attention(dil_qkv, cos, sin):
    B, S, _ = dil_qkv.shape
    qkv = dil_qkv.reshape(B, S, DIL_GROUPS, 3, DIL_HEADS, DIL_HD)
    outs, lses = [], []
    for g, (window, dil) in enumerate(DIL_PAIRS):
        q = partial_rope(qkv[:, :, g, 0], cos, sin)
        k = partial_rope(qkv[:, :, g, 1], cos, sin)
        v = qkv[:, :, g, 2]
        o, lse = banded_attention(to_strided(q, dil), to_strided(k, dil), to_strided(v, dil),
                                  window // (2 * dil))
        outs.append(from_strided(o, B, dil))
        lses.append(from_strided(lse, B, dil))
    alpha = jax.nn.softmax(jnp.stack(lses, axis=0), axis=0)
    out = jnp.einsum('gbsh,gbshd->bshd', alpha, jnp.stack(outs, axis=0).astype(jnp.float32))
    return out.astype(dil_qkv.dtype).reshape(B, S, DIL_WIDTH)


def _fwd_setup_inputs(seed: int = 0) -> dict:
    key = jax.random.key(seed)
    ks = jax.random.split(key, 10)
    f32 = jnp.float32
    x = jax.random.normal(ks[0], (BATCH, SEQ, D_MODEL), f32)
    norm_g = 1.0 + 0.02 * jax.random.normal(ks[1], (DEPTH, D_MODEL), f32)
    w_in = jax.random.normal(ks[2], (DEPTH, D_MODEL, IN_WIDTH), f32) * D_MODEL ** -0.5
    q_norm_g = 1.0 + 0.02 * jax.random.normal(ks[3], (DEPTH, Q_LORA), f32)
    kv_norm_g = 1.0 + 0.02 * jax.random.normal(ks[4], (DEPTH, KV_LORA), f32)
    w_uq = jax.random.normal(ks[5], (DEPTH, Q_LORA, MLA_HEADS * (MLA_NOPE + MLA_ROPE)), f32) * Q_LORA ** -0.5
    w_ukv = jax.random.normal(ks[6], (DEPTH, KV_LORA, MLA_HEADS * (MLA_NOPE + MLA_V)), f32) * KV_LORA ** -0.5
    w_out = jax.random.normal(ks[7], (DEPTH, MIX_WIDTH, D_MODEL), f32) * MIX_WIDTH ** -0.5
    final_g = 1.0 + 0.02 * jax.random.normal(ks[8], (D_MODEL,), f32)
    return {"x": x, "norm_g": norm_g, "w_in": w_in, "q_norm_g": q_norm_g, "kv_norm_g": kv_norm_g,
            "w_uq": w_uq, "w_ukv": w_ukv, "w_out": w_out, "final_g": final_g}


def _fwd_reference(x, norm_g, w_in, q_norm_g, kv_norm_g, w_uq, w_ukv, w_out, final_g):
    S = x.shape[1]
    cos_m, sin_m = rope_tables(S, MLA_ROPE)
    cos_d, sin_d = rope_tables(S, ROT_DIM)
    split_points = [sum(IN_SPLITS[:i + 1]) for i in range(len(IN_SPLITS) - 1)]
    for layer in range(DEPTH):
        h = rms_norm(x, norm_g[layer])
        p = h @ w_in[layer]
        c_q, c_kv, k_r, gate_a, dil_qkv, gate_b = jnp.split(p, split_points, axis=-1)
        a = mla_attention(c_q, c_kv, k_r, q_norm_g[layer], kv_norm_g[layer],
                          w_uq[layer], w_ukv[layer], cos_m, sin_m) * jax.nn.silu(gate_a)
        b = dilated_attention(dil_qkv, cos_d, sin_d) * jax.nn.silu(gate_b)
        x = x + jnp.concatenate([a, b], axis=-1) @ w_out[layer]
    return rms_norm(x, final_g)


import jax as _jax
import jax.numpy as _jnp

TWIN_FORMAT = 'train_step'
FWD_PARAMS = ['x', 'norm_g', 'w_in', 'q_norm_g', 'kv_norm_g', 'w_uq', 'w_ukv', 'w_out', 'final_g']
TWIN_WEIGHTS = ['norm_g', 'w_in', 'q_norm_g', 'kv_norm_g', 'w_uq', 'w_ukv', 'w_out', 'final_g']
TWIN_DIFF_INPUT = 'x'
TWIN_INPUTS = ['x', 'norm_g', 'w_in', 'q_norm_g', 'kv_norm_g', 'w_uq', 'w_ukv', 'w_out', 'final_g', 'loss_target', 'm_norm_g', 'm_w_in', 'm_q_norm_g', 'm_kv_norm_g', 'm_w_uq', 'm_w_ukv', 'm_w_out', 'm_final_g', 'v_norm_g', 'v_w_in', 'v_q_norm_g', 'v_kv_norm_g', 'v_w_uq', 'v_w_ukv', 'v_w_out', 'v_final_g']
TWIN_OUTPUTS = ['loss', 'grad_x', 'grad_norm_g', 'grad_w_in', 'grad_q_norm_g', 'grad_kv_norm_g', 'grad_w_uq', 'grad_w_ukv', 'grad_w_out', 'grad_final_g', 'delta_norm_g', 'delta_w_in', 'delta_q_norm_g', 'delta_kv_norm_g', 'delta_w_uq', 'delta_w_ukv', 'delta_w_out', 'delta_final_g', 'new_m_norm_g', 'new_m_w_in', 'new_m_q_norm_g', 'new_m_kv_norm_g', 'new_m_w_uq', 'new_m_w_ukv', 'new_m_w_out', 'new_m_final_g', 'new_v_norm_g', 'new_v_w_in', 'new_v_q_norm_g', 'new_v_kv_norm_g', 'new_v_w_uq', 'new_v_w_ukv', 'new_v_w_out', 'new_v_final_g']
TWIN_LEAF_KINDS = {'loss': 'loss', 'grad_x': 'grad_x', 'grad_norm_g': 'grad_w', 'grad_w_in': 'grad_w', 'grad_q_norm_g': 'grad_w', 'grad_kv_norm_g': 'grad_w', 'grad_w_uq': 'grad_w', 'grad_w_ukv': 'grad_w', 'grad_w_out': 'grad_w', 'grad_final_g': 'grad_w', 'delta_norm_g': 'delta_w', 'delta_w_in': 'delta_w', 'delta_q_norm_g': 'delta_w', 'delta_kv_norm_g': 'delta_w', 'delta_w_uq': 'delta_w', 'delta_w_ukv': 'delta_w', 'delta_w_out': 'delta_w', 'delta_final_g': 'delta_w', 'new_m_norm_g': 'new_m', 'new_m_w_in': 'new_m', 'new_m_q_norm_g': 'new_m', 'new_m_kv_norm_g': 'new_m', 'new_m_w_uq': 'new_m', 'new_m_w_ukv': 'new_m', 'new_m_w_out': 'new_m', 'new_m_final_g': 'new_m', 'new_v_norm_g': 'new_v', 'new_v_w_in': 'new_v', 'new_v_q_norm_g': 'new_v', 'new_v_kv_norm_g': 'new_v', 'new_v_w_uq': 'new_v', 'new_v_w_ukv': 'new_v', 'new_v_w_out': 'new_v', 'new_v_final_g': 'new_v'}


def _forward(args):
    return _fwd_reference(*[args[k] for k in FWD_PARAMS])


def _output_shape():
    def fwd():
        inp = _fwd_setup_inputs(0)
        return _fwd_reference(*[inp[k] for k in FWD_PARAMS])
    out = _jax.eval_shape(fwd)
    return out.shape, out.dtype

N_MICROBATCH = 1
ADAM_LR = 0.001
ADAM_B1 = 0.9
ADAM_B2 = 0.999
ADAM_EPS = 1e-08
ADAM_WD = 0.01
ADAM_STEP = 10
PER_EXAMPLE_BATCH_AXIS = {'x': 0, 'loss_target': 0}
SHARED_INPUTS = []
_WEIGHT_DTYPES = {'norm_g': _jnp.float32, 'w_in': _jnp.float32, 'q_norm_g': _jnp.float32, 'kv_norm_g': _jnp.float32, 'w_uq': _jnp.float32, 'w_ukv': _jnp.float32, 'w_out': _jnp.float32, 'final_g': _jnp.float32}
MOMENT_SCALE = {'norm_g': 3.611139e-02, 'w_in': 1.407992e-02, 'q_norm_g': 1.983571e-02, 'kv_norm_g': 3.551665e-02, 'w_uq': 1.397540e-02, 'w_ukv': 1.625260e-02, 'w_out': 1.790650e-02, 'final_g': 6.401274e+01}


def _to_microbatches(a, axis):
    t = _jnp.moveaxis(a, axis, 0)
    t = t.reshape((N_MICROBATCH, t.shape[0] // N_MICROBATCH) + t.shape[1:])
    return _jnp.moveaxis(t, 1, axis + 1)


def setup_inputs(seed: int = 0) -> dict:
    inp = _fwd_setup_inputs(seed)
    key = _jax.random.fold_in(_jax.random.key(seed), 7919)
    shape, _ = _output_shape()
    out = dict(inp)
    out["loss_target"] = _jax.random.normal(_jax.random.fold_in(key, 0), shape, _jnp.float32)
    for i, name in enumerate(TWIN_WEIGHTS):
        w = inp[name].astype(_jnp.float32)
        if MOMENT_SCALE is None:
            s = _jnp.sqrt(_jnp.mean(_jnp.square(w)) + 1e-30)
        else:
            s = MOMENT_SCALE[name]
        km, kv = _jax.random.split(_jax.random.fold_in(key, i + 1))
        out[name] = w
        out["m_" + name] = s * _jax.random.normal(km, w.shape, _jnp.float32)
        out["v_" + name] = (s * s) * _jax.random.uniform(kv, w.shape, _jnp.float32, 0.5, 1.5)
    if N_MICROBATCH > 1:
        for name, axis in PER_EXAMPLE_BATCH_AXIS.items():
            out[name] = _to_microbatches(out[name], axis)
    return {'x': out['x'], 'norm_g': out['norm_g'], 'w_in': out['w_in'], 'q_norm_g': out['q_norm_g'], 'kv_norm_g': out['kv_norm_g'], 'w_uq': out['w_uq'], 'w_ukv': out['w_ukv'], 'w_out': out['w_out'], 'final_g': out['final_g'], 'loss_target': out['loss_target'], 'm_norm_g': out['m_norm_g'], 'm_w_in': out['m_w_in'], 'm_q_norm_g': out['m_q_norm_g'], 'm_kv_norm_g': out['m_kv_norm_g'], 'm_w_uq': out['m_w_uq'], 'm_w_ukv': out['m_w_ukv'], 'm_w_out': out['m_w_out'], 'm_final_g': out['m_final_g'], 'v_norm_g': out['v_norm_g'], 'v_w_in': out['v_w_in'], 'v_q_norm_g': out['v_q_norm_g'], 'v_kv_norm_g': out['v_kv_norm_g'], 'v_w_uq': out['v_w_uq'], 'v_w_ukv': out['v_w_ukv'], 'v_w_out': out['v_w_out'], 'v_final_g': out['v_final_g']}


def _loss(weights, diff, rest, loss_target):
    with _jax.named_scope("forward"):
        args = {**rest, TWIN_DIFF_INPUT: diff, **{k: w.astype(_WEIGHT_DTYPES[k]) for k, w in weights.items()}}
        y = _forward(args)
    with _jax.named_scope("loss_head"):
        err = _jnp.square(y.astype(_jnp.float32) - loss_target)
        return 0.5 * _jnp.sum(_jnp.mean(err, axis=-1)) if err.ndim else 0.5 * err


def _adamw(w, g, m, v):
    m = ADAM_B1 * m + (1.0 - ADAM_B1) * g
    v = ADAM_B2 * v + (1.0 - ADAM_B2) * _jnp.square(g)
    m_hat = m / (1.0 - ADAM_B1 ** ADAM_STEP)
    v_hat = v / (1.0 - ADAM_B2 ** ADAM_STEP)
    delta = -ADAM_LR * (m_hat / (_jnp.sqrt(v_hat) + ADAM_EPS) + ADAM_WD * w)
    return delta, m, v


def reference(x, norm_g, w_in, q_norm_g, kv_norm_g, w_uq, w_ukv, w_out, final_g, loss_target, m_norm_g, m_w_in, m_q_norm_g, m_kv_norm_g, m_w_uq, m_w_ukv, m_w_out, m_final_g, v_norm_g, v_w_in, v_q_norm_g, v_kv_norm_g, v_w_uq, v_w_ukv, v_w_out, v_final_g):
    given = dict(x=x, norm_g=norm_g, w_in=w_in, q_norm_g=q_norm_g, kv_norm_g=kv_norm_g, w_uq=w_uq, w_ukv=w_ukv, w_out=w_out, final_g=final_g, loss_target=loss_target, m_norm_g=m_norm_g, m_w_in=m_w_in, m_q_norm_g=m_q_norm_g, m_kv_norm_g=m_kv_norm_g, m_w_uq=m_w_uq, m_w_ukv=m_w_ukv, m_w_out=m_w_out, m_final_g=m_final_g, v_norm_g=v_norm_g, v_w_in=v_w_in, v_q_norm_g=v_q_norm_g, v_kv_norm_g=v_kv_norm_g, v_w_uq=v_w_uq, v_w_ukv=v_w_ukv, v_w_out=v_w_out, v_final_g=v_final_g)
    weights = {n: given[n] for n in TWIN_WEIGHTS}
    shared = {n: given[n] for n in SHARED_INPUTS}
    per_example = {n: given[n] for n in ['x']}
    grad_fn = _jax.value_and_grad(_loss, argnums=(0, 1))

    def one_microbatch(ex, loss_target):
        ex = dict(ex)
        diff = ex.pop(TWIN_DIFF_INPUT)
        return grad_fn(weights, diff, {**shared, **ex}, loss_target)

    if N_MICROBATCH == 1:
        loss, (grad_w, grad_x) = one_microbatch(per_example, given["loss_target"])
    else:
        def body(carry, xs):
            loss_sum, grad_sum = carry
            l_k, (gw_k, gx_k) = one_microbatch(xs[0], xs[1])
            with _jax.named_scope("update"):
                return (loss_sum + l_k, _jax.tree.map(_jnp.add, grad_sum, gw_k)), gx_k

        init = (_jnp.zeros((), _jnp.float32), _jax.tree.map(_jnp.zeros_like, weights))
        (loss, grad_w), grad_x = _jax.lax.scan(body, init, (per_example, given["loss_target"]))
    with _jax.named_scope("update"):
        delta_w, new_m, new_v = {}, {}, {}
        for n in TWIN_WEIGHTS:
            delta_w[n], new_m[n], new_v[n] = _adamw(weights[n], grad_w[n], given["m_" + n], given["v_" + n])
    return (loss, grad_x, *[grad_w[n] for n in TWIN_WEIGHTS], *[delta_w[n] for n in TWIN_WEIGHTS],
            *[new_m[n] for n in TWIN_WEIGHTS], *[new_v[n] for n in TWIN_WEIGHTS])
```

```python
import functools

import jax
import jax.numpy as jnp
from jax import lax
from jax.experimental import pallas as pl
from jax.experimental.pallas import tpu as pltpu

F32 = jnp.float32
BF16 = jnp.bfloat16
MESH = pl.DeviceIdType.MESH

D_MODEL = 1024
DEPTH = 4
MLA_HEADS = 8
MLA_NOPE = 64
MLA_ROPE = 32
Q_LORA = 384
KV_LORA = 256
DIL_PAIRS = ((128, 1), (512, 4), (2048, 16))
DIL_HD = 64
DIL_HALF = 64
ROT_DIM = 16
ROPE_THETA = 500000.0
EPS = 1e-6
IN_WIDTH = 6304
N_SHARD = 4

P_WIDTH = 6656
P_MLA = 1024
P_GATE = 1024
P_DIL0 = 2048
LANES = 128
HEAD_W = 512

ADAM_LR = 0.001
ADAM_B1 = 0.9
ADAM_B2 = 0.999
ADAM_EPS = 1e-08
ADAM_WD = 0.01
ADAM_STEP = 10

N_WIN = DEPTH * D_MODEL * (IN_WIDTH // N_SHARD)
N_WUQ = DEPTH * Q_LORA * (768 // N_SHARD)
N_WUKV = DEPTH * KV_LORA * (1024 // N_SHARD)
N_WOUT = DEPTH * (1024 // N_SHARD) * D_MODEL
N_BIG = N_WIN + N_WUQ + N_WUKV + N_WOUT
ROWS_W = N_BIG // LANES
N_SMALL = DEPTH * (D_MODEL + Q_LORA + KV_LORA) + D_MODEL
ROWS_G = 63040
N_PAD = ROWS_G * LANES - N_BIG - N_SMALL
VMEM_BIG_MB = 48


def _cparams(vmem_mb=None):
    if vmem_mb is None:
        return None
    return pltpu.CompilerParams(vmem_limit_bytes=vmem_mb << 20)


def _sigmoid(x):
    return 1.0 / (1.0 + jnp.exp(-x))


def _rope(x, c, a, b, sh):
    return x * c + pltpu.roll(x, LANES - sh, 1) * a + pltpu.roll(x, sh, 1) * b


def _headsum_bcast(x):
    r = lax.broadcasted_iota(jnp.int32, (LANES, LANES), 0) // DIL_HD
    c = lax.broadcasted_iota(jnp.int32, (LANES, LANES), 1) // DIL_HD
    ones = jnp.where(r == c, 1.0, 0.0).astype(BF16)
    hi = x.astype(BF16)
    lo = (x - hi.astype(F32)).astype(BF16)
    return (jnp.dot(hi, ones, preferred_element_type=F32) + jnp.dot(lo, ones, preferred_element_type=F32))


def _mm(a, b, *, tm, tn, tk, out_dtype, name, add=None):
    M, K = a.shape
    N = b.shape[1]
    tm, tn, tk = min(tm, M), min(tn, N), min(tk, K)
    assert M % tm == 0 and N % tn == 0 and K % tk == 0, (a.shape, b.shape)
    nk = K // tk
    has_add = add is not None

    def body(*refs):
        if has_add:
            a_ref, b_ref, add_ref, o_ref, acc = refs
        else:
            a_ref, b_ref, o_ref, acc = refs
        k = pl.program_id(2)
        part = jnp.dot(a_ref[...].astype(BF16), b_ref[...].astype(BF16), preferred_element_type=F32)

        @pl.when(k == 0)
        def _():
            acc[...] = part

        @pl.when(k > 0)
        def _():
            acc[...] += part

        @pl.when(k == nk - 1)
        def _():
            r = acc[...]
            if has_add:
                r = r + add_ref[...]
            o_ref[...] = r.astype(out_dtype)

    in_specs = [pl.BlockSpec((tm, tk), lambda i, j, k: (i, k)), pl.BlockSpec((tk, tn), lambda i, j, k: (k, j))]
    args = [a, b]
    if has_add:
        in_specs.append(pl.BlockSpec((tm, tn), lambda i, j, k: (i, j)))
        args.append(add)
    return pl.pallas_call(
        body, name=name, grid=(M // tm, N // tn, nk), in_specs=in_specs,
        out_specs=pl.BlockSpec((tm, tn), lambda i, j, k: (i, j)),
        out_shape=jax.ShapeDtypeStruct((M, N), out_dtype),
        scratch_shapes=[pltpu.VMEM((tm, tn), F32)], compiler_params=_cparams(VMEM_BIG_MB))(*args)


def _row_spec(tm, w, cb=0):
    return pl.BlockSpec((tm, w), lambda i: (i, cb))


def _const_spec(arr):
    nd = arr.ndim
    return pl.BlockSpec(arr.shape, lambda i: (0,) * nd)


def _rms_fwd(x, g, name):
    L, D = x.shape
    tm = min(512, L)

    def body(x_ref, g_ref, o_ref):
        xv = x_ref[...]
        r = lax.rsqrt(jnp.mean(xv * xv, axis=-1, keepdims=True) + EPS)
        o_ref[...] = (xv * r * g_ref[...]).astype(BF16)

    return pl.pallas_call(
        body, name=name, grid=(L // tm,), in_specs=[_row_spec(tm, D), _const_spec(g)],
        out_specs=_row_spec(tm, D), out_shape=jax.ShapeDtypeStruct((L, D), BF16))(x, g)


def _rms_bwd(dh, x, g, dres, name):
    L, D = x.shape
    tm = min(512, L)

    def body(dh_ref, x_ref, g_ref, dres_ref, dx_ref, dg_ref):
        xv = x_ref[...]
        dy = dh_ref[...]
        r = lax.rsqrt(jnp.mean(xv * xv, axis=-1, keepdims=True) + EPS)
        dyg = dy * g_ref[...]
        dx_ref[...] = dres_ref[...] + r * dyg - xv * (r * r * r) * jnp.mean(dyg * xv, axis=-1, keepdims=True)
        part = jnp.sum(dy * xv * r, axis=0, keepdims=True)

        @pl.when(pl.program_id(0) == 0)
        def _():
            dg_ref[...] = part

        @pl.when(pl.program_id(0) > 0)
        def _():
            dg_ref[...] += part

    return pl.pallas_call(
        body, name=name, grid=(L // tm,),
        in_specs=[_row_spec(tm, D), _row_spec(tm, D), _const_spec(g), _row_spec(tm, D)],
        out_specs=[_row_spec(tm, D), pl.BlockSpec((1, D), lambda i: (0, 0))],
        out_shape=[jax.ShapeDtypeStruct((L, D), F32), jax.ShapeDtypeStruct((1, D), F32)])(dh, x, g, dres)


def _loss_head(x, g, target, name):
    L, D = x.shape
    tm = min(512, L)

    def body(x_ref, g_ref, t_ref, loss_ref, dx_ref, dg_ref):
        xv = x_ref[...]
        gv = g_ref[...]
        r = lax.rsqrt(jnp.mean(xv * xv, axis=-1, keepdims=True) + EPS)
        xr = xv * r
        err = xr * gv - t_ref[...]
        lp = 0.5 * jnp.sum(jnp.mean(err * err, axis=-1, keepdims=True))
        dy = err * (1.0 / D)
        dyg = dy * gv
        dx_ref[...] = r * dyg - xv * (r * r * r) * jnp.mean(dyg * xv, axis=-1, keepdims=True)
        part = jnp.sum(dy * xr, axis=0, keepdims=True)

        @pl.when(pl.program_id(0) == 0)
        def _():
            dg_ref[...] = part
            loss_ref[...] = jnp.zeros(loss_ref.shape, F32) + lp

        @pl.when(pl.program_id(0) > 0)
        def _():
            dg_ref[...] += part
            loss_ref[...] += lp

    return pl.pallas_call(
        body, name=name, grid=(L // tm,),
        in_specs=[_row_spec(tm, D), _const_spec(g), _row_spec(tm, D)],
        out_specs=[pl.BlockSpec((8, LANES), lambda i: (0, 0)), _row_spec(tm, D), pl.BlockSpec((1, D), lambda i: (0, 0))],
        out_shape=[jax.ShapeDtypeStruct((8, LANES), F32), jax.ShapeDtypeStruct((L, D), F32),
                   jax.ShapeDtypeStruct((1, D), F32)])(x, g, target)


def _mla_prep(p, qg, kvg, wuq, wukv, tabs, name):
    L = p.shape[0]
    tm = min(512, L)
    scale = (MLA_NOPE + MLA_ROPE) ** -0.5
    tc, ta, tb = tabs

    def body(p_ref, qg_ref, kvg_ref, wuq_ref, wukv_ref, c_ref, a_ref, b_ref, q_ref, k_ref, v_ref, cqn_ref, ckvn_ref):
        c, a, b = c_ref[...], a_ref[...], b_ref[...]
        cq = p_ref[:, 0:Q_LORA]
        ckv = p_ref[:, Q_LORA:Q_LORA + KV_LORA]
        kr = p_ref[:, 640:768]
        cqn = (cq * lax.rsqrt(jnp.mean(cq * cq, axis=-1, keepdims=True) + EPS) * qg_ref[...]).astype(BF16)
        ckvn = (ckv * lax.rsqrt(jnp.mean(ckv * ckv, axis=-1, keepdims=True) + EPS) * kvg_ref[...]).astype(BF16)
        cqn_ref[...] = cqn
        ckvn_ref[...] = ckvn
        q = jnp.dot(cqn, wuq_ref[...], preferred_element_type=F32)
        kv = jnp.dot(ckvn, wukv_ref[...], preferred_element_type=F32)
        krr = _rope(kr, c, a, b, MLA_ROPE // 2)
        for h in range(MLA_HEADS):
            sl = slice(h * LANES, (h + 1) * LANES)
            q_ref[:, sl] = (_rope(q[:, sl], c, a, b, MLA_ROPE // 2) * scale).astype(BF16)
            k_ref[:, sl] = (kv[:, sl] + krr).astype(BF16)
        v_ref[...] = kv[:, 1024:1536].astype(BF16)

    return pl.pallas_call(
        body, name=name, grid=(L // tm,),
        in_specs=[_row_spec(tm, P_MLA, 0), _const_spec(qg), _const_spec(kvg), _const_spec(wuq), _const_spec(wukv),
                  _row_spec(tm, LANES), _row_spec(tm, LANES), _row_spec(tm, LANES)],
        out_specs=[_row_spec(tm, 1024), _row_spec(tm, 1024), _row_spec(tm, HEAD_W), _row_spec(tm, Q_LORA),
                   _row_spec(tm, KV_LORA)],
        out_shape=[jax.ShapeDtypeStruct((L, 1024), BF16), jax.ShapeDtypeStruct((L, 1024), BF16),
                   jax.ShapeDtypeStruct((L, HEAD_W), BF16), jax.ShapeDtypeStruct((L, Q_LORA), BF16),
                   jax.ShapeDtypeStruct((L, KV_LORA), BF16)],
        compiler_params=_cparams(VMEM_BIG_MB))(p, qg, kvg, wuq, wukv, tc, ta, tb)


def _mla_prep_bwd(dq, dk, dv, p, qg, kvg, wuq_t, wukv_t, tabs_t, name):
    L = p.shape[0]
    tm = min(512, L)
    scale = (MLA_NOPE + MLA_ROPE) ** -0.5
    tc, ta, tb = tabs_t

    def body(dq_ref, dk_ref, dv_ref, p_ref, qg_ref, kvg_ref, wuqt_ref, wukvt_ref, c_ref, a_ref, b_ref,
             dp_ref, dqp_ref, dkv_ref, dqg_ref, dkvg_ref):
        c, a, b = c_ref[...], a_ref[...], b_ref[...]
        dkr = jnp.zeros((tm, LANES), F32)
        for h in range(MLA_HEADS):
            sl = slice(h * LANES, (h + 1) * LANES)
            dqp_ref[:, sl] = (_rope(dq_ref[:, sl], c, a, b, MLA_ROPE // 2) * scale).astype(BF16)
            dkh = dk_ref[:, sl]
            dkv_ref[:, sl] = dkh.astype(BF16)
            dkr = dkr + dkh
        dkv_ref[:, 1024:1536] = dv_ref[...].astype(BF16)
        lane = lax.broadcasted_iota(jnp.int32, (tm, LANES), 1)
        dkr = jnp.where((lane >= MLA_NOPE) & (lane < MLA_NOPE + MLA_ROPE), _rope(dkr, c, a, b, MLA_ROPE // 2), 0.0)

        d_cqn = jnp.dot(dqp_ref[...], wuqt_ref[...], preferred_element_type=F32)
        d_ckvn = jnp.dot(dkv_ref[...], wukvt_ref[...], preferred_element_type=F32)

        def norm_bwd(xv, gv, dy):
            r = lax.rsqrt(jnp.mean(xv * xv, axis=-1, keepdims=True) + EPS)
            dyg = dy * gv
            dx = r * dyg - xv * (r * r * r) * jnp.mean(dyg * xv, axis=-1, keepdims=True)
            return dx, jnp.sum(dy * xv * r, axis=0, keepdims=True)

        d_cq, dqg = norm_bwd(p_ref[:, 0:Q_LORA], qg_ref[...], d_cqn)
        d_ckv, dkvg = norm_bwd(p_ref[:, Q_LORA:Q_LORA + KV_LORA], kvg_ref[...], d_ckvn)
        dp_ref[:, 0:Q_LORA] = d_cq.astype(BF16)
        dp_ref[:, Q_LORA:Q_LORA + KV_LORA] = d_ckv.astype(BF16)
        dp_ref[:, 640:768] = dkr.astype(BF16)
        dp_ref[:, 768:1024] = jnp.zeros((tm, 256), BF16)

        @pl.when(pl.program_id(0) == 0)
        def _():
            dqg_ref[...] = dqg
            dkvg_ref[...] = dkvg

        @pl.when(pl.program_id(0) > 0)
        def _():
            dqg_ref[...] += dqg
            dkvg_ref[...] += dkvg

    return pl.pallas_call(
        body, name=name, grid=(L // tm,),
        in_specs=[_row_spec(tm, 1024), _row_spec(tm, 1024), _row_spec(tm, HEAD_W), _row_spec(tm, P_MLA, 0),
                  _const_spec(qg), _const_spec(kvg), _const_spec(wuq_t), _const_spec(wukv_t),
                  _row_spec(tm, LANES), _row_spec(tm, LANES), _row_spec(tm, LANES)],
        out_specs=[_row_spec(tm, P_MLA), _row_spec(tm, 1024), _row_spec(tm, 1536),
                   pl.BlockSpec((1, Q_LORA), lambda i: (0, 0)), pl.BlockSpec((1, KV_LORA), lambda i: (0, 0))],
        out_shape=[jax.ShapeDtypeStruct((L, P_MLA), BF16), jax.ShapeDtypeStruct((L, 1024), BF16),
                   jax.ShapeDtypeStruct((L, 1536), BF16), jax.ShapeDtypeStruct((1, Q_LORA), F32),
                   jax.ShapeDtypeStruct((1, KV_LORA), F32)],
        compiler_params=_cparams(VMEM_BIG_MB))(dq, dk, dv, p, qg, kvg, wuq_t, wukv_t, tc, ta, tb)


def _dil_prep(p, tabs, name):
    L = p.shape[0]
    tm = min(512, L)
    tc, ta, tb = tabs
    scale = DIL_HD ** -0.5

    def body(*refs):
        ins, (c_ref, a_ref, b_ref), outs = refs[:9], refs[9:12], refs[12:]
        c, a, b = c_ref[...], a_ref[...], b_ref[...]
        for n in range(9):
            t = n % 3
            for cb in range(HEAD_W // LANES):
                sl = slice(cb * LANES, (cb + 1) * LANES)
                xv = ins[n][:, sl]
                if t == 0:
                    xv = _rope(xv, c, a, b, ROT_DIM // 2) * scale
                elif t == 1:
                    xv = _rope(xv, c, a, b, ROT_DIM // 2)
                outs[n][:, sl] = xv.astype(BF16)

    in_specs = [_row_spec(tm, HEAD_W, P_DIL0 // HEAD_W + n) for n in range(9)] + [_row_spec(tm, LANES)] * 3
    return pl.pallas_call(
        body, name=name, grid=(L // tm,), in_specs=in_specs,
        out_specs=[_row_spec(tm, HEAD_W)] * 9,
        out_shape=[jax.ShapeDtypeStruct((L, HEAD_W), BF16)] * 9)(*([p] * 9), tc, ta, tb)


def _dil_prep_bwd(grads, tabs_t, name):
    L = grads[0].shape[0]
    tm = min(512, L)
    tc, ta, tb = tabs_t
    scale = DIL_HD ** -0.5

    def body(*refs):
        ins, (c_ref, a_ref, b_ref), o_ref = refs[:9], refs[9:12], refs[12]
        c, a, b = c_ref[...], a_ref[...], b_ref[...]
        for n in range(9):
            t = n % 3
            for cb in range(HEAD_W // LANES):
                sl = slice(cb * LANES, (cb + 1) * LANES)
                xv = ins[n][:, sl]
                if t == 0:
                    xv = _rope(xv, c, a, b, ROT_DIM // 2) * scale
                elif t == 1:
                    xv = _rope(xv, c, a, b, ROT_DIM // 2)
                o_ref[:, n * HEAD_W + cb * LANES:n * HEAD_W + (cb + 1) * LANES] = xv.astype(BF16)

    return pl.pallas_call(
        body, name=name, grid=(L // tm,), in_specs=[_row_spec(tm, HEAD_W)] * 9 + [_row_spec(tm, LANES)] * 3,
        out_specs=_row_spec(tm, 9 * HEAD_W), out_shape=jax.ShapeDtypeStruct((L, 9 * HEAD_W), BF16),
        compiler_params=_cparams(VMEM_BIG_MB))(*grads, tc, ta, tb)


def _merge_gate(oa, p, o_g, lse_g, name):
    L = oa.shape[0]
    tm = min(512, L)

    def body(oa_ref, ga_ref, gb_ref, o1, o2, o3, l1, l2, l3, ab_ref, bm_ref, lt_ref):
        la, lb, lc = l1[...], l2[...], l3[...]
        m = jnp.maximum(jnp.maximum(la, lb), lc)
        ea, eb, ec = jnp.exp(la - m), jnp.exp(lb - m), jnp.exp(lc - m)
        den = ea + eb + ec
        bm = (ea * o1[...] + eb * o2[...] + ec * o3[...]) / den
        bm_ref[...] = bm
        lt_ref[...] = m + jnp.log(den)
        ga, gb = ga_ref[...], gb_ref[...]
        ab_ref[:, 0:HEAD_W] = (oa_ref[...] * (ga * _sigmoid(ga))).astype(BF16)
        ab_ref[:, HEAD_W:2 * HEAD_W] = (bm * (gb * _sigmoid(gb))).astype(BF16)

    w = _row_spec(tm, HEAD_W)
    return pl.pallas_call(
        body, name=name, grid=(L // tm,),
        in_specs=[w, _row_spec(tm, HEAD_W, 2), _row_spec(tm, HEAD_W, 3), w, w, w, w, w, w],
        out_specs=[_row_spec(tm, 2 * HEAD_W), w, w],
        out_shape=[jax.ShapeDtypeStruct((L, 2 * HEAD_W), BF16), jax.ShapeDtypeStruct((L, HEAD_W), F32),
                   jax.ShapeDtypeStruct((L, HEAD_W), F32)])(oa, p, p, *o_g, *lse_g)


def _gate_bwd(dab, p, oa, bm, name):
    L = oa.shape[0]
    tm = min(512, L)

    def body(da_ref, db_ref, ga_ref, gb_ref, oa_ref, bm_ref, doa_ref, dbm_ref, Da_ref, Db_ref, dg_ref):
        def one(d, g, o, do_ref, D_ref, col):
            sg = _sigmoid(g)
            do = d * (g * sg)
            do_ref[...] = do.astype(BF16)
            dg_ref[:, col:col + HEAD_W] = (d * o * (sg * (1.0 + g * (1.0 - sg)))).astype(BF16)
            prod = do * o
            for cb in range(HEAD_W // LANES):
                sl = slice(cb * LANES, (cb + 1) * LANES)
                D_ref[:, sl] = _headsum_bcast(prod[:, sl])

        one(da_ref[...], ga_ref[...], oa_ref[...], doa_ref, Da_ref, 0)
        one(db_ref[...], gb_ref[...], bm_ref[...], dbm_ref, Db_ref, HEAD_W)

    w = _row_spec(tm, HEAD_W)
    return pl.pallas_call(
        body, name=name, grid=(L // tm,),
        in_specs=[_row_spec(tm, HEAD_W, 0), _row_spec(tm, HEAD_W, 1), _row_spec(tm, HEAD_W, 2),
                  _row_spec(tm, HEAD_W, 3), w, w],
        out_specs=[w, w, w, w, _row_spec(tm, 2 * HEAD_W)],
        out_shape=[jax.ShapeDtypeStruct((L, HEAD_W), BF16), jax.ShapeDtypeStruct((L, HEAD_W), BF16),
                   jax.ShapeDtypeStruct((L, HEAD_W), F32), jax.ShapeDtypeStruct((L, HEAD_W), F32),
                   jax.ShapeDtypeStruct((L, 2 * HEAD_W), BF16)])(dab, dab, p, p, oa, bm)


NT = (((1,), (1,)), ((), ()))
TN = (((0,), (0,)), ((), ()))
NEG = -1e30


def _mla_fwd(q, k, v, name):
    L = q.shape[0]
    tq = tk = min(512, L)
    nq, nk = L // tq, L // tk
    npair = MLA_HEADS // 2

    def body(q_ref, k_ref, v_ref, o_ref, lse_ref, m_sc, l_sc, acc_sc):
        j = pl.program_id(2)

        @pl.when(j == 0)
        def _():
            m_sc[...] = jnp.full(m_sc.shape, NEG, F32)
            l_sc[...] = jnp.zeros(l_sc.shape, F32)
            acc_sc[...] = jnp.zeros(acc_sc.shape, F32)

        first = lax.broadcasted_iota(jnp.int32, (tq, LANES), 1) < DIL_HD
        vv = v_ref[...]
        alphas, pvs = [], []
        for hh in range(2):
            sl = slice(hh * LANES, (hh + 1) * LANES)
            s = lax.dot_general(q_ref[:, sl], k_ref[:, sl], NT, preferred_element_type=F32)
            m_prev = m_sc[hh]
            m_new = jnp.maximum(m_prev, jnp.max(s, axis=-1, keepdims=True))
            alpha = jnp.exp(m_prev - m_new)
            pr = jnp.exp(s - m_new)
            l_sc[hh] = alpha * l_sc[hh] + jnp.sum(pr, axis=-1, keepdims=True)
            m_sc[hh] = m_new
            pvs.append(jnp.dot(pr.astype(BF16), vv, preferred_element_type=F32))
            alphas.append(alpha)
        acc_sc[...] = jnp.where(first, alphas[0], alphas[1]) * acc_sc[...] + jnp.where(first, pvs[0], pvs[1])

        @pl.when(j == nk - 1)
        def _():
            l = jnp.where(first, l_sc[0], l_sc[1])
            m = jnp.where(first, m_sc[0], m_sc[1])
            o_ref[...] = acc_sc[...] / l
            lse_ref[...] = m + jnp.log(l)

    return pl.pallas_call(
        body, name=name, grid=(npair, nq, nk),
        in_specs=[pl.BlockSpec((tq, 2 * LANES), lambda pr, i, j: (i, pr)),
                  pl.BlockSpec((tk, 2 * LANES), lambda pr, i, j: (j, pr)),
                  pl.BlockSpec((tk, LANES), lambda pr, i, j: (j, pr))],
        out_specs=[pl.BlockSpec((tq, LANES), lambda pr, i, j: (i, pr)),
                   pl.BlockSpec((tq, LANES), lambda pr, i, j: (i, pr))],
        out_shape=[jax.ShapeDtypeStruct((L, HEAD_W), F32), jax.ShapeDtypeStruct((L, HEAD_W), F32)],
        scratch_shapes=[pltpu.VMEM((2, tq, 1), F32), pltpu.VMEM((2, tq, 1), F32), pltpu.VMEM((tq, LANES), F32)],
        compiler_params=_cparams(VMEM_BIG_MB))(q, k, v)


def _mla_bwd(q, k, v, do, lse_rows, d_rows, name):
    L = q.shape[0]
    tq = tk = min(512, L)
    nq, nk = L // tq, L // tk
    npair = MLA_HEADS // 2

    def body(q_ref, k_ref, v_ref, do_ref, lse_ref, d_ref, dq_ref, dk_ref, dv_ref):
        j, i = pl.program_id(1), pl.program_id(2)

        @pl.when((j == 0) & (i == 0))
        def _():
            dq_ref[...] = jnp.zeros(dq_ref.shape, F32)

        @pl.when(i == 0)
        def _():
            dk_ref[...] = jnp.zeros(dk_ref.shape, F32)
            dv_ref[...] = jnp.zeros(dv_ref.shape, F32)

        first = lax.broadcasted_iota(jnp.int32, (tq, LANES), 1) < DIL_HD
        dov = do_ref[...]
        vv = v_ref[...]
        rows = pl.ds(pl.multiple_of(i * tq, tq), tq)
        for hh in range(2):
            sl = slice(hh * LANES, (hh + 1) * LANES)
            qh, kh = q_ref[:, sl], k_ref[:, sl]
            do_h = jnp.where(first if hh == 0 else ~first, dov, jnp.zeros_like(dov))
            s_t = lax.dot_general(kh, qh, NT, preferred_element_type=F32)
            p_t = jnp.exp(s_t - lse_ref[hh:hh + 1, :])
            dv_ref[...] += jnp.dot(p_t.astype(BF16), do_h, preferred_element_type=F32)
            dp_t = lax.dot_general(vv, do_h, NT, preferred_element_type=F32)
            ds_t = (p_t * (dp_t - d_ref[hh:hh + 1, :])).astype(BF16)
            dk_ref[:, sl] += jnp.dot(ds_t, qh, preferred_element_type=F32)
            dq_ref[rows, sl] += lax.dot_general(ds_t, kh, TN, preferred_element_type=F32)

    return pl.pallas_call(
        body, name=name, grid=(npair, nk, nq),
        in_specs=[pl.BlockSpec((tq, 2 * LANES), lambda pr, j, i: (i, pr)),
                  pl.BlockSpec((tk, 2 * LANES), lambda pr, j, i: (j, pr)),
                  pl.BlockSpec((tk, LANES), lambda pr, j, i: (j, pr)),
                  pl.BlockSpec((tq, LANES), lambda pr, j, i: (i, pr)),
                  pl.BlockSpec((None, 2, tq), lambda pr, j, i: (pr, 0, i)),
                  pl.BlockSpec((None, 2, tq), lambda pr, j, i: (pr, 0, i))],
        out_specs=[pl.BlockSpec((L, 2 * LANES), lambda pr, j, i: (0, pr)),
                   pl.BlockSpec((tk, 2 * LANES), lambda pr, j, i: (j, pr)),
                   pl.BlockSpec((tk, LANES), lambda pr, j, i: (j, pr))],
        out_shape=[jax.ShapeDtypeStruct((L, 1024), F32), jax.ShapeDtypeStruct((L, 1024), F32),
                   jax.ShapeDtypeStruct((L, HEAD_W), F32)],
        compiler_params=_cparams(VMEM_BIG_MB))(q, k, v, do, lse_rows, d_rows)


def _dil_tiles(ld):
    tq = min(256, ld)
    kw = min(512, ld)
    return tq, kw


def _dil_fwd(q, k, v, name):
    d, ld, _ = q.shape
    tq, kw = _dil_tiles(ld)
    nq = ld // tq
    npair = HEAD_W // LANES

    def body(q_ref, k_ref, v_ref, o_ref, lse_ref):
        i = pl.program_id(2)
        a0 = i * tq
        start = pl.multiple_of(jnp.clip(a0 - LANES, 0, ld - kw), LANES)
        kwin = k_ref[pl.ds(start, kw), :]
        vwin = v_ref[pl.ds(start, kw), :]
        qpos = a0 + lax.broadcasted_iota(jnp.int32, (tq, kw), 0)
        kpos = start + lax.broadcasted_iota(jnp.int32, (tq, kw), 1)
        valid = jnp.abs(qpos - kpos) <= DIL_HALF
        first = lax.broadcasted_iota(jnp.int32, (tq, LANES), 1) < DIL_HD
        qv = q_ref[...]
        outs, lses = [], []
        for hh in range(2):
            qh = jnp.where(first if hh == 0 else ~first, qv, jnp.zeros_like(qv))
            s = lax.dot_general(qh, kwin, NT, preferred_element_type=F32)
            s = jnp.where(valid, s, NEG)
            m = jnp.max(s, axis=-1, keepdims=True)
            pr = jnp.exp(s - m)
            l = jnp.sum(pr, axis=-1, keepdims=True)
            outs.append(jnp.dot(pr.astype(BF16), vwin, preferred_element_type=F32) / l)
            lses.append(m + jnp.log(l))
        o_ref[...] = jnp.where(first, outs[0], outs[1])
        lse_ref[...] = jnp.where(first, lses[0], lses[1])

    blk = pl.BlockSpec((None, tq, LANES), lambda r, pr, i: (r, i, pr))
    full = pl.BlockSpec((None, ld, LANES), lambda r, pr, i: (r, 0, pr))
    return pl.pallas_call(
        body, name=name, grid=(d, npair, nq), in_specs=[blk, full, full], out_specs=[blk, blk],
        out_shape=[jax.ShapeDtypeStruct((d, ld, HEAD_W), F32), jax.ShapeDtypeStruct((d, ld, HEAD_W), F32)],
        compiler_params=_cparams(VMEM_BIG_MB))(q, k, v)


def _dil_bwd(q, k, v, do, lse_rows, d_rows, name):
    d, ld, _ = q.shape
    tq, kw = _dil_tiles(ld)
    nq = ld // tq
    npair = HEAD_W // LANES

    def body(q_ref, k_ref, v_ref, do_ref, lse_ref, d_ref, dq_ref, dk_ref, dv_ref):
        i = pl.program_id(2)

        @pl.when(i == 0)
        def _():
            dk_ref[...] = jnp.zeros(dk_ref.shape, F32)
            dv_ref[...] = jnp.zeros(dv_ref.shape, F32)

        a0 = i * tq
        start = pl.multiple_of(jnp.clip(a0 - LANES, 0, ld - kw), LANES)
        win = pl.ds(start, kw)
        kwin = k_ref[win, :]
        vwin = v_ref[win, :]
        kpos = start + lax.broadcasted_iota(jnp.int32, (kw, tq), 0)
        qpos = a0 + lax.broadcasted_iota(jnp.int32, (kw, tq), 1)
        valid = jnp.abs(qpos - kpos) <= DIL_HALF
        first = lax.broadcasted_iota(jnp.int32, (tq, LANES), 1) < DIL_HD
        qv = q_ref[...]
        dov = do_ref[...]
        dqs = []
        for hh in range(2):
            sel = first if hh == 0 else ~first
            qh = jnp.where(sel, qv, jnp.zeros_like(qv))
            do_h = jnp.where(sel, dov, jnp.zeros_like(dov))
            s_t = lax.dot_general(kwin, qh, NT, preferred_element_type=F32)
            p_t = jnp.where(valid, jnp.exp(jnp.where(valid, s_t, NEG) - lse_ref[hh:hh + 1, :]), 0.0)
            dv_ref[win, :] += jnp.dot(p_t.astype(BF16), do_h, preferred_element_type=F32)
            dp_t = lax.dot_general(vwin, do_h, NT, preferred_element_type=F32)
            ds_t = (p_t * (dp_t - d_ref[hh:hh + 1, :])).astype(BF16)
            dk_ref[win, :] += jnp.dot(ds_t, qh, preferred_element_type=F32)
            dqs.append(lax.dot_general(ds_t, kwin, TN, preferred_element_type=F32))
        dq_ref[...] = jnp.where(first, dqs[0], dqs[1])

    blk = pl.BlockSpec((None, tq, LANES), lambda r, pr, i: (r, i, pr))
    full = pl.BlockSpec((None, ld, LANES), lambda r, pr, i: (r, 0, pr))
    rowspec = pl.BlockSpec((None, None, 2, tq), lambda r, pr, i: (r, pr, 0, i))
    return pl.pallas_call(
        body, name=name, grid=(d, npair, nq), in_specs=[blk, full, full, blk, rowspec, rowspec],
        out_specs=[blk, full, full],
        out_shape=[jax.ShapeDtypeStruct((d, ld, HEAD_W), F32)] * 3,
        compiler_params=_cparams(VMEM_BIG_MB))(q, k, v, do, lse_rows, d_rows)


def _flat_tile(rows):
    for cand in (3152, 3936, 2048, 1024, 512, 256, 128, 64, 32, 16, 8):
        if rows % cand == 0:
            return cand
    return rows


def _add2(a, b, name):
    n, rows, _ = a.shape
    tr = _flat_tile(rows)

    def body(a_ref, b_ref, o_ref):
        o_ref[...] = a_ref[...] + b_ref[...]

    spec = pl.BlockSpec((None, tr, LANES), lambda t, i: (t, i, 0))
    return pl.pallas_call(body, name=name, grid=(n, rows // tr), in_specs=[spec, spec], out_specs=spec,
                          out_shape=jax.ShapeDtypeStruct(a.shape, F32))(a, b)


def _add4_ordered(a, name):
    _, rows, _ = a.shape
    tr = _flat_tile(rows)

    def body(a_ref, o_ref):
        o_ref[...] = ((a_ref[0] + a_ref[1]) + a_ref[2]) + a_ref[3]

    return pl.pallas_call(
        body, name=name, grid=(rows // tr,), in_specs=[pl.BlockSpec((4, tr, LANES), lambda i: (0, i, 0))],
        out_specs=pl.BlockSpec((tr, LANES), lambda i: (i, 0)),
        out_shape=jax.ShapeDtypeStruct((rows, LANES), F32))(a)


def _adamw(w, g, m, v, name):
    rows = w.shape[0]
    tr = _flat_tile(rows)
    bc1 = 1.0 - ADAM_B1 ** ADAM_STEP
    bc2 = 1.0 - ADAM_B2 ** ADAM_STEP

    def body(w_ref, g_ref, m_ref, v_ref, d_ref, nm_ref, nv_ref):
        gv = g_ref[...]
        nm = ADAM_B1 * m_ref[...] + (1.0 - ADAM_B1) * gv
        nv = ADAM_B2 * v_ref[...] + (1.0 - ADAM_B2) * (gv * gv)
        d_ref[...] = -ADAM_LR * ((nm / bc1) / (jnp.sqrt(nv / bc2) + ADAM_EPS) + ADAM_WD * w_ref[...])
        nm_ref[...] = nm
        nv_ref[...] = nv

    spec = pl.BlockSpec((tr, LANES), lambda i: (i, 0))
    return pl.pallas_call(body, name=name, grid=(rows // tr,), in_specs=[spec] * 4, out_specs=[spec] * 3,
                          out_shape=[jax.ShapeDtypeStruct(w.shape, F32)] * 3)(w, g, m, v)


ANY = pl.BlockSpec(memory_space=pl.ANY)


def _place():
    return lax.axis_index("x"), lax.axis_index("y"), lax.axis_index("c")


def _allgather_weights(wflat):
    rows = wflat.shape[0]
    hr = rows // 2

    def body(w_ref, g_ref, send_sems, recv_sems, local_sem):
        x, y, c = _place()
        s = 2 * x + y
        chips = [(1 - x, y), (x, 1 - y), (1 - x, 1 - y)]

        def half(shard, h):
            return g_ref.at[shard, pl.ds(h * hr, hr), :]

        def rcopy(n, src, dst, to):
            return pltpu.make_async_remote_copy(src_ref=src, dst_ref=dst, send_sem=send_sems.at[n],
                                                recv_sem=recv_sems.at[n], device_id=to, device_id_type=MESH)

        mine = pltpu.make_async_copy(w_ref, g_ref.at[s], local_sem)
        mine.start()
        sends = []
        for n, (cx, cy) in enumerate(chips):
            cp = rcopy(n, w_ref.at[pl.ds(c * hr, hr), :], half(s, c), (cx, cy, c))
            cp.start()
            sends.append(cp)
        for n, (cx, cy) in enumerate(chips):
            sj = 2 * cx + cy
            rcopy(n, half(sj, c), half(sj, c), (cx, cy, c)).wait_recv()
            fw = rcopy(3 + n, half(sj, c), half(sj, c), (x, y, 1 - c))
            fw.start()
            sends.append(fw)
        for n, (cx, cy) in enumerate(chips):
            sj = 2 * cx + cy
            rcopy(3 + n, half(sj, 1 - c), half(sj, 1 - c), (x, y, 1 - c)).wait_recv()
        for cp in sends:
            cp.wait_send()
        mine.wait()

    return pl.pallas_call(
        body, name="allgather_weights", in_specs=[ANY], out_specs=ANY,
        out_shape=jax.ShapeDtypeStruct((N_SHARD, rows, LANES), wflat.dtype),
        scratch_shapes=[pltpu.SemaphoreType.DMA((6,)), pltpu.SemaphoreType.DMA((6,)), pltpu.SemaphoreType.DMA])(wflat)


def _sibling_send_halves(g):
    _, _, hr, _ = g.shape

    def body(g_ref, o_ref, send_sems, recv_sems):
        x, y, c = _place()
        cps = []
        for t in range(N_SHARD):
            cp = pltpu.make_async_remote_copy(src_ref=g_ref.at[t, 1 - c], dst_ref=o_ref.at[t],
                                              send_sem=send_sems.at[t], recv_sem=recv_sems.at[t],
                                              device_id=(x, y, 1 - c), device_id_type=MESH)
            cp.start()
            cps.append(cp)
        for cp in cps:
            cp.wait()

    return pl.pallas_call(
        body, name="grad_sibling_exchange", in_specs=[ANY], out_specs=ANY,
        out_shape=jax.ShapeDtypeStruct((N_SHARD, hr, LANES), g.dtype),
        scratch_shapes=[pltpu.SemaphoreType.DMA((N_SHARD,)), pltpu.SemaphoreType.DMA((N_SHARD,))])(g)


def _chip_scatter(a):
    _, hr, _ = a.shape

    def body(a_ref, o_ref, send_sems, recv_sems, local_sem):
        x, y, c = _place()
        s = 2 * x + y
        chips = [(1 - x, y), (x, 1 - y), (1 - x, 1 - y)]
        mine = pltpu.make_async_copy(a_ref.at[s], o_ref.at[s], local_sem)
        mine.start()
        cps = []
        for n, (cx, cy) in enumerate(chips):
            cp = pltpu.make_async_remote_copy(src_ref=a_ref.at[2 * cx + cy], dst_ref=o_ref.at[s],
                                              send_sem=send_sems.at[n], recv_sem=recv_sems.at[n],
                                              device_id=(cx, cy, c), device_id_type=MESH)
            cp.start()
            cps.append(cp)
        for n, (cx, cy) in enumerate(chips):
            sj = 2 * cx + cy
            pltpu.make_async_remote_copy(src_ref=a_ref.at[sj], dst_ref=o_ref.at[sj], send_sem=send_sems.at[n],
                                         recv_sem=recv_sems.at[n], device_id=(cx, cy, c),
                                         device_id_type=MESH).wait_recv()
        for cp in cps:
            cp.wait_send()
        mine.wait()

    return pl.pallas_call(
        body, name="grad_chip_scatter", in_specs=[ANY], out_specs=ANY,
        out_shape=jax.ShapeDtypeStruct(a.shape, a.dtype),
        scratch_shapes=[pltpu.SemaphoreType.DMA((3,)), pltpu.SemaphoreType.DMA((3,)), pltpu.SemaphoreType.DMA])(a)


def _sibling_swap(r):
    def body(r_ref, o_ref, send_sem, recv_sem):
        x, y, c = _place()
        cp = pltpu.make_async_remote_copy(src_ref=r_ref, dst_ref=o_ref, send_sem=send_sem, recv_sem=recv_sem,
                                          device_id=(x, y, 1 - c), device_id_type=MESH)
        cp.start()
        cp.wait()

    return pl.pallas_call(
        body, name="grad_sibling_swap", in_specs=[ANY], out_specs=ANY,
        out_shape=jax.ShapeDtypeStruct(r.shape, r.dtype),
        scratch_shapes=[pltpu.SemaphoreType.DMA, pltpu.SemaphoreType.DMA])(r)


def _pack_shard(w_in, w_uq, w_ukv, w_out, small=None):
    parts = [w_in.reshape(-1), w_uq.reshape(-1), w_ukv.reshape(-1), w_out.reshape(-1)]
    if small is not None:
        parts += [small, jnp.zeros((N_PAD,), w_in.dtype)]
    return jnp.concatenate(parts).reshape(-1, LANES)


def _unpack_shard(flat):
    f = flat.reshape(-1)
    o = 0
    out = []
    for n, shape in ((N_WIN, (DEPTH, D_MODEL, IN_WIDTH // N_SHARD)), (N_WUQ, (DEPTH, Q_LORA, 768 // N_SHARD)),
                     (N_WUKV, (DEPTH, KV_LORA, 1024 // N_SHARD)), (N_WOUT, (DEPTH, 1024 // N_SHARD, D_MODEL))):
        out.append(f[o:o + n].reshape(shape))
        o += n
    return out, f[o:o + N_SMALL]


def _split_small(s):
    o = 0
    out = []
    for n, shape in ((DEPTH * D_MODEL, (DEPTH, D_MODEL)), (DEPTH * Q_LORA, (DEPTH, Q_LORA)),
                     (DEPTH * KV_LORA, (DEPTH, KV_LORA)), (D_MODEL, (D_MODEL,))):
        out.append(s[o:o + n].reshape(shape))
        o += n
    return out


def _unshard_weights(gathered):
    f = gathered.reshape(N_SHARD, -1)
    o = 0
    w_in = f[:, o:o + N_WIN].reshape(N_SHARD, DEPTH, D_MODEL, -1).transpose(1, 2, 0, 3).reshape(DEPTH, D_MODEL, IN_WIDTH)
    o += N_WIN
    w_uq = f[:, o:o + N_WUQ].reshape(N_SHARD, DEPTH, Q_LORA, -1).transpose(1, 2, 0, 3).reshape(DEPTH, Q_LORA, 768)
    o += N_WUQ
    w_ukv = f[:, o:o + N_WUKV].reshape(N_SHARD, DEPTH, KV_LORA, -1).transpose(1, 2, 0, 3).reshape(DEPTH, KV_LORA, 1024)
    o += N_WUKV
    w_out = f[:, o:o + N_WOUT].reshape(N_SHARD, DEPTH, -1, D_MODEL).transpose(1, 0, 2, 3).reshape(DEPTH, 1024, D_MODEL)
    return w_in, w_uq, w_ukv, w_out


def _pack_grads(dw_in, dw_uq, dw_ukv, dw_out, small):
    a = dw_in.reshape(DEPTH, D_MODEL, N_SHARD, -1).transpose(2, 0, 1, 3).reshape(N_SHARD, -1)
    b = dw_uq.reshape(DEPTH, Q_LORA, N_SHARD, -1).transpose(2, 0, 1, 3).reshape(N_SHARD, -1)
    c = dw_ukv.reshape(DEPTH, KV_LORA, N_SHARD, -1).transpose(2, 0, 1, 3).reshape(N_SHARD, -1)
    d = dw_out.reshape(DEPTH, N_SHARD, -1, D_MODEL).transpose(1, 0, 2, 3).reshape(N_SHARD, -1)
    e = jnp.broadcast_to(small[None], (N_SHARD, N_SMALL))
    z = jnp.zeros((N_SHARD, N_PAD), F32)
    return jnp.concatenate([a, b, c, d, e, z], axis=1).reshape(N_SHARD, ROWS_G, LANES)


def _pad_w_in(w):
    z = lambda n: jnp.zeros(w.shape[:-1] + (n,), w.dtype)
    return jnp.concatenate([w[..., 0:640], z(64), w[..., 640:672], z(32), z(256), w[..., 672:1184],
                            w[..., 5792:6304], w[..., 1184:5792]], axis=-1)


def _unpad_w_in(w):
    return jnp.concatenate([w[..., 0:640], w[..., 704:736], w[..., 1024:1536], w[..., 2048:6656],
                            w[..., 1536:2048]], axis=-1)


def _pad_w_uq(w):
    s = w.shape[:-1]
    w = w.reshape(s + (MLA_HEADS, 96))
    return jnp.pad(w, [(0, 0)] * (w.ndim - 1) + [(0, 32)]).reshape(s + (1024,))


def _unpad_w_uq(w):
    s = w.shape[:-1]
    return w.reshape(s + (MLA_HEADS, LANES))[..., :96].reshape(s + (768,))


def _pad_w_ukv(w):
    s = w.shape[:-1]
    w = w.reshape(s + (MLA_HEADS, 128))
    kpart = jnp.pad(w[..., :64], [(0, 0)] * (w.ndim - 1) + [(0, 64)]).reshape(s + (1024,))
    vpart = w[..., 64:].reshape(s + (512,))
    return jnp.concatenate([kpart, vpart], axis=-1)


def _unpad_w_ukv(w):
    s = w.shape[:-1]
    kpart = w[..., :1024].reshape(s + (MLA_HEADS, LANES))[..., :64]
    vpart = w[..., 1024:].reshape(s + (MLA_HEADS, 64))
    return jnp.concatenate([kpart, vpart], axis=-1).reshape(s + (1024,))


def _rope_tables(L, dim, lane_lo, period):
    half = dim // 2
    inv = 1.0 / (ROPE_THETA ** (jnp.arange(0, dim, 2, dtype=F32) / dim))
    ang = jnp.arange(L, dtype=F32)[:, None] * inv[None, :]
    cos, sin = jnp.cos(ang), jnp.sin(ang)
    one = lambda n: jnp.ones((L, n), F32)
    zero = lambda n: jnp.zeros((L, n), F32)
    rest = period - lane_lo - dim
    rep = LANES // period
    c = jnp.tile(jnp.concatenate([one(lane_lo), cos, cos, one(rest)], axis=1), (1, rep))
    a = jnp.tile(jnp.concatenate([zero(lane_lo), -sin, zero(half), zero(rest)], axis=1), (1, rep))
    b = jnp.tile(jnp.concatenate([zero(lane_lo + half), sin, zero(rest)], axis=1), (1, rep))
    return c, a, b


def _to_strided(t, d):
    L, w = t.shape
    return t.reshape(L // d, d, w).transpose(1, 0, 2)


def _from_strided(t):
    d, ld, w = t.shape
    return t.transpose(1, 0, 2).reshape(d * ld, w)


def _head_rows(t):
    return t[:, ::DIL_HD].T.reshape(MLA_HEADS // 2, 2, t.shape[0])


def _head_rows_strided(t, d):
    s = _to_strided(t[:, ::DIL_HD], d)
    return s.transpose(0, 2, 1).reshape(d, MLA_HEADS // 2, 2, s.shape[1])


def _local_grads(x, target, norm_g, w_in_p, q_norm_g, kv_norm_g, w_uq_p, w_ukv_p, w_out, final_g):
    L = x.shape[0]
    tabs_m = _rope_tables(L, MLA_ROPE, MLA_NOPE, LANES)
    tabs_d = _rope_tables(L, ROT_DIM, 0, DIL_HD)
    tabs_m_t = (tabs_m[0], -tabs_m[1], -tabs_m[2])
    tabs_d_t = (tabs_d[0], -tabs_d[1], -tabs_d[2])
    w_in_t = jnp.swapaxes(w_in_p, 1, 2)
    w_uq_t = jnp.swapaxes(w_uq_p, 1, 2)
    w_ukv_t = jnp.swapaxes(w_ukv_p, 1, 2)
    w_out_t = jnp.swapaxes(w_out, 1, 2)

    saved = []
    for l in range(DEPTH):
        h = _rms_fwd(x, norm_g[l:l + 1], "rms_fwd")
        p = _mm(h, w_in_p[l], tm=512, tn=3328, tk=1024, out_dtype=F32, name="in_proj")
        q, k, v, cqn, ckvn = _mla_prep(p, q_norm_g[l:l + 1], kv_norm_g[l:l + 1], w_uq_p[l], w_ukv_p[l], tabs_m,
                                       "mla_prep")
        oa, lse_a = _mla_fwd(q, k, v, "mla_fwd")
        dil = _dil_prep(p, tabs_d, "dil_prep")
        dil_s, o_g, lse_g = [], [], []
        for g, (_, dd) in enumerate(DIL_PAIRS):
            qs, ks, vs = (_to_strided(t, dd) for t in dil[3 * g:3 * g + 3])
            og, lg = _dil_fwd(qs, ks, vs, "dil_fwd_%d" % dd)
            dil_s.append((qs, ks, vs))
            o_g.append(_from_strided(og))
            lse_g.append(_from_strided(lg))
        ab, bm, lt = _merge_gate(oa, p, o_g, lse_g, "merge_gate")
        x_next = _mm(ab, w_out[l], tm=1024, tn=1024, tk=1024, out_dtype=F32, name="out_proj", add=x)
        saved.append((x, h, p, q, k, v, cqn, ckvn, oa, lse_a, dil_s, bm, lt, ab))
        x = x_next

    loss_b, dx, d_final = _loss_head(x, final_g[None, :], target, "loss_head")
    loss = loss_b[0, 0]

    d_norm, d_qn, d_kvn, d_win, d_wuq, d_wukv, d_wout = [], [], [], [], [], [], []
    for l in reversed(range(DEPTH)):
        x_l, h, p, q, k, v, cqn, ckvn, oa, lse_a, dil_s, bm, lt, ab = saved[l]
        dab = _mm(dx, w_out_t[l], tm=1024, tn=1024, tk=1024, out_dtype=F32, name="out_proj_dgrad")
        d_wout.append(_mm(ab.T, dx, tm=1024, tn=1024, tk=1024, out_dtype=F32, name="out_proj_wgrad"))
        doa, dbm, D_a, D_b, dgates = _gate_bwd(dab, p, oa, bm, "gate_bwd")
        dq, dk, dv = _mla_bwd(q, k, v, doa, _head_rows(lse_a), _head_rows(D_a), "mla_bwd")
        dp_mla, dq_pre, dkv, dqg, dkvg = _mla_prep_bwd(dq, dk, dv, p, q_norm_g[l:l + 1], kv_norm_g[l:l + 1],
                                                       w_uq_t[l], w_ukv_t[l], tabs_m_t, "mla_prep_bwd")
        d_wuq.append(_mm(cqn.T, dq_pre, tm=Q_LORA, tn=1024, tk=2048, out_dtype=F32, name="w_uq_wgrad"))
        d_wukv.append(_mm(ckvn.T, dkv, tm=KV_LORA, tn=1536, tk=2048, out_dtype=F32, name="w_ukv_wgrad"))
        dgr = []
        for g, (_, dd) in enumerate(DIL_PAIRS):
            qs, ks, vs = dil_s[g]
            dqs, dks, dvs = _dil_bwd(qs, ks, vs, _to_strided(dbm, dd), _head_rows_strided(lt, dd),
                                     _head_rows_strided(D_b, dd), "dil_bwd_%d" % dd)
            dgr += [_from_strided(dqs), _from_strided(dks), _from_strided(dvs)]
        dp_dil = _dil_prep_bwd(dgr, tabs_d_t, "dil_prep_bwd")
        dp = jnp.concatenate([dp_mla, dgates, dp_dil], axis=1)
        dh = _mm(dp, w_in_t[l], tm=1024, tn=1024, tk=1664, out_dtype=F32, name="in_proj_dgrad")
        d_win.append(_mm(h.T, dp, tm=1024, tn=1664, tk=1024, out_dtype=F32, name="in_proj_wgrad"))
        dx, dng = _rms_bwd(dh, x_l, norm_g[l:l + 1], dx, "rms_bwd")
        d_norm.append(dng[0])
        d_qn.append(dqg[0])
        d_kvn.append(dkvg[0])

    rev = lambda xs: jnp.stack(xs[::-1])
    return (loss, dx, rev(d_norm), rev(d_win), rev(d_qn), rev(d_kvn), rev(d_wuq), rev(d_wukv), rev(d_wout),
            d_final[0])


def kernel(x, norm_g, w_in, q_norm_g, kv_norm_g, w_uq, w_ukv, w_out, final_g, loss_target, m_norm_g, m_w_in, m_q_norm_g, m_kv_norm_g, m_w_uq, m_w_ukv, m_w_out, m_final_g, v_norm_g, v_w_in, v_q_norm_g, v_kv_norm_g, v_w_uq, v_w_ukv, v_w_out, v_final_g):
    c = lax.axis_index("c")

    wflat = _pack_shard(w_in, w_uq, w_ukv, w_out).astype(BF16)
    full_in, full_uq, full_ukv, full_out = _unshard_weights(_allgather_weights(wflat))

    (loss, dx, d_norm, d_win_p, d_qn, d_kvn, d_wuq_p, d_wukv_p, d_wout, d_final) = _local_grads(
        x[0], loss_target[0], norm_g, _pad_w_in(full_in), q_norm_g, kv_norm_g, _pad_w_uq(full_uq),
        _pad_w_ukv(full_ukv), full_out, final_g)
    loss = lax.psum(loss, ("x", "y", "c"))

    small = jnp.concatenate([d_norm.reshape(-1), d_qn.reshape(-1), d_kvn.reshape(-1), d_final])
    gflat = _pack_grads(_unpad_w_in(d_win_p), _unpad_w_uq(d_wuq_p), _unpad_w_ukv(d_wukv_p), d_wout, small)
    gh = gflat.reshape(N_SHARD, 2, ROWS_G // 2, LANES)
    from_sib = _sibling_send_halves(gh)
    chip_sum = _add2(lax.dynamic_index_in_dim(gh, c, axis=1, keepdims=False), from_sib, "grad_add_pair")
    red_half = _add4_ordered(_chip_scatter(chip_sum), "grad_add_chips")
    other_half = _sibling_swap(red_half)
    both = jnp.stack([red_half, other_half])
    gred = jnp.concatenate([lax.dynamic_index_in_dim(both, c, axis=0, keepdims=False),
                            lax.dynamic_index_in_dim(both, 1 - c, axis=0, keepdims=False)], axis=0)

    cat_small = lambda a, b, cc, d: jnp.concatenate([a.reshape(-1), b.reshape(-1), cc.reshape(-1), d.reshape(-1)])
    wf = _pack_shard(w_in, w_uq, w_ukv, w_out, cat_small(norm_g, q_norm_g, kv_norm_g, final_g))
    mf = _pack_shard(m_w_in, m_w_uq, m_w_ukv, m_w_out, cat_small(m_norm_g, m_q_norm_g, m_kv_norm_g, m_final_g))
    vf = _pack_shard(v_w_in, v_w_uq, v_w_ukv, v_w_out, cat_small(v_norm_g, v_q_norm_g, v_kv_norm_g, v_final_g))
    delta, new_m, new_v = _adamw(wf, gred, mf, vf, "adamw")

    def leaves(flat):
        (a_in, a_uq, a_ukv, a_out), s = _unpack_shard(flat)
        s_norm, s_qn, s_kvn, s_final = _split_small(s)
        return [s_norm, a_in, s_qn, s_kvn, a_uq, a_ukv, a_out, s_final]

    return (loss, dx[None], *leaves(gred), *leaves(delta), *leaves(new_m), *leaves(new_v))
```

```python
import functools

import jax
import jax.numpy as jnp
from jax import lax
from jax.experimental import pallas as pl
from jax.experimental.pallas import tpu as pltpu

F32 = jnp.float32
BF16 = jnp.bfloat16
MESH = pl.DeviceIdType.MESH

D_MODEL = 1024
DEPTH = 4
MLA_HEADS = 8
MLA_NOPE = 64
MLA_ROPE = 32
Q_LORA = 384
KV_LORA = 256
DIL_PAIRS = ((128, 1), (512, 4), (2048, 16))
DIL_HD = 64
DIL_HALF = 64
ROT_DIM = 16
ROPE_THETA = 500000.0
EPS = 1e-6
IN_WIDTH = 6304
N_SHARD = 4

P_WIDTH = 6656
P_MLA = 1024
P_GATE = 1024
P_DIL0 = 2048
LANES = 128
HEAD_W = 512

ADAM_LR = 0.001
ADAM_B1 = 0.9
ADAM_B2 = 0.999
ADAM_EPS = 1e-08
ADAM_WD = 0.01
ADAM_STEP = 10

SHARD_COLS_IN = IN_WIDTH // N_SHARD
FAM_SHAPES = ((DEPTH * D_MODEL, SHARD_COLS_IN), (DEPTH * Q_LORA, 768 // N_SHARD), (DEPTH * KV_LORA, 1024 // N_SHARD),
              (DEPTH * (1024 // N_SHARD), D_MODEL))
N_SMALL = DEPTH * (D_MODEL + Q_LORA + KV_LORA) + D_MODEL
SMALL_ROWS = 64
VMEM_BIG_MB = 48


def _cparams(vmem_mb=None):
    if vmem_mb is None:
        return None
    return pltpu.CompilerParams(vmem_limit_bytes=vmem_mb << 20)


def _sigmoid(x):
    return 1.0 / (1.0 + jnp.exp(-x))


def _rope(x, c, a, b, sh):
    return x * c + pltpu.roll(x, LANES - sh, 1) * a + pltpu.roll(x, sh, 1) * b


def _headsum_bcast(x):
    r = lax.broadcasted_iota(jnp.int32, (LANES, LANES), 0) // DIL_HD
    c = lax.broadcasted_iota(jnp.int32, (LANES, LANES), 1) // DIL_HD
    ones = jnp.where(r == c, 1.0, 0.0).astype(BF16)
    hi = x.astype(BF16)
    lo = (x - hi.astype(F32)).astype(BF16)
    return (jnp.dot(hi, ones, preferred_element_type=F32) + jnp.dot(lo, ones, preferred_element_type=F32))


def _mm(a, b, *, tm, tn, tk, out_dtype, name, add=None):
    M, K = a.shape
    N = b.shape[1]
    tm, tn, tk = min(tm, M), min(tn, N), min(tk, K)
    assert M % tm == 0 and N % tn == 0 and K % tk == 0, (a.shape, b.shape)
    nk = K // tk
    has_add = add is not None

    def body(*refs):
        if has_add:
            a_ref, b_ref, add_ref, o_ref, acc = refs
        else:
            a_ref, b_ref, o_ref, acc = refs
        k = pl.program_id(2)
        part = jnp.dot(a_ref[...].astype(BF16), b_ref[...].astype(BF16), preferred_element_type=F32)

        @pl.when(k == 0)
        def _():
            acc[...] = part

        @pl.when(k > 0)
        def _():
            acc[...] += part

        @pl.when(k == nk - 1)
        def _():
            r = acc[...]
            if has_add:
                r = r + add_ref[...]
            o_ref[...] = r.astype(out_dtype)

    in_specs = [pl.BlockSpec((tm, tk), lambda i, j, k: (i, k)), pl.BlockSpec((tk, tn), lambda i, j, k: (k, j))]
    args = [a, b]
    if has_add:
        in_specs.append(pl.BlockSpec((tm, tn), lambda i, j, k: (i, j)))
        args.append(add)
    return pl.pallas_call(
        body, name=name, grid=(M // tm, N // tn, nk), in_specs=in_specs,
        out_specs=pl.BlockSpec((tm, tn), lambda i, j, k: (i, j)),
        out_shape=jax.ShapeDtypeStruct((M, N), out_dtype),
        scratch_shapes=[pltpu.VMEM((tm, tn), F32)], compiler_params=_cparams(VMEM_BIG_MB))(*args)


def _row_spec(tm, w, cb=0):
    return pl.BlockSpec((tm, w), lambda i: (i, cb))


def _const_spec(arr):
    nd = arr.ndim
    return pl.BlockSpec(arr.shape, lambda i: (0,) * nd)


def _rms_fwd(x, g, name):
    L, D = x.shape
    tm = min(512, L)

    def body(x_ref, g_ref, o_ref):
        xv = x_ref[...]
        r = lax.rsqrt(jnp.mean(xv * xv, axis=-1, keepdims=True) + EPS)
        o_ref[...] = (xv * r * g_ref[...]).astype(BF16)

    return pl.pallas_call(
        body, name=name, grid=(L // tm,), in_specs=[_row_spec(tm, D), _const_spec(g)],
        out_specs=_row_spec(tm, D), out_shape=jax.ShapeDtypeStruct((L, D), BF16))(x, g)


def _rms_bwd(dh, x, g, dres, name):
    L, D = x.shape
    tm = min(512, L)

    def body(dh_ref, x_ref, g_ref, dres_ref, dx_ref, dg_ref):
        xv = x_ref[...]
        dy = dh_ref[...]
        r = lax.rsqrt(jnp.mean(xv * xv, axis=-1, keepdims=True) + EPS)
        dyg = dy * g_ref[...]
        dx_ref[...] = dres_ref[...] + r * dyg - xv * (r * r * r) * jnp.mean(dyg * xv, axis=-1, keepdims=True)
        part = jnp.sum(dy * xv * r, axis=0, keepdims=True)

        @pl.when(pl.program_id(0) == 0)
        def _():
            dg_ref[...] = part

        @pl.when(pl.program_id(0) > 0)
        def _():
            dg_ref[...] += part

    return pl.pallas_call(
        body, name=name, grid=(L // tm,),
        in_specs=[_row_spec(tm, D), _row_spec(tm, D), _const_spec(g), _row_spec(tm, D)],
        out_specs=[_row_spec(tm, D), pl.BlockSpec((1, D), lambda i: (0, 0))],
        out_shape=[jax.ShapeDtypeStruct((L, D), F32), jax.ShapeDtypeStruct((1, D), F32)])(dh, x, g, dres)


def _loss_head(x, g, target, name):
    L, D = x.shape
    tm = min(512, L)

    def body(x_ref, g_ref, t_ref, loss_ref, dx_ref, dg_ref):
        xv = x_ref[...]
        gv = g_ref[...]
        r = lax.rsqrt(jnp.mean(xv * xv, axis=-1, keepdims=True) + EPS)
        xr = xv * r
        err = xr * gv - t_ref[...]
        lp = 0.5 * jnp.sum(jnp.mean(err * err, axis=-1, keepdims=True))
        dy = err * (1.0 / D)
        dyg = dy * gv
        dx_ref[...] = r * dyg - xv * (r * r * r) * jnp.mean(dyg * xv, axis=-1, keepdims=True)
        part = jnp.sum(dy * xr, axis=0, keepdims=True)

        @pl.when(pl.program_id(0) == 0)
        def _():
            dg_ref[...] = part
            loss_ref[...] = jnp.zeros(loss_ref.shape, F32) + lp

        @pl.when(pl.program_id(0) > 0)
        def _():
            dg_ref[...] += part
            loss_ref[...] += lp

    return pl.pallas_call(
        body, name=name, grid=(L // tm,),
        in_specs=[_row_spec(tm, D), _const_spec(g), _row_spec(tm, D)],
        out_specs=[pl.BlockSpec((8, LANES), lambda i: (0, 0)), _row_spec(tm, D), pl.BlockSpec((1, D), lambda i: (0, 0))],
        out_shape=[jax.ShapeDtypeStruct((8, LANES), F32), jax.ShapeDtypeStruct((L, D), F32),
                   jax.ShapeDtypeStruct((1, D), F32)])(x, g, target)


def _mla_prep(p, qg, kvg, wuq, wukv, tabs, name):
    L = p.shape[0]
    tm = min(512, L)
    scale = (MLA_NOPE + MLA_ROPE) ** -0.5
    tc, ta, tb = tabs

    def body(p_ref, qg_ref, kvg_ref, wuq_ref, wukv_ref, c_ref, a_ref, b_ref, q_ref, k_ref, v_ref, cqn_ref, ckvn_ref):
        c, a, b = c_ref[...], a_ref[...], b_ref[...]
        cq = p_ref[:, 0:Q_LORA]
        ckv = p_ref[:, Q_LORA:Q_LORA + KV_LORA]
        kr = p_ref[:, 640:768]
        cqn = (cq * lax.rsqrt(jnp.mean(cq * cq, axis=-1, keepdims=True) + EPS) * qg_ref[...]).astype(BF16)
        ckvn = (ckv * lax.rsqrt(jnp.mean(ckv * ckv, axis=-1, keepdims=True) + EPS) * kvg_ref[...]).astype(BF16)
        cqn_ref[...] = cqn
        ckvn_ref[...] = ckvn
        q = jnp.dot(cqn, wuq_ref[...], preferred_element_type=F32)
        kv = jnp.dot(ckvn, wukv_ref[...], preferred_element_type=F32)
        krr = _rope(kr, c, a, b, MLA_ROPE // 2)
        for h in range(MLA_HEADS):
            sl = slice(h * LANES, (h + 1) * LANES)
            q_ref[:, sl] = (_rope(q[:, sl], c, a, b, MLA_ROPE // 2) * scale).astype(BF16)
            k_ref[:, sl] = (kv[:, sl] + krr).astype(BF16)
        v_ref[...] = kv[:, 1024:1536].astype(BF16)

    return pl.pallas_call(
        body, name=name, grid=(L // tm,),
        in_specs=[_row_spec(tm, P_MLA, 0), _const_spec(qg), _const_spec(kvg), _const_spec(wuq), _const_spec(wukv),
                  _row_spec(tm, LANES), _row_spec(tm, LANES), _row_spec(tm, LANES)],
        out_specs=[_row_spec(tm, 1024), _row_spec(tm, 1024), _row_spec(tm, HEAD_W), _row_spec(tm, Q_LORA),
                   _row_spec(tm, KV_LORA)],
        out_shape=[jax.ShapeDtypeStruct((L, 1024), BF16), jax.ShapeDtypeStruct((L, 1024), BF16),
                   jax.ShapeDtypeStruct((L, HEAD_W), BF16), jax.ShapeDtypeStruct((L, Q_LORA), BF16),
                   jax.ShapeDtypeStruct((L, KV_LORA), BF16)],
        compiler_params=_cparams(VMEM_BIG_MB))(p, qg, kvg, wuq, wukv, tc, ta, tb)


def _mla_prep_bwd(dq, dk, dv, p, qg, kvg, wuq_t, wukv_t, tabs_t, name):
    L = p.shape[0]
    tm = min(512, L)
    scale = (MLA_NOPE + MLA_ROPE) ** -0.5
    tc, ta, tb = tabs_t

    def body(dq_ref, dk_ref, dv_ref, p_ref, qg_ref, kvg_ref, wuqt_ref, wukvt_ref, c_ref, a_ref, b_ref,
             dp_ref, dqp_ref, dkv_ref, dqg_ref, dkvg_ref):
        c, a, b = c_ref[...], a_ref[...], b_ref[...]
        dkr = jnp.zeros((tm, LANES), F32)
        for h in range(MLA_HEADS):
            sl = slice(h * LANES, (h + 1) * LANES)
            dqp_ref[:, sl] = (_rope(dq_ref[:, sl], c, a, b, MLA_ROPE // 2) * scale).astype(BF16)
            dkh = dk_ref[:, sl]
            dkv_ref[:, sl] = dkh.astype(BF16)
            dkr = dkr + dkh
        dkv_ref[:, 1024:1536] = dv_ref[...].astype(BF16)
        lane = lax.broadcasted_iota(jnp.int32, (tm, LANES), 1)
        dkr = jnp.where((lane >= MLA_NOPE) & (lane < MLA_NOPE + MLA_ROPE), _rope(dkr, c, a, b, MLA_ROPE // 2), 0.0)

        d_cqn = jnp.dot(dqp_ref[...], wuqt_ref[...], preferred_element_type=F32)
        d_ckvn = jnp.dot(dkv_ref[...], wukvt_ref[...], preferred_element_type=F32)

        def norm_bwd(xv, gv, dy):
            r = lax.rsqrt(jnp.mean(xv * xv, axis=-1, keepdims=True) + EPS)
            dyg = dy * gv
            dx = r * dyg - xv * (r * r * r) * jnp.mean(dyg * xv, axis=-1, keepdims=True)
            return dx, jnp.sum(dy * xv * r, axis=0, keepdims=True)

        d_cq, dqg = norm_bwd(p_ref[:, 0:Q_LORA], qg_ref[...], d_cqn)
        d_ckv, dkvg = norm_bwd(p_ref[:, Q_LORA:Q_LORA + KV_LORA], kvg_ref[...], d_ckvn)
        dp_ref[:, 0:Q_LORA] = d_cq.astype(BF16)
        dp_ref[:, Q_LORA:Q_LORA + KV_LORA] = d_ckv.astype(BF16)
        dp_ref[:, 640:768] = dkr.astype(BF16)
        dp_ref[:, 768:1024] = jnp.zeros((tm, 256), BF16)

        @pl.when(pl.program_id(0) == 0)
        def _():
            dqg_ref[...] = dqg
            dkvg_ref[...] = dkvg

        @pl.when(pl.program_id(0) > 0)
        def _():
            dqg_ref[...] += dqg
            dkvg_ref[...] += dkvg

    return pl.pallas_call(
        body, name=name, grid=(L // tm,),
        in_specs=[_row_spec(tm, 1024), _row_spec(tm, 1024), _row_spec(tm, HEAD_W), _row_spec(tm, P_MLA, 0),
                  _const_spec(qg), _const_spec(kvg), _const_spec(wuq_t), _const_spec(wukv_t),
                  _row_spec(tm, LANES), _row_spec(tm, LANES), _row_spec(tm, LANES)],
        out_specs=[_row_spec(tm, P_MLA), _row_spec(tm, 1024), _row_spec(tm, 1536),
                   pl.BlockSpec((1, Q_LORA), lambda i: (0, 0)), pl.BlockSpec((1, KV_LORA), lambda i: (0, 0))],
        out_shape=[jax.ShapeDtypeStruct((L, P_MLA), BF16), jax.ShapeDtypeStruct((L, 1024), BF16),
                   jax.ShapeDtypeStruct((L, 1536), BF16), jax.ShapeDtypeStruct((1, Q_LORA), F32),
                   jax.ShapeDtypeStruct((1, KV_LORA), F32)],
        compiler_params=_cparams(VMEM_BIG_MB))(dq, dk, dv, p, qg, kvg, wuq_t, wukv_t, tc, ta, tb)


def _dil_prep(p, tabs, name):
    L = p.shape[0]
    tm = min(512, L)
    tc, ta, tb = tabs
    scale = DIL_HD ** -0.5

    def body(*refs):
        ins, (c_ref, a_ref, b_ref), outs = refs[:9], refs[9:12], refs[12:]
        c, a, b = c_ref[...], a_ref[...], b_ref[...]
        for n in range(9):
            t = n % 3
            for cb in range(HEAD_W // LANES):
                sl = slice(cb * LANES, (cb + 1) * LANES)
                xv = ins[n][:, sl]
                if t == 0:
                    xv = _rope(xv, c, a, b, ROT_DIM // 2) * scale
                elif t == 1:
                    xv = _rope(xv, c, a, b, ROT_DIM // 2)
                outs[n][:, sl] = xv.astype(BF16)

    in_specs = [_row_spec(tm, HEAD_W, P_DIL0 // HEAD_W + n) for n in range(9)] + [_row_spec(tm, LANES)] * 3
    return pl.pallas_call(
        body, name=name, grid=(L // tm,), in_specs=in_specs,
        out_specs=[_row_spec(tm, HEAD_W)] * 9,
        out_shape=[jax.ShapeDtypeStruct((L, HEAD_W), BF16)] * 9)(*([p] * 9), tc, ta, tb)


def _dil_prep_bwd(grads, tabs_t, name):
    L = grads[0].shape[0]
    tm = min(512, L)
    tc, ta, tb = tabs_t
    scale = DIL_HD ** -0.5

    def body(*refs):
        ins, (c_ref, a_ref, b_ref), o_ref = refs[:9], refs[9:12], refs[12]
        c, a, b = c_ref[...], a_ref[...], b_ref[...]
        for n in range(9):
            t = n % 3
            for cb in range(HEAD_W // LANES):
                sl = slice(cb * LANES, (cb + 1) * LANES)
                xv = ins[n][:, sl]
                if t == 0:
                    xv = _rope(xv, c, a, b, ROT_DIM // 2) * scale
                elif t == 1:
                    xv = _rope(xv, c, a, b, ROT_DIM // 2)
                o_ref[:, n * HEAD_W + cb * LANES:n * HEAD_W + (cb + 1) * LANES] = xv.astype(BF16)

    return pl.pallas_call(
        body, name=name, grid=(L // tm,), in_specs=[_row_spec(tm, HEAD_W)] * 9 + [_row_spec(tm, LANES)] * 3,
        out_specs=_row_spec(tm, 9 * HEAD_W), out_shape=jax.ShapeDtypeStruct((L, 9 * HEAD_W), BF16),
        compiler_params=_cparams(VMEM_BIG_MB))(*grads, tc, ta, tb)


def _merge_gate(oa, p, o_g, lse_g, name):
    L = oa.shape[0]
    tm = min(512, L)

    def body(oa_ref, ga_ref, gb_ref, o1, o2, o3, l1, l2, l3, ab_ref, bm_ref, lt_ref):
        la, lb, lc = l1[...], l2[...], l3[...]
        m = jnp.maximum(jnp.maximum(la, lb), lc)
        ea, eb, ec = jnp.exp(la - m), jnp.exp(lb - m), jnp.exp(lc - m)
        den = ea + eb + ec
        bm = (ea * o1[...] + eb * o2[...] + ec * o3[...]) / den
        bm_ref[...] = bm
        lt_ref[...] = m + jnp.log(den)
        ga, gb = ga_ref[...], gb_ref[...]
        ab_ref[:, 0:HEAD_W] = (oa_ref[...] * (ga * _sigmoid(ga))).astype(BF16)
        ab_ref[:, HEAD_W:2 * HEAD_W] = (bm * (gb * _sigmoid(gb))).astype(BF16)

    w = _row_spec(tm, HEAD_W)
    return pl.pallas_call(
        body, name=name, grid=(L // tm,),
        in_specs=[w, _row_spec(tm, HEAD_W, 2), _row_spec(tm, HEAD_W, 3), w, w, w, w, w, w],
        out_specs=[_row_spec(tm, 2 * HEAD_W), w, w],
        out_shape=[jax.ShapeDtypeStruct((L, 2 * HEAD_W), BF16), jax.ShapeDtypeStruct((L, HEAD_W), F32),
                   jax.ShapeDtypeStruct((L, HEAD_W), F32)])(oa, p, p, *o_g, *lse_g)


def _gate_bwd(dab, p, oa, bm, name):
    L = oa.shape[0]
    tm = min(512, L)

    def body(da_ref, db_ref, ga_ref, gb_ref, oa_ref, bm_ref, doa_ref, dbm_ref, Da_ref, Db_ref, dg_ref):
        def one(d, g, o, do_ref, D_ref, col):
            sg = _sigmoid(g)
            do = d * (g * sg)
            do_ref[...] = do.astype(BF16)
            dg_ref[:, col:col + HEAD_W] = (d * o * (sg * (1.0 + g * (1.0 - sg)))).astype(BF16)
            prod = do * o
            for cb in range(HEAD_W // LANES):
                sl = slice(cb * LANES, (cb + 1) * LANES)
                D_ref[:, sl] = _headsum_bcast(prod[:, sl])

        one(da_ref[...], ga_ref[...], oa_ref[...], doa_ref, Da_ref, 0)
        one(db_ref[...], gb_ref[...], bm_ref[...], dbm_ref, Db_ref, HEAD_W)

    w = _row_spec(tm, HEAD_W)
    return pl.pallas_call(
        body, name=name, grid=(L // tm,),
        in_specs=[_row_spec(tm, HEAD_W, 0), _row_spec(tm, HEAD_W, 1), _row_spec(tm, HEAD_W, 2),
                  _row_spec(tm, HEAD_W, 3), w, w],
        out_specs=[w, w, w, w, _row_spec(tm, 2 * HEAD_W)],
        out_shape=[jax.ShapeDtypeStruct((L, HEAD_W), BF16), jax.ShapeDtypeStruct((L, HEAD_W), BF16),
                   jax.ShapeDtypeStruct((L, HEAD_W), F32), jax.ShapeDtypeStruct((L, HEAD_W), F32),
                   jax.ShapeDtypeStruct((L, 2 * HEAD_W), BF16)])(dab, dab, p, p, oa, bm)


NT = (((1,), (1,)), ((), ()))
TN = (((0,), (0,)), ((), ()))
NEG = -1e30


def _mla_fwd(q, k, v_t, name):
    L = q.shape[0]
    tq = tk = min(512, L)
    nq, nk = L // tq, L // tk
    npair = MLA_HEADS // 2

    def body(q_ref, k_ref, vt_ref, o_ref, lse_ref, m_sc, l_sc, acc_sc):
        j = pl.program_id(2)

        @pl.when(j == 0)
        def _():
            m_sc[...] = jnp.full(m_sc.shape, NEG, F32)
            l_sc[...] = jnp.zeros(l_sc.shape, F32)
            acc_sc[...] = jnp.zeros(acc_sc.shape, F32)

        for hh in range(2):
            sl = slice(hh * LANES, (hh + 1) * LANES)
            hrows = slice(hh * DIL_HD, (hh + 1) * DIL_HD)
            s_t = lax.dot_general(k_ref[:, sl], q_ref[:, sl], NT, preferred_element_type=F32)
            m_prev = m_sc[hh:hh + 1, :]
            m_new = jnp.maximum(m_prev, jnp.max(s_t, axis=0, keepdims=True))
            alpha = jnp.exp(m_prev - m_new)
            p_t = jnp.exp(s_t - m_new)
            l_sc[hh:hh + 1, :] = alpha * l_sc[hh:hh + 1, :] + jnp.sum(p_t, axis=0, keepdims=True)
            m_sc[hh:hh + 1, :] = m_new
            pv = jnp.dot(vt_ref[hrows, :], p_t.astype(BF16), preferred_element_type=F32)
            acc_sc[hrows, :] = alpha * acc_sc[hrows, :] + pv

        @pl.when(j == nk - 1)
        def _():
            top = lax.broadcasted_iota(jnp.int32, (LANES, tq), 0) < DIL_HD
            l = jnp.where(top, l_sc[0:1, :], l_sc[1:2, :])
            o_ref[...] = (acc_sc[...] / l).T
            lse_ref[...] = m_sc[...] + jnp.log(l_sc[...])

    return pl.pallas_call(
        body, name=name, grid=(npair, nq, nk),
        in_specs=[pl.BlockSpec((tq, 2 * LANES), lambda pr, i, j: (i, pr)),
                  pl.BlockSpec((tk, 2 * LANES), lambda pr, i, j: (j, pr)),
                  pl.BlockSpec((LANES, tk), lambda pr, i, j: (pr, j))],
        out_specs=[pl.BlockSpec((tq, LANES), lambda pr, i, j: (i, pr)),
                   pl.BlockSpec((None, 2, tq), lambda pr, i, j: (pr, 0, i))],
        out_shape=[jax.ShapeDtypeStruct((L, HEAD_W), F32), jax.ShapeDtypeStruct((npair, 2, L), F32)],
        scratch_shapes=[pltpu.VMEM((2, tq), F32), pltpu.VMEM((2, tq), F32), pltpu.VMEM((LANES, tq), F32)],
        compiler_params=_cparams(VMEM_BIG_MB))(q, k, v_t)


def _mla_bwd(q, k, v, do, lse_rows, d_rows, name):
    L = q.shape[0]
    tq = tk = min(512, L)
    nq, nk = L // tq, L // tk
    npair = MLA_HEADS // 2

    def body(q_ref, k_ref, v_ref, do_ref, lse_ref, d_ref, dq_ref, dk_ref, dv_ref):
        j, i = pl.program_id(1), pl.program_id(2)

        @pl.when((j == 0) & (i == 0))
        def _():
            dq_ref[...] = jnp.zeros(dq_ref.shape, F32)

        @pl.when(i == 0)
        def _():
            dk_ref[...] = jnp.zeros(dk_ref.shape, F32)
            dv_ref[...] = jnp.zeros(dv_ref.shape, F32)

        first = lax.broadcasted_iota(jnp.int32, (tq, LANES), 1) < DIL_HD
        dov = do_ref[...]
        vv = v_ref[...]
        rows = pl.ds(pl.multiple_of(i * tq, tq), tq)
        for hh in range(2):
            sl = slice(hh * LANES, (hh + 1) * LANES)
            qh, kh = q_ref[:, sl], k_ref[:, sl]
            do_h = jnp.where(first if hh == 0 else ~first, dov, jnp.zeros_like(dov))
            s_t = lax.dot_general(kh, qh, NT, preferred_element_type=F32)
            p_t = jnp.exp(s_t - lse_ref[hh:hh + 1, :])
            dv_ref[...] += jnp.dot(p_t.astype(BF16), do_h, preferred_element_type=F32)
            dp_t = lax.dot_general(vv, do_h, NT, preferred_element_type=F32)
            ds_t = (p_t * (dp_t - d_ref[hh:hh + 1, :])).astype(BF16)
            dk_ref[:, sl] += jnp.dot(ds_t, qh, preferred_element_type=F32)
            dq_ref[rows, sl] += lax.dot_general(ds_t, kh, TN, preferred_element_type=F32)

    return pl.pallas_call(
        body, name=name, grid=(npair, nk, nq),
        in_specs=[pl.BlockSpec((tq, 2 * LANES), lambda pr, j, i: (i, pr)),
                  pl.BlockSpec((tk, 2 * LANES), lambda pr, j, i: (j, pr)),
                  pl.BlockSpec((tk, LANES), lambda pr, j, i: (j, pr)),
                  pl.BlockSpec((tq, LANES), lambda pr, j, i: (i, pr)),
                  pl.BlockSpec((None, 2, tq), lambda pr, j, i: (pr, 0, i)),
                  pl.BlockSpec((None, 2, tq), lambda pr, j, i: (pr, 0, i))],
        out_specs=[pl.BlockSpec((L, 2 * LANES), lambda pr, j, i: (0, pr)),
                   pl.BlockSpec((tk, 2 * LANES), lambda pr, j, i: (j, pr)),
                   pl.BlockSpec((tk, LANES), lambda pr, j, i: (j, pr))],
        out_shape=[jax.ShapeDtypeStruct((L, 1024), F32), jax.ShapeDtypeStruct((L, 1024), F32),
                   jax.ShapeDtypeStruct((L, HEAD_W), F32)],
        compiler_params=_cparams(VMEM_BIG_MB))(q, k, v, do, lse_rows, d_rows)


def _dil_tiles(ld):
    tq = min(256, ld)
    kw = min(512, ld)
    return tq, kw


def _dil_fwd(q, k, v, name):
    d, ld, _ = q.shape
    tq, kw = _dil_tiles(ld)
    nq = ld // tq
    npair = HEAD_W // LANES

    def body(q_ref, k_ref, v_ref, o_ref, lse_ref):
        i = pl.program_id(2)
        a0 = i * tq
        start = pl.multiple_of(jnp.clip(a0 - LANES, 0, ld - kw), LANES)
        kwin = k_ref[pl.ds(start, kw), :]
        vwin = v_ref[pl.ds(start, kw), :]
        qpos = a0 + lax.broadcasted_iota(jnp.int32, (tq, kw), 0)
        kpos = start + lax.broadcasted_iota(jnp.int32, (tq, kw), 1)
        valid = jnp.abs(qpos - kpos) <= DIL_HALF
        first = lax.broadcasted_iota(jnp.int32, (tq, LANES), 1) < DIL_HD
        qv = q_ref[...]
        outs, lses = [], []
        for hh in range(2):
            qh = jnp.where(first if hh == 0 else ~first, qv, jnp.zeros_like(qv))
            s = lax.dot_general(qh, kwin, NT, preferred_element_type=F32)
            s = jnp.where(valid, s, NEG)
            m = jnp.max(s, axis=-1, keepdims=True)
            pr = jnp.exp(s - m)
            l = jnp.sum(pr, axis=-1, keepdims=True)
            outs.append(jnp.dot(pr.astype(BF16), vwin, preferred_element_type=F32) / l)
            lses.append(m + jnp.log(l))
        o_ref[...] = jnp.where(first, outs[0], outs[1])
        lse_ref[...] = jnp.where(first, lses[0], lses[1])

    blk = pl.BlockSpec((None, tq, LANES), lambda r, pr, i: (r, i, pr))
    full = pl.BlockSpec((None, ld, LANES), lambda r, pr, i: (r, 0, pr))
    return pl.pallas_call(
        body, name=name, grid=(d, npair, nq), in_specs=[blk, full, full], out_specs=[blk, blk],
        out_shape=[jax.ShapeDtypeStruct((d, ld, HEAD_W), F32), jax.ShapeDtypeStruct((d, ld, HEAD_W), F32)],
        compiler_params=_cparams(VMEM_BIG_MB))(q, k, v)


def _dil_bwd(q, k, v, do, lse_rows, d_rows, name):
    d, ld, _ = q.shape
    tq, kw = _dil_tiles(ld)
    nq = ld // tq
    npair = HEAD_W // LANES

    def body(q_ref, k_ref, v_ref, do_ref, lse_ref, d_ref, dq_ref, dk_ref, dv_ref):
        i = pl.program_id(2)

        @pl.when(i == 0)
        def _():
            dk_ref[...] = jnp.zeros(dk_ref.shape, F32)
            dv_ref[...] = jnp.zeros(dv_ref.shape, F32)

        a0 = i * tq
        start = pl.multiple_of(jnp.clip(a0 - LANES, 0, ld - kw), LANES)
        win = pl.ds(start, kw)
        kwin = k_ref[win, :]
        vwin = v_ref[win, :]
        kpos = start + lax.broadcasted_iota(jnp.int32, (kw, tq), 0)
        qpos = a0 + lax.broadcasted_iota(jnp.int32, (kw, tq), 1)
        valid = jnp.abs(qpos - kpos) <= DIL_HALF
        first = lax.broadcasted_iota(jnp.int32, (tq, LANES), 1) < DIL_HD
        qv = q_ref[...]
        dov = do_ref[...]
        dqs = []
        for hh in range(2):
            sel = first if hh == 0 else ~first
            qh = jnp.where(sel, qv, jnp.zeros_like(qv))
            do_h = jnp.where(sel, dov, jnp.zeros_like(dov))
            s_t = lax.dot_general(kwin, qh, NT, preferred_element_type=F32)
            p_t = jnp.where(valid, jnp.exp(jnp.where(valid, s_t, NEG) - lse_ref[hh:hh + 1, :]), 0.0)
            dv_ref[win, :] += jnp.dot(p_t.astype(BF16), do_h, preferred_element_type=F32)
            dp_t = lax.dot_general(vwin, do_h, NT, preferred_element_type=F32)
            ds_t = (p_t * (dp_t - d_ref[hh:hh + 1, :])).astype(BF16)
            dk_ref[win, :] += jnp.dot(ds_t, qh, preferred_element_type=F32)
            dqs.append(lax.dot_general(ds_t, kwin, TN, preferred_element_type=F32))
        dq_ref[...] = jnp.where(first, dqs[0], dqs[1])

    blk = pl.BlockSpec((None, tq, LANES), lambda r, pr, i: (r, i, pr))
    full = pl.BlockSpec((None, ld, LANES), lambda r, pr, i: (r, 0, pr))
    rowspec = pl.BlockSpec((None, None, 2, tq), lambda r, pr, i: (r, pr, 0, i))
    return pl.pallas_call(
        body, name=name, grid=(d, npair, nq), in_specs=[blk, full, full, blk, rowspec, rowspec],
        out_specs=[blk, full, full],
        out_shape=[jax.ShapeDtypeStruct((d, ld, HEAD_W), F32)] * 3,
        compiler_params=_cparams(VMEM_BIG_MB))(q, k, v, do, lse_rows, d_rows)


TILE_BYTES = 1 << 21


def _row_tile(rows, cols, budget=TILE_BYTES):
    for parts in range(1, rows + 1):
        tr = rows // parts
        if rows % parts == 0 and tr % 8 == 0 and tr * cols * 4 <= budget:
            return tr
    return rows


def _add2(a, b, name):
    n, rows, cols = a.shape
    tr = _row_tile(rows, cols)

    def body(a_ref, b_ref, o_ref):
        o_ref[...] = a_ref[...] + b_ref[...]

    spec = pl.BlockSpec((None, tr, cols), lambda t, i: (t, i, 0))
    return pl.pallas_call(body, name=name, grid=(n, rows // tr), in_specs=[spec, spec], out_specs=spec,
                          out_shape=jax.ShapeDtypeStruct(a.shape, F32))(a, b)


def _add4_ordered(a, name):
    _, rows, cols = a.shape
    tr = _row_tile(rows, cols, TILE_BYTES // 4)

    def body(a_ref, o_ref):
        o_ref[...] = ((a_ref[0] + a_ref[1]) + a_ref[2]) + a_ref[3]

    return pl.pallas_call(
        body, name=name, grid=(rows // tr,), in_specs=[pl.BlockSpec((4, tr, cols), lambda i: (0, i, 0))],
        out_specs=pl.BlockSpec((tr, cols), lambda i: (i, 0)),
        out_shape=jax.ShapeDtypeStruct((rows, cols), F32))(a)


def _adamw(w, g, m, v, name):
    rows, cols = w.shape
    tr = _row_tile(rows, cols)
    bc1 = 1.0 - ADAM_B1 ** ADAM_STEP
    bc2 = 1.0 - ADAM_B2 ** ADAM_STEP

    def body(w_ref, g_ref, m_ref, v_ref, d_ref, nm_ref, nv_ref):
        gv = g_ref[...]
        nm = ADAM_B1 * m_ref[...] + (1.0 - ADAM_B1) * gv
        nv = ADAM_B2 * v_ref[...] + (1.0 - ADAM_B2) * (gv * gv)
        d_ref[...] = -ADAM_LR * ((nm / bc1) / (jnp.sqrt(nv / bc2) + ADAM_EPS) + ADAM_WD * w_ref[...])
        nm_ref[...] = nm
        nv_ref[...] = nv

    spec = pl.BlockSpec((tr, cols), lambda i: (i, 0))
    return pl.pallas_call(body, name=name, grid=(rows // tr,), in_specs=[spec] * 4, out_specs=[spec] * 3,
                          out_shape=[jax.ShapeDtypeStruct(w.shape, F32)] * 3,
                          compiler_params=_cparams(VMEM_BIG_MB))(w, g, m, v)


ANY = pl.BlockSpec(memory_space=pl.ANY)


def _place():
    return lax.axis_index("x"), lax.axis_index("y"), lax.axis_index("c")


def _rcopy(send_sems, recv_sems, n, src, dst, to):
    return pltpu.make_async_remote_copy(src_ref=src, dst_ref=dst, send_sem=send_sems.at[n], recv_sem=recv_sems.at[n],
                                        device_id=to, device_id_type=MESH)


def _allgather_weights(shards):
    na = len(shards)

    def body(*refs):
        w_refs, g_refs = refs[:na], refs[na:2 * na]
        send_sems, recv_sems, local_sems = refs[2 * na:]
        x, y, c = _place()
        s = 2 * x + y
        chips = [(1 - x, y), (x, 1 - y), (1 - x, 1 - y)]

        def half(a, shard, h):
            hr = shards[a].shape[0] // 2
            return g_refs[a].at[shard, pl.ds(h * hr, hr), :]

        started = []
        for a in range(na):
            hr = shards[a].shape[0] // 2
            mine = pltpu.make_async_copy(w_refs[a], g_refs[a].at[s], local_sems.at[a])
            mine.start()
            started.append(mine)
        sends = []
        for a in range(na):
            hr = shards[a].shape[0] // 2
            for n, (cx, cy) in enumerate(chips):
                cp = _rcopy(send_sems, recv_sems, 6 * a + n, w_refs[a].at[pl.ds(c * hr, hr), :], half(a, s, c),
                            (cx, cy, c))
                cp.start()
                sends.append(cp)
        for a in range(na):
            for n, (cx, cy) in enumerate(chips):
                sj = 2 * cx + cy
                _rcopy(send_sems, recv_sems, 6 * a + n, half(a, sj, c), half(a, sj, c), (cx, cy, c)).wait_recv()
                fw = _rcopy(send_sems, recv_sems, 6 * a + 3 + n, half(a, sj, c), half(a, sj, c), (x, y, 1 - c))
                fw.start()
                sends.append(fw)
        for a in range(na):
            for n, (cx, cy) in enumerate(chips):
                sj = 2 * cx + cy
                _rcopy(send_sems, recv_sems, 6 * a + 3 + n, half(a, sj, 1 - c), half(a, sj, 1 - c),
                       (x, y, 1 - c)).wait_recv()
        for cp in sends:
            cp.wait_send()
        for mine in started:
            mine.wait()

    return pl.pallas_call(
        body, name="allgather_weights", in_specs=[ANY] * na, out_specs=[ANY] * na,
        out_shape=[jax.ShapeDtypeStruct((N_SHARD,) + t.shape, t.dtype) for t in shards],
        scratch_shapes=[pltpu.SemaphoreType.DMA((6 * na,)), pltpu.SemaphoreType.DMA((6 * na,)),
                        pltpu.SemaphoreType.DMA((na,))])(*shards)


def _sibling_send_halves(gs):
    na = len(gs)

    def body(*refs):
        g_refs, o_refs = refs[:na], refs[na:2 * na]
        send_sems, recv_sems = refs[2 * na:]
        x, y, c = _place()
        cps = []
        for a in range(na):
            for t in range(N_SHARD):
                cp = _rcopy(send_sems, recv_sems, N_SHARD * a + t, g_refs[a].at[t, 1 - c], o_refs[a].at[t],
                            (x, y, 1 - c))
                cp.start()
                cps.append(cp)
        for cp in cps:
            cp.wait()

    return pl.pallas_call(
        body, name="grad_sibling_exchange", in_specs=[ANY] * na, out_specs=[ANY] * na,
        out_shape=[jax.ShapeDtypeStruct((N_SHARD,) + g.shape[2:], g.dtype) for g in gs],
        scratch_shapes=[pltpu.SemaphoreType.DMA((N_SHARD * na,)), pltpu.SemaphoreType.DMA((N_SHARD * na,))])(*gs)


def _chip_scatter(parts):
    na = len(parts)

    def body(*refs):
        a_refs, o_refs = refs[:na], refs[na:2 * na]
        send_sems, recv_sems, local_sems = refs[2 * na:]
        x, y, c = _place()
        s = 2 * x + y
        chips = [(1 - x, y), (x, 1 - y), (1 - x, 1 - y)]
        started, cps = [], []
        for a in range(na):
            mine = pltpu.make_async_copy(a_refs[a].at[s], o_refs[a].at[s], local_sems.at[a])
            mine.start()
            started.append(mine)
            for n, (cx, cy) in enumerate(chips):
                cp = _rcopy(send_sems, recv_sems, 3 * a + n, a_refs[a].at[2 * cx + cy], o_refs[a].at[s], (cx, cy, c))
                cp.start()
                cps.append(cp)
        for a in range(na):
            for n, (cx, cy) in enumerate(chips):
                sj = 2 * cx + cy
                _rcopy(send_sems, recv_sems, 3 * a + n, a_refs[a].at[sj], o_refs[a].at[sj], (cx, cy, c)).wait_recv()
        for cp in cps:
            cp.wait_send()
        for mine in started:
            mine.wait()

    return pl.pallas_call(
        body, name="grad_chip_scatter", in_specs=[ANY] * na, out_specs=[ANY] * na,
        out_shape=[jax.ShapeDtypeStruct(t.shape, t.dtype) for t in parts],
        scratch_shapes=[pltpu.SemaphoreType.DMA((3 * na,)), pltpu.SemaphoreType.DMA((3 * na,)),
                        pltpu.SemaphoreType.DMA((na,))])(*parts)


def _sibling_swap(rs):
    na = len(rs)

    def body(*refs):
        r_refs, o_refs = refs[:na], refs[na:2 * na]
        send_sems, recv_sems = refs[2 * na:]
        x, y, c = _place()
        cps = []
        for a in range(na):
            cp = _rcopy(send_sems, recv_sems, a, r_refs[a], o_refs[a], (x, y, 1 - c))
            cp.start()
            cps.append(cp)
        for cp in cps:
            cp.wait()

    return pl.pallas_call(
        body, name="grad_sibling_swap", in_specs=[ANY] * na, out_specs=[ANY] * na,
        out_shape=[jax.ShapeDtypeStruct(t.shape, t.dtype) for t in rs],
        scratch_shapes=[pltpu.SemaphoreType.DMA((na,)), pltpu.SemaphoreType.DMA((na,))])(*rs)


def _pack_small(norm_g, q_norm_g, kv_norm_g, final_g):
    flat = jnp.concatenate([norm_g.reshape(-1), q_norm_g.reshape(-1), kv_norm_g.reshape(-1), final_g.reshape(-1),
                            jnp.zeros((SMALL_ROWS * LANES - N_SMALL,), F32)])
    return flat.reshape(SMALL_ROWS, LANES)


def _split_small(s):
    s = s.reshape(-1)
    o = 0
    out = []
    for n, shape in ((DEPTH * D_MODEL, (DEPTH, D_MODEL)), (DEPTH * Q_LORA, (DEPTH, Q_LORA)),
                     (DEPTH * KV_LORA, (DEPTH, KV_LORA)), (D_MODEL, (D_MODEL,))):
        out.append(s[o:o + n].reshape(shape))
        o += n
    return out


def _assemble_w_in(sh):
    z = lambda n: jnp.zeros(sh.shape[1:3] + (n,), sh.dtype)
    s0, s1, s2, s3 = sh[0], sh[1], sh[2], sh[3]
    return jnp.concatenate([s0[..., 0:640], z(64), s0[..., 640:672], z(32), z(256), s0[..., 672:1184],
                            s3[..., 1064:1576], s0[..., 1184:1576], s1, s2, s3[..., 0:1064]], axis=-1)


def _split_w_in_grad(d):
    sh0 = jnp.concatenate([d[..., 0:640], d[..., 704:736], d[..., 1024:1536], d[..., 2048:2440]], axis=-1)
    sh3 = jnp.concatenate([d[..., 5592:6656], d[..., 1536:2048]], axis=-1)
    return jnp.stack([sh0, d[..., 2440:4016], d[..., 4016:5592], sh3]).reshape(N_SHARD, DEPTH * D_MODEL, SHARD_COLS_IN)


def _col_shards(w):
    dl, r, cc = w.shape
    return w.reshape(dl, r, N_SHARD, cc // N_SHARD).transpose(2, 0, 1, 3).reshape(N_SHARD, dl * r, cc // N_SHARD)


def _from_col_shards(g, rows):
    cc = g.shape[-1]
    return g.reshape(N_SHARD, DEPTH, rows, cc).transpose(1, 2, 0, 3).reshape(DEPTH, rows, N_SHARD * cc)


def _pad_w_in(w):
    z = lambda n: jnp.zeros(w.shape[:-1] + (n,), w.dtype)
    return jnp.concatenate([w[..., 0:640], z(64), w[..., 640:672], z(32), z(256), w[..., 672:1184],
                            w[..., 5792:6304], w[..., 1184:5792]], axis=-1)


def _unpad_w_in(w):
    return jnp.concatenate([w[..., 0:640], w[..., 704:736], w[..., 1024:1536], w[..., 2048:6656],
                            w[..., 1536:2048]], axis=-1)


def _pad_w_uq(w):
    s = w.shape[:-1]
    w = w.reshape(s + (MLA_HEADS, 96))
    return jnp.pad(w, [(0, 0)] * (w.ndim - 1) + [(0, 32)]).reshape(s + (1024,))


def _unpad_w_uq(w):
    s = w.shape[:-1]
    return w.reshape(s + (MLA_HEADS, LANES))[..., :96].reshape(s + (768,))


def _pad_w_ukv(w):
    s = w.shape[:-1]
    w = w.reshape(s + (MLA_HEADS, 128))
    kpart = jnp.pad(w[..., :64], [(0, 0)] * (w.ndim - 1) + [(0, 64)]).reshape(s + (1024,))
    vpart = w[..., 64:].reshape(s + (512,))
    return jnp.concatenate([kpart, vpart], axis=-1)


def _unpad_w_ukv(w):
    s = w.shape[:-1]
    kpart = w[..., :1024].reshape(s + (MLA_HEADS, LANES))[..., :64]
    vpart = w[..., 1024:].reshape(s + (MLA_HEADS, 64))
    return jnp.concatenate([kpart, vpart], axis=-1).reshape(s + (1024,))


def _rope_tables(L, dim, lane_lo, period):
    half = dim // 2
    inv = 1.0 / (ROPE_THETA ** (jnp.arange(0, dim, 2, dtype=F32) / dim))
    ang = jnp.arange(L, dtype=F32)[:, None] * inv[None, :]
    cos, sin = jnp.cos(ang), jnp.sin(ang)
    one = lambda n: jnp.ones((L, n), F32)
    zero = lambda n: jnp.zeros((L, n), F32)
    rest = period - lane_lo - dim
    rep = LANES // period
    c = jnp.tile(jnp.concatenate([one(lane_lo), cos, cos, one(rest)], axis=1), (1, rep))
    a = jnp.tile(jnp.concatenate([zero(lane_lo), -sin, zero(half), zero(rest)], axis=1), (1, rep))
    b = jnp.tile(jnp.concatenate([zero(lane_lo + half), sin, zero(rest)], axis=1), (1, rep))
    return c, a, b


def _to_strided(t, d):
    L, w = t.shape
    return t.reshape(L // d, d, w).transpose(1, 0, 2)


def _from_strided(t):
    d, ld, w = t.shape
    return t.transpose(1, 0, 2).reshape(d * ld, w)


def _head_rows(t):
    return t[:, ::DIL_HD].T.reshape(MLA_HEADS // 2, 2, t.shape[0])


def _head_rows_strided(t, d):
    s = _to_strided(t[:, ::DIL_HD], d)
    return s.transpose(0, 2, 1).reshape(d, MLA_HEADS // 2, 2, s.shape[1])


def _local_grads(x, target, norm_g, w_in_p, q_norm_g, kv_norm_g, w_uq_p, w_ukv_p, w_out, final_g):
    L = x.shape[0]
    tabs_m = _rope_tables(L, MLA_ROPE, MLA_NOPE, LANES)
    tabs_d = _rope_tables(L, ROT_DIM, 0, DIL_HD)
    tabs_m_t = (tabs_m[0], -tabs_m[1], -tabs_m[2])
    tabs_d_t = (tabs_d[0], -tabs_d[1], -tabs_d[2])
    w_in_t = jnp.swapaxes(w_in_p, 1, 2)
    w_uq_t = jnp.swapaxes(w_uq_p, 1, 2)
    w_ukv_t = jnp.swapaxes(w_ukv_p, 1, 2)
    w_out_t = jnp.swapaxes(w_out, 1, 2)

    saved = []
    for l in range(DEPTH):
        h = _rms_fwd(x, norm_g[l:l + 1], "rms_fwd")
        p = _mm(h, w_in_p[l], tm=512, tn=3328, tk=1024, out_dtype=F32, name="in_proj")
        q, k, v, cqn, ckvn = _mla_prep(p, q_norm_g[l:l + 1], kv_norm_g[l:l + 1], w_uq_p[l], w_ukv_p[l], tabs_m,
                                       "mla_prep")
        oa, lse_a = _mla_fwd(q, k, v.T, "mla_fwd")
        dil = _dil_prep(p, tabs_d, "dil_prep")
        dil_s, o_g, lse_g = [], [], []
        for g, (_, dd) in enumerate(DIL_PAIRS):
            qs, ks, vs = (_to_strided(t, dd) for t in dil[3 * g:3 * g + 3])
            og, lg = _dil_fwd(qs, ks, vs, "dil_fwd_%d" % dd)
            dil_s.append((qs, ks, vs))
            o_g.append(_from_strided(og))
            lse_g.append(_from_strided(lg))
        ab, bm, lt = _merge_gate(oa, p, o_g, lse_g, "merge_gate")
        x_next = _mm(ab, w_out[l], tm=1024, tn=1024, tk=1024, out_dtype=F32, name="out_proj", add=x)
        saved.append((x, h, p, q, k, v, cqn, ckvn, oa, lse_a, dil_s, bm, lt, ab))
        x = x_next

    loss_b, dx, d_final = _loss_head(x, final_g[None, :], target, "loss_head")
    loss = loss_b[0, 0]

    d_norm, d_qn, d_kvn, d_win, d_wuq, d_wukv, d_wout = [], [], [], [], [], [], []
    for l in reversed(range(DEPTH)):
        x_l, h, p, q, k, v, cqn, ckvn, oa, lse_a, dil_s, bm, lt, ab = saved[l]
        dab = _mm(dx, w_out_t[l], tm=1024, tn=1024, tk=1024, out_dtype=F32, name="out_proj_dgrad")
        d_wout.append(_mm(ab.T, dx, tm=1024, tn=1024, tk=1024, out_dtype=F32, name="out_proj_wgrad"))
        doa, dbm, D_a, D_b, dgates = _gate_bwd(dab, p, oa, bm, "gate_bwd")
        dq, dk, dv = _mla_bwd(q, k, v, doa, lse_a, _head_rows(D_a), "mla_bwd")
        dp_mla, dq_pre, dkv, dqg, dkvg = _mla_prep_bwd(dq, dk, dv, p, q_norm_g[l:l + 1], kv_norm_g[l:l + 1],
                                                       w_uq_t[l], w_ukv_t[l], tabs_m_t, "mla_prep_bwd")
        d_wuq.append(_mm(cqn.T, dq_pre, tm=Q_LORA, tn=1024, tk=2048, out_dtype=F32, name="w_uq_wgrad"))
        d_wukv.append(_mm(ckvn.T, dkv, tm=KV_LORA, tn=1536, tk=2048, out_dtype=F32, name="w_ukv_wgrad"))
        dgr = []
        for g, (_, dd) in enumerate(DIL_PAIRS):
            qs, ks, vs = dil_s[g]
            dqs, dks, dvs = _dil_bwd(qs, ks, vs, _to_strided(dbm, dd), _head_rows_strided(lt, dd),
                                     _head_rows_strided(D_b, dd), "dil_bwd_%d" % dd)
            dgr += [_from_strided(dqs), _from_strided(dks), _from_strided(dvs)]
        dp_dil = _dil_prep_bwd(dgr, tabs_d_t, "dil_prep_bwd")
        dp = jnp.concatenate([dp_mla, dgates, dp_dil], axis=1)
        dh = _mm(dp, w_in_t[l], tm=1024, tn=1024, tk=1664, out_dtype=F32, name="in_proj_dgrad")
        d_win.append(_mm(h.T, dp, tm=1024, tn=1664, tk=1024, out_dtype=F32, name="in_proj_wgrad"))
        dx, dng = _rms_bwd(dh, x_l, norm_g[l:l + 1], dx, "rms_bwd")
        d_norm.append(dng[0])
        d_qn.append(dqg[0])
        d_kvn.append(dkvg[0])

    rev = lambda xs: jnp.stack(xs[::-1])
    return (loss, dx, rev(d_norm), rev(d_win), rev(d_qn), rev(d_kvn), rev(d_wuq), rev(d_wukv), rev(d_wout),
            d_final[0])


def kernel(x, norm_g, w_in, q_norm_g, kv_norm_g, w_uq, w_ukv, w_out, final_g, loss_target, m_norm_g, m_w_in, m_q_norm_g, m_kv_norm_g, m_w_uq, m_w_ukv, m_w_out, m_final_g, v_norm_g, v_w_in, v_q_norm_g, v_kv_norm_g, v_w_uq, v_w_ukv, v_w_out, v_final_g):
    c = lax.axis_index("c")

    def families(a_in, a_uq, a_ukv, a_out):
        return [t.reshape(shape) for t, shape in zip((a_in, a_uq, a_ukv, a_out), FAM_SHAPES)]

    g_in, g_uq, g_ukv, g_out = _allgather_weights([t.astype(BF16) for t in families(w_in, w_uq, w_ukv, w_out)])
    w_in_p = _assemble_w_in(g_in.reshape(N_SHARD, DEPTH, D_MODEL, SHARD_COLS_IN))
    w_uq_p = _pad_w_uq(_from_col_shards(g_uq, Q_LORA))
    w_ukv_p = _pad_w_ukv(_from_col_shards(g_ukv, KV_LORA))
    w_out_f = g_out.reshape(N_SHARD, DEPTH, 1024 // N_SHARD, D_MODEL).transpose(1, 0, 2, 3).reshape(DEPTH, 1024, D_MODEL)

    (loss, dx, d_norm, d_win_p, d_qn, d_kvn, d_wuq_p, d_wukv_p, d_wout, d_final) = _local_grads(
        x[0], loss_target[0], norm_g, w_in_p, q_norm_g, kv_norm_g, w_uq_p, w_ukv_p, w_out_f, final_g)
    loss = lax.psum(loss, ("x", "y", "c"))

    small = _pack_small(d_norm, d_qn, d_kvn, d_final)
    grads = [_split_w_in_grad(d_win_p), _col_shards(_unpad_w_uq(d_wuq_p)), _col_shards(_unpad_w_ukv(d_wukv_p)),
             d_wout.reshape(DEPTH, N_SHARD, 1024 // N_SHARD, D_MODEL).transpose(1, 0, 2, 3).reshape(
                 N_SHARD, DEPTH * (1024 // N_SHARD), D_MODEL),
             jnp.broadcast_to(small[None], (N_SHARD, SMALL_ROWS, LANES))]
    halves = [g.reshape(N_SHARD, 2, g.shape[1] // 2, g.shape[2]) for g in grads]
    from_sib = _sibling_send_halves(halves)
    chip_sum = [_add2(lax.dynamic_index_in_dim(h, c, axis=1, keepdims=False), f, "grad_add_pair")
                for h, f in zip(halves, from_sib)]
    red_half = [_add4_ordered(t, "grad_add_chips") for t in _chip_scatter(chip_sum)]
    other_half = _sibling_swap(red_half)
    gred = []
    for mine, other in zip(red_half, other_half):
        both = jnp.stack([mine, other])
        gred.append(jnp.concatenate([lax.dynamic_index_in_dim(both, c, axis=0, keepdims=False),
                                     lax.dynamic_index_in_dim(both, 1 - c, axis=0, keepdims=False)], axis=0))

    wf = families(w_in, w_uq, w_ukv, w_out) + [_pack_small(norm_g, q_norm_g, kv_norm_g, final_g)]
    mf = families(m_w_in, m_w_uq, m_w_ukv, m_w_out) + [_pack_small(m_norm_g, m_q_norm_g, m_kv_norm_g, m_final_g)]
    vf = families(v_w_in, v_w_uq, v_w_ukv, v_w_out) + [_pack_small(v_norm_g, v_q_norm_g, v_kv_norm_g, v_final_g)]
    upd = [_adamw(w, g, m, v, "adamw") for w, g, m, v in zip(wf, gred, mf, vf)]

    def leaves(fams):
        a_in, a_uq, a_ukv, a_out, s = fams
        s_norm, s_qn, s_kvn, s_final = _split_small(s)
        return [s_norm, a_in.reshape(w_in.shape), s_qn, s_kvn, a_uq.reshape(w_uq.shape), a_ukv.reshape(w_ukv.shape),
                a_out.reshape(w_out.shape), s_final]

    return (loss, dx[None], *leaves(gred), *leaves([u[0] for u in upd]), *leaves([u[1] for u in upd]),
            *leaves([u[2] for u in upd]))
```

```python
import functools

import jax
import jax.numpy as jnp
from jax import lax
from jax.experimental import pallas as pl
from jax.experimental.pallas import tpu as pltpu

F32 = jnp.float32
BF16 = jnp.bfloat16
MESH = pl.DeviceIdType.MESH

D_MODEL = 1024
DEPTH = 4
MLA_HEADS = 8
MLA_NOPE = 64
MLA_ROPE = 32
Q_LORA = 384
KV_LORA = 256
DIL_PAIRS = ((128, 1), (512, 4), (2048, 16))
DIL_HD = 64
DIL_HALF = 64
ROT_DIM = 16
ROPE_THETA = 500000.0
EPS = 1e-6
IN_WIDTH = 6304
N_SHARD = 4

P_WIDTH = 6656
P_MLA = 1024
P_GATE = 1024
P_DIL0 = 2048
LANES = 128
HEAD_W = 512

ADAM_LR = 0.001
ADAM_B1 = 0.9
ADAM_B2 = 0.999
ADAM_EPS = 1e-08
ADAM_WD = 0.01
ADAM_STEP = 10

SHARD_COLS_IN = IN_WIDTH // N_SHARD
FAM_SHAPES = ((DEPTH * D_MODEL, SHARD_COLS_IN), (DEPTH * Q_LORA, 768 // N_SHARD), (DEPTH * KV_LORA, 1024 // N_SHARD),
              (DEPTH * (1024 // N_SHARD), D_MODEL))
N_SMALL = DEPTH * (D_MODEL + Q_LORA + KV_LORA) + D_MODEL
SMALL_ROWS = 64
VMEM_BIG_MB = 48


def _cparams(vmem_mb=None):
    if vmem_mb is None:
        return None
    return pltpu.CompilerParams(vmem_limit_bytes=vmem_mb << 20)


def _sigmoid(x):
    return 1.0 / (1.0 + jnp.exp(-x))


def _rope(x, c, a, b, sh):
    return x * c + pltpu.roll(x, LANES - sh, 1) * a + pltpu.roll(x, sh, 1) * b


def _headsum_bcast(x):
    r = lax.broadcasted_iota(jnp.int32, (LANES, LANES), 0) // DIL_HD
    c = lax.broadcasted_iota(jnp.int32, (LANES, LANES), 1) // DIL_HD
    ones = jnp.where(r == c, 1.0, 0.0).astype(BF16)
    hi = x.astype(BF16)
    lo = (x - hi.astype(F32)).astype(BF16)
    return (jnp.dot(hi, ones, preferred_element_type=F32) + jnp.dot(lo, ones, preferred_element_type=F32))


def _mm(a, b, *, tm, tn, tk, out_dtype, name, add=None):
    M, K = a.shape
    N = b.shape[1]
    tm, tn, tk = min(tm, M), min(tn, N), min(tk, K)
    assert M % tm == 0 and N % tn == 0 and K % tk == 0, (a.shape, b.shape)
    nk = K // tk
    has_add = add is not None

    def body(*refs):
        if has_add:
            a_ref, b_ref, add_ref, o_ref, acc = refs
        else:
            a_ref, b_ref, o_ref, acc = refs
        k = pl.program_id(2)
        part = jnp.dot(a_ref[...].astype(BF16), b_ref[...].astype(BF16), preferred_element_type=F32)

        @pl.when(k == 0)
        def _():
            acc[...] = part

        @pl.when(k > 0)
        def _():
            acc[...] += part

        @pl.when(k == nk - 1)
        def _():
            r = acc[...]
            if has_add:
                r = r + add_ref[...]
            o_ref[...] = r.astype(out_dtype)

    in_specs = [pl.BlockSpec((tm, tk), lambda i, j, k: (i, k)), pl.BlockSpec((tk, tn), lambda i, j, k: (k, j))]
    args = [a, b]
    if has_add:
        in_specs.append(pl.BlockSpec((tm, tn), lambda i, j, k: (i, j)))
        args.append(add)
    return pl.pallas_call(
        body, name=name, grid=(M // tm, N // tn, nk), in_specs=in_specs,
        out_specs=pl.BlockSpec((tm, tn), lambda i, j, k: (i, j)),
        out_shape=jax.ShapeDtypeStruct((M, N), out_dtype),
        scratch_shapes=[pltpu.VMEM((tm, tn), F32)], compiler_params=_cparams(VMEM_BIG_MB))(*args)


def _row_spec(tm, w, cb=0):
    return pl.BlockSpec((tm, w), lambda i: (i, cb))


def _const_spec(arr):
    nd = arr.ndim
    return pl.BlockSpec(arr.shape, lambda i: (0,) * nd)


def _rms_fwd(x, g, name):
    L, D = x.shape
    tm = min(512, L)

    def body(x_ref, g_ref, o_ref):
        xv = x_ref[...]
        r = lax.rsqrt(jnp.mean(xv * xv, axis=-1, keepdims=True) + EPS)
        o_ref[...] = (xv * r * g_ref[...]).astype(BF16)

    return pl.pallas_call(
        body, name=name, grid=(L // tm,), in_specs=[_row_spec(tm, D), _const_spec(g)],
        out_specs=_row_spec(tm, D), out_shape=jax.ShapeDtypeStruct((L, D), BF16))(x, g)


def _rms_bwd(dh, x, g, dres, name):
    L, D = x.shape
    tm = min(512, L)

    def body(dh_ref, x_ref, g_ref, dres_ref, dx_ref, dg_ref):
        xv = x_ref[...]
        dy = dh_ref[...]
        r = lax.rsqrt(jnp.mean(xv * xv, axis=-1, keepdims=True) + EPS)
        dyg = dy * g_ref[...]
        dx_ref[...] = dres_ref[...] + r * dyg - xv * (r * r * r) * jnp.mean(dyg * xv, axis=-1, keepdims=True)
        part = jnp.sum(dy * xv * r, axis=0, keepdims=True)

        @pl.when(pl.program_id(0) == 0)
        def _():
            dg_ref[...] = part

        @pl.when(pl.program_id(0) > 0)
        def _():
            dg_ref[...] += part

    return pl.pallas_call(
        body, name=name, grid=(L // tm,),
        in_specs=[_row_spec(tm, D), _row_spec(tm, D), _const_spec(g), _row_spec(tm, D)],
        out_specs=[_row_spec(tm, D), pl.BlockSpec((1, D), lambda i: (0, 0))],
        out_shape=[jax.ShapeDtypeStruct((L, D), F32), jax.ShapeDtypeStruct((1, D), F32)])(dh, x, g, dres)


def _loss_head(x, g, target, name):
    L, D = x.shape
    tm = min(512, L)

    def body(x_ref, g_ref, t_ref, loss_ref, dx_ref, dg_ref):
        xv = x_ref[...]
        gv = g_ref[...]
        r = lax.rsqrt(jnp.mean(xv * xv, axis=-1, keepdims=True) + EPS)
        xr = xv * r
        err = xr * gv - t_ref[...]
        lp = 0.5 * jnp.sum(jnp.mean(err * err, axis=-1, keepdims=True))
        dy = err * (1.0 / D)
        dyg = dy * gv
        dx_ref[...] = r * dyg - xv * (r * r * r) * jnp.mean(dyg * xv, axis=-1, keepdims=True)
        part = jnp.sum(dy * xr, axis=0, keepdims=True)

        @pl.when(pl.program_id(0) == 0)
        def _():
            dg_ref[...] = part
            loss_ref[...] = jnp.zeros(loss_ref.shape, F32) + lp

        @pl.when(pl.program_id(0) > 0)
        def _():
            dg_ref[...] += part
            loss_ref[...] += lp

    return pl.pallas_call(
        body, name=name, grid=(L // tm,),
        in_specs=[_row_spec(tm, D), _const_spec(g), _row_spec(tm, D)],
        out_specs=[pl.BlockSpec((8, LANES), lambda i: (0, 0)), _row_spec(tm, D), pl.BlockSpec((1, D), lambda i: (0, 0))],
        out_shape=[jax.ShapeDtypeStruct((8, LANES), F32), jax.ShapeDtypeStruct((L, D), F32),
                   jax.ShapeDtypeStruct((1, D), F32)])(x, g, target)


def _mla_prep(p, qg, kvg, wuq, wukv, tabs, name):
    L = p.shape[0]
    tm = min(512, L)
    scale = (MLA_NOPE + MLA_ROPE) ** -0.5
    tc, ta, tb = tabs

    def body(p_ref, qg_ref, kvg_ref, wuq_ref, wukv_ref, c_ref, a_ref, b_ref, q_ref, k_ref, v_ref, cqn_ref, ckvn_ref):
        c, a, b = c_ref[...], a_ref[...], b_ref[...]
        cq = p_ref[:, 0:Q_LORA]
        ckv = p_ref[:, Q_LORA:Q_LORA + KV_LORA]
        kr = p_ref[:, 640:768]
        cqn = (cq * lax.rsqrt(jnp.mean(cq * cq, axis=-1, keepdims=True) + EPS) * qg_ref[...]).astype(BF16)
        ckvn = (ckv * lax.rsqrt(jnp.mean(ckv * ckv, axis=-1, keepdims=True) + EPS) * kvg_ref[...]).astype(BF16)
        cqn_ref[...] = cqn
        ckvn_ref[...] = ckvn
        q = jnp.dot(cqn, wuq_ref[...], preferred_element_type=F32)
        kv = jnp.dot(ckvn, wukv_ref[...], preferred_element_type=F32)
        krr = _rope(kr, c, a, b, MLA_ROPE // 2)
        for h in range(MLA_HEADS):
            sl = slice(h * LANES, (h + 1) * LANES)
            q_ref[:, sl] = (_rope(q[:, sl], c, a, b, MLA_ROPE // 2) * (scale * LOG2E)).astype(BF16)
            k_ref[:, sl] = (kv[:, sl] + krr).astype(BF16)
        v_ref[...] = kv[:, 1024:1536].astype(BF16)

    return pl.pallas_call(
        body, name=name, grid=(L // tm,),
        in_specs=[_row_spec(tm, P_MLA, 0), _const_spec(qg), _const_spec(kvg), _const_spec(wuq), _const_spec(wukv),
                  _row_spec(tm, LANES), _row_spec(tm, LANES), _row_spec(tm, LANES)],
        out_specs=[_row_spec(tm, 1024), _row_spec(tm, 1024), _row_spec(tm, HEAD_W), _row_spec(tm, Q_LORA),
                   _row_spec(tm, KV_LORA)],
        out_shape=[jax.ShapeDtypeStruct((L, 1024), BF16), jax.ShapeDtypeStruct((L, 1024), BF16),
                   jax.ShapeDtypeStruct((L, HEAD_W), BF16), jax.ShapeDtypeStruct((L, Q_LORA), BF16),
                   jax.ShapeDtypeStruct((L, KV_LORA), BF16)],
        compiler_params=_cparams(VMEM_BIG_MB))(p, qg, kvg, wuq, wukv, tc, ta, tb)


def _mla_prep_bwd(dq, dk, dv, p, qg, kvg, wuq_t, wukv_t, tabs_t, name):
    L = p.shape[0]
    tm = min(512, L)
    scale = (MLA_NOPE + MLA_ROPE) ** -0.5
    tc, ta, tb = tabs_t

    def body(dq_ref, dk_ref, dv_ref, p_ref, qg_ref, kvg_ref, wuqt_ref, wukvt_ref, c_ref, a_ref, b_ref,
             dp_ref, dqp_ref, dkv_ref, dqg_ref, dkvg_ref):
        c, a, b = c_ref[...], a_ref[...], b_ref[...]
        dkr = jnp.zeros((tm, LANES), F32)
        for h in range(MLA_HEADS):
            sl = slice(h * LANES, (h + 1) * LANES)
            dqp_ref[:, sl] = (_rope(dq_ref[:, sl], c, a, b, MLA_ROPE // 2) * scale).astype(BF16)
            dkh = dk_ref[:, sl] * LN2
            dkv_ref[:, sl] = dkh.astype(BF16)
            dkr = dkr + dkh
        dkv_ref[:, 1024:1536] = dv_ref[...].astype(BF16)
        lane = lax.broadcasted_iota(jnp.int32, (tm, LANES), 1)
        dkr = jnp.where((lane >= MLA_NOPE) & (lane < MLA_NOPE + MLA_ROPE), _rope(dkr, c, a, b, MLA_ROPE // 2), 0.0)

        d_cqn = jnp.dot(dqp_ref[...], wuqt_ref[...], preferred_element_type=F32)
        d_ckvn = jnp.dot(dkv_ref[...], wukvt_ref[...], preferred_element_type=F32)

        def norm_bwd(xv, gv, dy):
            r = lax.rsqrt(jnp.mean(xv * xv, axis=-1, keepdims=True) + EPS)
            dyg = dy * gv
            dx = r * dyg - xv * (r * r * r) * jnp.mean(dyg * xv, axis=-1, keepdims=True)
            return dx, jnp.sum(dy * xv * r, axis=0, keepdims=True)

        d_cq, dqg = norm_bwd(p_ref[:, 0:Q_LORA], qg_ref[...], d_cqn)
        d_ckv, dkvg = norm_bwd(p_ref[:, Q_LORA:Q_LORA + KV_LORA], kvg_ref[...], d_ckvn)
        dp_ref[:, 0:Q_LORA] = d_cq.astype(BF16)
        dp_ref[:, Q_LORA:Q_LORA + KV_LORA] = d_ckv.astype(BF16)
        dp_ref[:, 640:768] = dkr.astype(BF16)
        dp_ref[:, 768:1024] = jnp.zeros((tm, 256), BF16)

        @pl.when(pl.program_id(0) == 0)
        def _():
            dqg_ref[...] = dqg
            dkvg_ref[...] = dkvg

        @pl.when(pl.program_id(0) > 0)
        def _():
            dqg_ref[...] += dqg
            dkvg_ref[...] += dkvg

    return pl.pallas_call(
        body, name=name, grid=(L // tm,),
        in_specs=[_row_spec(tm, 1024), _row_spec(tm, 1024), _row_spec(tm, HEAD_W), _row_spec(tm, P_MLA, 0),
                  _const_spec(qg), _const_spec(kvg), _const_spec(wuq_t), _const_spec(wukv_t),
                  _row_spec(tm, LANES), _row_spec(tm, LANES), _row_spec(tm, LANES)],
        out_specs=[_row_spec(tm, P_MLA), _row_spec(tm, 1024), _row_spec(tm, 1536),
                   pl.BlockSpec((1, Q_LORA), lambda i: (0, 0)), pl.BlockSpec((1, KV_LORA), lambda i: (0, 0))],
        out_shape=[jax.ShapeDtypeStruct((L, P_MLA), BF16), jax.ShapeDtypeStruct((L, 1024), BF16),
                   jax.ShapeDtypeStruct((L, 1536), BF16), jax.ShapeDtypeStruct((1, Q_LORA), F32),
                   jax.ShapeDtypeStruct((1, KV_LORA), F32)],
        compiler_params=_cparams(VMEM_BIG_MB))(dq, dk, dv, p, qg, kvg, wuq_t, wukv_t, tc, ta, tb)


def _dil_prep(p, tabs, name):
    L = p.shape[0]
    tm = min(512, L)
    tc, ta, tb = tabs
    scale = DIL_HD ** -0.5

    def body(*refs):
        ins, (c_ref, a_ref, b_ref), outs = refs[:9], refs[9:12], refs[12:]
        c, a, b = c_ref[...], a_ref[...], b_ref[...]
        for n in range(9):
            t = n % 3
            for cb in range(HEAD_W // LANES):
                sl = slice(cb * LANES, (cb + 1) * LANES)
                xv = ins[n][:, sl]
                if t == 0:
                    xv = _rope(xv, c, a, b, ROT_DIM // 2) * scale
                elif t == 1:
                    xv = _rope(xv, c, a, b, ROT_DIM // 2)
                outs[n][:, sl] = xv.astype(BF16)

    in_specs = [_row_spec(tm, HEAD_W, P_DIL0 // HEAD_W + n) for n in range(9)] + [_row_spec(tm, LANES)] * 3
    return pl.pallas_call(
        body, name=name, grid=(L // tm,), in_specs=in_specs,
        out_specs=[_row_spec(tm, HEAD_W)] * 9,
        out_shape=[jax.ShapeDtypeStruct((L, HEAD_W), BF16)] * 9)(*([p] * 9), tc, ta, tb)


def _dil_prep_bwd(grads, tabs_t, name):
    L = grads[0].shape[0]
    tm = min(512, L)
    tc, ta, tb = tabs_t
    scale = DIL_HD ** -0.5

    def body(*refs):
        ins, (c_ref, a_ref, b_ref), o_ref = refs[:9], refs[9:12], refs[12]
        c, a, b = c_ref[...], a_ref[...], b_ref[...]
        for n in range(9):
            t = n % 3
            for cb in range(HEAD_W // LANES):
                sl = slice(cb * LANES, (cb + 1) * LANES)
                xv = ins[n][:, sl]
                if t == 0:
                    xv = _rope(xv, c, a, b, ROT_DIM // 2) * scale
                elif t == 1:
                    xv = _rope(xv, c, a, b, ROT_DIM // 2)
                o_ref[:, n * HEAD_W + cb * LANES:n * HEAD_W + (cb + 1) * LANES] = xv.astype(BF16)

    return pl.pallas_call(
        body, name=name, grid=(L // tm,), in_specs=[_row_spec(tm, HEAD_W)] * 9 + [_row_spec(tm, LANES)] * 3,
        out_specs=_row_spec(tm, 9 * HEAD_W), out_shape=jax.ShapeDtypeStruct((L, 9 * HEAD_W), BF16),
        compiler_params=_cparams(VMEM_BIG_MB))(*grads, tc, ta, tb)


def _merge_gate(oa, p, o_g, lse_g, name):
    L = oa.shape[0]
    tm = min(512, L)

    def body(oa_ref, ga_ref, gb_ref, o1, o2, o3, l1, l2, l3, ab_ref, bm_ref, lt_ref):
        la, lb, lc = l1[...], l2[...], l3[...]
        m = jnp.maximum(jnp.maximum(la, lb), lc)
        ea, eb, ec = jnp.exp(la - m), jnp.exp(lb - m), jnp.exp(lc - m)
        den = ea + eb + ec
        bm = (ea * o1[...] + eb * o2[...] + ec * o3[...]) / den
        bm_ref[...] = bm
        lt_ref[...] = m + jnp.log(den)
        ga, gb = ga_ref[...], gb_ref[...]
        ab_ref[:, 0:HEAD_W] = (oa_ref[...] * (ga * _sigmoid(ga))).astype(BF16)
        ab_ref[:, HEAD_W:2 * HEAD_W] = (bm * (gb * _sigmoid(gb))).astype(BF16)

    w = _row_spec(tm, HEAD_W)
    return pl.pallas_call(
        body, name=name, grid=(L // tm,),
        in_specs=[w, _row_spec(tm, HEAD_W, 2), _row_spec(tm, HEAD_W, 3), w, w, w, w, w, w],
        out_specs=[_row_spec(tm, 2 * HEAD_W), w, w],
        out_shape=[jax.ShapeDtypeStruct((L, 2 * HEAD_W), BF16), jax.ShapeDtypeStruct((L, HEAD_W), F32),
                   jax.ShapeDtypeStruct((L, HEAD_W), F32)])(oa, p, p, *o_g, *lse_g)


def _gate_bwd(dab, p, oa, bm, name):
    L = oa.shape[0]
    tm = min(512, L)

    def body(da_ref, db_ref, ga_ref, gb_ref, oa_ref, bm_ref, doa_ref, dbm_ref, Da_ref, Db_ref, dg_ref):
        def one(d, g, o, do_ref, D_ref, col):
            sg = _sigmoid(g)
            do = d * (g * sg)
            do_ref[...] = do.astype(BF16)
            dg_ref[:, col:col + HEAD_W] = (d * o * (sg * (1.0 + g * (1.0 - sg)))).astype(BF16)
            prod = do * o
            for cb in range(HEAD_W // LANES):
                sl = slice(cb * LANES, (cb + 1) * LANES)
                D_ref[:, sl] = _headsum_bcast(prod[:, sl])

        one(da_ref[...], ga_ref[...], oa_ref[...], doa_ref, Da_ref, 0)
        one(db_ref[...], gb_ref[...], bm_ref[...], dbm_ref, Db_ref, HEAD_W)

    w = _row_spec(tm, HEAD_W)
    return pl.pallas_call(
        body, name=name, grid=(L // tm,),
        in_specs=[_row_spec(tm, HEAD_W, 0), _row_spec(tm, HEAD_W, 1), _row_spec(tm, HEAD_W, 2),
                  _row_spec(tm, HEAD_W, 3), w, w],
        out_specs=[w, w, w, w, _row_spec(tm, 2 * HEAD_W)],
        out_shape=[jax.ShapeDtypeStruct((L, HEAD_W), BF16), jax.ShapeDtypeStruct((L, HEAD_W), BF16),
                   jax.ShapeDtypeStruct((L, HEAD_W), F32), jax.ShapeDtypeStruct((L, HEAD_W), F32),
                   jax.ShapeDtypeStruct((L, 2 * HEAD_W), BF16)])(dab, dab, p, p, oa, bm)


NT = (((1,), (1,)), ((), ()))
TN = (((0,), (0,)), ((), ()))
NEG = -1e30


MLA_TQ = 512
MLA_TK = 2048
MLA_BWD_TK = 1024
LOG2E = 1.4426950408889634
LN2 = 0.6931471805599453


def _mla_fwd(q, k, v_t, name):
    L = q.shape[0]
    tq, tk = min(MLA_TQ, L), min(MLA_TK, L)
    nq, nk = L // tq, L // tk
    npair = MLA_HEADS // 2

    def body(q_ref, k_ref, vt_ref, o_ref, lse_ref, m0, l0, a0, m1, l1, a1):
        j = pl.program_id(2)
        stats = ((m0, l0, a0), (m1, l1, a1))

        @pl.when(j == 0)
        def _():
            for m_sc, l_sc, acc_sc in stats:
                m_sc[...] = jnp.full(m_sc.shape, NEG, F32)
                l_sc[...] = jnp.zeros(l_sc.shape, F32)
                acc_sc[...] = jnp.zeros(acc_sc.shape, F32)

        s_ts = [lax.dot_general(k_ref[:, hh * LANES:(hh + 1) * LANES], q_ref[:, hh * LANES:(hh + 1) * LANES], NT,
                                preferred_element_type=F32) for hh in range(2)]
        for hh in range(2):
            m_sc, l_sc, acc_sc = stats[hh]
            s_t = s_ts[hh]
            m_prev = m_sc[...]
            m_new = jnp.maximum(m_prev, jnp.max(s_t, axis=0, keepdims=True))
            alpha = jnp.exp2(m_prev - m_new)
            p_t = jnp.exp2(s_t - m_new)
            l_sc[...] = alpha * l_sc[...] + jnp.sum(p_t, axis=0, keepdims=True)
            m_sc[...] = m_new
            pv = jnp.dot(vt_ref[hh * DIL_HD:(hh + 1) * DIL_HD, :], p_t.astype(BF16),
                         preferred_element_type=F32)
            acc_sc[...] = alpha * acc_sc[...] + pv

        @pl.when(j == nk - 1)
        def _():
            o_ref[...] = jnp.concatenate([a0[...] / l0[...], a1[...] / l1[...]], axis=0).T
            lse_ref[...] = jnp.concatenate([m0[...] + jnp.log2(l0[...]), m1[...] + jnp.log2(l1[...])], axis=0)

    stat = [pltpu.VMEM((1, tq), F32), pltpu.VMEM((1, tq), F32), pltpu.VMEM((DIL_HD, tq), F32)]
    return pl.pallas_call(
        body, name=name, grid=(npair, nq, nk),
        in_specs=[pl.BlockSpec((tq, 2 * LANES), lambda pr, i, j: (i, pr)),
                  pl.BlockSpec((tk, 2 * LANES), lambda pr, i, j: (j, pr)),
                  pl.BlockSpec((LANES, tk), lambda pr, i, j: (pr, j))],
        out_specs=[pl.BlockSpec((tq, LANES), lambda pr, i, j: (i, pr)),
                   pl.BlockSpec((None, 2, tq), lambda pr, i, j: (pr, 0, i))],
        out_shape=[jax.ShapeDtypeStruct((L, HEAD_W), F32), jax.ShapeDtypeStruct((npair, 2, L), F32)],
        scratch_shapes=stat + stat, compiler_params=_cparams(VMEM_BIG_MB))(q, k, v_t)


def _mla_bwd(q, k, v, do, lse_rows, d_rows, name):
    L = q.shape[0]
    tq, tk = min(512, L), min(MLA_BWD_TK, L)
    nq, nk = L // tq, L // tk
    npair = MLA_HEADS // 2

    def body(q_ref, k_ref, v_ref, do_ref, lse_ref, d_ref, dq_ref, dk_ref, dv_ref):
        j, i = pl.program_id(1), pl.program_id(2)

        @pl.when((j == 0) & (i == 0))
        def _():
            dq_ref[...] = jnp.zeros(dq_ref.shape, F32)

        @pl.when(i == 0)
        def _():
            dk_ref[...] = jnp.zeros(dk_ref.shape, F32)
            dv_ref[...] = jnp.zeros(dv_ref.shape, F32)

        first = lax.broadcasted_iota(jnp.int32, (tq, LANES), 1) < DIL_HD
        dov = do_ref[...]
        vv = v_ref[...]
        rows = pl.ds(pl.multiple_of(i * tq, tq), tq)
        for hh in range(2):
            sl = slice(hh * LANES, (hh + 1) * LANES)
            qh, kh = q_ref[:, sl], k_ref[:, sl]
            do_h = jnp.where(first if hh == 0 else ~first, dov, jnp.zeros_like(dov))
            s_t = lax.dot_general(kh, qh, NT, preferred_element_type=F32)
            p_t = jnp.exp2(s_t - lse_ref[hh:hh + 1, :])
            dv_ref[...] += jnp.dot(p_t.astype(BF16), do_h, preferred_element_type=F32)
            dp_t = lax.dot_general(vv, do_h, NT, preferred_element_type=F32)
            ds_t = (p_t * (dp_t - d_ref[hh:hh + 1, :])).astype(BF16)
            dk_ref[:, sl] += jnp.dot(ds_t, qh, preferred_element_type=F32)
            dq_ref[rows, sl] += lax.dot_general(ds_t, kh, TN, preferred_element_type=F32)

    return pl.pallas_call(
        body, name=name, grid=(npair, nk, nq),
        in_specs=[pl.BlockSpec((tq, 2 * LANES), lambda pr, j, i: (i, pr)),
                  pl.BlockSpec((tk, 2 * LANES), lambda pr, j, i: (j, pr)),
                  pl.BlockSpec((tk, LANES), lambda pr, j, i: (j, pr)),
                  pl.BlockSpec((tq, LANES), lambda pr, j, i: (i, pr)),
                  pl.BlockSpec((None, 2, tq), lambda pr, j, i: (pr, 0, i)),
                  pl.BlockSpec((None, 2, tq), lambda pr, j, i: (pr, 0, i))],
        out_specs=[pl.BlockSpec((L, 2 * LANES), lambda pr, j, i: (0, pr)),
                   pl.BlockSpec((tk, 2 * LANES), lambda pr, j, i: (j, pr)),
                   pl.BlockSpec((tk, LANES), lambda pr, j, i: (j, pr))],
        out_shape=[jax.ShapeDtypeStruct((L, 1024), F32), jax.ShapeDtypeStruct((L, 1024), F32),
                   jax.ShapeDtypeStruct((L, HEAD_W), F32)],
        compiler_params=_cparams(VMEM_BIG_MB))(q, k, v, do, lse_rows, d_rows)


def _dil_tiles(ld):
    tq = min(256, ld)
    kw = min(512, ld)
    return tq, kw


def _dil_fwd(q, k, v, name):
    d, ld, _ = q.shape
    tq, kw = _dil_tiles(ld)
    nq = ld // tq
    npair = HEAD_W // LANES

    def body(q_ref, k_ref, v_ref, o_ref, lse_ref):
        i = pl.program_id(2)
        a0 = i * tq
        start = pl.multiple_of(jnp.clip(a0 - LANES, 0, ld - kw), LANES)
        kwin = k_ref[pl.ds(start, kw), :]
        vwin = v_ref[pl.ds(start, kw), :]
        qpos = a0 + lax.broadcasted_iota(jnp.int32, (tq, kw), 0)
        kpos = start + lax.broadcasted_iota(jnp.int32, (tq, kw), 1)
        valid = jnp.abs(qpos - kpos) <= DIL_HALF
        first = lax.broadcasted_iota(jnp.int32, (tq, LANES), 1) < DIL_HD
        qv = q_ref[...]
        outs, lses = [], []
        for hh in range(2):
            qh = jnp.where(first if hh == 0 else ~first, qv, jnp.zeros_like(qv))
            s = lax.dot_general(qh, kwin, NT, preferred_element_type=F32)
            s = jnp.where(valid, s, NEG)
            m = jnp.max(s, axis=-1, keepdims=True)
            pr = jnp.exp(s - m)
            l = jnp.sum(pr, axis=-1, keepdims=True)
            outs.append(jnp.dot(pr.astype(BF16), vwin, preferred_element_type=F32) / l)
            lses.append(m + jnp.log(l))
        o_ref[...] = jnp.where(first, outs[0], outs[1])
        lse_ref[...] = jnp.where(first, lses[0], lses[1])

    blk = pl.BlockSpec((None, tq, LANES), lambda r, pr, i: (r, i, pr))
    full = pl.BlockSpec((None, ld, LANES), lambda r, pr, i: (r, 0, pr))
    return pl.pallas_call(
        body, name=name, grid=(d, npair, nq), in_specs=[blk, full, full], out_specs=[blk, blk],
        out_shape=[jax.ShapeDtypeStruct((d, ld, HEAD_W), F32), jax.ShapeDtypeStruct((d, ld, HEAD_W), F32)],
        compiler_params=_cparams(VMEM_BIG_MB))(q, k, v)


def _dil_bwd(q, k, v, do, lse_rows, d_rows, name):
    d, ld, _ = q.shape
    tq, kw = _dil_tiles(ld)
    nq = ld // tq
    npair = HEAD_W // LANES

    def body(q_ref, k_ref, v_ref, do_ref, lse_ref, d_ref, dq_ref, dk_ref, dv_ref):
        i = pl.program_id(2)

        @pl.when(i == 0)
        def _():
            dk_ref[...] = jnp.zeros(dk_ref.shape, F32)
            dv_ref[...] = jnp.zeros(dv_ref.shape, F32)

        a0 = i * tq
        start = pl.multiple_of(jnp.clip(a0 - LANES, 0, ld - kw), LANES)
        win = pl.ds(start, kw)
        kwin = k_ref[win, :]
        vwin = v_ref[win, :]
        kpos = start + lax.broadcasted_iota(jnp.int32, (kw, tq), 0)
        qpos = a0 + lax.broadcasted_iota(jnp.int32, (kw, tq), 1)
        valid = jnp.abs(qpos - kpos) <= DIL_HALF
        first = lax.broadcasted_iota(jnp.int32, (tq, LANES), 1) < DIL_HD
        qv = q_ref[...]
        dov = do_ref[...]
        dqs = []
        for hh in range(2):
            sel = first if hh == 0 else ~first
            qh = jnp.where(sel, qv, jnp.zeros_like(qv))
            do_h = jnp.where(sel, dov, jnp.zeros_like(dov))
            s_t = lax.dot_general(kwin, qh, NT, preferred_element_type=F32)
            p_t = jnp.where(valid, jnp.exp(jnp.where(valid, s_t, NEG) - lse_ref[hh:hh + 1, :]), 0.0)
            dv_ref[win, :] += jnp.dot(p_t.astype(BF16), do_h, preferred_element_type=F32)
            dp_t = lax.dot_general(vwin, do_h, NT, preferred_element_type=F32)
            ds_t = (p_t * (dp_t - d_ref[hh:hh + 1, :])).astype(BF16)
            dk_ref[win, :] += jnp.dot(ds_t, qh, preferred_element_type=F32)
            dqs.append(lax.dot_general(ds_t, kwin, TN, preferred_element_type=F32))
        dq_ref[...] = jnp.where(first, dqs[0], dqs[1])

    blk = pl.BlockSpec((None, tq, LANES), lambda r, pr, i: (r, i, pr))
    full = pl.BlockSpec((None, ld, LANES), lambda r, pr, i: (r, 0, pr))
    rowspec = pl.BlockSpec((None, None, 2, tq), lambda r, pr, i: (r, pr, 0, i))
    return pl.pallas_call(
        body, name=name, grid=(d, npair, nq), in_specs=[blk, full, full, blk, rowspec, rowspec],
        out_specs=[blk, full, full],
        out_shape=[jax.ShapeDtypeStruct((d, ld, HEAD_W), F32)] * 3,
        compiler_params=_cparams(VMEM_BIG_MB))(q, k, v, do, lse_rows, d_rows)


TILE_BYTES = 1 << 21


def _row_tile(rows, cols, budget=TILE_BYTES):
    for parts in range(1, rows + 1):
        tr = rows // parts
        if rows % parts == 0 and tr % 8 == 0 and tr * cols * 4 <= budget:
            return tr
    return rows


def _add2(a, b, name, out_dtype):
    n, rows, cols = a.shape
    tr = _row_tile(rows, cols)

    def body(a_ref, b_ref, o_ref):
        o_ref[...] = (a_ref[...] + b_ref[...]).astype(out_dtype)

    spec = pl.BlockSpec((None, tr, cols), lambda t, i: (t, i, 0))
    return pl.pallas_call(body, name=name, grid=(n, rows // tr), in_specs=[spec, spec], out_specs=spec,
                          out_shape=jax.ShapeDtypeStruct(a.shape, out_dtype))(a, b)


def _add4_ordered(a, name):
    _, rows, cols = a.shape
    tr = _row_tile(rows, cols, TILE_BYTES // 4)

    def body(a_ref, o_ref):
        o_ref[...] = ((a_ref[0].astype(F32) + a_ref[1].astype(F32)) + a_ref[2].astype(F32)) + a_ref[3].astype(F32)

    return pl.pallas_call(
        body, name=name, grid=(rows // tr,), in_specs=[pl.BlockSpec((4, tr, cols), lambda i: (0, i, 0))],
        out_specs=pl.BlockSpec((tr, cols), lambda i: (i, 0)),
        out_shape=jax.ShapeDtypeStruct((rows, cols), F32))(a)


def _adamw(w, g, m, v, name):
    rows, cols = w.shape
    tr = _row_tile(rows, cols)
    bc1 = 1.0 - ADAM_B1 ** ADAM_STEP
    bc2 = 1.0 - ADAM_B2 ** ADAM_STEP

    def body(w_ref, g_ref, m_ref, v_ref, d_ref, nm_ref, nv_ref):
        gv = g_ref[...]
        nm = ADAM_B1 * m_ref[...] + (1.0 - ADAM_B1) * gv
        nv = ADAM_B2 * v_ref[...] + (1.0 - ADAM_B2) * (gv * gv)
        d_ref[...] = -ADAM_LR * ((nm / bc1) / (jnp.sqrt(nv / bc2) + ADAM_EPS) + ADAM_WD * w_ref[...])
        nm_ref[...] = nm
        nv_ref[...] = nv

    spec = pl.BlockSpec((tr, cols), lambda i: (i, 0))
    return pl.pallas_call(body, name=name, grid=(rows // tr,), in_specs=[spec] * 4, out_specs=[spec] * 3,
                          out_shape=[jax.ShapeDtypeStruct(w.shape, F32)] * 3,
                          compiler_params=_cparams(VMEM_BIG_MB))(w, g, m, v)


ANY = pl.BlockSpec(memory_space=pl.ANY)


def _place():
    return lax.axis_index("x"), lax.axis_index("y"), lax.axis_index("c")


def _rcopy(send_sems, recv_sems, n, src, dst, to):
    return pltpu.make_async_remote_copy(src_ref=src, dst_ref=dst, send_sem=send_sems.at[n], recv_sem=recv_sems.at[n],
                                        device_id=to, device_id_type=MESH)


def _allgather_weights(shards):
    na = len(shards)

    def body(*refs):
        w_refs, g_refs = refs[:na], refs[na:2 * na]
        send_sems, recv_sems, local_sems = refs[2 * na:]
        x, y, c = _place()
        s = 2 * x + y
        chips = [(1 - x, y), (x, 1 - y), (1 - x, 1 - y)]

        def half(a, shard, h):
            hr = shards[a].shape[0] // 2
            return g_refs[a].at[shard, pl.ds(h * hr, hr), :]

        started = []
        for a in range(na):
            hr = shards[a].shape[0] // 2
            mine = pltpu.make_async_copy(w_refs[a], g_refs[a].at[s], local_sems.at[a])
            mine.start()
            started.append(mine)
        sends = []
        for a in range(na):
            hr = shards[a].shape[0] // 2
            for n, (cx, cy) in enumerate(chips):
                cp = _rcopy(send_sems, recv_sems, 6 * a + n, w_refs[a].at[pl.ds(c * hr, hr), :], half(a, s, c),
                            (cx, cy, c))
                cp.start()
                sends.append(cp)
        for a in range(na):
            for n, (cx, cy) in enumerate(chips):
                sj = 2 * cx + cy
                _rcopy(send_sems, recv_sems, 6 * a + n, half(a, sj, c), half(a, sj, c), (cx, cy, c)).wait_recv()
                fw = _rcopy(send_sems, recv_sems, 6 * a + 3 + n, half(a, sj, c), half(a, sj, c), (x, y, 1 - c))
                fw.start()
                sends.append(fw)
        for a in range(na):
            for n, (cx, cy) in enumerate(chips):
                sj = 2 * cx + cy
                _rcopy(send_sems, recv_sems, 6 * a + 3 + n, half(a, sj, 1 - c), half(a, sj, 1 - c),
                       (x, y, 1 - c)).wait_recv()
        for cp in sends:
            cp.wait_send()
        for mine in started:
            mine.wait()

    return pl.pallas_call(
        body, name="allgather_weights", in_specs=[ANY] * na, out_specs=[ANY] * na,
        out_shape=[jax.ShapeDtypeStruct((N_SHARD,) + t.shape, t.dtype) for t in shards],
        scratch_shapes=[pltpu.SemaphoreType.DMA((6 * na,)), pltpu.SemaphoreType.DMA((6 * na,)),
                        pltpu.SemaphoreType.DMA((na,))])(*shards)


def _sibling_send_halves(gs):
    na = len(gs)

    def body(*refs):
        g_refs, o_refs = refs[:na], refs[na:2 * na]
        send_sems, recv_sems = refs[2 * na:]
        x, y, c = _place()
        cps = []
        for a in range(na):
            for t in range(N_SHARD):
                cp = _rcopy(send_sems, recv_sems, N_SHARD * a + t, g_refs[a].at[t, 1 - c], o_refs[a].at[t],
                            (x, y, 1 - c))
                cp.start()
                cps.append(cp)
        for cp in cps:
            cp.wait()

    return pl.pallas_call(
        body, name="grad_sibling_exchange", in_specs=[ANY] * na, out_specs=[ANY] * na,
        out_shape=[jax.ShapeDtypeStruct((N_SHARD,) + g.shape[2:], g.dtype) for g in gs],
        scratch_shapes=[pltpu.SemaphoreType.DMA((N_SHARD * na,)), pltpu.SemaphoreType.DMA((N_SHARD * na,))])(*gs)


def _chip_scatter(parts):
    na = len(parts)

    def body(*refs):
        a_refs, o_refs = refs[:na], refs[na:2 * na]
        send_sems, recv_sems, local_sems = refs[2 * na:]
        x, y, c = _place()
        s = 2 * x + y
        chips = [(1 - x, y), (x, 1 - y), (1 - x, 1 - y)]
        started, cps = [], []
        for a in range(na):
            mine = pltpu.make_async_copy(a_refs[a].at[s], o_refs[a].at[s], local_sems.at[a])
            mine.start()
            started.append(mine)
            for n, (cx, cy) in enumerate(chips):
                cp = _rcopy(send_sems, recv_sems, 3 * a + n, a_refs[a].at[2 * cx + cy], o_refs[a].at[s], (cx, cy, c))
                cp.start()
                cps.append(cp)
        for a in range(na):
            for n, (cx, cy) in enumerate(chips):
                sj = 2 * cx + cy
                _rcopy(send_sems, recv_sems, 3 * a + n, a_refs[a].at[sj], o_refs[a].at[sj], (cx, cy, c)).wait_recv()
        for cp in cps:
            cp.wait_send()
        for mine in started:
            mine.wait()

    return pl.pallas_call(
        body, name="grad_chip_scatter", in_specs=[ANY] * na, out_specs=[ANY] * na,
        out_shape=[jax.ShapeDtypeStruct(t.shape, t.dtype) for t in parts],
        scratch_shapes=[pltpu.SemaphoreType.DMA((3 * na,)), pltpu.SemaphoreType.DMA((3 * na,)),
                        pltpu.SemaphoreType.DMA((na,))])(*parts)


def _sibling_swap(rs):
    na = len(rs)

    def body(*refs):
        r_refs, o_refs = refs[:na], refs[na:2 * na]
        send_sems, recv_sems = refs[2 * na:]
        x, y, c = _place()
        cps = []
        for a in range(na):
            cp = _rcopy(send_sems, recv_sems, a, r_refs[a], o_refs[a], (x, y, 1 - c))
            cp.start()
            cps.append(cp)
        for cp in cps:
            cp.wait()

    return pl.pallas_call(
        body, name="grad_sibling_swap", in_specs=[ANY] * na, out_specs=[ANY] * na,
        out_shape=[jax.ShapeDtypeStruct(t.shape, t.dtype) for t in rs],
        scratch_shapes=[pltpu.SemaphoreType.DMA((na,)), pltpu.SemaphoreType.DMA((na,))])(*rs)


def _pack_small(norm_g, q_norm_g, kv_norm_g, final_g):
    flat = jnp.concatenate([norm_g.reshape(-1), q_norm_g.reshape(-1), kv_norm_g.reshape(-1), final_g.reshape(-1),
                            jnp.zeros((SMALL_ROWS * LANES - N_SMALL,), F32)])
    return flat.reshape(SMALL_ROWS, LANES)


def _split_small(s):
    s = s.reshape(-1)
    o = 0
    out = []
    for n, shape in ((DEPTH * D_MODEL, (DEPTH, D_MODEL)), (DEPTH * Q_LORA, (DEPTH, Q_LORA)),
                     (DEPTH * KV_LORA, (DEPTH, KV_LORA)), (D_MODEL, (D_MODEL,))):
        out.append(s[o:o + n].reshape(shape))
        o += n
    return out


def _assemble_w_in(sh):
    z = lambda n: jnp.zeros(sh.shape[1:3] + (n,), sh.dtype)
    s0, s1, s2, s3 = sh[0], sh[1], sh[2], sh[3]
    return jnp.concatenate([s0[..., 0:640], z(64), s0[..., 640:672], z(32), z(256), s0[..., 672:1184],
                            s3[..., 1064:1576], s0[..., 1184:1576], s1, s2, s3[..., 0:1064]], axis=-1)


def _split_w_in_grad(d):
    sh0 = jnp.concatenate([d[..., 0:640], d[..., 704:736], d[..., 1024:1536], d[..., 2048:2440]], axis=-1)
    sh3 = jnp.concatenate([d[..., 5592:6656], d[..., 1536:2048]], axis=-1)
    return jnp.stack([sh0, d[..., 2440:4016], d[..., 4016:5592], sh3]).reshape(N_SHARD, DEPTH * D_MODEL, SHARD_COLS_IN)


def _col_shards(w):
    dl, r, cc = w.shape
    return w.reshape(dl, r, N_SHARD, cc // N_SHARD).transpose(2, 0, 1, 3).reshape(N_SHARD, dl * r, cc // N_SHARD)


def _from_col_shards(g, rows):
    cc = g.shape[-1]
    return g.reshape(N_SHARD, DEPTH, rows, cc).transpose(1, 2, 0, 3).reshape(DEPTH, rows, N_SHARD * cc)


def _pad_w_in(w):
    z = lambda n: jnp.zeros(w.shape[:-1] + (n,), w.dtype)
    return jnp.concatenate([w[..., 0:640], z(64), w[..., 640:672], z(32), z(256), w[..., 672:1184],
                            w[..., 5792:6304], w[..., 1184:5792]], axis=-1)


def _unpad_w_in(w):
    return jnp.concatenate([w[..., 0:640], w[..., 704:736], w[..., 1024:1536], w[..., 2048:6656],
                            w[..., 1536:2048]], axis=-1)


def _pad_w_uq(w):
    s = w.shape[:-1]
    w = w.reshape(s + (MLA_HEADS, 96))
    return jnp.pad(w, [(0, 0)] * (w.ndim - 1) + [(0, 32)]).reshape(s + (1024,))


def _unpad_w_uq(w):
    s = w.shape[:-1]
    return w.reshape(s + (MLA_HEADS, LANES))[..., :96].reshape(s + (768,))


def _pad_w_ukv(w):
    s = w.shape[:-1]
    w = w.reshape(s + (MLA_HEADS, 128))
    kpart = jnp.pad(w[..., :64], [(0, 0)] * (w.ndim - 1) + [(0, 64)]).reshape(s + (1024,))
    vpart = w[..., 64:].reshape(s + (512,))
    return jnp.concatenate([kpart, vpart], axis=-1)


def _unpad_w_ukv(w):
    s = w.shape[:-1]
    kpart = w[..., :1024].reshape(s + (MLA_HEADS, LANES))[..., :64]
    vpart = w[..., 1024:].reshape(s + (MLA_HEADS, 64))
    return jnp.concatenate([kpart, vpart], axis=-1).reshape(s + (1024,))


def _rope_tables(L, dim, lane_lo, period):
    half = dim // 2
    inv = 1.0 / (ROPE_THETA ** (jnp.arange(0, dim, 2, dtype=F32) / dim))
    ang = jnp.arange(L, dtype=F32)[:, None] * inv[None, :]
    cos, sin = jnp.cos(ang), jnp.sin(ang)
    one = lambda n: jnp.ones((L, n), F32)
    zero = lambda n: jnp.zeros((L, n), F32)
    rest = period - lane_lo - dim
    rep = LANES // period
    c = jnp.tile(jnp.concatenate([one(lane_lo), cos, cos, one(rest)], axis=1), (1, rep))
    a = jnp.tile(jnp.concatenate([zero(lane_lo), -sin, zero(half), zero(rest)], axis=1), (1, rep))
    b = jnp.tile(jnp.concatenate([zero(lane_lo + half), sin, zero(rest)], axis=1), (1, rep))
    return c, a, b


def _to_strided(t, d):
    L, w = t.shape
    return t.reshape(L // d, d, w).transpose(1, 0, 2)


def _from_strided(t):
    d, ld, w = t.shape
    return t.transpose(1, 0, 2).reshape(d * ld, w)


def _head_rows(t):
    return t[:, ::DIL_HD].T.reshape(MLA_HEADS // 2, 2, t.shape[0])


def _head_rows_strided(t, d):
    s = _to_strided(t[:, ::DIL_HD], d)
    return s.transpose(0, 2, 1).reshape(d, MLA_HEADS // 2, 2, s.shape[1])


def _local_grads(x, target, norm_g, w_in_p, q_norm_g, kv_norm_g, w_uq_p, w_ukv_p, w_out, final_g):
    L = x.shape[0]
    tabs_m = _rope_tables(L, MLA_ROPE, MLA_NOPE, LANES)
    tabs_d = _rope_tables(L, ROT_DIM, 0, DIL_HD)
    tabs_m_t = (tabs_m[0], -tabs_m[1], -tabs_m[2])
    tabs_d_t = (tabs_d[0], -tabs_d[1], -tabs_d[2])
    w_in_t = jnp.swapaxes(w_in_p, 1, 2)
    w_uq_t = jnp.swapaxes(w_uq_p, 1, 2)
    w_ukv_t = jnp.swapaxes(w_ukv_p, 1, 2)
    w_out_t = jnp.swapaxes(w_out, 1, 2)

    saved = []
    for l in range(DEPTH):
        h = _rms_fwd(x, norm_g[l:l + 1], "rms_fwd")
        p = _mm(h, w_in_p[l], tm=512, tn=3328, tk=1024, out_dtype=F32, name="in_proj")
        q, k, v, cqn, ckvn = _mla_prep(p, q_norm_g[l:l + 1], kv_norm_g[l:l + 1], w_uq_p[l], w_ukv_p[l], tabs_m,
                                       "mla_prep")
        oa, lse_a = _mla_fwd(q, k, v.T, "mla_fwd")
        dil = _dil_prep(p, tabs_d, "dil_prep")
        dil_s, o_g, lse_g = [], [], []
        for g, (_, dd) in enumerate(DIL_PAIRS):
            qs, ks, vs = (_to_strided(t, dd) for t in dil[3 * g:3 * g + 3])
            og, lg = _dil_fwd(qs, ks, vs, "dil_fwd_%d" % dd)
            dil_s.append((qs, ks, vs))
            o_g.append(_from_strided(og))
            lse_g.append(_from_strided(lg))
        ab, bm, lt = _merge_gate(oa, p, o_g, lse_g, "merge_gate")
        x_next = _mm(ab, w_out[l], tm=1024, tn=1024, tk=1024, out_dtype=F32, name="out_proj", add=x)
        saved.append((x, h, p, q, k, v, cqn, ckvn, oa, lse_a, dil_s, bm, lt, ab))
        x = x_next

    loss_b, dx, d_final = _loss_head(x, final_g[None, :], target, "loss_head")
    loss = loss_b[0, 0]

    d_norm, d_qn, d_kvn, d_win, d_wuq, d_wukv, d_wout = [], [], [], [], [], [], []
    for l in reversed(range(DEPTH)):
        x_l, h, p, q, k, v, cqn, ckvn, oa, lse_a, dil_s, bm, lt, ab = saved[l]
        dab = _mm(dx, w_out_t[l], tm=1024, tn=1024, tk=1024, out_dtype=F32, name="out_proj_dgrad")
        d_wout.append(_mm(ab.T, dx, tm=1024, tn=1024, tk=1024, out_dtype=F32, name="out_proj_wgrad"))
        doa, dbm, D_a, D_b, dgates = _gate_bwd(dab, p, oa, bm, "gate_bwd")
        dq, dk, dv = _mla_bwd(q, k, v, doa, lse_a, _head_rows(D_a), "mla_bwd")
        dp_mla, dq_pre, dkv, dqg, dkvg = _mla_prep_bwd(dq, dk, dv, p, q_norm_g[l:l + 1], kv_norm_g[l:l + 1],
                                                       w_uq_t[l], w_ukv_t[l], tabs_m_t, "mla_prep_bwd")
        d_wuq.append(_mm(cqn.T, dq_pre, tm=Q_LORA, tn=1024, tk=2048, out_dtype=F32, name="w_uq_wgrad"))
        d_wukv.append(_mm(ckvn.T, dkv, tm=KV_LORA, tn=1536, tk=2048, out_dtype=F32, name="w_ukv_wgrad"))
        dgr = []
        for g, (_, dd) in enumerate(DIL_PAIRS):
            qs, ks, vs = dil_s[g]
            dqs, dks, dvs = _dil_bwd(qs, ks, vs, _to_strided(dbm, dd), _head_rows_strided(lt, dd),
                                     _head_rows_strided(D_b, dd), "dil_bwd_%d" % dd)
            dgr += [_from_strided(dqs), _from_strided(dks), _from_strided(dvs)]
        dp_dil = _dil_prep_bwd(dgr, tabs_d_t, "dil_prep_bwd")
        dp = jnp.concatenate([dp_mla, dgates, dp_dil], axis=1)
        dh = _mm(dp, w_in_t[l], tm=1024, tn=1024, tk=1664, out_dtype=F32, name="in_proj_dgrad")
        d_win.append(_mm(h.T, dp, tm=1024, tn=1664, tk=1024, out_dtype=F32, name="in_proj_wgrad"))
        dx, dng = _rms_bwd(dh, x_l, norm_g[l:l + 1], dx, "rms_bwd")
        d_norm.append(dng[0])
        d_qn.append(dqg[0])
        d_kvn.append(dkvg[0])

    rev = lambda xs: jnp.stack(xs[::-1])
    return (loss, dx, rev(d_norm), rev(d_win), rev(d_qn), rev(d_kvn), rev(d_wuq), rev(d_wukv), rev(d_wout),
            d_final[0])


def kernel(x, norm_g, w_in, q_norm_g, kv_norm_g, w_uq, w_ukv, w_out, final_g, loss_target, m_norm_g, m_w_in, m_q_norm_g, m_kv_norm_g, m_w_uq, m_w_ukv, m_w_out, m_final_g, v_norm_g, v_w_in, v_q_norm_g, v_kv_norm_g, v_w_uq, v_w_ukv, v_w_out, v_final_g):
    c = lax.axis_index("c")

    def families(a_in, a_uq, a_ukv, a_out):
        return [t.reshape(shape) for t, shape in zip((a_in, a_uq, a_ukv, a_out), FAM_SHAPES)]

    g_in, g_uq, g_ukv, g_out = _allgather_weights([t.astype(BF16) for t in families(w_in, w_uq, w_ukv, w_out)])
    w_in_p = _assemble_w_in(g_in.reshape(N_SHARD, DEPTH, D_MODEL, SHARD_COLS_IN))
    w_uq_p = _pad_w_uq(_from_col_shards(g_uq, Q_LORA))
    w_ukv_p = _pad_w_ukv(_from_col_shards(g_ukv, KV_LORA))
    w_out_f = g_out.reshape(N_SHARD, DEPTH, 1024 // N_SHARD, D_MODEL).transpose(1, 0, 2, 3).reshape(DEPTH, 1024, D_MODEL)

    (loss, dx, d_norm, d_win_p, d_qn, d_kvn, d_wuq_p, d_wukv_p, d_wout, d_final) = _local_grads(
        x[0], loss_target[0], norm_g, w_in_p, q_norm_g, kv_norm_g, w_uq_p, w_ukv_p, w_out_f, final_g)
    loss = lax.psum(loss, ("x", "y", "c"))

    small = _pack_small(d_norm, d_qn, d_kvn, d_final)
    grads = [_split_w_in_grad(d_win_p), _col_shards(_unpad_w_uq(d_wuq_p)), _col_shards(_unpad_w_ukv(d_wukv_p)),
             d_wout.reshape(DEPTH, N_SHARD, 1024 // N_SHARD, D_MODEL).transpose(1, 0, 2, 3).reshape(
                 N_SHARD, DEPTH * (1024 // N_SHARD), D_MODEL),
             jnp.broadcast_to(small[None], (N_SHARD, SMALL_ROWS, LANES))]
    halves = [g.reshape(N_SHARD, 2, g.shape[1] // 2, g.shape[2]) for g in grads]
    from_sib = _sibling_send_halves(halves)
    chip_sum = [_add2(lax.dynamic_index_in_dim(h, c, axis=1, keepdims=False), f, "grad_add_pair", BF16)
                for h, f in zip(halves, from_sib)]
    red_half = [_add4_ordered(t, "grad_add_chips") for t in _chip_scatter(chip_sum)]
    other_half = _sibling_swap(red_half)
    gred = []
    for mine, other in zip(red_half, other_half):
        both = jnp.stack([mine, other])
        gred.append(jnp.concatenate([lax.dynamic_index_in_dim(both, c, axis=0, keepdims=False),
                                     lax.dynamic_index_in_dim(both, 1 - c, axis=0, keepdims=False)], axis=0))

    wf = families(w_in, w_uq, w_ukv, w_out) + [_pack_small(norm_g, q_norm_g, kv_norm_g, final_g)]
    mf = families(m_w_in, m_w_uq, m_w_ukv, m_w_out) + [_pack_small(m_norm_g, m_q_norm_g, m_kv_norm_g, m_final_g)]
    vf = families(v_w_in, v_w_uq, v_w_ukv, v_w_out) + [_pack_small(v_norm_g, v_q_norm_g, v_kv_norm_g, v_final_g)]
    upd = [_adamw(w, g, m, v, "adamw") for w, g, m, v in zip(wf, gred, mf, vf)]

    def leaves(fams):
        a_in, a_uq, a_ukv, a_out, s = fams
        s_norm, s_qn, s_kvn, s_final = _split_small(s)
        return [s_norm, a_in.reshape(w_in.shape), s_qn, s_kvn, a_uq.reshape(w_uq.shape), a_ukv.reshape(w_ukv.shape),
                a_out.reshape(w_out.shape), s_final]

    return (loss, dx[None], *leaves(gred), *leaves([u[0] for u in upd]), *leaves([u[1] for u in upd]),
            *leaves([u[2] for u in upd]))
```

```python
import functools

import jax
import jax.numpy as jnp
from jax import lax
from jax.experimental import pallas as pl
from jax.experimental.pallas import tpu as pltpu

F32 = jnp.float32
BF16 = jnp.bfloat16
MESH = pl.DeviceIdType.MESH

D_MODEL = 1024
DEPTH = 4
MLA_HEADS = 8
MLA_NOPE = 64
MLA_ROPE = 32
Q_LORA = 384
KV_LORA = 256
DIL_PAIRS = ((128, 1), (512, 4), (2048, 16))
DIL_HD = 64
DIL_HALF = 64
ROT_DIM = 16
ROPE_THETA = 500000.0
EPS = 1e-6
IN_WIDTH = 6304
N_SHARD = 4

P_WIDTH = 6656
P_MLA = 1024
P_GATE = 1024
P_DIL0 = 2048
LANES = 128
HEAD_W = 512

ADAM_LR = 0.001
ADAM_B1 = 0.9
ADAM_B2 = 0.999
ADAM_EPS = 1e-08
ADAM_WD = 0.01
ADAM_STEP = 10

SHARD_COLS_IN = IN_WIDTH // N_SHARD
FAM_SHAPES = ((DEPTH * D_MODEL, SHARD_COLS_IN), (DEPTH * Q_LORA, 768 // N_SHARD), (DEPTH * KV_LORA, 1024 // N_SHARD),
              (DEPTH * (1024 // N_SHARD), D_MODEL))
N_SMALL = DEPTH * (D_MODEL + Q_LORA + KV_LORA) + D_MODEL
SMALL_ROWS = 64
VMEM_BIG_MB = 48


def _cparams(vmem_mb=None):
    if vmem_mb is None:
        return None
    return pltpu.CompilerParams(vmem_limit_bytes=vmem_mb << 20)


def _sigmoid(x):
    return 1.0 / (1.0 + jnp.exp(-x))


def _rope(x, c, a, b, sh):
    return x * c + pltpu.roll(x, LANES - sh, 1) * a + pltpu.roll(x, sh, 1) * b


def _headsum_bcast(x):
    r = lax.broadcasted_iota(jnp.int32, (LANES, LANES), 0) // DIL_HD
    c = lax.broadcasted_iota(jnp.int32, (LANES, LANES), 1) // DIL_HD
    ones = jnp.where(r == c, 1.0, 0.0).astype(BF16)
    hi = x.astype(BF16)
    lo = (x - hi.astype(F32)).astype(BF16)
    return (jnp.dot(hi, ones, preferred_element_type=F32) + jnp.dot(lo, ones, preferred_element_type=F32))


def _mm(a, b, *, tm, tn, tk, out_dtype, name, add=None):
    M, K = a.shape
    N = b.shape[1]
    tm, tn, tk = min(tm, M), min(tn, N), min(tk, K)
    assert M % tm == 0 and N % tn == 0 and K % tk == 0, (a.shape, b.shape)
    nk = K // tk
    has_add = add is not None

    def body(*refs):
        if has_add:
            a_ref, b_ref, add_ref, o_ref, acc = refs
        else:
            a_ref, b_ref, o_ref, acc = refs
        k = pl.program_id(2)
        part = jnp.dot(a_ref[...].astype(BF16), b_ref[...].astype(BF16), preferred_element_type=F32)

        @pl.when(k == 0)
        def _():
            acc[...] = part

        @pl.when(k > 0)
        def _():
            acc[...] += part

        @pl.when(k == nk - 1)
        def _():
            r = acc[...]
            if has_add:
                r = r + add_ref[...]
            o_ref[...] = r.astype(out_dtype)

    in_specs = [pl.BlockSpec((tm, tk), lambda i, j, k: (i, k)), pl.BlockSpec((tk, tn), lambda i, j, k: (k, j))]
    args = [a, b]
    if has_add:
        in_specs.append(pl.BlockSpec((tm, tn), lambda i, j, k: (i, j)))
        args.append(add)
    return pl.pallas_call(
        body, name=name, grid=(M // tm, N // tn, nk), in_specs=in_specs,
        out_specs=pl.BlockSpec((tm, tn), lambda i, j, k: (i, j)),
        out_shape=jax.ShapeDtypeStruct((M, N), out_dtype),
        scratch_shapes=[pltpu.VMEM((tm, tn), F32)], compiler_params=_cparams(VMEM_BIG_MB))(*args)


def _row_spec(tm, w, cb=0):
    return pl.BlockSpec((tm, w), lambda i: (i, cb))


def _const_spec(arr):
    nd = arr.ndim
    return pl.BlockSpec(arr.shape, lambda i: (0,) * nd)


def _rms_fwd(x, g, name):
    L, D = x.shape
    tm = min(512, L)

    def body(x_ref, g_ref, o_ref):
        xv = x_ref[...]
        r = lax.rsqrt(jnp.mean(xv * xv, axis=-1, keepdims=True) + EPS)
        o_ref[...] = (xv * r * g_ref[...]).astype(BF16)

    return pl.pallas_call(
        body, name=name, grid=(L // tm,), in_specs=[_row_spec(tm, D), _const_spec(g)],
        out_specs=_row_spec(tm, D), out_shape=jax.ShapeDtypeStruct((L, D), BF16))(x, g)


def _rms_bwd(dh, x, g, dres, name):
    L, D = x.shape
    tm = min(512, L)

    def body(dh_ref, x_ref, g_ref, dres_ref, dx_ref, dg_ref):
        xv = x_ref[...]
        dy = dh_ref[...]
        r = lax.rsqrt(jnp.mean(xv * xv, axis=-1, keepdims=True) + EPS)
        dyg = dy * g_ref[...]
        dx_ref[...] = dres_ref[...] + r * dyg - xv * (r * r * r) * jnp.mean(dyg * xv, axis=-1, keepdims=True)
        part = jnp.sum(dy * xv * r, axis=0, keepdims=True)

        @pl.when(pl.program_id(0) == 0)
        def _():
            dg_ref[...] = part

        @pl.when(pl.program_id(0) > 0)
        def _():
            dg_ref[...] += part

    return pl.pallas_call(
        body, name=name, grid=(L // tm,),
        in_specs=[_row_spec(tm, D), _row_spec(tm, D), _const_spec(g), _row_spec(tm, D)],
        out_specs=[_row_spec(tm, D), pl.BlockSpec((1, D), lambda i: (0, 0))],
        out_shape=[jax.ShapeDtypeStruct((L, D), F32), jax.ShapeDtypeStruct((1, D), F32)])(dh, x, g, dres)


def _loss_head(x, g, target, name):
    L, D = x.shape
    tm = min(512, L)

    def body(x_ref, g_ref, t_ref, loss_ref, dx_ref, dg_ref):
        xv = x_ref[...]
        gv = g_ref[...]
        r = lax.rsqrt(jnp.mean(xv * xv, axis=-1, keepdims=True) + EPS)
        xr = xv * r
        err = xr * gv - t_ref[...]
        lp = 0.5 * jnp.sum(jnp.mean(err * err, axis=-1, keepdims=True))
        dy = err * (1.0 / D)
        dyg = dy * gv
        dx_ref[...] = r * dyg - xv * (r * r * r) * jnp.mean(dyg * xv, axis=-1, keepdims=True)
        part = jnp.sum(dy * xr, axis=0, keepdims=True)

        @pl.when(pl.program_id(0) == 0)
        def _():
            dg_ref[...] = part
            loss_ref[...] = jnp.zeros(loss_ref.shape, F32) + lp

        @pl.when(pl.program_id(0) > 0)
        def _():
            dg_ref[...] += part
            loss_ref[...] += lp

    return pl.pallas_call(
        body, name=name, grid=(L // tm,),
        in_specs=[_row_spec(tm, D), _const_spec(g), _row_spec(tm, D)],
        out_specs=[pl.BlockSpec((8, LANES), lambda i: (0, 0)), _row_spec(tm, D), pl.BlockSpec((1, D), lambda i: (0, 0))],
        out_shape=[jax.ShapeDtypeStruct((8, LANES), F32), jax.ShapeDtypeStruct((L, D), F32),
                   jax.ShapeDtypeStruct((1, D), F32)])(x, g, target)


def _mla_prep(p, qg, kvg, wuq, wukv, tabs, name):
    L = p.shape[0]
    tm = min(512, L)
    scale = (MLA_NOPE + MLA_ROPE) ** -0.5
    tc, ta, tb = tabs

    def body(p_ref, qg_ref, kvg_ref, wuq_ref, wukv_ref, c_ref, a_ref, b_ref, q_ref, k_ref, v_ref, cqn_ref, ckvn_ref):
        c, a, b = c_ref[...], a_ref[...], b_ref[...]
        cq = p_ref[:, 0:Q_LORA].astype(F32)
        ckv = p_ref[:, Q_LORA:Q_LORA + KV_LORA].astype(F32)
        kr = p_ref[:, 640:768].astype(F32)
        cqn = (cq * lax.rsqrt(jnp.mean(cq * cq, axis=-1, keepdims=True) + EPS) * qg_ref[...]).astype(BF16)
        ckvn = (ckv * lax.rsqrt(jnp.mean(ckv * ckv, axis=-1, keepdims=True) + EPS) * kvg_ref[...]).astype(BF16)
        cqn_ref[...] = cqn
        ckvn_ref[...] = ckvn
        q = jnp.dot(cqn, wuq_ref[...], preferred_element_type=F32)
        kv = jnp.dot(ckvn, wukv_ref[...], preferred_element_type=F32)
        krr = _rope(kr, c, a, b, MLA_ROPE // 2)
        for h in range(MLA_HEADS):
            sl = slice(h * LANES, (h + 1) * LANES)
            q_ref[:, sl] = (_rope(q[:, sl], c, a, b, MLA_ROPE // 2) * (scale * LOG2E)).astype(BF16)
            k_ref[:, sl] = (kv[:, sl] + krr).astype(BF16)
        v_ref[...] = kv[:, 1024:1536].astype(BF16)

    return pl.pallas_call(
        body, name=name, grid=(L // tm,),
        in_specs=[_row_spec(tm, P_MLA, 0), _const_spec(qg), _const_spec(kvg), _const_spec(wuq), _const_spec(wukv),
                  _row_spec(tm, LANES), _row_spec(tm, LANES), _row_spec(tm, LANES)],
        out_specs=[_row_spec(tm, 1024), _row_spec(tm, 1024), _row_spec(tm, HEAD_W), _row_spec(tm, Q_LORA),
                   _row_spec(tm, KV_LORA)],
        out_shape=[jax.ShapeDtypeStruct((L, 1024), BF16), jax.ShapeDtypeStruct((L, 1024), BF16),
                   jax.ShapeDtypeStruct((L, HEAD_W), BF16), jax.ShapeDtypeStruct((L, Q_LORA), BF16),
                   jax.ShapeDtypeStruct((L, KV_LORA), BF16)],
        compiler_params=_cparams(VMEM_BIG_MB))(p, qg, kvg, wuq, wukv, tc, ta, tb)


def _mla_prep_bwd(dq, dk, dv, p, qg, kvg, wuq_t, wukv_t, tabs_t, name):
    L = p.shape[0]
    tm = min(512, L)
    scale = (MLA_NOPE + MLA_ROPE) ** -0.5
    tc, ta, tb = tabs_t

    def body(dq_ref, dk_ref, dv_ref, p_ref, qg_ref, kvg_ref, wuqt_ref, wukvt_ref, c_ref, a_ref, b_ref,
             dp_ref, dqp_ref, dkv_ref, dqg_ref, dkvg_ref):
        c, a, b = c_ref[...], a_ref[...], b_ref[...]
        dkr = jnp.zeros((tm, LANES), F32)
        for h in range(MLA_HEADS):
            sl = slice(h * LANES, (h + 1) * LANES)
            dqp_ref[:, sl] = (_rope(dq_ref[:, sl], c, a, b, MLA_ROPE // 2) * scale).astype(BF16)
            dkh = dk_ref[:, sl] * LN2
            dkv_ref[:, sl] = dkh.astype(BF16)
            dkr = dkr + dkh
        dkv_ref[:, 1024:1536] = dv_ref[...].astype(BF16)
        lane = lax.broadcasted_iota(jnp.int32, (tm, LANES), 1)
        dkr = jnp.where((lane >= MLA_NOPE) & (lane < MLA_NOPE + MLA_ROPE), _rope(dkr, c, a, b, MLA_ROPE // 2), 0.0)

        d_cqn = jnp.dot(dqp_ref[...], wuqt_ref[...], preferred_element_type=F32)
        d_ckvn = jnp.dot(dkv_ref[...], wukvt_ref[...], preferred_element_type=F32)

        def norm_bwd(xv, gv, dy):
            r = lax.rsqrt(jnp.mean(xv * xv, axis=-1, keepdims=True) + EPS)
            dyg = dy * gv
            dx = r * dyg - xv * (r * r * r) * jnp.mean(dyg * xv, axis=-1, keepdims=True)
            return dx, jnp.sum(dy * xv * r, axis=0, keepdims=True)

        d_cq, dqg = norm_bwd(p_ref[:, 0:Q_LORA].astype(F32), qg_ref[...], d_cqn)
        d_ckv, dkvg = norm_bwd(p_ref[:, Q_LORA:Q_LORA + KV_LORA].astype(F32), kvg_ref[...], d_ckvn)
        dp_ref[:, 0:Q_LORA] = d_cq.astype(BF16)
        dp_ref[:, Q_LORA:Q_LORA + KV_LORA] = d_ckv.astype(BF16)
        dp_ref[:, 640:768] = dkr.astype(BF16)
        dp_ref[:, 768:1024] = jnp.zeros((tm, 256), BF16)

        @pl.when(pl.program_id(0) == 0)
        def _():
            dqg_ref[...] = dqg
            dkvg_ref[...] = dkvg

        @pl.when(pl.program_id(0) > 0)
        def _():
            dqg_ref[...] += dqg
            dkvg_ref[...] += dkvg

    return pl.pallas_call(
        body, name=name, grid=(L // tm,),
        in_specs=[_row_spec(tm, 1024), _row_spec(tm, 1024), _row_spec(tm, HEAD_W), _row_spec(tm, P_MLA, 0),
                  _const_spec(qg), _const_spec(kvg), _const_spec(wuq_t), _const_spec(wukv_t),
                  _row_spec(tm, LANES), _row_spec(tm, LANES), _row_spec(tm, LANES)],
        out_specs=[_row_spec(tm, P_MLA), _row_spec(tm, 1024), _row_spec(tm, 1536),
                   pl.BlockSpec((1, Q_LORA), lambda i: (0, 0)), pl.BlockSpec((1, KV_LORA), lambda i: (0, 0))],
        out_shape=[jax.ShapeDtypeStruct((L, P_MLA), BF16), jax.ShapeDtypeStruct((L, 1024), BF16),
                   jax.ShapeDtypeStruct((L, 1536), BF16), jax.ShapeDtypeStruct((1, Q_LORA), F32),
                   jax.ShapeDtypeStruct((1, KV_LORA), F32)],
        compiler_params=_cparams(VMEM_BIG_MB))(dq, dk, dv, p, qg, kvg, wuq_t, wukv_t, tc, ta, tb)


def _dil_prep(p, tabs, name):
    L = p.shape[0]
    tm = min(512, L)
    tc, ta, tb = tabs
    scale = DIL_HD ** -0.5

    def body(*refs):
        ins, (c_ref, a_ref, b_ref), outs = refs[:9], refs[9:12], refs[12:]
        c, a, b = c_ref[...], a_ref[...], b_ref[...]
        for n in range(9):
            t = n % 3
            for cb in range(HEAD_W // LANES):
                sl = slice(cb * LANES, (cb + 1) * LANES)
                xv = ins[n][:, sl].astype(F32)
                if t == 0:
                    xv = _rope(xv, c, a, b, ROT_DIM // 2) * (scale * LOG2E)
                elif t == 1:
                    xv = _rope(xv, c, a, b, ROT_DIM // 2)
                outs[n][:, sl] = xv.astype(BF16)

    in_specs = [_row_spec(tm, HEAD_W, P_DIL0 // HEAD_W + n) for n in range(9)] + [_row_spec(tm, LANES)] * 3
    return pl.pallas_call(
        body, name=name, grid=(L // tm,), in_specs=in_specs,
        out_specs=[_row_spec(tm, HEAD_W)] * 9,
        out_shape=[jax.ShapeDtypeStruct((L, HEAD_W), BF16)] * 9)(*([p] * 9), tc, ta, tb)


def _dil_prep_bwd(grads, tabs_t, name):
    L = grads[0].shape[0]
    tm = min(512, L)
    tc, ta, tb = tabs_t
    scale = DIL_HD ** -0.5

    def body(*refs):
        ins, (c_ref, a_ref, b_ref), o_ref = refs[:9], refs[9:12], refs[12]
        c, a, b = c_ref[...], a_ref[...], b_ref[...]
        for n in range(9):
            t = n % 3
            for cb in range(HEAD_W // LANES):
                sl = slice(cb * LANES, (cb + 1) * LANES)
                xv = ins[n][:, sl]
                if t == 0:
                    xv = _rope(xv, c, a, b, ROT_DIM // 2) * scale
                elif t == 1:
                    xv = _rope(xv, c, a, b, ROT_DIM // 2) * LN2
                o_ref[:, n * HEAD_W + cb * LANES:n * HEAD_W + (cb + 1) * LANES] = xv.astype(BF16)

    return pl.pallas_call(
        body, name=name, grid=(L // tm,), in_specs=[_row_spec(tm, HEAD_W)] * 9 + [_row_spec(tm, LANES)] * 3,
        out_specs=_row_spec(tm, 9 * HEAD_W), out_shape=jax.ShapeDtypeStruct((L, 9 * HEAD_W), BF16),
        compiler_params=_cparams(VMEM_BIG_MB))(*grads, tc, ta, tb)


def _merge_gate(oa, p, o_g, lse_g, name):
    L = oa.shape[0]
    tm = min(512, L)

    def body(oa_ref, ga_ref, gb_ref, o1, o2, o3, l1, l2, l3, ab_ref, bm_ref, lt_ref):
        la, lb, lc = l1[...], l2[...], l3[...]
        m = jnp.maximum(jnp.maximum(la, lb), lc)
        ea, eb, ec = jnp.exp2(la - m), jnp.exp2(lb - m), jnp.exp2(lc - m)
        den = ea + eb + ec
        bm = (ea * o1[...] + eb * o2[...] + ec * o3[...]) / den
        bm_ref[...] = bm
        lt_ref[...] = m + jnp.log2(den)
        ga, gb = ga_ref[...].astype(F32), gb_ref[...].astype(F32)
        ab_ref[:, 0:HEAD_W] = (oa_ref[...] * (ga * _sigmoid(ga))).astype(BF16)
        ab_ref[:, HEAD_W:2 * HEAD_W] = (bm * (gb * _sigmoid(gb))).astype(BF16)

    w = _row_spec(tm, HEAD_W)
    return pl.pallas_call(
        body, name=name, grid=(L // tm,),
        in_specs=[w, _row_spec(tm, HEAD_W, 2), _row_spec(tm, HEAD_W, 3), w, w, w, w, w, w],
        out_specs=[_row_spec(tm, 2 * HEAD_W), w, w],
        out_shape=[jax.ShapeDtypeStruct((L, 2 * HEAD_W), BF16), jax.ShapeDtypeStruct((L, HEAD_W), F32),
                   jax.ShapeDtypeStruct((L, HEAD_W), F32)])(oa, p, p, *o_g, *lse_g)


def _gate_bwd(dab, p, oa, bm, name):
    L = oa.shape[0]
    tm = min(512, L)

    def body(da_ref, db_ref, ga_ref, gb_ref, oa_ref, bm_ref, doa_ref, dbm_ref, Da_ref, Db_ref, dg_ref):
        def one(d, g, o, do_ref, D_ref, col):
            sg = _sigmoid(g)
            do = d * (g * sg)
            do_ref[...] = do.astype(BF16)
            dg_ref[:, col:col + HEAD_W] = (d * o * (sg * (1.0 + g * (1.0 - sg)))).astype(BF16)
            prod = do * o
            for cb in range(HEAD_W // LANES):
                sl = slice(cb * LANES, (cb + 1) * LANES)
                D_ref[:, sl] = _headsum_bcast(prod[:, sl])

        one(da_ref[...], ga_ref[...].astype(F32), oa_ref[...], doa_ref, Da_ref, 0)
        one(db_ref[...], gb_ref[...].astype(F32), bm_ref[...], dbm_ref, Db_ref, HEAD_W)

    w = _row_spec(tm, HEAD_W)
    return pl.pallas_call(
        body, name=name, grid=(L // tm,),
        in_specs=[_row_spec(tm, HEAD_W, 0), _row_spec(tm, HEAD_W, 1), _row_spec(tm, HEAD_W, 2),
                  _row_spec(tm, HEAD_W, 3), w, w],
        out_specs=[w, w, w, w, _row_spec(tm, 2 * HEAD_W)],
        out_shape=[jax.ShapeDtypeStruct((L, HEAD_W), BF16), jax.ShapeDtypeStruct((L, HEAD_W), BF16),
                   jax.ShapeDtypeStruct((L, HEAD_W), F32), jax.ShapeDtypeStruct((L, HEAD_W), F32),
                   jax.ShapeDtypeStruct((L, 2 * HEAD_W), BF16)])(dab, dab, p, p, oa, bm)


NT = (((1,), (1,)), ((), ()))
TN = (((0,), (0,)), ((), ()))
NEG = -1e30


MLA_TQ = 512
MLA_TK = 2048
MLA_BWD_TK = 1024
LOG2E = 1.4426950408889634
LN2 = 0.6931471805599453


def _mla_fwd(q, k, v_t, name):
    L = q.shape[0]
    tq, tk = min(MLA_TQ, L), min(MLA_TK, L)
    nq, nk = L // tq, L // tk
    npair = MLA_HEADS // 2

    def body(q_ref, k_ref, vt_ref, o_ref, lse_ref, m0, l0, a0, m1, l1, a1):
        j = pl.program_id(2)
        stats = ((m0, l0, a0), (m1, l1, a1))

        @pl.when(j == 0)
        def _():
            for m_sc, l_sc, acc_sc in stats:
                m_sc[...] = jnp.full(m_sc.shape, NEG, F32)
                l_sc[...] = jnp.zeros(l_sc.shape, F32)
                acc_sc[...] = jnp.zeros(acc_sc.shape, F32)

        s_ts = [lax.dot_general(k_ref[:, hh * LANES:(hh + 1) * LANES], q_ref[:, hh * LANES:(hh + 1) * LANES], NT,
                                preferred_element_type=F32) for hh in range(2)]
        for hh in range(2):
            m_sc, l_sc, acc_sc = stats[hh]
            s_t = s_ts[hh]
            m_prev = m_sc[...]
            m_new = jnp.maximum(m_prev, jnp.max(s_t, axis=0, keepdims=True))
            alpha = jnp.exp2(m_prev - m_new)
            p_t = jnp.exp2(s_t - m_new)
            l_sc[...] = alpha * l_sc[...] + jnp.sum(p_t, axis=0, keepdims=True)
            m_sc[...] = m_new
            pv = jnp.dot(vt_ref[hh * DIL_HD:(hh + 1) * DIL_HD, :], p_t.astype(BF16),
                         preferred_element_type=F32)
            acc_sc[...] = alpha * acc_sc[...] + pv

        @pl.when(j == nk - 1)
        def _():
            o_ref[...] = jnp.concatenate([a0[...] / l0[...], a1[...] / l1[...]], axis=0).T
            lse_ref[...] = jnp.concatenate([m0[...] + jnp.log2(l0[...]), m1[...] + jnp.log2(l1[...])], axis=0)

    stat = [pltpu.VMEM((1, tq), F32), pltpu.VMEM((1, tq), F32), pltpu.VMEM((DIL_HD, tq), F32)]
    return pl.pallas_call(
        body, name=name, grid=(npair, nq, nk),
        in_specs=[pl.BlockSpec((tq, 2 * LANES), lambda pr, i, j: (i, pr)),
                  pl.BlockSpec((tk, 2 * LANES), lambda pr, i, j: (j, pr)),
                  pl.BlockSpec((LANES, tk), lambda pr, i, j: (pr, j))],
        out_specs=[pl.BlockSpec((tq, LANES), lambda pr, i, j: (i, pr)),
                   pl.BlockSpec((None, 2, tq), lambda pr, i, j: (pr, 0, i))],
        out_shape=[jax.ShapeDtypeStruct((L, HEAD_W), F32), jax.ShapeDtypeStruct((npair, 2, L), F32)],
        scratch_shapes=stat + stat, compiler_params=_cparams(VMEM_BIG_MB))(q, k, v_t)


def _mla_bwd(q, k, v, do, lse_rows, d_rows, name):
    L = q.shape[0]
    tq, tk = min(512, L), min(MLA_BWD_TK, L)
    nq, nk = L // tq, L // tk
    npair = MLA_HEADS // 2

    def body(q_ref, k_ref, v_ref, do_ref, lse_ref, d_ref, dq_ref, dk_ref, dv_ref):
        j, i = pl.program_id(1), pl.program_id(2)

        @pl.when((j == 0) & (i == 0))
        def _():
            dq_ref[...] = jnp.zeros(dq_ref.shape, F32)

        @pl.when(i == 0)
        def _():
            dk_ref[...] = jnp.zeros(dk_ref.shape, F32)
            dv_ref[...] = jnp.zeros(dv_ref.shape, F32)

        first = lax.broadcasted_iota(jnp.int32, (tq, LANES), 1) < DIL_HD
        dov = do_ref[...]
        vv = v_ref[...]
        rows = pl.ds(pl.multiple_of(i * tq, tq), tq)
        for hh in range(2):
            sl = slice(hh * LANES, (hh + 1) * LANES)
            qh, kh = q_ref[:, sl], k_ref[:, sl]
            do_h = jnp.where(first if hh == 0 else ~first, dov, jnp.zeros_like(dov))
            s_t = lax.dot_general(kh, qh, NT, preferred_element_type=F32)
            p_t = jnp.exp2(s_t - lse_ref[hh:hh + 1, :])
            dv_ref[...] += jnp.dot(p_t.astype(BF16), do_h, preferred_element_type=F32)
            dp_t = lax.dot_general(vv, do_h, NT, preferred_element_type=F32)
            ds_t = (p_t * (dp_t - d_ref[hh:hh + 1, :])).astype(BF16)
            dk_ref[:, sl] += jnp.dot(ds_t, qh, preferred_element_type=F32)
            dq_ref[rows, sl] += lax.dot_general(ds_t, kh, TN, preferred_element_type=F32)

    return pl.pallas_call(
        body, name=name, grid=(npair, nk, nq),
        in_specs=[pl.BlockSpec((tq, 2 * LANES), lambda pr, j, i: (i, pr)),
                  pl.BlockSpec((tk, 2 * LANES), lambda pr, j, i: (j, pr)),
                  pl.BlockSpec((tk, LANES), lambda pr, j, i: (j, pr)),
                  pl.BlockSpec((tq, LANES), lambda pr, j, i: (i, pr)),
                  pl.BlockSpec((None, 2, tq), lambda pr, j, i: (pr, 0, i)),
                  pl.BlockSpec((None, 2, tq), lambda pr, j, i: (pr, 0, i))],
        out_specs=[pl.BlockSpec((L, 2 * LANES), lambda pr, j, i: (0, pr)),
                   pl.BlockSpec((tk, 2 * LANES), lambda pr, j, i: (j, pr)),
                   pl.BlockSpec((tk, LANES), lambda pr, j, i: (j, pr))],
        out_shape=[jax.ShapeDtypeStruct((L, 1024), F32), jax.ShapeDtypeStruct((L, 1024), F32),
                   jax.ShapeDtypeStruct((L, HEAD_W), F32)],
        compiler_params=_cparams(VMEM_BIG_MB))(q, k, v, do, lse_rows, d_rows)


def _dil_tiles(ld):
    tq = min(512, ld)
    kw = min(768, ld)
    kb = min(LANES, ld)
    return tq, kw, kb


def _dil_fwd(q, k, v_t4, name):
    d, ld, _ = q.shape
    tq, kw, kb = _dil_tiles(ld)
    nq = ld // tq
    npair = HEAD_W // LANES

    def body(q_ref, k_ref, vt_ref, o_ref, lse_ref):
        i = pl.program_id(2)
        a0 = i * tq
        start = pl.multiple_of(jnp.clip(a0 - LANES, 0, ld - kw), kb)
        kk0 = start // kb
        kwin = k_ref[pl.ds(start, kw), :]
        kpos = start + lax.broadcasted_iota(jnp.int32, (kw, tq), 0)
        qpos = a0 + lax.broadcasted_iota(jnp.int32, (kw, tq), 1)
        valid = jnp.abs(qpos - kpos) <= DIL_HALF
        first = lax.broadcasted_iota(jnp.int32, (tq, LANES), 1) < DIL_HD
        qv = q_ref[...]
        outs, lses = [], []
        for hh in range(2):
            qh = jnp.where(first if hh == 0 else ~first, qv, jnp.zeros_like(qv))
            s_t = jnp.where(valid, lax.dot_general(kwin, qh, NT, preferred_element_type=F32), NEG)
            m = jnp.max(s_t, axis=0, keepdims=True)
            p32 = jnp.exp2(s_t - m)
            l = jnp.sum(p32, axis=0, keepdims=True)
            p_t = p32.astype(BF16)
            pv = jnp.zeros((DIL_HD, tq), F32)
            for b in range(kw // kb):
                pv = pv + jnp.dot(vt_ref[kk0 + b, hh * DIL_HD:(hh + 1) * DIL_HD, :], p_t[b * kb:(b + 1) * kb, :],
                                  preferred_element_type=F32)
            outs.append(pv / l)
            lses.append(jnp.broadcast_to(m + jnp.log2(l), (DIL_HD, tq)))
        o_ref[...] = jnp.concatenate(outs, axis=0).T
        lse_ref[...] = jnp.concatenate(lses, axis=0).T

    blk = pl.BlockSpec((None, tq, LANES), lambda r, pr, i: (r, i, pr))
    full = pl.BlockSpec((None, ld, LANES), lambda r, pr, i: (r, 0, pr))
    vspec = pl.BlockSpec((None, ld // kb, LANES, kb), lambda r, pr, i: (r, 0, pr, 0))
    return pl.pallas_call(
        body, name=name, grid=(d, npair, nq), in_specs=[blk, full, vspec], out_specs=[blk, blk],
        out_shape=[jax.ShapeDtypeStruct((d, ld, HEAD_W), F32), jax.ShapeDtypeStruct((d, ld, HEAD_W), F32)],
        compiler_params=_cparams(VMEM_BIG_MB))(q, k, v_t4)


def _dil_bwd(q, k, v, do, lse_rows, d_rows, name):
    d, ld, _ = q.shape
    tq, kw, kb = _dil_tiles(ld)
    nq = ld // tq
    npair = HEAD_W // LANES

    def body(q_ref, k_ref, v_ref, do_ref, lse_ref, d_ref, dq_ref, dk_ref, dv_ref):
        i = pl.program_id(2)

        @pl.when(i == 0)
        def _():
            dk_ref[...] = jnp.zeros(dk_ref.shape, F32)
            dv_ref[...] = jnp.zeros(dv_ref.shape, F32)

        a0 = i * tq
        start = pl.multiple_of(jnp.clip(a0 - LANES, 0, ld - kw), kb)
        win = pl.ds(start, kw)
        kwin = k_ref[win, :]
        vwin = v_ref[win, :]
        kpos = start + lax.broadcasted_iota(jnp.int32, (kw, tq), 0)
        qpos = a0 + lax.broadcasted_iota(jnp.int32, (kw, tq), 1)
        valid = jnp.abs(qpos - kpos) <= DIL_HALF
        first = lax.broadcasted_iota(jnp.int32, (tq, LANES), 1) < DIL_HD
        qv = q_ref[...]
        dov = do_ref[...]
        dqs = []
        for hh in range(2):
            sel = first if hh == 0 else ~first
            qh = jnp.where(sel, qv, jnp.zeros_like(qv))
            do_h = jnp.where(sel, dov, jnp.zeros_like(dov))
            s_t = lax.dot_general(kwin, qh, NT, preferred_element_type=F32)
            p_t = jnp.exp2(jnp.where(valid, s_t, NEG) - lse_ref[hh:hh + 1, :])
            dv_ref[win, :] += jnp.dot(p_t.astype(BF16), do_h, preferred_element_type=F32)
            dp_t = lax.dot_general(vwin, do_h, NT, preferred_element_type=F32)
            ds_t = (p_t * (dp_t - d_ref[hh:hh + 1, :])).astype(BF16)
            dk_ref[win, :] += jnp.dot(ds_t, qh, preferred_element_type=F32)
            dqs.append(lax.dot_general(ds_t, kwin, TN, preferred_element_type=F32))
        dq_ref[...] = jnp.where(first, dqs[0], dqs[1])

    blk = pl.BlockSpec((None, tq, LANES), lambda r, pr, i: (r, i, pr))
    full = pl.BlockSpec((None, ld, LANES), lambda r, pr, i: (r, 0, pr))
    rowspec = pl.BlockSpec((None, None, 2, tq), lambda r, pr, i: (r, pr, 0, i))
    return pl.pallas_call(
        body, name=name, grid=(d, npair, nq), in_specs=[blk, full, full, blk, rowspec, rowspec],
        out_specs=[blk, full, full],
        out_shape=[jax.ShapeDtypeStruct((d, ld, HEAD_W), F32)] * 3,
        compiler_params=_cparams(VMEM_BIG_MB))(q, k, v, do, lse_rows, d_rows)


TILE_BYTES = 1 << 21


def _row_tile(rows, cols, budget=TILE_BYTES):
    for parts in range(1, rows + 1):
        tr = rows // parts
        if rows % parts == 0 and tr % 8 == 0 and tr * cols * 4 <= budget:
            return tr
    return rows


def _add2(a, b, name, out_dtype):
    n, rows, cols = a.shape
    tr = _row_tile(rows, cols)

    def body(a_ref, b_ref, o_ref):
        o_ref[...] = (a_ref[...] + b_ref[...]).astype(out_dtype)

    spec = pl.BlockSpec((None, tr, cols), lambda t, i: (t, i, 0))
    return pl.pallas_call(body, name=name, grid=(n, rows // tr), in_specs=[spec, spec], out_specs=spec,
                          out_shape=jax.ShapeDtypeStruct(a.shape, out_dtype))(a, b)


def _add4_ordered(a, name):
    _, rows, cols = a.shape
    tr = _row_tile(rows, cols, TILE_BYTES // 4)

    def body(a_ref, o_ref):
        o_ref[...] = ((a_ref[0].astype(F32) + a_ref[1].astype(F32)) + a_ref[2].astype(F32)) + a_ref[3].astype(F32)

    return pl.pallas_call(
        body, name=name, grid=(rows // tr,), in_specs=[pl.BlockSpec((4, tr, cols), lambda i: (0, i, 0))],
        out_specs=pl.BlockSpec((tr, cols), lambda i: (i, 0)),
        out_shape=jax.ShapeDtypeStruct((rows, cols), F32))(a)


def _adamw(w, g, m, v, name):
    rows, cols = w.shape
    tr = _row_tile(rows, cols)
    bc1 = 1.0 - ADAM_B1 ** ADAM_STEP
    bc2 = 1.0 - ADAM_B2 ** ADAM_STEP

    def body(w_ref, g_ref, m_ref, v_ref, d_ref, nm_ref, nv_ref):
        gv = g_ref[...]
        nm = ADAM_B1 * m_ref[...] + (1.0 - ADAM_B1) * gv
        nv = ADAM_B2 * v_ref[...] + (1.0 - ADAM_B2) * (gv * gv)
        d_ref[...] = -ADAM_LR * ((nm / bc1) / (jnp.sqrt(nv / bc2) + ADAM_EPS) + ADAM_WD * w_ref[...])
        nm_ref[...] = nm
        nv_ref[...] = nv

    spec = pl.BlockSpec((tr, cols), lambda i: (i, 0))
    return pl.pallas_call(body, name=name, grid=(rows // tr,), in_specs=[spec] * 4, out_specs=[spec] * 3,
                          out_shape=[jax.ShapeDtypeStruct(w.shape, F32)] * 3,
                          compiler_params=_cparams(VMEM_BIG_MB))(w, g, m, v)


ANY = pl.BlockSpec(memory_space=pl.ANY)


def _place():
    return lax.axis_index("x"), lax.axis_index("y"), lax.axis_index("c")


def _rcopy(send_sems, recv_sems, n, src, dst, to):
    return pltpu.make_async_remote_copy(src_ref=src, dst_ref=dst, send_sem=send_sems.at[n], recv_sem=recv_sems.at[n],
                                        device_id=to, device_id_type=MESH)


def _allgather_weights(shards):
    na = len(shards)

    def body(*refs):
        w_refs, g_refs = refs[:na], refs[na:2 * na]
        send_sems, recv_sems, local_sems = refs[2 * na:]
        x, y, c = _place()
        s = 2 * x + y
        chips = [(1 - x, y), (x, 1 - y), (1 - x, 1 - y)]

        def half(a, shard, h):
            hr = shards[a].shape[0] // 2
            return g_refs[a].at[shard, pl.ds(h * hr, hr), :]

        started = []
        for a in range(na):
            hr = shards[a].shape[0] // 2
            mine = pltpu.make_async_copy(w_refs[a], g_refs[a].at[s], local_sems.at[a])
            mine.start()
            started.append(mine)
        sends = []
        for a in range(na):
            hr = shards[a].shape[0] // 2
            for n, (cx, cy) in enumerate(chips):
                cp = _rcopy(send_sems, recv_sems, 6 * a + n, w_refs[a].at[pl.ds(c * hr, hr), :], half(a, s, c),
                            (cx, cy, c))
                cp.start()
                sends.append(cp)
        for a in range(na):
            for n, (cx, cy) in enumerate(chips):
                sj = 2 * cx + cy
                _rcopy(send_sems, recv_sems, 6 * a + n, half(a, sj, c), half(a, sj, c), (cx, cy, c)).wait_recv()
                fw = _rcopy(send_sems, recv_sems, 6 * a + 3 + n, half(a, sj, c), half(a, sj, c), (x, y, 1 - c))
                fw.start()
                sends.append(fw)
        for a in range(na):
            for n, (cx, cy) in enumerate(chips):
                sj = 2 * cx + cy
                _rcopy(send_sems, recv_sems, 6 * a + 3 + n, half(a, sj, 1 - c), half(a, sj, 1 - c),
                       (x, y, 1 - c)).wait_recv()
        for cp in sends:
            cp.wait_send()
        for mine in started:
            mine.wait()

    return pl.pallas_call(
        body, name="allgather_weights", in_specs=[ANY] * na, out_specs=[ANY] * na,
        out_shape=[jax.ShapeDtypeStruct((N_SHARD,) + t.shape, t.dtype) for t in shards],
        scratch_shapes=[pltpu.SemaphoreType.DMA((6 * na,)), pltpu.SemaphoreType.DMA((6 * na,)),
                        pltpu.SemaphoreType.DMA((na,))])(*shards)


def _sibling_send_halves(gs):
    na = len(gs)

    def body(*refs):
        g_refs, o_refs = refs[:na], refs[na:2 * na]
        send_sems, recv_sems = refs[2 * na:]
        x, y, c = _place()
        cps = []
        for a in range(na):
            for t in range(N_SHARD):
                cp = _rcopy(send_sems, recv_sems, N_SHARD * a + t, g_refs[a].at[t, 1 - c], o_refs[a].at[t],
                            (x, y, 1 - c))
                cp.start()
                cps.append(cp)
        for cp in cps:
            cp.wait()

    return pl.pallas_call(
        body, name="grad_sibling_exchange", in_specs=[ANY] * na, out_specs=[ANY] * na,
        out_shape=[jax.ShapeDtypeStruct((N_SHARD,) + g.shape[2:], g.dtype) for g in gs],
        scratch_shapes=[pltpu.SemaphoreType.DMA((N_SHARD * na,)), pltpu.SemaphoreType.DMA((N_SHARD * na,))])(*gs)


def _chip_scatter(parts):
    na = len(parts)

    def body(*refs):
        a_refs, o_refs = refs[:na], refs[na:2 * na]
        send_sems, recv_sems, local_sems = refs[2 * na:]
        x, y, c = _place()
        s = 2 * x + y
        chips = [(1 - x, y), (x, 1 - y), (1 - x, 1 - y)]
        started, cps = [], []
        for a in range(na):
            mine = pltpu.make_async_copy(a_refs[a].at[s], o_refs[a].at[s], local_sems.at[a])
            mine.start()
            started.append(mine)
            for n, (cx, cy) in enumerate(chips):
                cp = _rcopy(send_sems, recv_sems, 3 * a + n, a_refs[a].at[2 * cx + cy], o_refs[a].at[s], (cx, cy, c))
                cp.start()
                cps.append(cp)
        for a in range(na):
            for n, (cx, cy) in enumerate(chips):
                sj = 2 * cx + cy
                _rcopy(send_sems, recv_sems, 3 * a + n, a_refs[a].at[sj], o_refs[a].at[sj], (cx, cy, c)).wait_recv()
        for cp in cps:
            cp.wait_send()
        for mine in started:
            mine.wait()

    return pl.pallas_call(
        body, name="grad_chip_scatter", in_specs=[ANY] * na, out_specs=[ANY] * na,
        out_shape=[jax.ShapeDtypeStruct(t.shape, t.dtype) for t in parts],
        scratch_shapes=[pltpu.SemaphoreType.DMA((3 * na,)), pltpu.SemaphoreType.DMA((3 * na,)),
                        pltpu.SemaphoreType.DMA((na,))])(*parts)


def _sibling_swap(rs):
    na = len(rs)

    def body(*refs):
        r_refs, o_refs = refs[:na], refs[na:2 * na]
        send_sems, recv_sems = refs[2 * na:]
        x, y, c = _place()
        cps = []
        for a in range(na):
            cp = _rcopy(send_sems, recv_sems, a, r_refs[a], o_refs[a], (x, y, 1 - c))
            cp.start()
            cps.append(cp)
        for cp in cps:
            cp.wait()

    return pl.pallas_call(
        body, name="grad_sibling_swap", in_specs=[ANY] * na, out_specs=[ANY] * na,
        out_shape=[jax.ShapeDtypeStruct(t.shape, t.dtype) for t in rs],
        scratch_shapes=[pltpu.SemaphoreType.DMA((na,)), pltpu.SemaphoreType.DMA((na,))])(*rs)


def _pack_small(norm_g, q_norm_g, kv_norm_g, final_g):
    flat = jnp.concatenate([norm_g.reshape(-1), q_norm_g.reshape(-1), kv_norm_g.reshape(-1), final_g.reshape(-1),
                            jnp.zeros((SMALL_ROWS * LANES - N_SMALL,), F32)])
    return flat.reshape(SMALL_ROWS, LANES)


def _split_small(s):
    s = s.reshape(-1)
    o = 0
    out = []
    for n, shape in ((DEPTH * D_MODEL, (DEPTH, D_MODEL)), (DEPTH * Q_LORA, (DEPTH, Q_LORA)),
                     (DEPTH * KV_LORA, (DEPTH, KV_LORA)), (D_MODEL, (D_MODEL,))):
        out.append(s[o:o + n].reshape(shape))
        o += n
    return out


def _assemble_w_in(sh):
    z = lambda n: jnp.zeros(sh.shape[1:3] + (n,), sh.dtype)
    s0, s1, s2, s3 = sh[0], sh[1], sh[2], sh[3]
    return jnp.concatenate([s0[..., 0:640], z(64), s0[..., 640:672], z(32), z(256), s0[..., 672:1184],
                            s3[..., 1064:1576], s0[..., 1184:1576], s1, s2, s3[..., 0:1064]], axis=-1)


def _split_w_in_grad(d):
    sh0 = jnp.concatenate([d[..., 0:640], d[..., 704:736], d[..., 1024:1536], d[..., 2048:2440]], axis=-1)
    sh3 = jnp.concatenate([d[..., 5592:6656], d[..., 1536:2048]], axis=-1)
    return jnp.stack([sh0, d[..., 2440:4016], d[..., 4016:5592], sh3]).reshape(N_SHARD, DEPTH * D_MODEL, SHARD_COLS_IN)


def _col_shards(w):
    dl, r, cc = w.shape
    return w.reshape(dl, r, N_SHARD, cc // N_SHARD).transpose(2, 0, 1, 3).reshape(N_SHARD, dl * r, cc // N_SHARD)


def _from_col_shards(g, rows):
    cc = g.shape[-1]
    return g.reshape(N_SHARD, DEPTH, rows, cc).transpose(1, 2, 0, 3).reshape(DEPTH, rows, N_SHARD * cc)


def _pad_w_in(w):
    z = lambda n: jnp.zeros(w.shape[:-1] + (n,), w.dtype)
    return jnp.concatenate([w[..., 0:640], z(64), w[..., 640:672], z(32), z(256), w[..., 672:1184],
                            w[..., 5792:6304], w[..., 1184:5792]], axis=-1)


def _unpad_w_in(w):
    return jnp.concatenate([w[..., 0:640], w[..., 704:736], w[..., 1024:1536], w[..., 2048:6656],
                            w[..., 1536:2048]], axis=-1)


def _pad_w_uq(w):
    s = w.shape[:-1]
    w = w.reshape(s + (MLA_HEADS, 96))
    return jnp.pad(w, [(0, 0)] * (w.ndim - 1) + [(0, 32)]).reshape(s + (1024,))


def _unpad_w_uq(w):
    s = w.shape[:-1]
    return w.reshape(s + (MLA_HEADS, LANES))[..., :96].reshape(s + (768,))


def _pad_w_ukv(w):
    s = w.shape[:-1]
    w = w.reshape(s + (MLA_HEADS, 128))
    kpart = jnp.pad(w[..., :64], [(0, 0)] * (w.ndim - 1) + [(0, 64)]).reshape(s + (1024,))
    vpart = w[..., 64:].reshape(s + (512,))
    return jnp.concatenate([kpart, vpart], axis=-1)


def _unpad_w_ukv(w):
    s = w.shape[:-1]
    kpart = w[..., :1024].reshape(s + (MLA_HEADS, LANES))[..., :64]
    vpart = w[..., 1024:].reshape(s + (MLA_HEADS, 64))
    return jnp.concatenate([kpart, vpart], axis=-1).reshape(s + (1024,))


def _rope_tables(L, dim, lane_lo, period):
    half = dim // 2
    inv = 1.0 / (ROPE_THETA ** (jnp.arange(0, dim, 2, dtype=F32) / dim))
    ang = jnp.arange(L, dtype=F32)[:, None] * inv[None, :]
    cos, sin = jnp.cos(ang), jnp.sin(ang)
    one = lambda n: jnp.ones((L, n), F32)
    zero = lambda n: jnp.zeros((L, n), F32)
    rest = period - lane_lo - dim
    rep = LANES // period
    c = jnp.tile(jnp.concatenate([one(lane_lo), cos, cos, one(rest)], axis=1), (1, rep))
    a = jnp.tile(jnp.concatenate([zero(lane_lo), -sin, zero(half), zero(rest)], axis=1), (1, rep))
    b = jnp.tile(jnp.concatenate([zero(lane_lo + half), sin, zero(rest)], axis=1), (1, rep))
    return c, a, b


def _to_strided(t, d):
    L, w = t.shape
    return t.reshape(L // d, d, w).transpose(1, 0, 2)


def _from_strided(t):
    d, ld, w = t.shape
    return t.transpose(1, 0, 2).reshape(d * ld, w)


def _head_rows(t):
    return t[:, ::DIL_HD].T.reshape(MLA_HEADS // 2, 2, t.shape[0])


def _head_rows_strided(t, d):
    s = _to_strided(t[:, ::DIL_HD], d)
    return s.transpose(0, 2, 1).reshape(d, MLA_HEADS // 2, 2, s.shape[1])


def _local_grads(x, target, norm_g, w_in_p, q_norm_g, kv_norm_g, w_uq_p, w_ukv_p, w_out, final_g):
    L = x.shape[0]
    tabs_m = _rope_tables(L, MLA_ROPE, MLA_NOPE, LANES)
    tabs_d = _rope_tables(L, ROT_DIM, 0, DIL_HD)
    tabs_m_t = (tabs_m[0], -tabs_m[1], -tabs_m[2])
    tabs_d_t = (tabs_d[0], -tabs_d[1], -tabs_d[2])
    w_in_t = jnp.swapaxes(w_in_p, 1, 2)
    w_uq_t = jnp.swapaxes(w_uq_p, 1, 2)
    w_ukv_t = jnp.swapaxes(w_ukv_p, 1, 2)
    w_out_t = jnp.swapaxes(w_out, 1, 2)

    saved = []
    for l in range(DEPTH):
        h = _rms_fwd(x, norm_g[l:l + 1], "rms_fwd")
        p = _mm(h, w_in_p[l], tm=512, tn=3328, tk=1024, out_dtype=BF16, name="in_proj")
        q, k, v, cqn, ckvn = _mla_prep(p, q_norm_g[l:l + 1], kv_norm_g[l:l + 1], w_uq_p[l], w_ukv_p[l], tabs_m,
                                       "mla_prep")
        oa, lse_a = _mla_fwd(q, k, v.T, "mla_fwd")
        dil = _dil_prep(p, tabs_d, "dil_prep")
        dil_s, o_g, lse_g = [], [], []
        for g, (_, dd) in enumerate(DIL_PAIRS):
            qs, ks, vs = (_to_strided(t, dd) for t in dil[3 * g:3 * g + 3])
            ld = L // dd
            kb = _dil_tiles(ld)[2]
            v_t4 = vs.reshape(dd, ld // kb, kb, HEAD_W).transpose(0, 1, 3, 2)
            og, lg = _dil_fwd(qs, ks, v_t4, "dil_fwd_%d" % dd)
            dil_s.append((qs, ks, vs))
            o_g.append(_from_strided(og))
            lse_g.append(_from_strided(lg))
        ab, bm, lt = _merge_gate(oa, p, o_g, lse_g, "merge_gate")
        x_next = _mm(ab, w_out[l], tm=1024, tn=1024, tk=1024, out_dtype=F32, name="out_proj", add=x)
        saved.append((x, h, p, q, k, v, cqn, ckvn, oa, lse_a, dil_s, bm, lt, ab))
        x = x_next

    loss_b, dx, d_final = _loss_head(x, final_g[None, :], target, "loss_head")
    loss = loss_b[0, 0]

    d_norm, d_qn, d_kvn, d_win, d_wuq, d_wukv, d_wout = [], [], [], [], [], [], []
    for l in reversed(range(DEPTH)):
        x_l, h, p, q, k, v, cqn, ckvn, oa, lse_a, dil_s, bm, lt, ab = saved[l]
        dab = _mm(dx, w_out_t[l], tm=1024, tn=1024, tk=1024, out_dtype=F32, name="out_proj_dgrad")
        d_wout.append(_mm(ab.T, dx, tm=1024, tn=1024, tk=1024, out_dtype=F32, name="out_proj_wgrad"))
        doa, dbm, D_a, D_b, dgates = _gate_bwd(dab, p, oa, bm, "gate_bwd")
        dq, dk, dv = _mla_bwd(q, k, v, doa, lse_a, _head_rows(D_a), "mla_bwd")
        dp_mla, dq_pre, dkv, dqg, dkvg = _mla_prep_bwd(dq, dk, dv, p, q_norm_g[l:l + 1], kv_norm_g[l:l + 1],
                                                       w_uq_t[l], w_ukv_t[l], tabs_m_t, "mla_prep_bwd")
        d_wuq.append(_mm(cqn.T, dq_pre, tm=Q_LORA, tn=1024, tk=2048, out_dtype=F32, name="w_uq_wgrad"))
        d_wukv.append(_mm(ckvn.T, dkv, tm=KV_LORA, tn=1536, tk=2048, out_dtype=F32, name="w_ukv_wgrad"))
        dgr = []
        for g, (_, dd) in enumerate(DIL_PAIRS):
            qs, ks, vs = dil_s[g]
            dqs, dks, dvs = _dil_bwd(qs, ks, vs, _to_strided(dbm, dd), _head_rows_strided(lt, dd),
                                     _head_rows_strided(D_b, dd), "dil_bwd_%d" % dd)
            dgr += [_from_strided(dqs), _from_strided(dks), _from_strided(dvs)]
        dp_dil = _dil_prep_bwd(dgr, tabs_d_t, "dil_prep_bwd")
        dp = jnp.concatenate([dp_mla, dgates, dp_dil], axis=1)
        dh = _mm(dp, w_in_t[l], tm=1024, tn=1024, tk=1664, out_dtype=F32, name="in_proj_dgrad")
        d_win.append(_mm(h.T, dp, tm=1024, tn=1664, tk=1024, out_dtype=F32, name="in_proj_wgrad"))
        dx, dng = _rms_bwd(dh, x_l, norm_g[l:l + 1], dx, "rms_bwd")
        d_norm.append(dng[0])
        d_qn.append(dqg[0])
        d_kvn.append(dkvg[0])

    rev = lambda xs: jnp.stack(xs[::-1])
    return (loss, dx, rev(d_norm), rev(d_win), rev(d_qn), rev(d_kvn), rev(d_wuq), rev(d_wukv), rev(d_wout),
            d_final[0])


def kernel(x, norm_g, w_in, q_norm_g, kv_norm_g, w_uq, w_ukv, w_out, final_g, loss_target, m_norm_g, m_w_in, m_q_norm_g, m_kv_norm_g, m_w_uq, m_w_ukv, m_w_out, m_final_g, v_norm_g, v_w_in, v_q_norm_g, v_kv_norm_g, v_w_uq, v_w_ukv, v_w_out, v_final_g):
    c = lax.axis_index("c")

    def families(a_in, a_uq, a_ukv, a_out):
        return [t.reshape(shape) for t, shape in zip((a_in, a_uq, a_ukv, a_out), FAM_SHAPES)]

    g_in, g_uq, g_ukv, g_out = _allgather_weights([t.astype(BF16) for t in families(w_in, w_uq, w_ukv, w_out)])
    w_in_p = _assemble_w_in(g_in.reshape(N_SHARD, DEPTH, D_MODEL, SHARD_COLS_IN))
    w_uq_p = _pad_w_uq(_from_col_shards(g_uq, Q_LORA))
    w_ukv_p = _pad_w_ukv(_from_col_shards(g_ukv, KV_LORA))
    w_out_f = g_out.reshape(N_SHARD, DEPTH, 1024 // N_SHARD, D_MODEL).transpose(1, 0, 2, 3).reshape(DEPTH, 1024, D_MODEL)

    (loss, dx, d_norm, d_win_p, d_qn, d_kvn, d_wuq_p, d_wukv_p, d_wout, d_final) = _local_grads(
        x[0], loss_target[0], norm_g, w_in_p, q_norm_g, kv_norm_g, w_uq_p, w_ukv_p, w_out_f, final_g)
    loss = lax.psum(loss, ("x", "y", "c"))

    small = _pack_small(d_norm, d_qn, d_kvn, d_final)
    grads = [_split_w_in_grad(d_win_p), _col_shards(_unpad_w_uq(d_wuq_p)), _col_shards(_unpad_w_ukv(d_wukv_p)),
             d_wout.reshape(DEPTH, N_SHARD, 1024 // N_SHARD, D_MODEL).transpose(1, 0, 2, 3).reshape(
                 N_SHARD, DEPTH * (1024 // N_SHARD), D_MODEL),
             jnp.broadcast_to(small[None], (N_SHARD, SMALL_ROWS, LANES))]
    halves = [g.reshape(N_SHARD, 2, g.shape[1] // 2, g.shape[2]) for g in grads]
    from_sib = _sibling_send_halves(halves)
    chip_sum = [_add2(lax.dynamic_index_in_dim(h, c, axis=1, keepdims=False), f, "grad_add_pair", BF16)
                for h, f in zip(halves, from_sib)]
    red_half = [_add4_ordered(t, "grad_add_chips") for t in _chip_scatter(chip_sum)]
    other_half = _sibling_swap(red_half)
    gred = []
    for mine, other in zip(red_half, other_half):
        both = jnp.stack([mine, other])
        gred.append(jnp.concatenate([lax.dynamic_index_in_dim(both, c, axis=0, keepdims=False),
                                     lax.dynamic_index_in_dim(both, 1 - c, axis=0, keepdims=False)], axis=0))

    wf = families(w_in, w_uq, w_ukv, w_out) + [_pack_small(norm_g, q_norm_g, kv_norm_g, final_g)]
    mf = families(m_w_in, m_w_uq, m_w_ukv, m_w_out) + [_pack_small(m_norm_g, m_q_norm_g, m_kv_norm_g, m_final_g)]
    vf = families(v_w_in, v_w_uq, v_w_ukv, v_w_out) + [_pack_small(v_norm_g, v_q_norm_g, v_kv_norm_g, v_final_g)]
    upd = [_adamw(w, g, m, v, "adamw") for w, g, m, v in zip(wf, gred, mf, vf)]

    def leaves(fams):
        a_in, a_uq, a_ukv, a_out, s = fams
        s_norm, s_qn, s_kvn, s_final = _split_small(s)
        return [s_norm, a_in.reshape(w_in.shape), s_qn, s_kvn, a_uq.reshape(w_uq.shape), a_ukv.reshape(w_ukv.shape),
                a_out.reshape(w_out.shape), s_final]

    return (loss, dx[None], *leaves(gred), *leaves([u[0] for u in upd]), *leaves([u[1] for u in upd]),
            *leaves([u[2] for u in upd]))
```

```python
import functools

import jax
import jax.numpy as jnp
from jax import lax
from jax.experimental import pallas as pl
from jax.experimental.pallas import tpu as pltpu

F32 = jnp.float32
BF16 = jnp.bfloat16
MESH = pl.DeviceIdType.MESH

D_MODEL = 1024
DEPTH = 4
MLA_HEADS = 8
MLA_NOPE = 64
MLA_ROPE = 32
Q_LORA = 384
KV_LORA = 256
DIL_PAIRS = ((128, 1), (512, 4), (2048, 16))
DIL_HD = 64
DIL_HALF = 64
ROT_DIM = 16
ROPE_THETA = 500000.0
EPS = 1e-6
IN_WIDTH = 6304
N_SHARD = 4

P_WIDTH = 6656
P_MLA = 1024
P_GATE = 1024
P_DIL0 = 2048
LANES = 128
HEAD_W = 512

ADAM_LR = 0.001
ADAM_B1 = 0.9
ADAM_B2 = 0.999
ADAM_EPS = 1e-08
ADAM_WD = 0.01
ADAM_STEP = 10

SHARD_COLS_IN = IN_WIDTH // N_SHARD
FAM_SHAPES = ((DEPTH * D_MODEL, SHARD_COLS_IN), (DEPTH * Q_LORA, 768 // N_SHARD), (DEPTH * KV_LORA, 1024 // N_SHARD),
              (DEPTH * (1024 // N_SHARD), D_MODEL))
N_SMALL = DEPTH * (D_MODEL + Q_LORA + KV_LORA) + D_MODEL
SMALL_ROWS = 64
VMEM_BIG_MB = 48


def _cparams(vmem_mb=None):
    if vmem_mb is None:
        return None
    return pltpu.CompilerParams(vmem_limit_bytes=vmem_mb << 20)


def _sigmoid(x):
    return 1.0 / (1.0 + jnp.exp(-x))


def _rope(x, c, a, b, sh):
    return x * c + pltpu.roll(x, LANES - sh, 1) * a + pltpu.roll(x, sh, 1) * b


def _per_head8(x, pick_first):
    r = lax.broadcasted_iota(jnp.int32, (HEAD_W, MLA_HEADS), 0)
    c = lax.broadcasted_iota(jnp.int32, (HEAD_W, MLA_HEADS), 1)
    sel = (r == c * DIL_HD) if pick_first else (r // DIL_HD == c)
    mat = jnp.where(sel, 1.0, 0.0).astype(BF16)
    out = jnp.zeros((x.shape[0], MLA_HEADS), F32)
    for _ in range(3):
        part = x.astype(BF16)
        out = out + jnp.dot(part, mat, preferred_element_type=F32)
        x = x - part.astype(F32)
    return out


def _mm(a, b, *, tm, tn, tk, out_dtype, name, add=None, a_is_kxm=False):
    K, M = a.shape if a_is_kxm else a.shape[::-1]
    N = b.shape[1]
    tm, tn, tk = min(tm, M), min(tn, N), min(tk, K)
    assert M % tm == 0 and N % tn == 0 and K % tk == 0, (a.shape, b.shape)
    nk = K // tk
    has_add = add is not None

    def body(*refs):
        if has_add:
            a_ref, b_ref, add_ref, o_ref, acc = refs
        else:
            a_ref, b_ref, o_ref, acc = refs
        k = pl.program_id(2)
        if a_is_kxm:
            part = lax.dot_general(a_ref[...].astype(BF16), b_ref[...].astype(BF16), (((0,), (0,)), ((), ())),
                                   preferred_element_type=F32)
        else:
            part = jnp.dot(a_ref[...].astype(BF16), b_ref[...].astype(BF16), preferred_element_type=F32)

        @pl.when(k == 0)
        def _():
            acc[...] = part

        @pl.when(k > 0)
        def _():
            acc[...] += part

        @pl.when(k == nk - 1)
        def _():
            r = acc[...]
            if has_add:
                r = r + add_ref[...]
            o_ref[...] = r.astype(out_dtype)

    a_spec = pl.BlockSpec((tk, tm), lambda i, j, k: (k, i)) if a_is_kxm else pl.BlockSpec((tm, tk), lambda i, j, k: (i, k))
    in_specs = [a_spec, pl.BlockSpec((tk, tn), lambda i, j, k: (k, j))]
    args = [a, b]
    if has_add:
        in_specs.append(pl.BlockSpec((tm, tn), lambda i, j, k: (i, j)))
        args.append(add)
    return pl.pallas_call(
        body, name=name, grid=(M // tm, N // tn, nk), in_specs=in_specs,
        out_specs=pl.BlockSpec((tm, tn), lambda i, j, k: (i, j)),
        out_shape=jax.ShapeDtypeStruct((M, N), out_dtype),
        scratch_shapes=[pltpu.VMEM((tm, tn), F32)], compiler_params=_cparams(VMEM_BIG_MB))(*args)


def _row_spec(tm, w, cb=0):
    return pl.BlockSpec((tm, w), lambda i: (i, cb))


def _const_spec(arr):
    nd = arr.ndim
    return pl.BlockSpec(arr.shape, lambda i: (0,) * nd)


def _rms_fwd(x, g, name):
    L, D = x.shape
    tm = min(512, L)

    def body(x_ref, g_ref, o_ref):
        xv = x_ref[...]
        r = lax.rsqrt(jnp.mean(xv * xv, axis=-1, keepdims=True) + EPS)
        o_ref[...] = (xv * r * g_ref[...]).astype(BF16)

    return pl.pallas_call(
        body, name=name, grid=(L // tm,), in_specs=[_row_spec(tm, D), _const_spec(g)],
        out_specs=_row_spec(tm, D), out_shape=jax.ShapeDtypeStruct((L, D), BF16))(x, g)


def _rms_bwd(dh, x, g, dres, name):
    L, D = x.shape
    tm = min(512, L)

    def body(dh_ref, x_ref, g_ref, dres_ref, dx_ref, dg_ref):
        xv = x_ref[...]
        dy = dh_ref[...]
        r = lax.rsqrt(jnp.mean(xv * xv, axis=-1, keepdims=True) + EPS)
        dyg = dy * g_ref[...]
        dx_ref[...] = dres_ref[...] + r * dyg - xv * (r * r * r) * jnp.mean(dyg * xv, axis=-1, keepdims=True)
        part = jnp.sum(dy * xv * r, axis=0, keepdims=True)

        @pl.when(pl.program_id(0) == 0)
        def _():
            dg_ref[...] = part

        @pl.when(pl.program_id(0) > 0)
        def _():
            dg_ref[...] += part

    return pl.pallas_call(
        body, name=name, grid=(L // tm,),
        in_specs=[_row_spec(tm, D), _row_spec(tm, D), _const_spec(g), _row_spec(tm, D)],
        out_specs=[_row_spec(tm, D), pl.BlockSpec((1, D), lambda i: (0, 0))],
        out_shape=[jax.ShapeDtypeStruct((L, D), F32), jax.ShapeDtypeStruct((1, D), F32)])(dh, x, g, dres)


def _loss_head(x, g, target, name):
    L, D = x.shape
    tm = min(512, L)

    def body(x_ref, g_ref, t_ref, loss_ref, dx_ref, dg_ref):
        xv = x_ref[...]
        gv = g_ref[...]
        r = lax.rsqrt(jnp.mean(xv * xv, axis=-1, keepdims=True) + EPS)
        xr = xv * r
        err = xr * gv - t_ref[...]
        lp = 0.5 * jnp.sum(jnp.mean(err * err, axis=-1, keepdims=True))
        dy = err * (1.0 / D)
        dyg = dy * gv
        dx_ref[...] = r * dyg - xv * (r * r * r) * jnp.mean(dyg * xv, axis=-1, keepdims=True)
        part = jnp.sum(dy * xr, axis=0, keepdims=True)

        @pl.when(pl.program_id(0) == 0)
        def _():
            dg_ref[...] = part
            loss_ref[...] = jnp.zeros(loss_ref.shape, F32) + lp

        @pl.when(pl.program_id(0) > 0)
        def _():
            dg_ref[...] += part
            loss_ref[...] += lp

    return pl.pallas_call(
        body, name=name, grid=(L // tm,),
        in_specs=[_row_spec(tm, D), _const_spec(g), _row_spec(tm, D)],
        out_specs=[pl.BlockSpec((8, LANES), lambda i: (0, 0)), _row_spec(tm, D), pl.BlockSpec((1, D), lambda i: (0, 0))],
        out_shape=[jax.ShapeDtypeStruct((8, LANES), F32), jax.ShapeDtypeStruct((L, D), F32),
                   jax.ShapeDtypeStruct((1, D), F32)])(x, g, target)


def _mla_prep(p, qg, kvg, wuq, wukv, tabs, name):
    L = p.shape[0]
    tm = min(512, L)
    scale = (MLA_NOPE + MLA_ROPE) ** -0.5
    tc, ta, tb = tabs

    def body(p_ref, qg_ref, kvg_ref, wuq_ref, wukv_ref, c_ref, a_ref, b_ref, q_ref, k_ref, v_ref, cqn_ref, ckvn_ref):
        c, a, b = c_ref[...], a_ref[...], b_ref[...]
        cq = p_ref[:, 0:Q_LORA].astype(F32)
        ckv = p_ref[:, Q_LORA:Q_LORA + KV_LORA].astype(F32)
        kr = p_ref[:, 640:768].astype(F32)
        cqn = (cq * lax.rsqrt(jnp.mean(cq * cq, axis=-1, keepdims=True) + EPS) * qg_ref[...]).astype(BF16)
        ckvn = (ckv * lax.rsqrt(jnp.mean(ckv * ckv, axis=-1, keepdims=True) + EPS) * kvg_ref[...]).astype(BF16)
        cqn_ref[...] = cqn
        ckvn_ref[...] = ckvn
        q = jnp.dot(cqn, wuq_ref[...], preferred_element_type=F32)
        kv = jnp.dot(ckvn, wukv_ref[...], preferred_element_type=F32)
        krr = _rope(kr, c, a, b, MLA_ROPE // 2)
        for h in range(MLA_HEADS):
            sl = slice(h * LANES, (h + 1) * LANES)
            q_ref[:, sl] = (_rope(q[:, sl], c, a, b, MLA_ROPE // 2) * (scale * LOG2E)).astype(BF16)
            k_ref[:, sl] = (kv[:, sl] + krr).astype(BF16)
        v_ref[...] = kv[:, 1024:1536].astype(BF16)

    return pl.pallas_call(
        body, name=name, grid=(L // tm,),
        in_specs=[_row_spec(tm, P_MLA, 0), _const_spec(qg), _const_spec(kvg), _const_spec(wuq), _const_spec(wukv),
                  _row_spec(tm, LANES), _row_spec(tm, LANES), _row_spec(tm, LANES)],
        out_specs=[_row_spec(tm, 1024), _row_spec(tm, 1024), _row_spec(tm, HEAD_W), _row_spec(tm, Q_LORA),
                   _row_spec(tm, KV_LORA)],
        out_shape=[jax.ShapeDtypeStruct((L, 1024), BF16), jax.ShapeDtypeStruct((L, 1024), BF16),
                   jax.ShapeDtypeStruct((L, HEAD_W), BF16), jax.ShapeDtypeStruct((L, Q_LORA), BF16),
                   jax.ShapeDtypeStruct((L, KV_LORA), BF16)],
        compiler_params=_cparams(VMEM_BIG_MB))(p, qg, kvg, wuq, wukv, tc, ta, tb)


def _mla_prep_bwd(dq, dk, dv, p, qg, kvg, wuq_t, wukv_t, tabs_t, name):
    L = p.shape[0]
    tm = min(512, L)
    scale = (MLA_NOPE + MLA_ROPE) ** -0.5
    tc, ta, tb = tabs_t

    def body(dq_ref, dk_ref, dv_ref, p_ref, qg_ref, kvg_ref, wuqt_ref, wukvt_ref, c_ref, a_ref, b_ref,
             dp_ref, dqp_ref, dkv_ref, dqg_ref, dkvg_ref):
        c, a, b = c_ref[...], a_ref[...], b_ref[...]
        dkr = jnp.zeros((tm, LANES), F32)
        for h in range(MLA_HEADS):
            sl = slice(h * LANES, (h + 1) * LANES)
            dqp_ref[:, sl] = (_rope(dq_ref[:, sl], c, a, b, MLA_ROPE // 2) * scale).astype(BF16)
            dkh = dk_ref[:, sl] * LN2
            dkv_ref[:, sl] = dkh.astype(BF16)
            dkr = dkr + dkh
        dkv_ref[:, 1024:1536] = dv_ref[...].astype(BF16)
        lane = lax.broadcasted_iota(jnp.int32, (tm, LANES), 1)
        dkr = jnp.where((lane >= MLA_NOPE) & (lane < MLA_NOPE + MLA_ROPE), _rope(dkr, c, a, b, MLA_ROPE // 2), 0.0)

        d_cqn = jnp.dot(dqp_ref[...], wuqt_ref[...], preferred_element_type=F32)
        d_ckvn = jnp.dot(dkv_ref[...], wukvt_ref[...], preferred_element_type=F32)

        def norm_bwd(xv, gv, dy):
            r = lax.rsqrt(jnp.mean(xv * xv, axis=-1, keepdims=True) + EPS)
            dyg = dy * gv
            dx = r * dyg - xv * (r * r * r) * jnp.mean(dyg * xv, axis=-1, keepdims=True)
            return dx, jnp.sum(dy * xv * r, axis=0, keepdims=True)

        d_cq, dqg = norm_bwd(p_ref[:, 0:Q_LORA].astype(F32), qg_ref[...], d_cqn)
        d_ckv, dkvg = norm_bwd(p_ref[:, Q_LORA:Q_LORA + KV_LORA].astype(F32), kvg_ref[...], d_ckvn)
        dp_ref[:, 0:Q_LORA] = d_cq.astype(BF16)
        dp_ref[:, Q_LORA:Q_LORA + KV_LORA] = d_ckv.astype(BF16)
        dp_ref[:, 640:768] = dkr.astype(BF16)
        dp_ref[:, 768:1024] = jnp.zeros((tm, 256), BF16)

        @pl.when(pl.program_id(0) == 0)
        def _():
            dqg_ref[...] = dqg
            dkvg_ref[...] = dkvg

        @pl.when(pl.program_id(0) > 0)
        def _():
            dqg_ref[...] += dqg
            dkvg_ref[...] += dkvg

    return pl.pallas_call(
        body, name=name, grid=(L // tm,),
        in_specs=[_row_spec(tm, 1024), _row_spec(tm, 1024), _row_spec(tm, HEAD_W), _row_spec(tm, P_MLA, 0),
                  _const_spec(qg), _const_spec(kvg), _const_spec(wuq_t), _const_spec(wukv_t),
                  _row_spec(tm, LANES), _row_spec(tm, LANES), _row_spec(tm, LANES)],
        out_specs=[_row_spec(tm, P_MLA), _row_spec(tm, 1024), _row_spec(tm, 1536),
                   pl.BlockSpec((1, Q_LORA), lambda i: (0, 0)), pl.BlockSpec((1, KV_LORA), lambda i: (0, 0))],
        out_shape=[jax.ShapeDtypeStruct((L, P_MLA), BF16), jax.ShapeDtypeStruct((L, 1024), BF16),
                   jax.ShapeDtypeStruct((L, 1536), BF16), jax.ShapeDtypeStruct((1, Q_LORA), F32),
                   jax.ShapeDtypeStruct((1, KV_LORA), F32)],
        compiler_params=_cparams(VMEM_BIG_MB))(dq, dk, dv, p, qg, kvg, wuq_t, wukv_t, tc, ta, tb)


def _dil_prep(p, tabs, name):
    L = p.shape[0]
    tm = min(512, L)
    tc, ta, tb = tabs
    scale = DIL_HD ** -0.5

    def body(*refs):
        ins, (c_ref, a_ref, b_ref), outs = refs[:9], refs[9:12], refs[12:]
        c, a, b = c_ref[...], a_ref[...], b_ref[...]
        for n in range(9):
            t = n % 3
            for cb in range(HEAD_W // LANES):
                sl = slice(cb * LANES, (cb + 1) * LANES)
                xv = ins[n][:, sl].astype(F32)
                if t == 0:
                    xv = _rope(xv, c, a, b, ROT_DIM // 2) * (scale * LOG2E)
                elif t == 1:
                    xv = _rope(xv, c, a, b, ROT_DIM // 2)
                outs[n][:, sl] = xv.astype(BF16)

    in_specs = [_row_spec(tm, HEAD_W, P_DIL0 // HEAD_W + n) for n in range(9)] + [_row_spec(tm, LANES)] * 3
    return pl.pallas_call(
        body, name=name, grid=(L // tm,), in_specs=in_specs,
        out_specs=[_row_spec(tm, HEAD_W)] * 9,
        out_shape=[jax.ShapeDtypeStruct((L, HEAD_W), BF16)] * 9)(*([p] * 9), tc, ta, tb)


def _dil_prep_bwd(grads, tabs_t, name):
    L = grads[0].shape[0]
    tm = min(512, L)
    tc, ta, tb = tabs_t
    scale = DIL_HD ** -0.5

    def body(*refs):
        ins, (c_ref, a_ref, b_ref), o_ref = refs[:9], refs[9:12], refs[12]
        c, a, b = c_ref[...], a_ref[...], b_ref[...]
        for n in range(9):
            t = n % 3
            for cb in range(HEAD_W // LANES):
                sl = slice(cb * LANES, (cb + 1) * LANES)
                xv = ins[n][:, sl]
                if t == 0:
                    xv = _rope(xv, c, a, b, ROT_DIM // 2) * scale
                elif t == 1:
                    xv = _rope(xv, c, a, b, ROT_DIM // 2) * LN2
                o_ref[:, n * HEAD_W + cb * LANES:n * HEAD_W + (cb + 1) * LANES] = xv.astype(BF16)

    return pl.pallas_call(
        body, name=name, grid=(L // tm,), in_specs=[_row_spec(tm, HEAD_W)] * 9 + [_row_spec(tm, LANES)] * 3,
        out_specs=_row_spec(tm, 9 * HEAD_W), out_shape=jax.ShapeDtypeStruct((L, 9 * HEAD_W), BF16),
        compiler_params=_cparams(VMEM_BIG_MB))(*grads, tc, ta, tb)


def _merge_gate(oa, p, o_g, lse_g, name):
    L = oa.shape[0]
    tm = min(512, L)

    def body(oa_ref, ga_ref, gb_ref, o1, o2, o3, l1, l2, l3, ab_ref, bm_ref, lt_ref):
        la, lb, lc = l1[...], l2[...], l3[...]
        m = jnp.maximum(jnp.maximum(la, lb), lc)
        ea, eb, ec = jnp.exp2(la - m), jnp.exp2(lb - m), jnp.exp2(lc - m)
        den = ea + eb + ec
        bm = (ea * o1[...] + eb * o2[...] + ec * o3[...]) / den
        bm_ref[...] = bm
        lt_ref[...] = _per_head8(m + jnp.log2(den), True)
        ga, gb = ga_ref[...].astype(F32), gb_ref[...].astype(F32)
        ab_ref[:, 0:HEAD_W] = (oa_ref[...] * (ga * _sigmoid(ga))).astype(BF16)
        ab_ref[:, HEAD_W:2 * HEAD_W] = (bm * (gb * _sigmoid(gb))).astype(BF16)

    w = _row_spec(tm, HEAD_W)
    return pl.pallas_call(
        body, name=name, grid=(L // tm,),
        in_specs=[w, _row_spec(tm, HEAD_W, 2), _row_spec(tm, HEAD_W, 3), w, w, w, w, w, w],
        out_specs=[_row_spec(tm, 2 * HEAD_W), w, _row_spec(tm, MLA_HEADS)],
        out_shape=[jax.ShapeDtypeStruct((L, 2 * HEAD_W), BF16), jax.ShapeDtypeStruct((L, HEAD_W), F32),
                   jax.ShapeDtypeStruct((L, MLA_HEADS), F32)])(oa, p, p, *o_g, *lse_g)


def _gate_bwd(dab, p, oa, bm, name):
    L = oa.shape[0]
    tm = min(512, L)

    def body(da_ref, db_ref, ga_ref, gb_ref, oa_ref, bm_ref, doa_ref, dbm_ref, Da_ref, Db_ref, dg_ref):
        def one(d, g, o, do_ref, D_ref, col):
            sg = _sigmoid(g)
            do = d * (g * sg)
            do_ref[...] = do.astype(BF16)
            dg_ref[:, col:col + HEAD_W] = (d * o * (sg * (1.0 + g * (1.0 - sg)))).astype(BF16)
            D_ref[...] = _per_head8(do * o, False)

        one(da_ref[...], ga_ref[...].astype(F32), oa_ref[...], doa_ref, Da_ref, 0)
        one(db_ref[...], gb_ref[...].astype(F32), bm_ref[...], dbm_ref, Db_ref, HEAD_W)

    w = _row_spec(tm, HEAD_W)
    w8 = _row_spec(tm, MLA_HEADS)
    return pl.pallas_call(
        body, name=name, grid=(L // tm,),
        in_specs=[_row_spec(tm, HEAD_W, 0), _row_spec(tm, HEAD_W, 1), _row_spec(tm, HEAD_W, 2),
                  _row_spec(tm, HEAD_W, 3), w, w],
        out_specs=[w, w, w8, w8, _row_spec(tm, 2 * HEAD_W)],
        out_shape=[jax.ShapeDtypeStruct((L, HEAD_W), BF16), jax.ShapeDtypeStruct((L, HEAD_W), BF16),
                   jax.ShapeDtypeStruct((L, MLA_HEADS), F32), jax.ShapeDtypeStruct((L, MLA_HEADS), F32),
                   jax.ShapeDtypeStruct((L, 2 * HEAD_W), BF16)])(dab, dab, p, p, oa, bm)


NT = (((1,), (1,)), ((), ()))
TN = (((0,), (0,)), ((), ()))
NEG = -1e30


MLA_TQ = 512
MLA_TK = 2048
MLA_BWD_TK = 1024
LOG2E = 1.4426950408889634
LN2 = 0.6931471805599453


def _mla_fwd(q, k, v_t, name):
    L = q.shape[0]
    tq, tk = min(MLA_TQ, L), min(MLA_TK, L)
    nq, nk = L // tq, L // tk
    npair = MLA_HEADS // 2

    def body(q_ref, k_ref, vt_ref, o_ref, lse_ref, m0, l0, a0, m1, l1, a1):
        j = pl.program_id(2)
        stats = ((m0, l0, a0), (m1, l1, a1))

        @pl.when(j == 0)
        def _():
            for m_sc, l_sc, acc_sc in stats:
                m_sc[...] = jnp.full(m_sc.shape, NEG, F32)
                l_sc[...] = jnp.zeros(l_sc.shape, F32)
                acc_sc[...] = jnp.zeros(acc_sc.shape, F32)

        s_ts = [lax.dot_general(k_ref[:, hh * LANES:(hh + 1) * LANES], q_ref[:, hh * LANES:(hh + 1) * LANES], NT,
                                preferred_element_type=F32) for hh in range(2)]
        for hh in range(2):
            m_sc, l_sc, acc_sc = stats[hh]
            s_t = s_ts[hh]
            m_prev = m_sc[...]
            m_new = jnp.maximum(m_prev, jnp.max(s_t, axis=0, keepdims=True))
            alpha = jnp.exp2(m_prev - m_new)
            p_t = jnp.exp2(s_t - m_new)
            l_sc[...] = alpha * l_sc[...] + jnp.sum(p_t, axis=0, keepdims=True)
            m_sc[...] = m_new
            pv = jnp.dot(vt_ref[hh * DIL_HD:(hh + 1) * DIL_HD, :], p_t.astype(BF16),
                         preferred_element_type=F32)
            acc_sc[...] = alpha * acc_sc[...] + pv

        @pl.when(j == nk - 1)
        def _():
            o_ref[...] = jnp.concatenate([a0[...] / l0[...], a1[...] / l1[...]], axis=0).T
            lse_ref[...] = jnp.concatenate([m0[...] + jnp.log2(l0[...]), m1[...] + jnp.log2(l1[...])], axis=0)

    stat = [pltpu.VMEM((1, tq), F32), pltpu.VMEM((1, tq), F32), pltpu.VMEM((DIL_HD, tq), F32)]
    return pl.pallas_call(
        body, name=name, grid=(npair, nq, nk),
        in_specs=[pl.BlockSpec((tq, 2 * LANES), lambda pr, i, j: (i, pr)),
                  pl.BlockSpec((tk, 2 * LANES), lambda pr, i, j: (j, pr)),
                  pl.BlockSpec((LANES, tk), lambda pr, i, j: (pr, j))],
        out_specs=[pl.BlockSpec((tq, LANES), lambda pr, i, j: (i, pr)),
                   pl.BlockSpec((None, 2, tq), lambda pr, i, j: (pr, 0, i))],
        out_shape=[jax.ShapeDtypeStruct((L, HEAD_W), F32), jax.ShapeDtypeStruct((npair, 2, L), F32)],
        scratch_shapes=stat + stat, compiler_params=_cparams(VMEM_BIG_MB))(q, k, v_t)


def _mla_bwd(q, k, v, do, lse_rows, d_rows, name):
    L = q.shape[0]
    tq, tk = min(512, L), min(MLA_BWD_TK, L)
    nq, nk = L // tq, L // tk
    npair = MLA_HEADS // 2

    def body(q_ref, k_ref, v_ref, do_ref, lse_ref, d_ref, dq_ref, dk_ref, dv_ref):
        j, i = pl.program_id(1), pl.program_id(2)

        @pl.when((j == 0) & (i == 0))
        def _():
            dq_ref[...] = jnp.zeros(dq_ref.shape, F32)

        @pl.when(i == 0)
        def _():
            dk_ref[...] = jnp.zeros(dk_ref.shape, F32)
            dv_ref[...] = jnp.zeros(dv_ref.shape, F32)

        first = lax.broadcasted_iota(jnp.int32, (tq, LANES), 1) < DIL_HD
        dov = do_ref[...]
        vv = v_ref[...]
        rows = pl.ds(pl.multiple_of(i * tq, tq), tq)
        for hh in range(2):
            sl = slice(hh * LANES, (hh + 1) * LANES)
            qh, kh = q_ref[:, sl], k_ref[:, sl]
            do_h = jnp.where(first if hh == 0 else ~first, dov, jnp.zeros_like(dov))
            s_t = lax.dot_general(kh, qh, NT, preferred_element_type=F32)
            p_t = jnp.exp2(s_t - lse_ref[hh:hh + 1, :])
            dv_ref[...] += jnp.dot(p_t.astype(BF16), do_h, preferred_element_type=F32)
            dp_t = lax.dot_general(vv, do_h, NT, preferred_element_type=F32)
            ds_t = (p_t * (dp_t - d_ref[hh:hh + 1, :])).astype(BF16)
            dk_ref[:, sl] += jnp.dot(ds_t, qh, preferred_element_type=F32)
            dq_ref[rows, sl] += lax.dot_general(ds_t, kh, TN, preferred_element_type=F32)

    return pl.pallas_call(
        body, name=name, grid=(npair, nk, nq),
        in_specs=[pl.BlockSpec((tq, 2 * LANES), lambda pr, j, i: (i, pr)),
                  pl.BlockSpec((tk, 2 * LANES), lambda pr, j, i: (j, pr)),
                  pl.BlockSpec((tk, LANES), lambda pr, j, i: (j, pr)),
                  pl.BlockSpec((tq, LANES), lambda pr, j, i: (i, pr)),
                  pl.BlockSpec((None, 2, tq), lambda pr, j, i: (pr, 0, i)),
                  pl.BlockSpec((None, 2, tq), lambda pr, j, i: (pr, 0, i))],
        out_specs=[pl.BlockSpec((L, 2 * LANES), lambda pr, j, i: (0, pr)),
                   pl.BlockSpec((tk, 2 * LANES), lambda pr, j, i: (j, pr)),
                   pl.BlockSpec((tk, LANES), lambda pr, j, i: (j, pr))],
        out_shape=[jax.ShapeDtypeStruct((L, 1024), F32), jax.ShapeDtypeStruct((L, 1024), F32),
                   jax.ShapeDtypeStruct((L, HEAD_W), F32)],
        compiler_params=_cparams(VMEM_BIG_MB))(q, k, v, do, lse_rows, d_rows)


DIL_SQ = 128
DIL_SW = DIL_SQ + 2 * DIL_HALF


def _dil_band_mask(a_sub, ld):
    kidx = lax.broadcasted_iota(jnp.int32, (DIL_SW, 2 * DIL_SQ), 0)
    qidx = lax.broadcasted_iota(jnp.int32, (DIL_SW, 2 * DIL_SQ), 1) % DIL_SQ
    rel = kidx - DIL_HALF - qidx
    return (jnp.abs(rel) <= DIL_HALF) & (kidx >= DIL_HALF - a_sub) & (kidx < ld + DIL_HALF - a_sub)


def _pair_rows(x, first):
    zero = jnp.zeros_like(x)
    return jnp.concatenate([jnp.where(first, x, zero), jnp.where(first, zero, x)], axis=0)


def _dil_fwd(q, kp, vp_t4, name):
    d, ld, _ = q.shape
    assert ld % DIL_SQ == 0
    tq = min(512, ld)
    nq = ld // tq
    npair = HEAD_W // LANES
    nb = (ld + 2 * DIL_HALF) // LANES

    def body(q_ref, k_ref, vt_ref, o_ref, lse_ref):
        i = pl.program_id(2)
        first = lax.broadcasted_iota(jnp.int32, (DIL_SQ, LANES), 1) < DIL_HD
        for u in range(tq // DIL_SQ):
            a_sub = pl.multiple_of(i * tq + u * DIL_SQ, DIL_SQ)
            kk = a_sub // LANES
            rows = slice(u * DIL_SQ, (u + 1) * DIL_SQ)
            kwin = k_ref[pl.ds(a_sub, DIL_SW), :]
            valid = _dil_band_mask(a_sub, ld)[:, 0:DIL_SQ]
            qv = q_ref[rows, :]
            outs, lses = [], []
            for hh in range(2):
                qh = jnp.where(first if hh == 0 else ~first, qv, jnp.zeros_like(qv))
                s_t = jnp.where(valid, lax.dot_general(kwin, qh, NT, preferred_element_type=F32), NEG)
                m = jnp.max(s_t, axis=0, keepdims=True)
                p32 = jnp.exp2(s_t - m)
                l = jnp.sum(p32, axis=0, keepdims=True)
                p_t = p32.astype(BF16)
                hrows = slice(hh * DIL_HD, (hh + 1) * DIL_HD)
                pv = (jnp.dot(vt_ref[kk, hrows, :], p_t[0:LANES, :], preferred_element_type=F32)
                      + jnp.dot(vt_ref[kk + 1, hrows, :], p_t[LANES:DIL_SW, :], preferred_element_type=F32))
                outs.append(pv / l)
                lses.append(jnp.broadcast_to(m + jnp.log2(l), (DIL_HD, DIL_SQ)))
            o_ref[rows, :] = jnp.concatenate(outs, axis=0).T
            lse_ref[rows, :] = jnp.concatenate(lses, axis=0).T

    blk = pl.BlockSpec((None, tq, LANES), lambda r, pr, i: (r, i, pr))
    full = pl.BlockSpec((None, ld + 2 * DIL_HALF, LANES), lambda r, pr, i: (r, 0, pr))
    vspec = pl.BlockSpec((None, nb, LANES, LANES), lambda r, pr, i: (r, 0, pr, 0))
    return pl.pallas_call(
        body, name=name, grid=(d, npair, nq), in_specs=[blk, full, vspec], out_specs=[blk, blk],
        out_shape=[jax.ShapeDtypeStruct((d, ld, HEAD_W), F32), jax.ShapeDtypeStruct((d, ld, HEAD_W), F32)],
        compiler_params=_cparams(VMEM_BIG_MB))(q, kp, vp_t4)


def _dil_bwd(q, kp, vp, do, lse_rows, d_rows, name):
    d, ld, _ = q.shape
    assert ld % DIL_SQ == 0
    tq = min(512, ld)
    nq = ld // tq
    npair = HEAD_W // LANES
    ldp = ld + 2 * DIL_HALF
    span = tq + 2 * DIL_HALF

    def body(q_ref, k_ref, v_ref, do_ref, lse_ref, d_ref, dq_ref, dk_ref, dv_ref, dk_sc, dv_sc):
        i = pl.program_id(2)

        @pl.when(i == 0)
        def _():
            dk_ref[...] = jnp.zeros(dk_ref.shape, F32)
            dv_ref[...] = jnp.zeros(dv_ref.shape, F32)

        dk_sc[...] = jnp.zeros(dk_sc.shape, F32)
        dv_sc[...] = jnp.zeros(dv_sc.shape, F32)
        first = lax.broadcasted_iota(jnp.int32, (DIL_SQ, LANES), 1) < DIL_HD
        for u in range(tq // DIL_SQ):
            a_sub = pl.multiple_of(i * tq + u * DIL_SQ, DIL_SQ)
            rows = slice(u * DIL_SQ, (u + 1) * DIL_SQ)
            win = slice(u * DIL_SQ, u * DIL_SQ + DIL_SW)
            kwin = k_ref[pl.ds(a_sub, DIL_SW), :]
            vwin = v_ref[pl.ds(a_sub, DIL_SW), :]
            q2 = _pair_rows(q_ref[rows, :], first)
            do2 = _pair_rows(do_ref[rows, :], first)
            lse2 = jnp.concatenate([lse_ref[0:1, rows], lse_ref[1:2, rows]], axis=1)
            dd2 = jnp.concatenate([d_ref[0:1, rows], d_ref[1:2, rows]], axis=1)
            s_t = lax.dot_general(kwin, q2, NT, preferred_element_type=F32)
            p_t = jnp.exp2(jnp.where(_dil_band_mask(a_sub, ld), s_t, NEG) - lse2)
            dv_sc[win, :] += jnp.dot(p_t.astype(BF16), do2, preferred_element_type=F32)
            dp_t = lax.dot_general(vwin, do2, NT, preferred_element_type=F32)
            ds_t = (p_t * (dp_t - dd2)).astype(BF16)
            dk_sc[win, :] += jnp.dot(ds_t, q2, preferred_element_type=F32)
            dq2 = lax.dot_general(ds_t, kwin, TN, preferred_element_type=F32)
            dq_ref[rows, :] = jnp.where(first, dq2[0:DIL_SQ, :], dq2[DIL_SQ:2 * DIL_SQ, :])
        out_rows = pl.ds(pl.multiple_of(i * tq, DIL_SQ), span)
        dk_ref[out_rows, :] += dk_sc[...]
        dv_ref[out_rows, :] += dv_sc[...]

    blk = pl.BlockSpec((None, tq, LANES), lambda r, pr, i: (r, i, pr))
    full = pl.BlockSpec((None, ldp, LANES), lambda r, pr, i: (r, 0, pr))
    rowspec = pl.BlockSpec((None, None, 2, tq), lambda r, pr, i: (r, pr, 0, i))
    return pl.pallas_call(
        body, name=name, grid=(d, npair, nq), in_specs=[blk, full, full, blk, rowspec, rowspec],
        out_specs=[blk, full, full],
        out_shape=[jax.ShapeDtypeStruct((d, ld, HEAD_W), F32), jax.ShapeDtypeStruct((d, ldp, HEAD_W), F32),
                   jax.ShapeDtypeStruct((d, ldp, HEAD_W), F32)],
        scratch_shapes=[pltpu.VMEM((span, LANES), F32), pltpu.VMEM((span, LANES), F32)],
        compiler_params=_cparams(VMEM_BIG_MB))(q, kp, vp, do, lse_rows, d_rows)


TILE_BYTES = 1 << 21


def _row_tile(rows, cols, budget=TILE_BYTES):
    for parts in range(1, rows + 1):
        tr = rows // parts
        if rows % parts == 0 and tr % 8 == 0 and tr * cols * 4 <= budget:
            return tr
    return rows


def _add2(a, b, name, out_dtype):
    n, rows, cols = a.shape
    tr = _row_tile(rows, cols)

    def body(a_ref, b_ref, o_ref):
        o_ref[...] = (a_ref[...] + b_ref[...]).astype(out_dtype)

    spec = pl.BlockSpec((None, tr, cols), lambda t, i: (t, i, 0))
    return pl.pallas_call(body, name=name, grid=(n, rows // tr), in_specs=[spec, spec], out_specs=spec,
                          out_shape=jax.ShapeDtypeStruct(a.shape, out_dtype))(a, b)


def _add4_ordered(a, name):
    _, rows, cols = a.shape
    tr = _row_tile(rows, cols, TILE_BYTES // 4)

    def body(a_ref, o_ref):
        o_ref[...] = ((a_ref[0].astype(F32) + a_ref[1].astype(F32)) + a_ref[2].astype(F32)) + a_ref[3].astype(F32)

    return pl.pallas_call(
        body, name=name, grid=(rows // tr,), in_specs=[pl.BlockSpec((4, tr, cols), lambda i: (0, i, 0))],
        out_specs=pl.BlockSpec((tr, cols), lambda i: (i, 0)),
        out_shape=jax.ShapeDtypeStruct((rows, cols), F32))(a)


def _adamw(w, g, m, v, name):
    rows, cols = w.shape
    tr = _row_tile(rows, cols)
    bc1 = 1.0 - ADAM_B1 ** ADAM_STEP
    bc2 = 1.0 - ADAM_B2 ** ADAM_STEP

    def body(w_ref, g_ref, m_ref, v_ref, d_ref, nm_ref, nv_ref):
        gv = g_ref[...]
        nm = ADAM_B1 * m_ref[...] + (1.0 - ADAM_B1) * gv
        nv = ADAM_B2 * v_ref[...] + (1.0 - ADAM_B2) * (gv * gv)
        d_ref[...] = -ADAM_LR * ((nm / bc1) / (jnp.sqrt(nv / bc2) + ADAM_EPS) + ADAM_WD * w_ref[...])
        nm_ref[...] = nm
        nv_ref[...] = nv

    spec = pl.BlockSpec((tr, cols), lambda i: (i, 0))
    return pl.pallas_call(body, name=name, grid=(rows // tr,), in_specs=[spec] * 4, out_specs=[spec] * 3,
                          out_shape=[jax.ShapeDtypeStruct(w.shape, F32)] * 3,
                          compiler_params=_cparams(VMEM_BIG_MB))(w, g, m, v)


ANY = pl.BlockSpec(memory_space=pl.ANY)


def _place():
    return lax.axis_index("x"), lax.axis_index("y"), lax.axis_index("c")


def _rcopy(send_sems, recv_sems, n, src, dst, to):
    return pltpu.make_async_remote_copy(src_ref=src, dst_ref=dst, send_sem=send_sems.at[n], recv_sem=recv_sems.at[n],
                                        device_id=to, device_id_type=MESH)


def _allgather_weights(shards):
    na = len(shards)

    def body(*refs):
        w_refs, g_refs = refs[:na], refs[na:2 * na]
        send_sems, recv_sems, local_sems = refs[2 * na:]
        x, y, c = _place()
        s = 2 * x + y
        chips = [(1 - x, y), (x, 1 - y), (1 - x, 1 - y)]

        def half(a, shard, h):
            hr = shards[a].shape[0] // 2
            return g_refs[a].at[shard, pl.ds(h * hr, hr), :]

        started = []
        for a in range(na):
            hr = shards[a].shape[0] // 2
            mine = pltpu.make_async_copy(w_refs[a], g_refs[a].at[s], local_sems.at[a])
            mine.start()
            started.append(mine)
        sends = []
        for a in range(na):
            hr = shards[a].shape[0] // 2
            for n, (cx, cy) in enumerate(chips):
                cp = _rcopy(send_sems, recv_sems, 6 * a + n, w_refs[a].at[pl.ds(c * hr, hr), :], half(a, s, c),
                            (cx, cy, c))
                cp.start()
                sends.append(cp)
        for a in range(na):
            for n, (cx, cy) in enumerate(chips):
                sj = 2 * cx + cy
                _rcopy(send_sems, recv_sems, 6 * a + n, half(a, sj, c), half(a, sj, c), (cx, cy, c)).wait_recv()
                fw = _rcopy(send_sems, recv_sems, 6 * a + 3 + n, half(a, sj, c), half(a, sj, c), (x, y, 1 - c))
                fw.start()
                sends.append(fw)
        for a in range(na):
            for n, (cx, cy) in enumerate(chips):
                sj = 2 * cx + cy
                _rcopy(send_sems, recv_sems, 6 * a + 3 + n, half(a, sj, 1 - c), half(a, sj, 1 - c),
                       (x, y, 1 - c)).wait_recv()
        for cp in sends:
            cp.wait_send()
        for mine in started:
            mine.wait()

    return pl.pallas_call(
        body, name="allgather_weights", in_specs=[ANY] * na, out_specs=[ANY] * na,
        out_shape=[jax.ShapeDtypeStruct((N_SHARD,) + t.shape, t.dtype) for t in shards],
        scratch_shapes=[pltpu.SemaphoreType.DMA((6 * na,)), pltpu.SemaphoreType.DMA((6 * na,)),
                        pltpu.SemaphoreType.DMA((na,))])(*shards)


def _sibling_send_halves(gs):
    na = len(gs)

    def body(*refs):
        g_refs, o_refs = refs[:na], refs[na:2 * na]
        send_sems, recv_sems = refs[2 * na:]
        x, y, c = _place()
        cps = []
        for a in range(na):
            for t in range(N_SHARD):
                cp = _rcopy(send_sems, recv_sems, N_SHARD * a + t, g_refs[a].at[t, 1 - c], o_refs[a].at[t],
                            (x, y, 1 - c))
                cp.start()
                cps.append(cp)
        for cp in cps:
            cp.wait()

    return pl.pallas_call(
        body, name="grad_sibling_exchange", in_specs=[ANY] * na, out_specs=[ANY] * na,
        out_shape=[jax.ShapeDtypeStruct((N_SHARD,) + g.shape[2:], g.dtype) for g in gs],
        scratch_shapes=[pltpu.SemaphoreType.DMA((N_SHARD * na,)), pltpu.SemaphoreType.DMA((N_SHARD * na,))])(*gs)


def _chip_scatter(parts):
    na = len(parts)

    def body(*refs):
        a_refs, o_refs = refs[:na], refs[na:2 * na]
        send_sems, recv_sems, local_sems = refs[2 * na:]
        x, y, c = _place()
        s = 2 * x + y
        chips = [(1 - x, y), (x, 1 - y), (1 - x, 1 - y)]
        started, cps = [], []
        for a in range(na):
            mine = pltpu.make_async_copy(a_refs[a].at[s], o_refs[a].at[s], local_sems.at[a])
            mine.start()
            started.append(mine)
            for n, (cx, cy) in enumerate(chips):
                cp = _rcopy(send_sems, recv_sems, 3 * a + n, a_refs[a].at[2 * cx + cy], o_refs[a].at[s], (cx, cy, c))
                cp.start()
                cps.append(cp)
        for a in range(na):
            for n, (cx, cy) in enumerate(chips):
                sj = 2 * cx + cy
                _rcopy(send_sems, recv_sems, 3 * a + n, a_refs[a].at[sj], o_refs[a].at[sj], (cx, cy, c)).wait_recv()
        for cp in cps:
            cp.wait_send()
        for mine in started:
            mine.wait()

    return pl.pallas_call(
        body, name="grad_chip_scatter", in_specs=[ANY] * na, out_specs=[ANY] * na,
        out_shape=[jax.ShapeDtypeStruct(t.shape, t.dtype) for t in parts],
        scratch_shapes=[pltpu.SemaphoreType.DMA((3 * na,)), pltpu.SemaphoreType.DMA((3 * na,)),
                        pltpu.SemaphoreType.DMA((na,))])(*parts)


def _sibling_swap(rs):
    na = len(rs)

    def body(*refs):
        r_refs, o_refs = refs[:na], refs[na:2 * na]
        send_sems, recv_sems = refs[2 * na:]
        x, y, c = _place()
        cps = []
        for a in range(na):
            cp = _rcopy(send_sems, recv_sems, a, r_refs[a], o_refs[a], (x, y, 1 - c))
            cp.start()
            cps.append(cp)
        for cp in cps:
            cp.wait()

    return pl.pallas_call(
        body, name="grad_sibling_swap", in_specs=[ANY] * na, out_specs=[ANY] * na,
        out_shape=[jax.ShapeDtypeStruct(t.shape, t.dtype) for t in rs],
        scratch_shapes=[pltpu.SemaphoreType.DMA((na,)), pltpu.SemaphoreType.DMA((na,))])(*rs)


def _pack_small(norm_g, q_norm_g, kv_norm_g, final_g):
    flat = jnp.concatenate([norm_g.reshape(-1), q_norm_g.reshape(-1), kv_norm_g.reshape(-1), final_g.reshape(-1),
                            jnp.zeros((SMALL_ROWS * LANES - N_SMALL,), F32)])
    return flat.reshape(SMALL_ROWS, LANES)


def _split_small(s):
    s = s.reshape(-1)
    o = 0
    out = []
    for n, shape in ((DEPTH * D_MODEL, (DEPTH, D_MODEL)), (DEPTH * Q_LORA, (DEPTH, Q_LORA)),
                     (DEPTH * KV_LORA, (DEPTH, KV_LORA)), (D_MODEL, (D_MODEL,))):
        out.append(s[o:o + n].reshape(shape))
        o += n
    return out


def _assemble_w_in(sh):
    z = lambda n: jnp.zeros(sh.shape[1:3] + (n,), sh.dtype)
    s0, s1, s2, s3 = sh[0], sh[1], sh[2], sh[3]
    return jnp.concatenate([s0[..., 0:640], z(64), s0[..., 640:672], z(32), z(256), s0[..., 672:1184],
                            s3[..., 1064:1576], s0[..., 1184:1576], s1, s2, s3[..., 0:1064]], axis=-1)


def _split_w_in_grad(d):
    sh0 = jnp.concatenate([d[..., 0:640], d[..., 704:736], d[..., 1024:1536], d[..., 2048:2440]], axis=-1)
    sh3 = jnp.concatenate([d[..., 5592:6656], d[..., 1536:2048]], axis=-1)
    return jnp.stack([sh0, d[..., 2440:4016], d[..., 4016:5592], sh3]).reshape(N_SHARD, DEPTH * D_MODEL, SHARD_COLS_IN)


def _col_shards(w):
    dl, r, cc = w.shape
    return w.reshape(dl, r, N_SHARD, cc // N_SHARD).transpose(2, 0, 1, 3).reshape(N_SHARD, dl * r, cc // N_SHARD)


def _from_col_shards(g, rows):
    cc = g.shape[-1]
    return g.reshape(N_SHARD, DEPTH, rows, cc).transpose(1, 2, 0, 3).reshape(DEPTH, rows, N_SHARD * cc)


def _pad_w_in(w):
    z = lambda n: jnp.zeros(w.shape[:-1] + (n,), w.dtype)
    return jnp.concatenate([w[..., 0:640], z(64), w[..., 640:672], z(32), z(256), w[..., 672:1184],
                            w[..., 5792:6304], w[..., 1184:5792]], axis=-1)


def _unpad_w_in(w):
    return jnp.concatenate([w[..., 0:640], w[..., 704:736], w[..., 1024:1536], w[..., 2048:6656],
                            w[..., 1536:2048]], axis=-1)


def _pad_w_uq(w):
    s = w.shape[:-1]
    w = w.reshape(s + (MLA_HEADS, 96))
    return jnp.pad(w, [(0, 0)] * (w.ndim - 1) + [(0, 32)]).reshape(s + (1024,))


def _unpad_w_uq(w):
    s = w.shape[:-1]
    return w.reshape(s + (MLA_HEADS, LANES))[..., :96].reshape(s + (768,))


def _pad_w_ukv(w):
    s = w.shape[:-1]
    w = w.reshape(s + (MLA_HEADS, 128))
    kpart = jnp.pad(w[..., :64], [(0, 0)] * (w.ndim - 1) + [(0, 64)]).reshape(s + (1024,))
    vpart = w[..., 64:].reshape(s + (512,))
    return jnp.concatenate([kpart, vpart], axis=-1)


def _unpad_w_ukv(w):
    s = w.shape[:-1]
    kpart = w[..., :1024].reshape(s + (MLA_HEADS, LANES))[..., :64]
    vpart = w[..., 1024:].reshape(s + (MLA_HEADS, 64))
    return jnp.concatenate([kpart, vpart], axis=-1).reshape(s + (1024,))


def _rope_tables(L, dim, lane_lo, period):
    half = dim // 2
    inv = 1.0 / (ROPE_THETA ** (jnp.arange(0, dim, 2, dtype=F32) / dim))
    ang = jnp.arange(L, dtype=F32)[:, None] * inv[None, :]
    cos, sin = jnp.cos(ang), jnp.sin(ang)
    one = lambda n: jnp.ones((L, n), F32)
    zero = lambda n: jnp.zeros((L, n), F32)
    rest = period - lane_lo - dim
    rep = LANES // period
    c = jnp.tile(jnp.concatenate([one(lane_lo), cos, cos, one(rest)], axis=1), (1, rep))
    a = jnp.tile(jnp.concatenate([zero(lane_lo), -sin, zero(half), zero(rest)], axis=1), (1, rep))
    b = jnp.tile(jnp.concatenate([zero(lane_lo + half), sin, zero(rest)], axis=1), (1, rep))
    return c, a, b


def _to_strided(t, d):
    L, w = t.shape
    return t.reshape(L // d, d, w).transpose(1, 0, 2)


def _from_strided(t):
    d, ld, w = t.shape
    return t.transpose(1, 0, 2).reshape(d * ld, w)


def _head_rows(t):
    return t.T.reshape(MLA_HEADS // 2, 2, t.shape[0])


def _head_rows_strided(t, d):
    s = _to_strided(t, d)
    return s.transpose(0, 2, 1).reshape(d, MLA_HEADS // 2, 2, s.shape[1])


def _pad_keys(t):
    return jnp.pad(t, ((0, 0), (DIL_HALF, DIL_HALF), (0, 0)))


def _local_grads(x, target, norm_g, w_in_p, q_norm_g, kv_norm_g, w_uq_p, w_ukv_p, w_out, final_g):
    L = x.shape[0]
    tabs_m = _rope_tables(L, MLA_ROPE, MLA_NOPE, LANES)
    tabs_d = _rope_tables(L, ROT_DIM, 0, DIL_HD)
    tabs_m_t = (tabs_m[0], -tabs_m[1], -tabs_m[2])
    tabs_d_t = (tabs_d[0], -tabs_d[1], -tabs_d[2])
    w_in_t = jnp.swapaxes(w_in_p, 1, 2)
    w_uq_t = jnp.swapaxes(w_uq_p, 1, 2)
    w_ukv_t = jnp.swapaxes(w_ukv_p, 1, 2)
    w_out_t = jnp.swapaxes(w_out, 1, 2)

    saved = []
    for l in range(DEPTH):
        h = _rms_fwd(x, norm_g[l:l + 1], "rms_fwd")
        p = _mm(h, w_in_p[l], tm=512, tn=3328, tk=1024, out_dtype=BF16, name="in_proj")
        q, k, v, cqn, ckvn = _mla_prep(p, q_norm_g[l:l + 1], kv_norm_g[l:l + 1], w_uq_p[l], w_ukv_p[l], tabs_m,
                                       "mla_prep")
        oa, lse_a = _mla_fwd(q, k, v.T, "mla_fwd")
        dil = _dil_prep(p, tabs_d, "dil_prep")
        dil_s, o_g, lse_g = [], [], []
        for g, (_, dd) in enumerate(DIL_PAIRS):
            qs, ks, vs = (_to_strided(t, dd) for t in dil[3 * g:3 * g + 3])
            ks, vs = _pad_keys(ks), _pad_keys(vs)
            v_t4 = vs.reshape(dd, vs.shape[1] // LANES, LANES, HEAD_W).transpose(0, 1, 3, 2)
            og, lg = _dil_fwd(qs, ks, v_t4, "dil_fwd_%d" % dd)
            dil_s.append((qs, ks, vs))
            o_g.append(_from_strided(og))
            lse_g.append(_from_strided(lg))
        ab, bm, lt = _merge_gate(oa, p, o_g, lse_g, "merge_gate")
        x_next = _mm(ab, w_out[l], tm=1024, tn=1024, tk=1024, out_dtype=F32, name="out_proj", add=x)
        saved.append((x, h, p, q, k, v, cqn, ckvn, oa, lse_a, dil_s, bm, lt, ab))
        x = x_next

    loss_b, dx, d_final = _loss_head(x, final_g[None, :], target, "loss_head")
    loss = loss_b[0, 0]

    d_norm, d_qn, d_kvn, d_win, d_wuq, d_wukv, d_wout = [], [], [], [], [], [], []
    for l in reversed(range(DEPTH)):
        x_l, h, p, q, k, v, cqn, ckvn, oa, lse_a, dil_s, bm, lt, ab = saved[l]
        dab = _mm(dx, w_out_t[l], tm=1024, tn=1024, tk=1024, out_dtype=F32, name="out_proj_dgrad")
        d_wout.append(_mm(ab, dx, tm=1024, tn=1024, tk=1024, out_dtype=F32, name="out_proj_wgrad", a_is_kxm=True))
        doa, dbm, D_a, D_b, dgates = _gate_bwd(dab, p, oa, bm, "gate_bwd")
        dq, dk, dv = _mla_bwd(q, k, v, doa, lse_a, _head_rows(D_a), "mla_bwd")
        dp_mla, dq_pre, dkv, dqg, dkvg = _mla_prep_bwd(dq, dk, dv, p, q_norm_g[l:l + 1], kv_norm_g[l:l + 1],
                                                       w_uq_t[l], w_ukv_t[l], tabs_m_t, "mla_prep_bwd")
        d_wuq.append(_mm(cqn, dq_pre, tm=Q_LORA, tn=1024, tk=2048, out_dtype=F32, name="w_uq_wgrad", a_is_kxm=True))
        d_wukv.append(_mm(ckvn, dkv, tm=KV_LORA, tn=1536, tk=2048, out_dtype=F32, name="w_ukv_wgrad", a_is_kxm=True))
        dgr = []
        for g, (_, dd) in enumerate(DIL_PAIRS):
            qs, ks, vs = dil_s[g]
            dqs, dks, dvs = _dil_bwd(qs, ks, vs, _to_strided(dbm, dd), _head_rows_strided(lt, dd),
                                     _head_rows_strided(D_b, dd), "dil_bwd_%d" % dd)
            unpad = lambda t: t[:, DIL_HALF:DIL_HALF + L // dd]
            dgr += [_from_strided(dqs), _from_strided(unpad(dks)), _from_strided(unpad(dvs))]
        dp_dil = _dil_prep_bwd(dgr, tabs_d_t, "dil_prep_bwd")
        dp = jnp.concatenate([dp_mla, dgates, dp_dil], axis=1)
        dh = _mm(dp, w_in_t[l], tm=1024, tn=1024, tk=1664, out_dtype=F32, name="in_proj_dgrad")
        d_win.append(_mm(h, dp, tm=1024, tn=1664, tk=1024, out_dtype=F32, name="in_proj_wgrad", a_is_kxm=True))
        dx, dng = _rms_bwd(dh, x_l, norm_g[l:l + 1], dx, "rms_bwd")
        d_norm.append(dng[0])
        d_qn.append(dqg[0])
        d_kvn.append(dkvg[0])

    rev = lambda xs: jnp.stack(xs[::-1])
    return (loss, dx, rev(d_norm), rev(d_win), rev(d_qn), rev(d_kvn), rev(d_wuq), rev(d_wukv), rev(d_wout),
            d_final[0])


def kernel(x, norm_g, w_in, q_norm_g, kv_norm_g, w_uq, w_ukv, w_out, final_g, loss_target, m_norm_g, m_w_in, m_q_norm_g, m_kv_norm_g, m_w_uq, m_w_ukv, m_w_out, m_final_g, v_norm_g, v_w_in, v_q_norm_g, v_kv_norm_g, v_w_uq, v_w_ukv, v_w_out, v_final_g):
    c = lax.axis_index("c")

    def families(a_in, a_uq, a_ukv, a_out):
        return [t.reshape(shape) for t, shape in zip((a_in, a_uq, a_ukv, a_out), FAM_SHAPES)]

    g_in, g_uq, g_ukv, g_out = _allgather_weights([t.astype(BF16) for t in families(w_in, w_uq, w_ukv, w_out)])
    w_in_p = _assemble_w_in(g_in.reshape(N_SHARD, DEPTH, D_MODEL, SHARD_COLS_IN))
    w_uq_p = _pad_w_uq(_from_col_shards(g_uq, Q_LORA))
    w_ukv_p = _pad_w_ukv(_from_col_shards(g_ukv, KV_LORA))
    w_out_f = g_out.reshape(N_SHARD, DEPTH, 1024 // N_SHARD, D_MODEL).transpose(1, 0, 2, 3).reshape(DEPTH, 1024, D_MODEL)

    (loss, dx, d_norm, d_win_p, d_qn, d_kvn, d_wuq_p, d_wukv_p, d_wout, d_final) = _local_grads(
        x[0], loss_target[0], norm_g, w_in_p, q_norm_g, kv_norm_g, w_uq_p, w_ukv_p, w_out_f, final_g)
    loss = lax.psum(loss, ("x", "y", "c"))

    small = _pack_small(d_norm, d_qn, d_kvn, d_final)
    grads = [_split_w_in_grad(d_win_p), _col_shards(_unpad_w_uq(d_wuq_p)), _col_shards(_unpad_w_ukv(d_wukv_p)),
             d_wout.reshape(DEPTH, N_SHARD, 1024 // N_SHARD, D_MODEL).transpose(1, 0, 2, 3).reshape(
                 N_SHARD, DEPTH * (1024 // N_SHARD), D_MODEL),
             jnp.broadcast_to(small[None], (N_SHARD, SMALL_ROWS, LANES))]
    halves = [g.reshape(N_SHARD, 2, g.shape[1] // 2, g.shape[2]) for g in grads]
    from_sib = _sibling_send_halves(halves)
    chip_sum = [_add2(lax.dynamic_index_in_dim(h, c, axis=1, keepdims=False), f, "grad_add_pair", BF16)
                for h, f in zip(halves, from_sib)]
    red_half = [_add4_ordered(t, "grad_add_chips") for t in _chip_scatter(chip_sum)]
    other_half = _sibling_swap(red_half)
    gred = []
    for mine, other in zip(red_half, other_half):
        both = jnp.stack([mine, other])
        gred.append(jnp.concatenate([lax.dynamic_index_in_dim(both, c, axis=0, keepdims=False),
                                     lax.dynamic_index_in_dim(both, 1 - c, axis=0, keepdims=False)], axis=0))

    wf = families(w_in, w_uq, w_ukv, w_out) + [_pack_small(norm_g, q_norm_g, kv_norm_g, final_g)]
    mf = families(m_w_in, m_w_uq, m_w_ukv, m_w_out) + [_pack_small(m_norm_g, m_q_norm_g, m_kv_norm_g, m_final_g)]
    vf = families(v_w_in, v_w_uq, v_w_ukv, v_w_out) + [_pack_small(v_norm_g, v_q_norm_g, v_kv_norm_g, v_final_g)]
    upd = [_adamw(w, g, m, v, "adamw") for w, g, m, v in zip(wf, gred, mf, vf)]

    def leaves(fams):
        a_in, a_uq, a_ukv, a_out, s = fams
        s_norm, s_qn, s_kvn, s_final = _split_small(s)
        return [s_norm, a_in.reshape(w_in.shape), s_qn, s_kvn, a_uq.reshape(w_uq.shape), a_ukv.reshape(w_ukv.shape),
                a_out.reshape(w_out.shape), s_final]

    return (loss, dx[None], *leaves(gred), *leaves([u[0] for u in upd]), *leaves([u[1] for u in upd]),
            *leaves([u[2] for u in upd]))
```

```python
import functools

import jax
import jax.numpy as jnp
from jax import lax
from jax.experimental import pallas as pl
from jax.experimental.pallas import tpu as pltpu

F32 = jnp.float32
BF16 = jnp.bfloat16
MESH = pl.DeviceIdType.MESH

D_MODEL = 1024
DEPTH = 4
MLA_HEADS = 8
MLA_NOPE = 64
MLA_ROPE = 32
Q_LORA = 384
KV_LORA = 256
DIL_PAIRS = ((128, 1), (512, 4), (2048, 16))
DIL_HD = 64
DIL_HALF = 64
ROT_DIM = 16
ROPE_THETA = 500000.0
EPS = 1e-6
IN_WIDTH = 6304
N_SHARD = 4

P_WIDTH = 6656
P_MLA = 1024
P_GATE = 1024
P_DIL0 = 2048
LANES = 128
HEAD_W = 512

ADAM_LR = 0.001
ADAM_B1 = 0.9
ADAM_B2 = 0.999
ADAM_EPS = 1e-08
ADAM_WD = 0.01
ADAM_STEP = 10

SHARD_COLS_IN = IN_WIDTH // N_SHARD
FAM_SHAPES = ((DEPTH * D_MODEL, SHARD_COLS_IN), (DEPTH * Q_LORA, 768 // N_SHARD), (DEPTH * KV_LORA, 1024 // N_SHARD),
              (DEPTH * (1024 // N_SHARD), D_MODEL))
N_SMALL = DEPTH * (D_MODEL + Q_LORA + KV_LORA) + D_MODEL
SMALL_ROWS = 64
VMEM_BIG_MB = 48


def _cparams(vmem_mb=None):
    if vmem_mb is None:
        return None
    return pltpu.CompilerParams(vmem_limit_bytes=vmem_mb << 20)


def _sigmoid(x):
    return 1.0 / (1.0 + jnp.exp(-x))


def _rope(x, c, a, b, sh):
    return x * c + pltpu.roll(x, LANES - sh, 1) * a + pltpu.roll(x, sh, 1) * b


def _per_head8(x, pick_first):
    r = lax.broadcasted_iota(jnp.int32, (HEAD_W, MLA_HEADS), 0)
    c = lax.broadcasted_iota(jnp.int32, (HEAD_W, MLA_HEADS), 1)
    sel = (r == c * DIL_HD) if pick_first else (r // DIL_HD == c)
    mat = jnp.where(sel, 1.0, 0.0).astype(BF16)
    out = jnp.zeros((x.shape[0], MLA_HEADS), F32)
    for _ in range(3):
        part = x.astype(BF16)
        out = out + jnp.dot(part, mat, preferred_element_type=F32)
        x = x - part.astype(F32)
    return out


def _mm(a, b, *, tm, tn, tk, out_dtype, name, add=None, a_is_kxm=False):
    K, M = a.shape if a_is_kxm else a.shape[::-1]
    N = b.shape[1]
    tm, tn, tk = min(tm, M), min(tn, N), min(tk, K)
    assert M % tm == 0 and N % tn == 0 and K % tk == 0, (a.shape, b.shape)
    nk = K // tk
    has_add = add is not None

    def body(*refs):
        if has_add:
            a_ref, b_ref, add_ref, o_ref, acc = refs
        else:
            a_ref, b_ref, o_ref, acc = refs
        k = pl.program_id(2)
        if a_is_kxm:
            part = lax.dot_general(a_ref[...].astype(BF16), b_ref[...].astype(BF16), (((0,), (0,)), ((), ())),
                                   preferred_element_type=F32)
        else:
            part = jnp.dot(a_ref[...].astype(BF16), b_ref[...].astype(BF16), preferred_element_type=F32)

        @pl.when(k == 0)
        def _():
            acc[...] = part

        @pl.when(k > 0)
        def _():
            acc[...] += part

        @pl.when(k == nk - 1)
        def _():
            r = acc[...]
            if has_add:
                r = r + add_ref[...]
            o_ref[...] = r.astype(out_dtype)

    a_spec = pl.BlockSpec((tk, tm), lambda i, j, k: (k, i)) if a_is_kxm else pl.BlockSpec((tm, tk), lambda i, j, k: (i, k))
    in_specs = [a_spec, pl.BlockSpec((tk, tn), lambda i, j, k: (k, j))]
    args = [a, b]
    if has_add:
        in_specs.append(pl.BlockSpec((tm, tn), lambda i, j, k: (i, j)))
        args.append(add)
    return pl.pallas_call(
        body, name=name, grid=(M // tm, N // tn, nk), in_specs=in_specs,
        out_specs=pl.BlockSpec((tm, tn), lambda i, j, k: (i, j)),
        out_shape=jax.ShapeDtypeStruct((M, N), out_dtype),
        scratch_shapes=[pltpu.VMEM((tm, tn), F32)], compiler_params=_cparams(VMEM_BIG_MB))(*args)


def _row_spec(tm, w, cb=0):
    return pl.BlockSpec((tm, w), lambda i: (i, cb))


def _const_spec(arr):
    nd = arr.ndim
    return pl.BlockSpec(arr.shape, lambda i: (0,) * nd)


def _rms_fwd(x, g, name):
    L, D = x.shape
    tm = min(512, L)

    def body(x_ref, g_ref, o_ref):
        xv = x_ref[...]
        r = lax.rsqrt(jnp.mean(xv * xv, axis=-1, keepdims=True) + EPS)
        o_ref[...] = (xv * r * g_ref[...]).astype(BF16)

    return pl.pallas_call(
        body, name=name, grid=(L // tm,), in_specs=[_row_spec(tm, D), _const_spec(g)],
        out_specs=_row_spec(tm, D), out_shape=jax.ShapeDtypeStruct((L, D), BF16))(x, g)


def _rms_bwd(dh, x, g, dres, name):
    L, D = x.shape
    tm = min(512, L)

    def body(dh_ref, x_ref, g_ref, dres_ref, dx_ref, dg_ref):
        xv = x_ref[...]
        dy = dh_ref[...]
        r = lax.rsqrt(jnp.mean(xv * xv, axis=-1, keepdims=True) + EPS)
        dyg = dy * g_ref[...]
        dx_ref[...] = dres_ref[...] + r * dyg - xv * (r * r * r) * jnp.mean(dyg * xv, axis=-1, keepdims=True)
        part = jnp.sum(dy * xv * r, axis=0, keepdims=True)

        @pl.when(pl.program_id(0) == 0)
        def _():
            dg_ref[...] = part

        @pl.when(pl.program_id(0) > 0)
        def _():
            dg_ref[...] += part

    return pl.pallas_call(
        body, name=name, grid=(L // tm,),
        in_specs=[_row_spec(tm, D), _row_spec(tm, D), _const_spec(g), _row_spec(tm, D)],
        out_specs=[_row_spec(tm, D), pl.BlockSpec((1, D), lambda i: (0, 0))],
        out_shape=[jax.ShapeDtypeStruct((L, D), F32), jax.ShapeDtypeStruct((1, D), F32)])(dh, x, g, dres)


def _loss_head(x, g, target, name):
    L, D = x.shape
    tm = min(512, L)

    def body(x_ref, g_ref, t_ref, loss_ref, dx_ref, dg_ref):
        xv = x_ref[...]
        gv = g_ref[...]
        r = lax.rsqrt(jnp.mean(xv * xv, axis=-1, keepdims=True) + EPS)
        xr = xv * r
        err = xr * gv - t_ref[...]
        lp = 0.5 * jnp.sum(jnp.mean(err * err, axis=-1, keepdims=True))
        dy = err * (1.0 / D)
        dyg = dy * gv
        dx_ref[...] = r * dyg - xv * (r * r * r) * jnp.mean(dyg * xv, axis=-1, keepdims=True)
        part = jnp.sum(dy * xr, axis=0, keepdims=True)

        @pl.when(pl.program_id(0) == 0)
        def _():
            dg_ref[...] = part
            loss_ref[...] = jnp.zeros(loss_ref.shape, F32) + lp

        @pl.when(pl.program_id(0) > 0)
        def _():
            dg_ref[...] += part
            loss_ref[...] += lp

    return pl.pallas_call(
        body, name=name, grid=(L // tm,),
        in_specs=[_row_spec(tm, D), _const_spec(g), _row_spec(tm, D)],
        out_specs=[pl.BlockSpec((8, LANES), lambda i: (0, 0)), _row_spec(tm, D), pl.BlockSpec((1, D), lambda i: (0, 0))],
        out_shape=[jax.ShapeDtypeStruct((8, LANES), F32), jax.ShapeDtypeStruct((L, D), F32),
                   jax.ShapeDtypeStruct((1, D), F32)])(x, g, target)


def _mla_prep(p, qg, kvg, wuq, wukv, tabs, name):
    L = p.shape[0]
    tm = min(512, L)
    scale = (MLA_NOPE + MLA_ROPE) ** -0.5
    tc, ta, tb = tabs

    def body(p_ref, qg_ref, kvg_ref, wuq_ref, wukv_ref, c_ref, a_ref, b_ref, q_ref, k_ref, v_ref, cqn_ref, ckvn_ref):
        c, a, b = c_ref[...], a_ref[...], b_ref[...]
        cq = p_ref[:, 0:Q_LORA].astype(F32)
        ckv = p_ref[:, Q_LORA:Q_LORA + KV_LORA].astype(F32)
        kr = p_ref[:, 640:768].astype(F32)
        cqn = (cq * lax.rsqrt(jnp.mean(cq * cq, axis=-1, keepdims=True) + EPS) * qg_ref[...]).astype(BF16)
        ckvn = (ckv * lax.rsqrt(jnp.mean(ckv * ckv, axis=-1, keepdims=True) + EPS) * kvg_ref[...]).astype(BF16)
        cqn_ref[...] = cqn
        ckvn_ref[...] = ckvn
        q = jnp.dot(cqn, wuq_ref[...], preferred_element_type=F32)
        kv = jnp.dot(ckvn, wukv_ref[...], preferred_element_type=F32)
        krr = _rope(kr, c, a, b, MLA_ROPE // 2)
        for h in range(MLA_HEADS):
            sl = slice(h * LANES, (h + 1) * LANES)
            q_ref[:, sl] = (_rope(q[:, sl], c, a, b, MLA_ROPE // 2) * (scale * LOG2E)).astype(BF16)
            k_ref[:, sl] = (kv[:, sl] + krr).astype(BF16)
        v_ref[...] = kv[:, 1024:1536].astype(BF16)

    return pl.pallas_call(
        body, name=name, grid=(L // tm,),
        in_specs=[_row_spec(tm, P_MLA, 0), _const_spec(qg), _const_spec(kvg), _const_spec(wuq), _const_spec(wukv),
                  _row_spec(tm, LANES), _row_spec(tm, LANES), _row_spec(tm, LANES)],
        out_specs=[_row_spec(tm, 1024), _row_spec(tm, 1024), _row_spec(tm, HEAD_W), _row_spec(tm, Q_LORA),
                   _row_spec(tm, KV_LORA)],
        out_shape=[jax.ShapeDtypeStruct((L, 1024), BF16), jax.ShapeDtypeStruct((L, 1024), BF16),
                   jax.ShapeDtypeStruct((L, HEAD_W), BF16), jax.ShapeDtypeStruct((L, Q_LORA), BF16),
                   jax.ShapeDtypeStruct((L, KV_LORA), BF16)],
        compiler_params=_cparams(VMEM_BIG_MB))(p, qg, kvg, wuq, wukv, tc, ta, tb)


def _mla_prep_bwd(dq, dk, dv, p, qg, kvg, wuq_t, wukv_t, tabs_t, name):
    L = p.shape[0]
    tm = min(512, L)
    scale = (MLA_NOPE + MLA_ROPE) ** -0.5
    tc, ta, tb = tabs_t

    def body(dq_ref, dk_ref, dv_ref, p_ref, qg_ref, kvg_ref, wuqt_ref, wukvt_ref, c_ref, a_ref, b_ref,
             dp_ref, dqp_ref, dkv_ref, dqg_ref, dkvg_ref):
        c, a, b = c_ref[...], a_ref[...], b_ref[...]
        dkr = jnp.zeros((tm, LANES), F32)
        for h in range(MLA_HEADS):
            sl = slice(h * LANES, (h + 1) * LANES)
            dqp_ref[:, sl] = (_rope(dq_ref[:, sl], c, a, b, MLA_ROPE // 2) * scale).astype(BF16)
            dkh = dk_ref[:, sl] * LN2
            dkv_ref[:, sl] = dkh.astype(BF16)
            dkr = dkr + dkh
        dkv_ref[:, 1024:1536] = dv_ref[...].astype(BF16)
        lane = lax.broadcasted_iota(jnp.int32, (tm, LANES), 1)
        dkr = jnp.where((lane >= MLA_NOPE) & (lane < MLA_NOPE + MLA_ROPE), _rope(dkr, c, a, b, MLA_ROPE // 2), 0.0)

        d_cqn = jnp.dot(dqp_ref[...], wuqt_ref[...], preferred_element_type=F32)
        d_ckvn = jnp.dot(dkv_ref[...], wukvt_ref[...], preferred_element_type=F32)

        def norm_bwd(xv, gv, dy):
            r = lax.rsqrt(jnp.mean(xv * xv, axis=-1, keepdims=True) + EPS)
            dyg = dy * gv
            dx = r * dyg - xv * (r * r * r) * jnp.mean(dyg * xv, axis=-1, keepdims=True)
            return dx, jnp.sum(dy * xv * r, axis=0, keepdims=True)

        d_cq, dqg = norm_bwd(p_ref[:, 0:Q_LORA].astype(F32), qg_ref[...], d_cqn)
        d_ckv, dkvg = norm_bwd(p_ref[:, Q_LORA:Q_LORA + KV_LORA].astype(F32), kvg_ref[...], d_ckvn)
        dp_ref[:, 0:Q_LORA] = d_cq.astype(BF16)
        dp_ref[:, Q_LORA:Q_LORA + KV_LORA] = d_ckv.astype(BF16)
        dp_ref[:, 640:768] = dkr.astype(BF16)
        dp_ref[:, 768:1024] = jnp.zeros((tm, 256), BF16)

        @pl.when(pl.program_id(0) == 0)
        def _():
            dqg_ref[...] = dqg
            dkvg_ref[...] = dkvg

        @pl.when(pl.program_id(0) > 0)
        def _():
            dqg_ref[...] += dqg
            dkvg_ref[...] += dkvg

    return pl.pallas_call(
        body, name=name, grid=(L // tm,),
        in_specs=[_row_spec(tm, 1024), _row_spec(tm, 1024), _row_spec(tm, HEAD_W), _row_spec(tm, P_MLA, 0),
                  _const_spec(qg), _const_spec(kvg), _const_spec(wuq_t), _const_spec(wukv_t),
                  _row_spec(tm, LANES), _row_spec(tm, LANES), _row_spec(tm, LANES)],
        out_specs=[_row_spec(tm, P_MLA), _row_spec(tm, 1024), _row_spec(tm, 1536),
                   pl.BlockSpec((1, Q_LORA), lambda i: (0, 0)), pl.BlockSpec((1, KV_LORA), lambda i: (0, 0))],
        out_shape=[jax.ShapeDtypeStruct((L, P_MLA), BF16), jax.ShapeDtypeStruct((L, 1024), BF16),
                   jax.ShapeDtypeStruct((L, 1536), BF16), jax.ShapeDtypeStruct((1, Q_LORA), F32),
                   jax.ShapeDtypeStruct((1, KV_LORA), F32)],
        compiler_params=_cparams(VMEM_BIG_MB))(dq, dk, dv, p, qg, kvg, wuq_t, wukv_t, tc, ta, tb)


def _dil_prep(p, tabs, name):
    L = p.shape[0]
    tm = min(512, L)
    tc, ta, tb = tabs
    scale = DIL_HD ** -0.5

    def body(*refs):
        ins, (c_ref, a_ref, b_ref), outs = refs[:9], refs[9:12], refs[12:]
        c, a, b = c_ref[...], a_ref[...], b_ref[...]
        for n in range(9):
            t = n % 3
            for cb in range(HEAD_W // LANES):
                sl = slice(cb * LANES, (cb + 1) * LANES)
                xv = ins[n][:, sl].astype(F32)
                if t == 0:
                    xv = _rope(xv, c, a, b, ROT_DIM // 2) * (scale * LOG2E)
                elif t == 1:
                    xv = _rope(xv, c, a, b, ROT_DIM // 2)
                outs[n][:, sl] = xv.astype(BF16)

    in_specs = [_row_spec(tm, HEAD_W, P_DIL0 // HEAD_W + n) for n in range(9)] + [_row_spec(tm, LANES)] * 3
    return pl.pallas_call(
        body, name=name, grid=(L // tm,), in_specs=in_specs,
        out_specs=[_row_spec(tm, HEAD_W)] * 9,
        out_shape=[jax.ShapeDtypeStruct((L, HEAD_W), BF16)] * 9)(*([p] * 9), tc, ta, tb)


def _dil_prep_bwd(grads, tabs_t, name):
    L = grads[0].shape[0]
    tm = min(512, L)
    tc, ta, tb = tabs_t
    scale = DIL_HD ** -0.5

    def body(*refs):
        ins, (c_ref, a_ref, b_ref), o_ref = refs[:9], refs[9:12], refs[12]
        c, a, b = c_ref[...], a_ref[...], b_ref[...]
        for n in range(9):
            t = n % 3
            for cb in range(HEAD_W // LANES):
                sl = slice(cb * LANES, (cb + 1) * LANES)
                xv = ins[n][:, sl]
                if t == 0:
                    xv = _rope(xv, c, a, b, ROT_DIM // 2) * scale
                elif t == 1:
                    xv = _rope(xv, c, a, b, ROT_DIM // 2) * LN2
                o_ref[:, n * HEAD_W + cb * LANES:n * HEAD_W + (cb + 1) * LANES] = xv.astype(BF16)

    return pl.pallas_call(
        body, name=name, grid=(L // tm,), in_specs=[_row_spec(tm, HEAD_W)] * 9 + [_row_spec(tm, LANES)] * 3,
        out_specs=_row_spec(tm, 9 * HEAD_W), out_shape=jax.ShapeDtypeStruct((L, 9 * HEAD_W), BF16),
        compiler_params=_cparams(VMEM_BIG_MB))(*grads, tc, ta, tb)


def _merge_gate(oa, p, o_g, lse_g, name):
    L = oa.shape[0]
    tm = min(512, L)

    def body(oa_ref, ga_ref, gb_ref, o1, o2, o3, l1, l2, l3, ab_ref, bm_ref, lt_ref):
        la, lb, lc = l1[...], l2[...], l3[...]
        m = jnp.maximum(jnp.maximum(la, lb), lc)
        ea, eb, ec = jnp.exp2(la - m), jnp.exp2(lb - m), jnp.exp2(lc - m)
        den = ea + eb + ec
        bm = (ea * o1[...] + eb * o2[...] + ec * o3[...]) / den
        bm_ref[...] = bm
        lt_ref[...] = _per_head8(m + jnp.log2(den), True)
        ga, gb = ga_ref[...].astype(F32), gb_ref[...].astype(F32)
        ab_ref[:, 0:HEAD_W] = (oa_ref[...] * (ga * _sigmoid(ga))).astype(BF16)
        ab_ref[:, HEAD_W:2 * HEAD_W] = (bm * (gb * _sigmoid(gb))).astype(BF16)

    w = _row_spec(tm, HEAD_W)
    return pl.pallas_call(
        body, name=name, grid=(L // tm,),
        in_specs=[w, _row_spec(tm, HEAD_W, 2), _row_spec(tm, HEAD_W, 3), w, w, w, w, w, w],
        out_specs=[_row_spec(tm, 2 * HEAD_W), w, _row_spec(tm, MLA_HEADS)],
        out_shape=[jax.ShapeDtypeStruct((L, 2 * HEAD_W), BF16), jax.ShapeDtypeStruct((L, HEAD_W), F32),
                   jax.ShapeDtypeStruct((L, MLA_HEADS), F32)])(oa, p, p, *o_g, *lse_g)


def _gate_bwd(dab, p, oa, bm, name):
    L = oa.shape[0]
    tm = min(512, L)

    def body(da_ref, db_ref, ga_ref, gb_ref, oa_ref, bm_ref, doa_ref, dbm_ref, Da_ref, Db_ref, dg_ref):
        def one(d, g, o, do_ref, D_ref, col):
            sg = _sigmoid(g)
            do = d * (g * sg)
            do_ref[...] = do.astype(BF16)
            dg_ref[:, col:col + HEAD_W] = (d * o * (sg * (1.0 + g * (1.0 - sg)))).astype(BF16)
            D_ref[...] = _per_head8(do * o, False)

        one(da_ref[...], ga_ref[...].astype(F32), oa_ref[...], doa_ref, Da_ref, 0)
        one(db_ref[...], gb_ref[...].astype(F32), bm_ref[...], dbm_ref, Db_ref, HEAD_W)

    w = _row_spec(tm, HEAD_W)
    w8 = _row_spec(tm, MLA_HEADS)
    return pl.pallas_call(
        body, name=name, grid=(L // tm,),
        in_specs=[_row_spec(tm, HEAD_W, 0), _row_spec(tm, HEAD_W, 1), _row_spec(tm, HEAD_W, 2),
                  _row_spec(tm, HEAD_W, 3), w, w],
        out_specs=[w, w, w8, w8, _row_spec(tm, 2 * HEAD_W)],
        out_shape=[jax.ShapeDtypeStruct((L, HEAD_W), BF16), jax.ShapeDtypeStruct((L, HEAD_W), BF16),
                   jax.ShapeDtypeStruct((L, MLA_HEADS), F32), jax.ShapeDtypeStruct((L, MLA_HEADS), F32),
                   jax.ShapeDtypeStruct((L, 2 * HEAD_W), BF16)])(dab, dab, p, p, oa, bm)


NT = (((1,), (1,)), ((), ()))
TN = (((0,), (0,)), ((), ()))
NEG = -1e30


MLA_TQ = 512
MLA_TK = 2048
MLA_BWD_TK = 1024
LOG2E = 1.4426950408889634
LN2 = 0.6931471805599453


def _mla_fwd(q, k, v_t, name):
    L = q.shape[0]
    tq, tk = min(MLA_TQ, L), min(MLA_TK, L)
    nq, nk = L // tq, L // tk
    npair = MLA_HEADS // 2

    def body(q_ref, k_ref, vt_ref, o_ref, lse_ref, m0, l0, a0, m1, l1, a1):
        j = pl.program_id(2)
        stats = ((m0, l0, a0), (m1, l1, a1))

        @pl.when(j == 0)
        def _():
            for m_sc, l_sc, acc_sc in stats:
                m_sc[...] = jnp.full(m_sc.shape, NEG, F32)
                l_sc[...] = jnp.zeros(l_sc.shape, F32)
                acc_sc[...] = jnp.zeros(acc_sc.shape, F32)

        s_ts = [lax.dot_general(k_ref[:, hh * LANES:(hh + 1) * LANES], q_ref[:, hh * LANES:(hh + 1) * LANES], NT,
                                preferred_element_type=F32) for hh in range(2)]
        for hh in range(2):
            m_sc, l_sc, acc_sc = stats[hh]
            s_t = s_ts[hh]
            m_prev = m_sc[...]
            m_new = jnp.maximum(m_prev, jnp.max(s_t, axis=0, keepdims=True))
            alpha = jnp.exp2(m_prev - m_new)
            p_t = jnp.exp2(s_t - m_new)
            l_sc[...] = alpha * l_sc[...] + jnp.sum(p_t, axis=0, keepdims=True)
            m_sc[...] = m_new
            pv = jnp.dot(vt_ref[hh * DIL_HD:(hh + 1) * DIL_HD, :], p_t.astype(BF16),
                         preferred_element_type=F32)
            acc_sc[...] = alpha * acc_sc[...] + pv

        @pl.when(j == nk - 1)
        def _():
            o_ref[...] = jnp.concatenate([a0[...] / l0[...], a1[...] / l1[...]], axis=0).T
            lse_ref[...] = jnp.concatenate([m0[...] + jnp.log2(l0[...]), m1[...] + jnp.log2(l1[...])], axis=0)

    stat = [pltpu.VMEM((1, tq), F32), pltpu.VMEM((1, tq), F32), pltpu.VMEM((DIL_HD, tq), F32)]
    return pl.pallas_call(
        body, name=name, grid=(npair, nq, nk),
        in_specs=[pl.BlockSpec((tq, 2 * LANES), lambda pr, i, j: (i, pr)),
                  pl.BlockSpec((tk, 2 * LANES), lambda pr, i, j: (j, pr)),
                  pl.BlockSpec((LANES, tk), lambda pr, i, j: (pr, j))],
        out_specs=[pl.BlockSpec((tq, LANES), lambda pr, i, j: (i, pr)),
                   pl.BlockSpec((None, 2, tq), lambda pr, i, j: (pr, 0, i))],
        out_shape=[jax.ShapeDtypeStruct((L, HEAD_W), F32), jax.ShapeDtypeStruct((npair, 2, L), F32)],
        scratch_shapes=stat + stat, compiler_params=_cparams(VMEM_BIG_MB))(q, k, v_t)


def _mla_bwd(q, k, v, q_t, do, do_t, lse_cols, d_cols, name):
    L = q.shape[0]
    tq, tk = min(512, L), min(MLA_BWD_TK, L)
    nq, nk = L // tq, L // tk
    npair = MLA_HEADS // 2

    def body(q_ref, k_ref, v_ref, qt_ref, do_ref, dot_ref, lse_ref, d_ref, dq_ref, dkt_ref, dvt_ref):
        j, i = pl.program_id(1), pl.program_id(2)

        @pl.when((j == 0) & (i == 0))
        def _():
            dq_ref[...] = jnp.zeros(dq_ref.shape, F32)

        @pl.when(i == 0)
        def _():
            dkt_ref[...] = jnp.zeros(dkt_ref.shape, F32)
            dvt_ref[...] = jnp.zeros(dvt_ref.shape, F32)

        first = lax.broadcasted_iota(jnp.int32, (tq, LANES), 1) < DIL_HD
        dov = do_ref[...]
        vv = v_ref[...]
        rows = pl.ds(pl.multiple_of(i * tq, tq), tq)
        for hh in range(2):
            sl = slice(hh * LANES, (hh + 1) * LANES)
            hrows = slice(hh * DIL_HD, (hh + 1) * DIL_HD)
            qh, kh = q_ref[:, sl], k_ref[:, sl]
            do_h = jnp.where(first if hh == 0 else ~first, dov, jnp.zeros_like(dov))
            s = lax.dot_general(qh, kh, NT, preferred_element_type=F32)
            p = jnp.exp2(s - lse_ref[:, hh:hh + 1])
            dvt_ref[hrows, :] += jnp.dot(dot_ref[hrows, :], p.astype(BF16), preferred_element_type=F32)
            dp = lax.dot_general(do_h, vv, NT, preferred_element_type=F32)
            ds = (p * (dp - d_ref[:, hh:hh + 1])).astype(BF16)
            dq_ref[rows, sl] += jnp.dot(ds, kh, preferred_element_type=F32)
            dkt_ref[sl, :] += jnp.dot(qt_ref[sl, :], ds, preferred_element_type=F32)

    colspec = pl.BlockSpec((None, tq, 2), lambda pr, j, i: (pr, i, 0))
    return pl.pallas_call(
        body, name=name, grid=(npair, nk, nq),
        in_specs=[pl.BlockSpec((tq, 2 * LANES), lambda pr, j, i: (i, pr)),
                  pl.BlockSpec((tk, 2 * LANES), lambda pr, j, i: (j, pr)),
                  pl.BlockSpec((tk, LANES), lambda pr, j, i: (j, pr)),
                  pl.BlockSpec((2 * LANES, tq), lambda pr, j, i: (pr, i)),
                  pl.BlockSpec((tq, LANES), lambda pr, j, i: (i, pr)),
                  pl.BlockSpec((LANES, tq), lambda pr, j, i: (pr, i)),
                  colspec, colspec],
        out_specs=[pl.BlockSpec((L, 2 * LANES), lambda pr, j, i: (0, pr)),
                   pl.BlockSpec((2 * LANES, tk), lambda pr, j, i: (pr, j)),
                   pl.BlockSpec((LANES, tk), lambda pr, j, i: (pr, j))],
        out_shape=[jax.ShapeDtypeStruct((L, 1024), F32), jax.ShapeDtypeStruct((1024, L), F32),
                   jax.ShapeDtypeStruct((HEAD_W, L), F32)],
        compiler_params=_cparams(VMEM_BIG_MB))(q, k, v, q_t, do, do_t, lse_cols, d_cols)


DIL_SQ = 128
DIL_SW = DIL_SQ + 2 * DIL_HALF


def _dil_band_mask(a_sub, ld):
    kidx = lax.broadcasted_iota(jnp.int32, (DIL_SW, 2 * DIL_SQ), 0)
    qidx = lax.broadcasted_iota(jnp.int32, (DIL_SW, 2 * DIL_SQ), 1) % DIL_SQ
    rel = kidx - DIL_HALF - qidx
    return (jnp.abs(rel) <= DIL_HALF) & (kidx >= DIL_HALF - a_sub) & (kidx < ld + DIL_HALF - a_sub)


def _pair_rows(x, first):
    zero = jnp.zeros_like(x)
    return jnp.concatenate([jnp.where(first, x, zero), jnp.where(first, zero, x)], axis=0)


def _dil_fwd(q, kp, vp_t4, name):
    d, ld, _ = q.shape
    assert ld % DIL_SQ == 0
    tq = min(512, ld)
    nq = ld // tq
    npair = HEAD_W // LANES
    nb = (ld + 2 * DIL_HALF) // LANES

    def body(q_ref, k_ref, vt_ref, o_ref, lse_ref):
        i = pl.program_id(2)
        first = lax.broadcasted_iota(jnp.int32, (DIL_SQ, LANES), 1) < DIL_HD
        for u in range(tq // DIL_SQ):
            a_sub = pl.multiple_of(i * tq + u * DIL_SQ, DIL_SQ)
            kk = a_sub // LANES
            rows = slice(u * DIL_SQ, (u + 1) * DIL_SQ)
            kwin = k_ref[pl.ds(a_sub, DIL_SW), :]
            valid = _dil_band_mask(a_sub, ld)[:, 0:DIL_SQ]
            qv = q_ref[rows, :]
            outs, lses = [], []
            for hh in range(2):
                qh = jnp.where(first if hh == 0 else ~first, qv, jnp.zeros_like(qv))
                s_t = jnp.where(valid, lax.dot_general(kwin, qh, NT, preferred_element_type=F32), NEG)
                m = jnp.max(s_t, axis=0, keepdims=True)
                p32 = jnp.exp2(s_t - m)
                l = jnp.sum(p32, axis=0, keepdims=True)
                p_t = p32.astype(BF16)
                hrows = slice(hh * DIL_HD, (hh + 1) * DIL_HD)
                pv = (jnp.dot(vt_ref[kk, hrows, :], p_t[0:LANES, :], preferred_element_type=F32)
                      + jnp.dot(vt_ref[kk + 1, hrows, :], p_t[LANES:DIL_SW, :], preferred_element_type=F32))
                outs.append(pv / l)
                lses.append(jnp.broadcast_to(m + jnp.log2(l), (DIL_HD, DIL_SQ)))
            o_ref[rows, :] = jnp.concatenate(outs, axis=0).T
            lse_ref[rows, :] = jnp.concatenate(lses, axis=0).T

    blk = pl.BlockSpec((None, tq, LANES), lambda r, pr, i: (r, i, pr))
    full = pl.BlockSpec((None, ld + 2 * DIL_HALF, LANES), lambda r, pr, i: (r, 0, pr))
    vspec = pl.BlockSpec((None, nb, LANES, LANES), lambda r, pr, i: (r, 0, pr, 0))
    return pl.pallas_call(
        body, name=name, grid=(d, npair, nq), in_specs=[blk, full, vspec], out_specs=[blk, blk],
        out_shape=[jax.ShapeDtypeStruct((d, ld, HEAD_W), F32), jax.ShapeDtypeStruct((d, ld, HEAD_W), F32)],
        compiler_params=_cparams(VMEM_BIG_MB))(q, kp, vp_t4)


def _dil_bwd(q, kp, vp, do, lse_rows, d_rows, name):
    d, ld, _ = q.shape
    assert ld % DIL_SQ == 0
    tq = min(512, ld)
    nq = ld // tq
    npair = HEAD_W // LANES
    ldp = ld + 2 * DIL_HALF
    span = tq + 2 * DIL_HALF

    def body(q_ref, k_ref, v_ref, do_ref, lse_ref, d_ref, dq_ref, dk_ref, dv_ref, dk_sc, dv_sc):
        i = pl.program_id(2)

        @pl.when(i == 0)
        def _():
            dk_ref[...] = jnp.zeros(dk_ref.shape, F32)
            dv_ref[...] = jnp.zeros(dv_ref.shape, F32)

        dk_sc[...] = jnp.zeros(dk_sc.shape, F32)
        dv_sc[...] = jnp.zeros(dv_sc.shape, F32)
        first = lax.broadcasted_iota(jnp.int32, (DIL_SQ, LANES), 1) < DIL_HD
        for u in range(tq // DIL_SQ):
            a_sub = pl.multiple_of(i * tq + u * DIL_SQ, DIL_SQ)
            rows = slice(u * DIL_SQ, (u + 1) * DIL_SQ)
            win = slice(u * DIL_SQ, u * DIL_SQ + DIL_SW)
            kwin = k_ref[pl.ds(a_sub, DIL_SW), :]
            vwin = v_ref[pl.ds(a_sub, DIL_SW), :]
            q2 = _pair_rows(q_ref[rows, :], first)
            do2 = _pair_rows(do_ref[rows, :], first)
            lse2 = jnp.concatenate([lse_ref[0:1, rows], lse_ref[1:2, rows]], axis=1)
            dd2 = jnp.concatenate([d_ref[0:1, rows], d_ref[1:2, rows]], axis=1)
            s_t = lax.dot_general(kwin, q2, NT, preferred_element_type=F32)
            p_t = jnp.exp2(jnp.where(_dil_band_mask(a_sub, ld), s_t, NEG) - lse2)
            dv_sc[win, :] += jnp.dot(p_t.astype(BF16), do2, preferred_element_type=F32)
            dp_t = lax.dot_general(vwin, do2, NT, preferred_element_type=F32)
            ds_t = (p_t * (dp_t - dd2)).astype(BF16)
            dk_sc[win, :] += jnp.dot(ds_t, q2, preferred_element_type=F32)
            dq2 = lax.dot_general(ds_t, kwin, TN, preferred_element_type=F32)
            dq_ref[rows, :] = jnp.where(first, dq2[0:DIL_SQ, :], dq2[DIL_SQ:2 * DIL_SQ, :])
        out_rows = pl.ds(pl.multiple_of(i * tq, DIL_SQ), span)
        dk_ref[out_rows, :] += dk_sc[...]
        dv_ref[out_rows, :] += dv_sc[...]

    blk = pl.BlockSpec((None, tq, LANES), lambda r, pr, i: (r, i, pr))
    full = pl.BlockSpec((None, ldp, LANES), lambda r, pr, i: (r, 0, pr))
    rowspec = pl.BlockSpec((None, None, 2, tq), lambda r, pr, i: (r, pr, 0, i))
    return pl.pallas_call(
        body, name=name, grid=(d, npair, nq), in_specs=[blk, full, full, blk, rowspec, rowspec],
        out_specs=[blk, full, full],
        out_shape=[jax.ShapeDtypeStruct((d, ld, HEAD_W), F32), jax.ShapeDtypeStruct((d, ldp, HEAD_W), F32),
                   jax.ShapeDtypeStruct((d, ldp, HEAD_W), F32)],
        scratch_shapes=[pltpu.VMEM((span, LANES), F32), pltpu.VMEM((span, LANES), F32)],
        compiler_params=_cparams(VMEM_BIG_MB))(q, kp, vp, do, lse_rows, d_rows)


TILE_BYTES = 1 << 21


def _row_tile(rows, cols, budget=TILE_BYTES):
    for parts in range(1, rows + 1):
        tr = rows // parts
        if rows % parts == 0 and tr % 8 == 0 and tr * cols * 4 <= budget:
            return tr
    return rows


def _add2(a, b, name, out_dtype):
    n, rows, cols = a.shape
    tr = _row_tile(rows, cols)

    def body(a_ref, b_ref, o_ref):
        o_ref[...] = (a_ref[...] + b_ref[...]).astype(out_dtype)

    spec = pl.BlockSpec((None, tr, cols), lambda t, i: (t, i, 0))
    return pl.pallas_call(body, name=name, grid=(n, rows // tr), in_specs=[spec, spec], out_specs=spec,
                          out_shape=jax.ShapeDtypeStruct(a.shape, out_dtype))(a, b)


def _add4_ordered(a, name):
    _, rows, cols = a.shape
    tr = _row_tile(rows, cols, TILE_BYTES // 4)

    def body(a_ref, o_ref):
        o_ref[...] = ((a_ref[0].astype(F32) + a_ref[1].astype(F32)) + a_ref[2].astype(F32)) + a_ref[3].astype(F32)

    return pl.pallas_call(
        body, name=name, grid=(rows // tr,), in_specs=[pl.BlockSpec((4, tr, cols), lambda i: (0, i, 0))],
        out_specs=pl.BlockSpec((tr, cols), lambda i: (i, 0)),
        out_shape=jax.ShapeDtypeStruct((rows, cols), F32))(a)


def _adamw(w, g, m, v, name):
    rows, cols = w.shape
    tr = _row_tile(rows, cols)
    bc1 = 1.0 - ADAM_B1 ** ADAM_STEP
    bc2 = 1.0 - ADAM_B2 ** ADAM_STEP

    def body(w_ref, g_ref, m_ref, v_ref, d_ref, nm_ref, nv_ref):
        gv = g_ref[...]
        nm = ADAM_B1 * m_ref[...] + (1.0 - ADAM_B1) * gv
        nv = ADAM_B2 * v_ref[...] + (1.0 - ADAM_B2) * (gv * gv)
        d_ref[...] = -ADAM_LR * ((nm / bc1) / (jnp.sqrt(nv / bc2) + ADAM_EPS) + ADAM_WD * w_ref[...])
        nm_ref[...] = nm
        nv_ref[...] = nv

    spec = pl.BlockSpec((tr, cols), lambda i: (i, 0))
    return pl.pallas_call(body, name=name, grid=(rows // tr,), in_specs=[spec] * 4, out_specs=[spec] * 3,
                          out_shape=[jax.ShapeDtypeStruct(w.shape, F32)] * 3,
                          compiler_params=_cparams(VMEM_BIG_MB))(w, g, m, v)


ANY = pl.BlockSpec(memory_space=pl.ANY)


def _place():
    return lax.axis_index("x"), lax.axis_index("y"), lax.axis_index("c")


def _rcopy(send_sems, recv_sems, n, src, dst, to):
    return pltpu.make_async_remote_copy(src_ref=src, dst_ref=dst, send_sem=send_sems.at[n], recv_sem=recv_sems.at[n],
                                        device_id=to, device_id_type=MESH)


def _allgather_weights(shards):
    na = len(shards)

    def body(*refs):
        w_refs, g_refs = refs[:na], refs[na:2 * na]
        send_sems, recv_sems, local_sems = refs[2 * na:]
        x, y, c = _place()
        s = 2 * x + y
        chips = [(1 - x, y), (x, 1 - y), (1 - x, 1 - y)]

        def half(a, shard, h):
            hr = shards[a].shape[0] // 2
            return g_refs[a].at[shard, pl.ds(h * hr, hr), :]

        started = []
        for a in range(na):
            hr = shards[a].shape[0] // 2
            mine = pltpu.make_async_copy(w_refs[a], g_refs[a].at[s], local_sems.at[a])
            mine.start()
            started.append(mine)
        sends = []
        for a in range(na):
            hr = shards[a].shape[0] // 2
            for n, (cx, cy) in enumerate(chips):
                cp = _rcopy(send_sems, recv_sems, 6 * a + n, w_refs[a].at[pl.ds(c * hr, hr), :], half(a, s, c),
                            (cx, cy, c))
                cp.start()
                sends.append(cp)
        for a in range(na):
            for n, (cx, cy) in enumerate(chips):
                sj = 2 * cx + cy
                _rcopy(send_sems, recv_sems, 6 * a + n, half(a, sj, c), half(a, sj, c), (cx, cy, c)).wait_recv()
                fw = _rcopy(send_sems, recv_sems, 6 * a + 3 + n, half(a, sj, c), half(a, sj, c), (x, y, 1 - c))
                fw.start()
                sends.append(fw)
        for a in range(na):
            for n, (cx, cy) in enumerate(chips):
                sj = 2 * cx + cy
                _rcopy(send_sems, recv_sems, 6 * a + 3 + n, half(a, sj, 1 - c), half(a, sj, 1 - c),
                       (x, y, 1 - c)).wait_recv()
        for cp in sends:
            cp.wait_send()
        for mine in started:
            mine.wait()

    return pl.pallas_call(
        body, name="allgather_weights", in_specs=[ANY] * na, out_specs=[ANY] * na,
        out_shape=[jax.ShapeDtypeStruct((N_SHARD,) + t.shape, t.dtype) for t in shards],
        scratch_shapes=[pltpu.SemaphoreType.DMA((6 * na,)), pltpu.SemaphoreType.DMA((6 * na,)),
                        pltpu.SemaphoreType.DMA((na,))])(*shards)


def _sibling_send_halves(gs):
    na = len(gs)

    def body(*refs):
        g_refs, o_refs = refs[:na], refs[na:2 * na]
        send_sems, recv_sems = refs[2 * na:]
        x, y, c = _place()
        cps = []
        for a in range(na):
            for t in range(N_SHARD):
                cp = _rcopy(send_sems, recv_sems, N_SHARD * a + t, g_refs[a].at[t, 1 - c], o_refs[a].at[t],
                            (x, y, 1 - c))
                cp.start()
                cps.append(cp)
        for cp in cps:
            cp.wait()

    return pl.pallas_call(
        body, name="grad_sibling_exchange", in_specs=[ANY] * na, out_specs=[ANY] * na,
        out_shape=[jax.ShapeDtypeStruct((N_SHARD,) + g.shape[2:], g.dtype) for g in gs],
        scratch_shapes=[pltpu.SemaphoreType.DMA((N_SHARD * na,)), pltpu.SemaphoreType.DMA((N_SHARD * na,))])(*gs)


def _chip_scatter(parts):
    na = len(parts)

    def body(*refs):
        a_refs, o_refs = refs[:na], refs[na:2 * na]
        send_sems, recv_sems, local_sems = refs[2 * na:]
        x, y, c = _place()
        s = 2 * x + y
        chips = [(1 - x, y), (x, 1 - y), (1 - x, 1 - y)]
        started, cps = [], []
        for a in range(na):
            mine = pltpu.make_async_copy(a_refs[a].at[s], o_refs[a].at[s], local_sems.at[a])
            mine.start()
            started.append(mine)
            for n, (cx, cy) in enumerate(chips):
                cp = _rcopy(send_sems, recv_sems, 3 * a + n, a_refs[a].at[2 * cx + cy], o_refs[a].at[s], (cx, cy, c))
                cp.start()
                cps.append(cp)
        for a in range(na):
            for n, (cx, cy) in enumerate(chips):
                sj = 2 * cx + cy
                _rcopy(send_sems, recv_sems, 3 * a + n, a_refs[a].at[sj], o_refs[a].at[sj], (cx, cy, c)).wait_recv()
        for cp in cps:
            cp.wait_send()
        for mine in started:
            mine.wait()

    return pl.pallas_call(
        body, name="grad_chip_scatter", in_specs=[ANY] * na, out_specs=[ANY] * na,
        out_shape=[jax.ShapeDtypeStruct(t.shape, t.dtype) for t in parts],
        scratch_shapes=[pltpu.SemaphoreType.DMA((3 * na,)), pltpu.SemaphoreType.DMA((3 * na,)),
                        pltpu.SemaphoreType.DMA((na,))])(*parts)


def _sibling_swap(rs):
    na = len(rs)

    def body(*refs):
        r_refs, o_refs = refs[:na], refs[na:2 * na]
        send_sems, recv_sems = refs[2 * na:]
        x, y, c = _place()
        cps = []
        for a in range(na):
            cp = _rcopy(send_sems, recv_sems, a, r_refs[a], o_refs[a], (x, y, 1 - c))
            cp.start()
            cps.append(cp)
        for cp in cps:
            cp.wait()

    return pl.pallas_call(
        body, name="grad_sibling_swap", in_specs=[ANY] * na, out_specs=[ANY] * na,
        out_shape=[jax.ShapeDtypeStruct(t.shape, t.dtype) for t in rs],
        scratch_shapes=[pltpu.SemaphoreType.DMA((na,)), pltpu.SemaphoreType.DMA((na,))])(*rs)


def _pack_small(norm_g, q_norm_g, kv_norm_g, final_g):
    flat = jnp.concatenate([norm_g.reshape(-1), q_norm_g.reshape(-1), kv_norm_g.reshape(-1), final_g.reshape(-1),
                            jnp.zeros((SMALL_ROWS * LANES - N_SMALL,), F32)])
    return flat.reshape(SMALL_ROWS, LANES)


def _split_small(s):
    s = s.reshape(-1)
    o = 0
    out = []
    for n, shape in ((DEPTH * D_MODEL, (DEPTH, D_MODEL)), (DEPTH * Q_LORA, (DEPTH, Q_LORA)),
                     (DEPTH * KV_LORA, (DEPTH, KV_LORA)), (D_MODEL, (D_MODEL,))):
        out.append(s[o:o + n].reshape(shape))
        o += n
    return out


def _assemble_w_in(sh):
    z = lambda n: jnp.zeros(sh.shape[1:3] + (n,), sh.dtype)
    s0, s1, s2, s3 = sh[0], sh[1], sh[2], sh[3]
    return jnp.concatenate([s0[..., 0:640], z(64), s0[..., 640:672], z(32), z(256), s0[..., 672:1184],
                            s3[..., 1064:1576], s0[..., 1184:1576], s1, s2, s3[..., 0:1064]], axis=-1)


def _split_w_in_grad(d):
    sh0 = jnp.concatenate([d[..., 0:640], d[..., 704:736], d[..., 1024:1536], d[..., 2048:2440]], axis=-1)
    sh3 = jnp.concatenate([d[..., 5592:6656], d[..., 1536:2048]], axis=-1)
    return jnp.stack([sh0, d[..., 2440:4016], d[..., 4016:5592], sh3]).reshape(N_SHARD, DEPTH * D_MODEL, SHARD_COLS_IN)


def _col_shards(w):
    dl, r, cc = w.shape
    return w.reshape(dl, r, N_SHARD, cc // N_SHARD).transpose(2, 0, 1, 3).reshape(N_SHARD, dl * r, cc // N_SHARD)


def _from_col_shards(g, rows):
    cc = g.shape[-1]
    return g.reshape(N_SHARD, DEPTH, rows, cc).transpose(1, 2, 0, 3).reshape(DEPTH, rows, N_SHARD * cc)


def _pad_w_in(w):
    z = lambda n: jnp.zeros(w.shape[:-1] + (n,), w.dtype)
    return jnp.concatenate([w[..., 0:640], z(64), w[..., 640:672], z(32), z(256), w[..., 672:1184],
                            w[..., 5792:6304], w[..., 1184:5792]], axis=-1)


def _unpad_w_in(w):
    return jnp.concatenate([w[..., 0:640], w[..., 704:736], w[..., 1024:1536], w[..., 2048:6656],
                            w[..., 1536:2048]], axis=-1)


def _pad_w_uq(w):
    s = w.shape[:-1]
    w = w.reshape(s + (MLA_HEADS, 96))
    return jnp.pad(w, [(0, 0)] * (w.ndim - 1) + [(0, 32)]).reshape(s + (1024,))


def _unpad_w_uq(w):
    s = w.shape[:-1]
    return w.reshape(s + (MLA_HEADS, LANES))[..., :96].reshape(s + (768,))


def _pad_w_ukv(w):
    s = w.shape[:-1]
    w = w.reshape(s + (MLA_HEADS, 128))
    kpart = jnp.pad(w[..., :64], [(0, 0)] * (w.ndim - 1) + [(0, 64)]).reshape(s + (1024,))
    vpart = w[..., 64:].reshape(s + (512,))
    return jnp.concatenate([kpart, vpart], axis=-1)


def _unpad_w_ukv(w):
    s = w.shape[:-1]
    kpart = w[..., :1024].reshape(s + (MLA_HEADS, LANES))[..., :64]
    vpart = w[..., 1024:].reshape(s + (MLA_HEADS, 64))
    return jnp.concatenate([kpart, vpart], axis=-1).reshape(s + (1024,))


def _rope_tables(L, dim, lane_lo, period):
    half = dim // 2
    inv = 1.0 / (ROPE_THETA ** (jnp.arange(0, dim, 2, dtype=F32) / dim))
    ang = jnp.arange(L, dtype=F32)[:, None] * inv[None, :]
    cos, sin = jnp.cos(ang), jnp.sin(ang)
    one = lambda n: jnp.ones((L, n), F32)
    zero = lambda n: jnp.zeros((L, n), F32)
    rest = period - lane_lo - dim
    rep = LANES // period
    c = jnp.tile(jnp.concatenate([one(lane_lo), cos, cos, one(rest)], axis=1), (1, rep))
    a = jnp.tile(jnp.concatenate([zero(lane_lo), -sin, zero(half), zero(rest)], axis=1), (1, rep))
    b = jnp.tile(jnp.concatenate([zero(lane_lo + half), sin, zero(rest)], axis=1), (1, rep))
    return c, a, b


def _to_strided(t, d):
    L, w = t.shape
    return t.reshape(L // d, d, w).transpose(1, 0, 2)


def _from_strided(t):
    d, ld, w = t.shape
    return t.transpose(1, 0, 2).reshape(d * ld, w)


def _head_rows(t):
    return t.T.reshape(MLA_HEADS // 2, 2, t.shape[0])


def _head_rows_strided(t, d):
    s = _to_strided(t, d)
    return s.transpose(0, 2, 1).reshape(d, MLA_HEADS // 2, 2, s.shape[1])


def _pad_keys(t):
    return jnp.pad(t, ((0, 0), (DIL_HALF, DIL_HALF), (0, 0)))


def _local_grads(x, target, norm_g, w_in_p, q_norm_g, kv_norm_g, w_uq_p, w_ukv_p, w_out, final_g):
    L = x.shape[0]
    tabs_m = _rope_tables(L, MLA_ROPE, MLA_NOPE, LANES)
    tabs_d = _rope_tables(L, ROT_DIM, 0, DIL_HD)
    tabs_m_t = (tabs_m[0], -tabs_m[1], -tabs_m[2])
    tabs_d_t = (tabs_d[0], -tabs_d[1], -tabs_d[2])
    w_in_t = jnp.swapaxes(w_in_p, 1, 2)
    w_uq_t = jnp.swapaxes(w_uq_p, 1, 2)
    w_ukv_t = jnp.swapaxes(w_ukv_p, 1, 2)
    w_out_t = jnp.swapaxes(w_out, 1, 2)

    saved = []
    for l in range(DEPTH):
        h = _rms_fwd(x, norm_g[l:l + 1], "rms_fwd")
        p = _mm(h, w_in_p[l], tm=512, tn=3328, tk=1024, out_dtype=BF16, name="in_proj")
        q, k, v, cqn, ckvn = _mla_prep(p, q_norm_g[l:l + 1], kv_norm_g[l:l + 1], w_uq_p[l], w_ukv_p[l], tabs_m,
                                       "mla_prep")
        oa, lse_a = _mla_fwd(q, k, v.T, "mla_fwd")
        dil = _dil_prep(p, tabs_d, "dil_prep")
        dil_s, o_g, lse_g = [], [], []
        for g, (_, dd) in enumerate(DIL_PAIRS):
            qs, ks, vs = (_to_strided(t, dd) for t in dil[3 * g:3 * g + 3])
            ks, vs = _pad_keys(ks), _pad_keys(vs)
            v_t4 = vs.reshape(dd, vs.shape[1] // LANES, LANES, HEAD_W).transpose(0, 1, 3, 2)
            og, lg = _dil_fwd(qs, ks, v_t4, "dil_fwd_%d" % dd)
            dil_s.append((qs, ks, vs))
            o_g.append(_from_strided(og))
            lse_g.append(_from_strided(lg))
        ab, bm, lt = _merge_gate(oa, p, o_g, lse_g, "merge_gate")
        x_next = _mm(ab, w_out[l], tm=1024, tn=1024, tk=1024, out_dtype=F32, name="out_proj", add=x)
        saved.append((x, h, p, q, k, v, cqn, ckvn, oa, lse_a, dil_s, bm, lt, ab))
        x = x_next

    loss_b, dx, d_final = _loss_head(x, final_g[None, :], target, "loss_head")
    loss = loss_b[0, 0]

    d_norm, d_qn, d_kvn, d_win, d_wuq, d_wukv, d_wout = [], [], [], [], [], [], []
    for l in reversed(range(DEPTH)):
        x_l, h, p, q, k, v, cqn, ckvn, oa, lse_a, dil_s, bm, lt, ab = saved[l]
        dab = _mm(dx, w_out_t[l], tm=1024, tn=1024, tk=1024, out_dtype=F32, name="out_proj_dgrad")
        d_wout.append(_mm(ab, dx, tm=1024, tn=1024, tk=1024, out_dtype=F32, name="out_proj_wgrad", a_is_kxm=True))
        doa, dbm, D_a, D_b, dgates = _gate_bwd(dab, p, oa, bm, "gate_bwd")
        dq, dk_t, dv_t = _mla_bwd(q, k, v, q.T, doa, doa.T, lse_a.transpose(0, 2, 1),
                                  D_a.reshape(L, MLA_HEADS // 2, 2).transpose(1, 0, 2), "mla_bwd")
        dk, dv = dk_t.T, dv_t.T
        dp_mla, dq_pre, dkv, dqg, dkvg = _mla_prep_bwd(dq, dk, dv, p, q_norm_g[l:l + 1], kv_norm_g[l:l + 1],
                                                       w_uq_t[l], w_ukv_t[l], tabs_m_t, "mla_prep_bwd")
        d_wuq.append(_mm(cqn, dq_pre, tm=Q_LORA, tn=1024, tk=2048, out_dtype=F32, name="w_uq_wgrad", a_is_kxm=True))
        d_wukv.append(_mm(ckvn, dkv, tm=KV_LORA, tn=1536, tk=2048, out_dtype=F32, name="w_ukv_wgrad", a_is_kxm=True))
        dgr = []
        for g, (_, dd) in enumerate(DIL_PAIRS):
            qs, ks, vs = dil_s[g]
            dqs, dks, dvs = _dil_bwd(qs, ks, vs, _to_strided(dbm, dd), _head_rows_strided(lt, dd),
                                     _head_rows_strided(D_b, dd), "dil_bwd_%d" % dd)
            unpad = lambda t: t[:, DIL_HALF:DIL_HALF + L // dd]
            dgr += [_from_strided(dqs), _from_strided(unpad(dks)), _from_strided(unpad(dvs))]
        dp_dil = _dil_prep_bwd(dgr, tabs_d_t, "dil_prep_bwd")
        dp = jnp.concatenate([dp_mla, dgates, dp_dil], axis=1)
        dh = _mm(dp, w_in_t[l], tm=1024, tn=1024, tk=1664, out_dtype=F32, name="in_proj_dgrad")
        d_win.append(_mm(h, dp, tm=1024, tn=1664, tk=1024, out_dtype=F32, name="in_proj_wgrad", a_is_kxm=True))
        dx, dng = _rms_bwd(dh, x_l, norm_g[l:l + 1], dx, "rms_bwd")
        d_norm.append(dng[0])
        d_qn.append(dqg[0])
        d_kvn.append(dkvg[0])

    rev = lambda xs: jnp.stack(xs[::-1])
    return (loss, dx, rev(d_norm), rev(d_win), rev(d_qn), rev(d_kvn), rev(d_wuq), rev(d_wukv), rev(d_wout),
            d_final[0])


def kernel(x, norm_g, w_in, q_norm_g, kv_norm_g, w_uq, w_ukv, w_out, final_g, loss_target, m_norm_g, m_w_in, m_q_norm_g, m_kv_norm_g, m_w_uq, m_w_ukv, m_w_out, m_final_g, v_norm_g, v_w_in, v_q_norm_g, v_kv_norm_g, v_w_uq, v_w_ukv, v_w_out, v_final_g):
    c = lax.axis_index("c")

    def families(a_in, a_uq, a_ukv, a_out):
        return [t.reshape(shape) for t, shape in zip((a_in, a_uq, a_ukv, a_out), FAM_SHAPES)]

    g_in, g_uq, g_ukv, g_out = _allgather_weights([t.astype(BF16) for t in families(w_in, w_uq, w_ukv, w_out)])
    w_in_p = _assemble_w_in(g_in.reshape(N_SHARD, DEPTH, D_MODEL, SHARD_COLS_IN))
    w_uq_p = _pad_w_uq(_from_col_shards(g_uq, Q_LORA))
    w_ukv_p = _pad_w_ukv(_from_col_shards(g_ukv, KV_LORA))
    w_out_f = g_out.reshape(N_SHARD, DEPTH, 1024 // N_SHARD, D_MODEL).transpose(1, 0, 2, 3).reshape(DEPTH, 1024, D_MODEL)

    (loss, dx, d_norm, d_win_p, d_qn, d_kvn, d_wuq_p, d_wukv_p, d_wout, d_final) = _local_grads(
        x[0], loss_target[0], norm_g, w_in_p, q_norm_g, kv_norm_g, w_uq_p, w_ukv_p, w_out_f, final_g)
    loss = lax.psum(loss, ("x", "y", "c"))

    small = _pack_small(d_norm, d_qn, d_kvn, d_final)
    grads = [_split_w_in_grad(d_win_p), _col_shards(_unpad_w_uq(d_wuq_p)), _col_shards(_unpad_w_ukv(d_wukv_p)),
             d_wout.reshape(DEPTH, N_SHARD, 1024 // N_SHARD, D_MODEL).transpose(1, 0, 2, 3).reshape(
                 N_SHARD, DEPTH * (1024 // N_SHARD), D_MODEL),
             jnp.broadcast_to(small[None], (N_SHARD, SMALL_ROWS, LANES))]
    halves = [g.reshape(N_SHARD, 2, g.shape[1] // 2, g.shape[2]) for g in grads]
    from_sib = _sibling_send_halves(halves)
    chip_sum = [_add2(lax.dynamic_index_in_dim(h, c, axis=1, keepdims=False), f, "grad_add_pair", BF16)
                for h, f in zip(halves, from_sib)]
    red_half = [_add4_ordered(t, "grad_add_chips") for t in _chip_scatter(chip_sum)]
    other_half = _sibling_swap(red_half)
    gred = []
    for mine, other in zip(red_half, other_half):
        both = jnp.stack([mine, other])
        gred.append(jnp.concatenate([lax.dynamic_index_in_dim(both, c, axis=0, keepdims=False),
                                     lax.dynamic_index_in_dim(both, 1 - c, axis=0, keepdims=False)], axis=0))

    wf = families(w_in, w_uq, w_ukv, w_out) + [_pack_small(norm_g, q_norm_g, kv_norm_g, final_g)]
    mf = families(m_w_in, m_w_uq, m_w_ukv, m_w_out) + [_pack_small(m_norm_g, m_q_norm_g, m_kv_norm_g, m_final_g)]
    vf = families(v_w_in, v_w_uq, v_w_ukv, v_w_out) + [_pack_small(v_norm_g, v_q_norm_g, v_kv_norm_g, v_final_g)]
    upd = [_adamw(w, g, m, v, "adamw") for w, g, m, v in zip(wf, gred, mf, vf)]

    def leaves(fams):
        a_in, a_uq, a_ukv, a_out, s = fams
        s_norm, s_qn, s_kvn, s_final = _split_small(s)
        return [s_norm, a_in.reshape(w_in.shape), s_qn, s_kvn, a_uq.reshape(w_uq.shape), a_ukv.reshape(w_ukv.shape),
                a_out.reshape(w_out.shape), s_final]

    return (loss, dx[None], *leaves(gred), *leaves([u[0] for u in upd]), *leaves([u[1] for u in upd]),
            *leaves([u[2] for u in upd]))
```

```python
import functools

import jax
import jax.numpy as jnp
from jax import lax
from jax.experimental import pallas as pl
from jax.experimental.pallas import tpu as pltpu

F32 = jnp.float32
BF16 = jnp.bfloat16
MESH = pl.DeviceIdType.MESH

D_MODEL = 1024
DEPTH = 4
MLA_HEADS = 8
MLA_NOPE = 64
MLA_ROPE = 32
Q_LORA = 384
KV_LORA = 256
DIL_PAIRS = ((128, 1), (512, 4), (2048, 16))
DIL_HD = 64
DIL_HALF = 64
ROT_DIM = 16
ROPE_THETA = 500000.0
EPS = 1e-6
IN_WIDTH = 6304
N_SHARD = 4

P_WIDTH = 6656
P_MLA = 1024
P_GATE = 1024
P_DIL0 = 2048
LANES = 128
HEAD_W = 512

ADAM_LR = 0.001
ADAM_B1 = 0.9
ADAM_B2 = 0.999
ADAM_EPS = 1e-08
ADAM_WD = 0.01
ADAM_STEP = 10

SHARD_COLS_IN = IN_WIDTH // N_SHARD
FAM_SHAPES = ((DEPTH * D_MODEL, SHARD_COLS_IN), (DEPTH * Q_LORA, 768 // N_SHARD), (DEPTH * KV_LORA, 1024 // N_SHARD),
              (DEPTH * (1024 // N_SHARD), D_MODEL))
N_SMALL = DEPTH * (D_MODEL + Q_LORA + KV_LORA) + D_MODEL
SMALL_ROWS = 64
VMEM_BIG_MB = 48


def _cparams(vmem_mb=None):
    if vmem_mb is None:
        return None
    return pltpu.CompilerParams(vmem_limit_bytes=vmem_mb << 20)


def _sigmoid(x):
    return 1.0 / (1.0 + jnp.exp(-x))


def _rope(x, c, a, b, sh):
    return x * c + pltpu.roll(x, LANES - sh, 1) * a + pltpu.roll(x, sh, 1) * b


def _per_head8(x, pick_first):
    r = lax.broadcasted_iota(jnp.int32, (HEAD_W, MLA_HEADS), 0)
    c = lax.broadcasted_iota(jnp.int32, (HEAD_W, MLA_HEADS), 1)
    sel = (r == c * DIL_HD) if pick_first else (r // DIL_HD == c)
    mat = jnp.where(sel, 1.0, 0.0).astype(BF16)
    out = jnp.zeros((x.shape[0], MLA_HEADS), F32)
    for _ in range(3):
        part = x.astype(BF16)
        out = out + jnp.dot(part, mat, preferred_element_type=F32)
        x = x - part.astype(F32)
    return out


def _mm(a, b, *, tm, tn, tk, out_dtype, name, add=None, a_is_kxm=False):
    K, M = a.shape if a_is_kxm else a.shape[::-1]
    N = b.shape[1]
    tm, tn, tk = min(tm, M), min(tn, N), min(tk, K)
    assert M % tm == 0 and N % tn == 0 and K % tk == 0, (a.shape, b.shape)
    nk = K // tk
    has_add = add is not None

    def body(*refs):
        if has_add:
            a_ref, b_ref, add_ref, o_ref, acc = refs
        else:
            a_ref, b_ref, o_ref, acc = refs
        k = pl.program_id(2)
        if a_is_kxm:
            part = lax.dot_general(a_ref[...].astype(BF16), b_ref[...].astype(BF16), (((0,), (0,)), ((), ())),
                                   preferred_element_type=F32)
        else:
            part = jnp.dot(a_ref[...].astype(BF16), b_ref[...].astype(BF16), preferred_element_type=F32)

        @pl.when(k == 0)
        def _():
            acc[...] = part

        @pl.when(k > 0)
        def _():
            acc[...] += part

        @pl.when(k == nk - 1)
        def _():
            r = acc[...]
            if has_add:
                r = r + add_ref[...]
            o_ref[...] = r.astype(out_dtype)

    a_spec = pl.BlockSpec((tk, tm), lambda i, j, k: (k, i)) if a_is_kxm else pl.BlockSpec((tm, tk), lambda i, j, k: (i, k))
    in_specs = [a_spec, pl.BlockSpec((tk, tn), lambda i, j, k: (k, j))]
    args = [a, b]
    if has_add:
        in_specs.append(pl.BlockSpec((tm, tn), lambda i, j, k: (i, j)))
        args.append(add)
    return pl.pallas_call(
        body, name=name, grid=(M // tm, N // tn, nk), in_specs=in_specs,
        out_specs=pl.BlockSpec((tm, tn), lambda i, j, k: (i, j)),
        out_shape=jax.ShapeDtypeStruct((M, N), out_dtype),
        scratch_shapes=[pltpu.VMEM((tm, tn), F32)], compiler_params=_cparams(VMEM_BIG_MB))(*args)


def _row_spec(tm, w, cb=0):
    return pl.BlockSpec((tm, w), lambda i: (i, cb))


def _const_spec(arr):
    nd = arr.ndim
    return pl.BlockSpec(arr.shape, lambda i: (0,) * nd)


def _rms_fwd(x, g, name):
    L, D = x.shape
    tm = min(512, L)

    def body(x_ref, g_ref, o_ref):
        xv = x_ref[...]
        r = lax.rsqrt(jnp.mean(xv * xv, axis=-1, keepdims=True) + EPS)
        o_ref[...] = (xv * r * g_ref[...]).astype(BF16)

    return pl.pallas_call(
        body, name=name, grid=(L // tm,), in_specs=[_row_spec(tm, D), _const_spec(g)],
        out_specs=_row_spec(tm, D), out_shape=jax.ShapeDtypeStruct((L, D), BF16))(x, g)


def _rms_bwd(dh, x, g, dres, name):
    L, D = x.shape
    tm = min(512, L)

    def body(dh_ref, x_ref, g_ref, dres_ref, dx_ref, dg_ref):
        xv = x_ref[...]
        dy = dh_ref[...]
        r = lax.rsqrt(jnp.mean(xv * xv, axis=-1, keepdims=True) + EPS)
        dyg = dy * g_ref[...]
        dx_ref[...] = dres_ref[...] + r * dyg - xv * (r * r * r) * jnp.mean(dyg * xv, axis=-1, keepdims=True)
        part = jnp.sum(dy * xv * r, axis=0, keepdims=True)

        @pl.when(pl.program_id(0) == 0)
        def _():
            dg_ref[...] = part

        @pl.when(pl.program_id(0) > 0)
        def _():
            dg_ref[...] += part

    return pl.pallas_call(
        body, name=name, grid=(L // tm,),
        in_specs=[_row_spec(tm, D), _row_spec(tm, D), _const_spec(g), _row_spec(tm, D)],
        out_specs=[_row_spec(tm, D), pl.BlockSpec((1, D), lambda i: (0, 0))],
        out_shape=[jax.ShapeDtypeStruct((L, D), F32), jax.ShapeDtypeStruct((1, D), F32)])(dh, x, g, dres)


def _loss_head(x, g, target, name):
    L, D = x.shape
    tm = min(512, L)

    def body(x_ref, g_ref, t_ref, loss_ref, dx_ref, dg_ref):
        xv = x_ref[...]
        gv = g_ref[...]
        r = lax.rsqrt(jnp.mean(xv * xv, axis=-1, keepdims=True) + EPS)
        xr = xv * r
        err = xr * gv - t_ref[...]
        lp = 0.5 * jnp.sum(jnp.mean(err * err, axis=-1, keepdims=True))
        dy = err * (1.0 / D)
        dyg = dy * gv
        dx_ref[...] = r * dyg - xv * (r * r * r) * jnp.mean(dyg * xv, axis=-1, keepdims=True)
        part = jnp.sum(dy * xr, axis=0, keepdims=True)

        @pl.when(pl.program_id(0) == 0)
        def _():
            dg_ref[...] = part
            loss_ref[...] = jnp.zeros(loss_ref.shape, F32) + lp

        @pl.when(pl.program_id(0) > 0)
        def _():
            dg_ref[...] += part
            loss_ref[...] += lp

    return pl.pallas_call(
        body, name=name, grid=(L // tm,),
        in_specs=[_row_spec(tm, D), _const_spec(g), _row_spec(tm, D)],
        out_specs=[pl.BlockSpec((8, LANES), lambda i: (0, 0)), _row_spec(tm, D), pl.BlockSpec((1, D), lambda i: (0, 0))],
        out_shape=[jax.ShapeDtypeStruct((8, LANES), F32), jax.ShapeDtypeStruct((L, D), F32),
                   jax.ShapeDtypeStruct((1, D), F32)])(x, g, target)


def _mla_prep(p, qg, kvg, wuq, wukv, tabs, name):
    L = p.shape[0]
    tm = min(512, L)
    scale = (MLA_NOPE + MLA_ROPE) ** -0.5
    tc, ta, tb = tabs

    def body(p_ref, qg_ref, kvg_ref, wuq_ref, wukv_ref, c_ref, a_ref, b_ref, q_ref, k_ref, v_ref, cqn_ref, ckvn_ref):
        c, a, b = c_ref[...], a_ref[...], b_ref[...]
        cq = p_ref[:, 0:Q_LORA].astype(F32)
        ckv = p_ref[:, Q_LORA:Q_LORA + KV_LORA].astype(F32)
        kr = p_ref[:, 640:768].astype(F32)
        cqn = (cq * lax.rsqrt(jnp.mean(cq * cq, axis=-1, keepdims=True) + EPS) * qg_ref[...]).astype(BF16)
        ckvn = (ckv * lax.rsqrt(jnp.mean(ckv * ckv, axis=-1, keepdims=True) + EPS) * kvg_ref[...]).astype(BF16)
        cqn_ref[...] = cqn
        ckvn_ref[...] = ckvn
        q = jnp.dot(cqn, wuq_ref[...], preferred_element_type=F32)
        kv = jnp.dot(ckvn, wukv_ref[...], preferred_element_type=F32)
        krr = _rope(kr, c, a, b, MLA_ROPE // 2)
        for h in range(MLA_HEADS):
            sl = slice(h * LANES, (h + 1) * LANES)
            q_ref[:, sl] = (_rope(q[:, sl], c, a, b, MLA_ROPE // 2) * (scale * LOG2E)).astype(BF16)
            k_ref[:, sl] = (kv[:, sl] + krr).astype(BF16)
        v_ref[...] = kv[:, 1024:1536].astype(BF16)

    return pl.pallas_call(
        body, name=name, grid=(L // tm,),
        in_specs=[_row_spec(tm, P_MLA, 0), _const_spec(qg), _const_spec(kvg), _const_spec(wuq), _const_spec(wukv),
                  _row_spec(tm, LANES), _row_spec(tm, LANES), _row_spec(tm, LANES)],
        out_specs=[_row_spec(tm, 1024), _row_spec(tm, 1024), _row_spec(tm, HEAD_W), _row_spec(tm, Q_LORA),
                   _row_spec(tm, KV_LORA)],
        out_shape=[jax.ShapeDtypeStruct((L, 1024), BF16), jax.ShapeDtypeStruct((L, 1024), BF16),
                   jax.ShapeDtypeStruct((L, HEAD_W), BF16), jax.ShapeDtypeStruct((L, Q_LORA), BF16),
                   jax.ShapeDtypeStruct((L, KV_LORA), BF16)],
        compiler_params=_cparams(VMEM_BIG_MB))(p, qg, kvg, wuq, wukv, tc, ta, tb)


def _mla_prep_bwd(dq, dk, dv, p, qg, kvg, wuq_t, wukv_t, tabs_t, name):
    L = p.shape[0]
    tm = min(512, L)
    scale = (MLA_NOPE + MLA_ROPE) ** -0.5
    tc, ta, tb = tabs_t

    def body(dq_ref, dk_ref, dv_ref, p_ref, qg_ref, kvg_ref, wuqt_ref, wukvt_ref, c_ref, a_ref, b_ref,
             dp_ref, dqp_ref, dkv_ref, dqg_ref, dkvg_ref):
        c, a, b = c_ref[...], a_ref[...], b_ref[...]
        dkr = jnp.zeros((tm, LANES), F32)
        for h in range(MLA_HEADS):
            sl = slice(h * LANES, (h + 1) * LANES)
            dqp_ref[:, sl] = (_rope(dq_ref[:, sl].astype(F32), c, a, b, MLA_ROPE // 2) * scale).astype(BF16)
            dkh = dk_ref[:, sl].astype(F32) * LN2
            dkv_ref[:, sl] = dkh.astype(BF16)
            dkr = dkr + dkh
        dkv_ref[:, 1024:1536] = dv_ref[...].astype(BF16)
        lane = lax.broadcasted_iota(jnp.int32, (tm, LANES), 1)
        dkr = jnp.where((lane >= MLA_NOPE) & (lane < MLA_NOPE + MLA_ROPE), _rope(dkr, c, a, b, MLA_ROPE // 2), 0.0)

        d_cqn = jnp.dot(dqp_ref[...], wuqt_ref[...], preferred_element_type=F32)
        d_ckvn = jnp.dot(dkv_ref[...], wukvt_ref[...], preferred_element_type=F32)

        def norm_bwd(xv, gv, dy):
            r = lax.rsqrt(jnp.mean(xv * xv, axis=-1, keepdims=True) + EPS)
            dyg = dy * gv
            dx = r * dyg - xv * (r * r * r) * jnp.mean(dyg * xv, axis=-1, keepdims=True)
            return dx, jnp.sum(dy * xv * r, axis=0, keepdims=True)

        d_cq, dqg = norm_bwd(p_ref[:, 0:Q_LORA].astype(F32), qg_ref[...], d_cqn)
        d_ckv, dkvg = norm_bwd(p_ref[:, Q_LORA:Q_LORA + KV_LORA].astype(F32), kvg_ref[...], d_ckvn)
        dp_ref[:, 0:Q_LORA] = d_cq.astype(BF16)
        dp_ref[:, Q_LORA:Q_LORA + KV_LORA] = d_ckv.astype(BF16)
        dp_ref[:, 640:768] = dkr.astype(BF16)
        dp_ref[:, 768:1024] = jnp.zeros((tm, 256), BF16)

        @pl.when(pl.program_id(0) == 0)
        def _():
            dqg_ref[...] = dqg
            dkvg_ref[...] = dkvg

        @pl.when(pl.program_id(0) > 0)
        def _():
            dqg_ref[...] += dqg
            dkvg_ref[...] += dkvg

    return pl.pallas_call(
        body, name=name, grid=(L // tm,),
        in_specs=[_row_spec(tm, 1024), _row_spec(tm, 1024), _row_spec(tm, HEAD_W), _row_spec(tm, P_MLA, 0),
                  _const_spec(qg), _const_spec(kvg), _const_spec(wuq_t), _const_spec(wukv_t),
                  _row_spec(tm, LANES), _row_spec(tm, LANES), _row_spec(tm, LANES)],
        out_specs=[_row_spec(tm, P_MLA), _row_spec(tm, 1024), _row_spec(tm, 1536),
                   pl.BlockSpec((1, Q_LORA), lambda i: (0, 0)), pl.BlockSpec((1, KV_LORA), lambda i: (0, 0))],
        out_shape=[jax.ShapeDtypeStruct((L, P_MLA), BF16), jax.ShapeDtypeStruct((L, 1024), BF16),
                   jax.ShapeDtypeStruct((L, 1536), BF16), jax.ShapeDtypeStruct((1, Q_LORA), F32),
                   jax.ShapeDtypeStruct((1, KV_LORA), F32)],
        compiler_params=_cparams(VMEM_BIG_MB))(dq, dk, dv, p, qg, kvg, wuq_t, wukv_t, tc, ta, tb)


def _dil_prep(p, tabs, name):
    L = p.shape[0]
    tm = min(512, L)
    tc, ta, tb = tabs
    scale = DIL_HD ** -0.5

    def body(*refs):
        ins, (c_ref, a_ref, b_ref), outs = refs[:9], refs[9:12], refs[12:]
        c, a, b = c_ref[...], a_ref[...], b_ref[...]
        for n in range(9):
            t = n % 3
            for cb in range(HEAD_W // LANES):
                sl = slice(cb * LANES, (cb + 1) * LANES)
                xv = ins[n][:, sl].astype(F32)
                if t == 0:
                    xv = _rope(xv, c, a, b, ROT_DIM // 2) * (scale * LOG2E)
                elif t == 1:
                    xv = _rope(xv, c, a, b, ROT_DIM // 2)
                outs[n][:, sl] = xv.astype(BF16)

    in_specs = [_row_spec(tm, HEAD_W, P_DIL0 // HEAD_W + n) for n in range(9)] + [_row_spec(tm, LANES)] * 3
    return pl.pallas_call(
        body, name=name, grid=(L // tm,), in_specs=in_specs,
        out_specs=[_row_spec(tm, HEAD_W)] * 9,
        out_shape=[jax.ShapeDtypeStruct((L, HEAD_W), BF16)] * 9)(*([p] * 9), tc, ta, tb)


def _dil_prep_bwd(grads, tabs_t, name):
    L = grads[0].shape[0]
    tm = min(512, L)
    tc, ta, tb = tabs_t
    scale = DIL_HD ** -0.5

    def body(*refs):
        ins, (c_ref, a_ref, b_ref), o_ref = refs[:9], refs[9:12], refs[12]
        c, a, b = c_ref[...], a_ref[...], b_ref[...]
        for n in range(9):
            t = n % 3
            for cb in range(HEAD_W // LANES):
                sl = slice(cb * LANES, (cb + 1) * LANES)
                xv = ins[n][:, sl].astype(F32)
                if t == 0:
                    xv = _rope(xv, c, a, b, ROT_DIM // 2) * scale
                elif t == 1:
                    xv = _rope(xv, c, a, b, ROT_DIM // 2) * LN2
                o_ref[:, n * HEAD_W + cb * LANES:n * HEAD_W + (cb + 1) * LANES] = xv.astype(BF16)

    return pl.pallas_call(
        body, name=name, grid=(L // tm,), in_specs=[_row_spec(tm, HEAD_W)] * 9 + [_row_spec(tm, LANES)] * 3,
        out_specs=_row_spec(tm, 9 * HEAD_W), out_shape=jax.ShapeDtypeStruct((L, 9 * HEAD_W), BF16),
        compiler_params=_cparams(VMEM_BIG_MB))(*grads, tc, ta, tb)


def _merge_gate(oa, p, o_g, lse_g, name):
    L = oa.shape[0]
    tm = min(512, L)

    def body(oa_ref, ga_ref, gb_ref, o1, o2, o3, l1, l2, l3, ab_ref, bm_ref, lt_ref):
        la, lb, lc = l1[...], l2[...], l3[...]
        m = jnp.maximum(jnp.maximum(la, lb), lc)
        ea, eb, ec = jnp.exp2(la - m), jnp.exp2(lb - m), jnp.exp2(lc - m)
        den = ea + eb + ec
        bm = (ea * o1[...] + eb * o2[...] + ec * o3[...]) / den
        bm_ref[...] = bm
        lt_ref[...] = _per_head8(m + jnp.log2(den), True)
        ga, gb = ga_ref[...].astype(F32), gb_ref[...].astype(F32)
        ab_ref[:, 0:HEAD_W] = (oa_ref[...] * (ga * _sigmoid(ga))).astype(BF16)
        ab_ref[:, HEAD_W:2 * HEAD_W] = (bm * (gb * _sigmoid(gb))).astype(BF16)

    w = _row_spec(tm, HEAD_W)
    return pl.pallas_call(
        body, name=name, grid=(L // tm,),
        in_specs=[w, _row_spec(tm, HEAD_W, 2), _row_spec(tm, HEAD_W, 3), w, w, w, w, w, w],
        out_specs=[_row_spec(tm, 2 * HEAD_W), w, _row_spec(tm, MLA_HEADS)],
        out_shape=[jax.ShapeDtypeStruct((L, 2 * HEAD_W), BF16), jax.ShapeDtypeStruct((L, HEAD_W), F32),
                   jax.ShapeDtypeStruct((L, MLA_HEADS), F32)])(oa, p, p, *o_g, *lse_g)


def _gate_bwd(dab, p, oa, bm, name):
    L = oa.shape[0]
    tm = min(512, L)

    def body(da_ref, db_ref, ga_ref, gb_ref, oa_ref, bm_ref, doa_ref, dbm_ref, Da_ref, Db_ref, dg_ref):
        def one(d, g, o, do_ref, D_ref, col):
            sg = _sigmoid(g)
            do = d * (g * sg)
            do_ref[...] = do.astype(BF16)
            dg_ref[:, col:col + HEAD_W] = (d * o * (sg * (1.0 + g * (1.0 - sg)))).astype(BF16)
            D_ref[...] = _per_head8(do * o, False)

        one(da_ref[...], ga_ref[...].astype(F32), oa_ref[...], doa_ref, Da_ref, 0)
        one(db_ref[...], gb_ref[...].astype(F32), bm_ref[...], dbm_ref, Db_ref, HEAD_W)

    w = _row_spec(tm, HEAD_W)
    w8 = _row_spec(tm, MLA_HEADS)
    return pl.pallas_call(
        body, name=name, grid=(L // tm,),
        in_specs=[_row_spec(tm, HEAD_W, 0), _row_spec(tm, HEAD_W, 1), _row_spec(tm, HEAD_W, 2),
                  _row_spec(tm, HEAD_W, 3), w, w],
        out_specs=[w, w, w8, w8, _row_spec(tm, 2 * HEAD_W)],
        out_shape=[jax.ShapeDtypeStruct((L, HEAD_W), BF16), jax.ShapeDtypeStruct((L, HEAD_W), BF16),
                   jax.ShapeDtypeStruct((L, MLA_HEADS), F32), jax.ShapeDtypeStruct((L, MLA_HEADS), F32),
                   jax.ShapeDtypeStruct((L, 2 * HEAD_W), BF16)])(dab, dab, p, p, oa, bm)


NT = (((1,), (1,)), ((), ()))
TN = (((0,), (0,)), ((), ()))
NEG = -1e30


MLA_TQ = 512
MLA_TK = 2048
MLA_BWD_TK = 1024
LOG2E = 1.4426950408889634
LN2 = 0.6931471805599453


def _mla_fwd(q, k, v_t, name):
    L = q.shape[0]
    tq, tk = min(MLA_TQ, L), min(MLA_TK, L)
    nq, nk = L // tq, L // tk
    npair = MLA_HEADS // 2

    def body(q_ref, k_ref, vt_ref, o_ref, lse_ref, m0, l0, a0, m1, l1, a1):
        j = pl.program_id(2)
        stats = ((m0, l0, a0), (m1, l1, a1))

        @pl.when(j == 0)
        def _():
            for m_sc, l_sc, acc_sc in stats:
                m_sc[...] = jnp.full(m_sc.shape, NEG, F32)
                l_sc[...] = jnp.zeros(l_sc.shape, F32)
                acc_sc[...] = jnp.zeros(acc_sc.shape, F32)

        s_ts = [lax.dot_general(k_ref[:, hh * LANES:(hh + 1) * LANES], q_ref[:, hh * LANES:(hh + 1) * LANES], NT,
                                preferred_element_type=F32) for hh in range(2)]
        for hh in range(2):
            m_sc, l_sc, acc_sc = stats[hh]
            s_t = s_ts[hh]
            m_prev = m_sc[...]
            m_new = jnp.maximum(m_prev, jnp.max(s_t, axis=0, keepdims=True))
            alpha = jnp.exp2(m_prev - m_new)
            p_t = jnp.exp2(s_t - m_new)
            l_sc[...] = alpha * l_sc[...] + jnp.sum(p_t, axis=0, keepdims=True)
            m_sc[...] = m_new
            pv = jnp.dot(vt_ref[hh * DIL_HD:(hh + 1) * DIL_HD, :], p_t.astype(BF16),
                         preferred_element_type=F32)
            acc_sc[...] = alpha * acc_sc[...] + pv

        @pl.when(j == nk - 1)
        def _():
            o_ref[...] = jnp.concatenate([a0[...] / l0[...], a1[...] / l1[...]], axis=0).T
            lse_ref[...] = jnp.concatenate([m0[...] + jnp.log2(l0[...]), m1[...] + jnp.log2(l1[...])], axis=0)

    stat = [pltpu.VMEM((1, tq), F32), pltpu.VMEM((1, tq), F32), pltpu.VMEM((DIL_HD, tq), F32)]
    return pl.pallas_call(
        body, name=name, grid=(npair, nq, nk),
        in_specs=[pl.BlockSpec((tq, 2 * LANES), lambda pr, i, j: (i, pr)),
                  pl.BlockSpec((tk, 2 * LANES), lambda pr, i, j: (j, pr)),
                  pl.BlockSpec((LANES, tk), lambda pr, i, j: (pr, j))],
        out_specs=[pl.BlockSpec((tq, LANES), lambda pr, i, j: (i, pr)),
                   pl.BlockSpec((None, 2, tq), lambda pr, i, j: (pr, 0, i))],
        out_shape=[jax.ShapeDtypeStruct((L, HEAD_W), F32), jax.ShapeDtypeStruct((npair, 2, L), F32)],
        scratch_shapes=stat + stat, compiler_params=_cparams(VMEM_BIG_MB))(q, k, v_t)


def _mla_bwd(q, k, v, q_t, do, do_t, lse_cols, d_cols, name):
    L = q.shape[0]
    tq, tk = min(512, L), min(MLA_BWD_TK, L)
    nq, nk = L // tq, L // tk
    npair = MLA_HEADS // 2

    def body(q_ref, k_ref, v_ref, qt_ref, do_ref, dot_ref, lse_ref, d_ref, dq_out, dkt_out, dvt_out,
             dq_ref, dkt_ref, dvt_ref):
        j, i = pl.program_id(1), pl.program_id(2)

        @pl.when((j == 0) & (i == 0))
        def _():
            dq_ref[...] = jnp.zeros(dq_ref.shape, F32)

        @pl.when(i == 0)
        def _():
            dkt_ref[...] = jnp.zeros(dkt_ref.shape, F32)
            dvt_ref[...] = jnp.zeros(dvt_ref.shape, F32)

        first = lax.broadcasted_iota(jnp.int32, (tq, LANES), 1) < DIL_HD
        dov = do_ref[...]
        vv = v_ref[...]
        rows = pl.ds(pl.multiple_of(i * tq, tq), tq)
        for hh in range(2):
            sl = slice(hh * LANES, (hh + 1) * LANES)
            hrows = slice(hh * DIL_HD, (hh + 1) * DIL_HD)
            qh, kh = q_ref[:, sl], k_ref[:, sl]
            do_h = jnp.where(first if hh == 0 else ~first, dov, jnp.zeros_like(dov))
            s = lax.dot_general(qh, kh, NT, preferred_element_type=F32)
            p = jnp.exp2(s - lse_ref[:, hh:hh + 1])
            dvt_ref[hrows, :] += jnp.dot(dot_ref[hrows, :], p.astype(BF16), preferred_element_type=F32)
            dp = lax.dot_general(do_h, vv, NT, preferred_element_type=F32)
            ds = (p * (dp - d_ref[:, hh:hh + 1])).astype(BF16)
            dq_ref[rows, sl] += jnp.dot(ds, kh, preferred_element_type=F32)
            dkt_ref[sl, :] += jnp.dot(qt_ref[sl, :], ds, preferred_element_type=F32)

        @pl.when(i == nq - 1)
        def _():
            dkt_out[...] = dkt_ref[...].astype(BF16)
            dvt_out[...] = dvt_ref[...].astype(BF16)

        @pl.when((j == nk - 1) & (i == nq - 1))
        def _():
            dq_out[...] = dq_ref[...].astype(BF16)

    colspec = pl.BlockSpec((None, tq, 2), lambda pr, j, i: (pr, i, 0))
    return pl.pallas_call(
        body, name=name, grid=(npair, nk, nq),
        in_specs=[pl.BlockSpec((tq, 2 * LANES), lambda pr, j, i: (i, pr)),
                  pl.BlockSpec((tk, 2 * LANES), lambda pr, j, i: (j, pr)),
                  pl.BlockSpec((tk, LANES), lambda pr, j, i: (j, pr)),
                  pl.BlockSpec((2 * LANES, tq), lambda pr, j, i: (pr, i)),
                  pl.BlockSpec((tq, LANES), lambda pr, j, i: (i, pr)),
                  pl.BlockSpec((LANES, tq), lambda pr, j, i: (pr, i)),
                  colspec, colspec],
        out_specs=[pl.BlockSpec((L, 2 * LANES), lambda pr, j, i: (0, pr)),
                   pl.BlockSpec((2 * LANES, tk), lambda pr, j, i: (pr, j)),
                   pl.BlockSpec((LANES, tk), lambda pr, j, i: (pr, j))],
        out_shape=[jax.ShapeDtypeStruct((L, 1024), BF16), jax.ShapeDtypeStruct((1024, L), BF16),
                   jax.ShapeDtypeStruct((HEAD_W, L), BF16)],
        scratch_shapes=[pltpu.VMEM((L, 2 * LANES), F32), pltpu.VMEM((2 * LANES, tk), F32),
                        pltpu.VMEM((LANES, tk), F32)],
        compiler_params=_cparams(VMEM_BIG_MB))(q, k, v, q_t, do, do_t, lse_cols, d_cols)


DIL_SQ = 128
DIL_SW = DIL_SQ + 2 * DIL_HALF


def _dil_window(a_sub, ld):
    return pl.multiple_of(jnp.clip(a_sub - DIL_HALF, 0, ld - DIL_SW), DIL_HALF)


def _dil_band_mask(shift, heads):
    kidx = lax.broadcasted_iota(jnp.int32, (DIL_SW, heads * DIL_SQ), 0)
    qidx = lax.broadcasted_iota(jnp.int32, (DIL_SW, heads * DIL_SQ), 1) % DIL_SQ
    return jnp.abs(shift + kidx - qidx) <= DIL_HALF


def _pair_rows(x, first):
    zero = jnp.zeros_like(x)
    return jnp.concatenate([jnp.where(first, x, zero), jnp.where(first, zero, x)], axis=0)


def _dil_fwd(q, kp, vp_t4, name):
    d, ld, _ = q.shape
    assert ld % DIL_SQ == 0
    tq = min(512, ld)
    nq = ld // tq
    npair = HEAD_W // LANES
    nb = (ld + 2 * DIL_HALF) // LANES

    def body(q_ref, k_ref, vt_ref, o_ref, lse_ref):
        i = pl.program_id(2)
        first = lax.broadcasted_iota(jnp.int32, (DIL_SQ, LANES), 1) < DIL_HD
        kidx = lax.broadcasted_iota(jnp.int32, (DIL_SW, DIL_SQ), 0)
        for u in range(tq // DIL_SQ):
            a_sub = pl.multiple_of(i * tq + u * DIL_SQ, DIL_SQ)
            kk = a_sub // LANES
            rows = slice(u * DIL_SQ, (u + 1) * DIL_SQ)
            kwin = k_ref[pl.ds(a_sub, DIL_SW), :]
            valid = _dil_band_mask(-DIL_HALF, 1) & (kidx >= DIL_HALF - a_sub) & (kidx < ld + DIL_HALF - a_sub)
            qv = q_ref[rows, :]
            outs, lses = [], []
            for hh in range(2):
                qh = jnp.where(first if hh == 0 else ~first, qv, jnp.zeros_like(qv))
                s_t = jnp.where(valid, lax.dot_general(kwin, qh, NT, preferred_element_type=F32), NEG)
                m = jnp.max(s_t, axis=0, keepdims=True)
                p32 = jnp.exp2(s_t - m)
                l = jnp.sum(p32, axis=0, keepdims=True)
                p_t = p32.astype(BF16)
                hrows = slice(hh * DIL_HD, (hh + 1) * DIL_HD)
                pv = (jnp.dot(vt_ref[kk, hrows, :], p_t[0:LANES, :], preferred_element_type=F32)
                      + jnp.dot(vt_ref[kk + 1, hrows, :], p_t[LANES:DIL_SW, :], preferred_element_type=F32))
                outs.append(pv / l)
                lses.append(jnp.broadcast_to(m + jnp.log2(l), (DIL_HD, DIL_SQ)))
            o_ref[rows, :] = jnp.concatenate(outs, axis=0).T
            lse_ref[rows, :] = jnp.concatenate(lses, axis=0).T

    blk = pl.BlockSpec((None, tq, LANES), lambda r, pr, i: (r, i, pr))
    full = pl.BlockSpec((None, ld + 2 * DIL_HALF, LANES), lambda r, pr, i: (r, 0, pr))
    vspec = pl.BlockSpec((None, nb, LANES, LANES), lambda r, pr, i: (r, 0, pr, 0))
    return pl.pallas_call(
        body, name=name, grid=(d, npair, nq), in_specs=[blk, full, vspec], out_specs=[blk, blk],
        out_shape=[jax.ShapeDtypeStruct((d, ld, HEAD_W), F32), jax.ShapeDtypeStruct((d, ld, HEAD_W), F32)],
        compiler_params=_cparams(VMEM_BIG_MB))(q, kp, vp_t4)


def _dil_bwd(q, k, v, do, lse_rows, d_rows, name):
    d, ld, _ = q.shape
    assert ld % DIL_SQ == 0 and ld >= DIL_SW
    tq = min(512, ld)
    nq = ld // tq
    npair = HEAD_W // LANES
    span = min(tq + 2 * DIL_HALF, ld)

    def body(q_ref, k_ref, v_ref, do_ref, lse_ref, d_ref, dq_ref, dk_ref, dv_ref, dk_sc, dv_sc, dk_acc, dv_acc):
        i = pl.program_id(2)

        @pl.when(i == 0)
        def _():
            dk_acc[...] = jnp.zeros(dk_acc.shape, F32)
            dv_acc[...] = jnp.zeros(dv_acc.shape, F32)

        dk_sc[...] = jnp.zeros(dk_sc.shape, F32)
        dv_sc[...] = jnp.zeros(dv_sc.shape, F32)
        first = lax.broadcasted_iota(jnp.int32, (DIL_SQ, LANES), 1) < DIL_HD
        base = pl.multiple_of(jnp.clip(i * tq - DIL_HALF, 0, ld - span), DIL_HALF)
        for u in range(tq // DIL_SQ):
            a_sub = i * tq + u * DIL_SQ
            ws = _dil_window(a_sub, ld)
            rows = slice(u * DIL_SQ, (u + 1) * DIL_SQ)
            win = pl.ds(pl.multiple_of(ws - base, DIL_HALF), DIL_SW)
            kwin = k_ref[pl.ds(ws, DIL_SW), :]
            vwin = v_ref[pl.ds(ws, DIL_SW), :]
            q2 = _pair_rows(q_ref[rows, :], first)
            do2 = _pair_rows(do_ref[rows, :], first)
            lse2 = jnp.concatenate([lse_ref[0:1, rows], lse_ref[1:2, rows]], axis=1)
            dd2 = jnp.concatenate([d_ref[0:1, rows], d_ref[1:2, rows]], axis=1)
            s_t = lax.dot_general(kwin, q2, NT, preferred_element_type=F32)
            p_t = jnp.exp2(jnp.where(_dil_band_mask(ws - a_sub, 2), s_t, NEG) - lse2)
            dv_sc[win, :] += jnp.dot(p_t.astype(BF16), do2, preferred_element_type=F32)
            dp_t = lax.dot_general(vwin, do2, NT, preferred_element_type=F32)
            ds_t = (p_t * (dp_t - dd2)).astype(BF16)
            dk_sc[win, :] += jnp.dot(ds_t, q2, preferred_element_type=F32)
            dq2 = lax.dot_general(ds_t, kwin, TN, preferred_element_type=F32)
            dq_ref[rows, :] = jnp.where(first, dq2[0:DIL_SQ, :], dq2[DIL_SQ:2 * DIL_SQ, :]).astype(BF16)
        dk_acc[pl.ds(base, span), :] += dk_sc[...]
        dv_acc[pl.ds(base, span), :] += dv_sc[...]

        @pl.when(i == nq - 1)
        def _():
            dk_ref[...] = dk_acc[...].astype(BF16)
            dv_ref[...] = dv_acc[...].astype(BF16)

    blk = pl.BlockSpec((None, tq, LANES), lambda r, pr, i: (r, i, pr))
    full = pl.BlockSpec((None, ld, LANES), lambda r, pr, i: (r, 0, pr))
    rowspec = pl.BlockSpec((None, None, 2, tq), lambda r, pr, i: (r, pr, 0, i))
    return pl.pallas_call(
        body, name=name, grid=(d, npair, nq), in_specs=[blk, full, full, blk, rowspec, rowspec],
        out_specs=[blk, full, full],
        out_shape=[jax.ShapeDtypeStruct((d, ld, HEAD_W), BF16)] * 3,
        scratch_shapes=[pltpu.VMEM((span, LANES), F32), pltpu.VMEM((span, LANES), F32),
                        pltpu.VMEM((ld, LANES), F32), pltpu.VMEM((ld, LANES), F32)],
        compiler_params=_cparams(VMEM_BIG_MB))(q, k, v, do, lse_rows, d_rows)


TILE_BYTES = 1 << 21


def _row_tile(rows, cols, budget=TILE_BYTES):
    for parts in range(1, rows + 1):
        tr = rows // parts
        if rows % parts == 0 and tr % 8 == 0 and tr * cols * 4 <= budget:
            return tr
    return rows


def _add2(a, b, name, out_dtype):
    n, rows, cols = a.shape
    tr = _row_tile(rows, cols)

    def body(a_ref, b_ref, o_ref):
        o_ref[...] = (a_ref[...] + b_ref[...]).astype(out_dtype)

    spec = pl.BlockSpec((None, tr, cols), lambda t, i: (t, i, 0))
    return pl.pallas_call(body, name=name, grid=(n, rows // tr), in_specs=[spec, spec], out_specs=spec,
                          out_shape=jax.ShapeDtypeStruct(a.shape, out_dtype))(a, b)


def _add4_ordered(a, name):
    _, rows, cols = a.shape
    tr = _row_tile(rows, cols, TILE_BYTES // 4)

    def body(a_ref, o_ref):
        o_ref[...] = ((a_ref[0].astype(F32) + a_ref[1].astype(F32)) + a_ref[2].astype(F32)) + a_ref[3].astype(F32)

    return pl.pallas_call(
        body, name=name, grid=(rows // tr,), in_specs=[pl.BlockSpec((4, tr, cols), lambda i: (0, i, 0))],
        out_specs=pl.BlockSpec((tr, cols), lambda i: (i, 0)),
        out_shape=jax.ShapeDtypeStruct((rows, cols), F32))(a)


def _adamw(w, g, m, v, name):
    rows, cols = w.shape
    tr = _row_tile(rows, cols)
    bc1 = 1.0 - ADAM_B1 ** ADAM_STEP
    bc2 = 1.0 - ADAM_B2 ** ADAM_STEP

    def body(w_ref, g_ref, m_ref, v_ref, d_ref, nm_ref, nv_ref):
        gv = g_ref[...]
        nm = ADAM_B1 * m_ref[...] + (1.0 - ADAM_B1) * gv
        nv = ADAM_B2 * v_ref[...] + (1.0 - ADAM_B2) * (gv * gv)
        d_ref[...] = -ADAM_LR * ((nm / bc1) / (jnp.sqrt(nv / bc2) + ADAM_EPS) + ADAM_WD * w_ref[...])
        nm_ref[...] = nm
        nv_ref[...] = nv

    spec = pl.BlockSpec((tr, cols), lambda i: (i, 0))
    return pl.pallas_call(body, name=name, grid=(rows // tr,), in_specs=[spec] * 4, out_specs=[spec] * 3,
                          out_shape=[jax.ShapeDtypeStruct(w.shape, F32)] * 3,
                          compiler_params=_cparams(VMEM_BIG_MB))(w, g, m, v)


ANY = pl.BlockSpec(memory_space=pl.ANY)


def _place():
    return lax.axis_index("x"), lax.axis_index("y"), lax.axis_index("c")


def _rcopy(send_sems, recv_sems, n, src, dst, to):
    return pltpu.make_async_remote_copy(src_ref=src, dst_ref=dst, send_sem=send_sems.at[n], recv_sem=recv_sems.at[n],
                                        device_id=to, device_id_type=MESH)


def _allgather_weights(shards):
    na = len(shards)

    def body(*refs):
        w_refs, g_refs = refs[:na], refs[na:2 * na]
        send_sems, recv_sems, local_sems = refs[2 * na:]
        x, y, c = _place()
        s = 2 * x + y
        chips = [(1 - x, y), (x, 1 - y), (1 - x, 1 - y)]

        def half(a, shard, h):
            hr = shards[a].shape[0] // 2
            return g_refs[a].at[shard, pl.ds(h * hr, hr), :]

        started = []
        for a in range(na):
            hr = shards[a].shape[0] // 2
            mine = pltpu.make_async_copy(w_refs[a], g_refs[a].at[s], local_sems.at[a])
            mine.start()
            started.append(mine)
        sends = []
        for a in range(na):
            hr = shards[a].shape[0] // 2
            for n, (cx, cy) in enumerate(chips):
                cp = _rcopy(send_sems, recv_sems, 6 * a + n, w_refs[a].at[pl.ds(c * hr, hr), :], half(a, s, c),
                            (cx, cy, c))
                cp.start()
                sends.append(cp)
        for a in range(na):
            for n, (cx, cy) in enumerate(chips):
                sj = 2 * cx + cy
                _rcopy(send_sems, recv_sems, 6 * a + n, half(a, sj, c), half(a, sj, c), (cx, cy, c)).wait_recv()
                fw = _rcopy(send_sems, recv_sems, 6 * a + 3 + n, half(a, sj, c), half(a, sj, c), (x, y, 1 - c))
                fw.start()
                sends.append(fw)
        for a in range(na):
            for n, (cx, cy) in enumerate(chips):
                sj = 2 * cx + cy
                _rcopy(send_sems, recv_sems, 6 * a + 3 + n, half(a, sj, 1 - c), half(a, sj, 1 - c),
                       (x, y, 1 - c)).wait_recv()
        for cp in sends:
            cp.wait_send()
        for mine in started:
            mine.wait()

    return pl.pallas_call(
        body, name="allgather_weights", in_specs=[ANY] * na, out_specs=[ANY] * na,
        out_shape=[jax.ShapeDtypeStruct((N_SHARD,) + t.shape, t.dtype) for t in shards],
        scratch_shapes=[pltpu.SemaphoreType.DMA((6 * na,)), pltpu.SemaphoreType.DMA((6 * na,)),
                        pltpu.SemaphoreType.DMA((na,))])(*shards)


def _sibling_send_halves(gs):
    na = len(gs)

    def body(*refs):
        g_refs, o_refs = refs[:na], refs[na:2 * na]
        send_sems, recv_sems = refs[2 * na:]
        x, y, c = _place()
        cps = []
        for a in range(na):
            for t in range(N_SHARD):
                cp = _rcopy(send_sems, recv_sems, N_SHARD * a + t, g_refs[a].at[t, 1 - c], o_refs[a].at[t],
                            (x, y, 1 - c))
                cp.start()
                cps.append(cp)
        for cp in cps:
            cp.wait()

    return pl.pallas_call(
        body, name="grad_sibling_exchange", in_specs=[ANY] * na, out_specs=[ANY] * na,
        out_shape=[jax.ShapeDtypeStruct((N_SHARD,) + g.shape[2:], g.dtype) for g in gs],
        scratch_shapes=[pltpu.SemaphoreType.DMA((N_SHARD * na,)), pltpu.SemaphoreType.DMA((N_SHARD * na,))])(*gs)


def _chip_scatter(parts):
    na = len(parts)

    def body(*refs):
        a_refs, o_refs = refs[:na], refs[na:2 * na]
        send_sems, recv_sems, local_sems = refs[2 * na:]
        x, y, c = _place()
        s = 2 * x + y
        chips = [(1 - x, y), (x, 1 - y), (1 - x, 1 - y)]
        started, cps = [], []
        for a in range(na):
            mine = pltpu.make_async_copy(a_refs[a].at[s], o_refs[a].at[s], local_sems.at[a])
            mine.start()
            started.append(mine)
            for n, (cx, cy) in enumerate(chips):
                cp = _rcopy(send_sems, recv_sems, 3 * a + n, a_refs[a].at[2 * cx + cy], o_refs[a].at[s], (cx, cy, c))
                cp.start()
                cps.append(cp)
        for a in range(na):
            for n, (cx, cy) in enumerate(chips):
                sj = 2 * cx + cy
                _rcopy(send_sems, recv_sems, 3 * a + n, a_refs[a].at[sj], o_refs[a].at[sj], (cx, cy, c)).wait_recv()
        for cp in cps:
            cp.wait_send()
        for mine in started:
            mine.wait()

    return pl.pallas_call(
        body, name="grad_chip_scatter", in_specs=[ANY] * na, out_specs=[ANY] * na,
        out_shape=[jax.ShapeDtypeStruct(t.shape, t.dtype) for t in parts],
        scratch_shapes=[pltpu.SemaphoreType.DMA((3 * na,)), pltpu.SemaphoreType.DMA((3 * na,)),
                        pltpu.SemaphoreType.DMA((na,))])(*parts)


def _sibling_swap(rs):
    na = len(rs)

    def body(*refs):
        r_refs, o_refs = refs[:na], refs[na:2 * na]
        send_sems, recv_sems = refs[2 * na:]
        x, y, c = _place()
        cps = []
        for a in range(na):
            cp = _rcopy(send_sems, recv_sems, a, r_refs[a], o_refs[a], (x, y, 1 - c))
            cp.start()
            cps.append(cp)
        for cp in cps:
            cp.wait()

    return pl.pallas_call(
        body, name="grad_sibling_swap", in_specs=[ANY] * na, out_specs=[ANY] * na,
        out_shape=[jax.ShapeDtypeStruct(t.shape, t.dtype) for t in rs],
        scratch_shapes=[pltpu.SemaphoreType.DMA((na,)), pltpu.SemaphoreType.DMA((na,))])(*rs)


def _pack_small(norm_g, q_norm_g, kv_norm_g, final_g):
    flat = jnp.concatenate([norm_g.reshape(-1), q_norm_g.reshape(-1), kv_norm_g.reshape(-1), final_g.reshape(-1),
                            jnp.zeros((SMALL_ROWS * LANES - N_SMALL,), F32)])
    return flat.reshape(SMALL_ROWS, LANES)


def _split_small(s):
    s = s.reshape(-1)
    o = 0
    out = []
    for n, shape in ((DEPTH * D_MODEL, (DEPTH, D_MODEL)), (DEPTH * Q_LORA, (DEPTH, Q_LORA)),
                     (DEPTH * KV_LORA, (DEPTH, KV_LORA)), (D_MODEL, (D_MODEL,))):
        out.append(s[o:o + n].reshape(shape))
        o += n
    return out


def _assemble_w_in(sh):
    z = lambda n: jnp.zeros(sh.shape[1:3] + (n,), sh.dtype)
    s0, s1, s2, s3 = sh[0], sh[1], sh[2], sh[3]
    return jnp.concatenate([s0[..., 0:640], z(64), s0[..., 640:672], z(32), z(256), s0[..., 672:1184],
                            s3[..., 1064:1576], s0[..., 1184:1576], s1, s2, s3[..., 0:1064]], axis=-1)


def _split_w_in_grad(d):
    sh0 = jnp.concatenate([d[..., 0:640], d[..., 704:736], d[..., 1024:1536], d[..., 2048:2440]], axis=-1)
    sh3 = jnp.concatenate([d[..., 5592:6656], d[..., 1536:2048]], axis=-1)
    return jnp.stack([sh0, d[..., 2440:4016], d[..., 4016:5592], sh3]).reshape(N_SHARD, DEPTH * D_MODEL, SHARD_COLS_IN)


def _col_shards(w):
    dl, r, cc = w.shape
    return w.reshape(dl, r, N_SHARD, cc // N_SHARD).transpose(2, 0, 1, 3).reshape(N_SHARD, dl * r, cc // N_SHARD)


def _from_col_shards(g, rows):
    cc = g.shape[-1]
    return g.reshape(N_SHARD, DEPTH, rows, cc).transpose(1, 2, 0, 3).reshape(DEPTH, rows, N_SHARD * cc)


def _pad_w_in(w):
    z = lambda n: jnp.zeros(w.shape[:-1] + (n,), w.dtype)
    return jnp.concatenate([w[..., 0:640], z(64), w[..., 640:672], z(32), z(256), w[..., 672:1184],
                            w[..., 5792:6304], w[..., 1184:5792]], axis=-1)


def _unpad_w_in(w):
    return jnp.concatenate([w[..., 0:640], w[..., 704:736], w[..., 1024:1536], w[..., 2048:6656],
                            w[..., 1536:2048]], axis=-1)


def _pad_w_uq(w):
    s = w.shape[:-1]
    w = w.reshape(s + (MLA_HEADS, 96))
    return jnp.pad(w, [(0, 0)] * (w.ndim - 1) + [(0, 32)]).reshape(s + (1024,))


def _unpad_w_uq(w):
    s = w.shape[:-1]
    return w.reshape(s + (MLA_HEADS, LANES))[..., :96].reshape(s + (768,))


def _pad_w_ukv(w):
    s = w.shape[:-1]
    w = w.reshape(s + (MLA_HEADS, 128))
    kpart = jnp.pad(w[..., :64], [(0, 0)] * (w.ndim - 1) + [(0, 64)]).reshape(s + (1024,))
    vpart = w[..., 64:].reshape(s + (512,))
    return jnp.concatenate([kpart, vpart], axis=-1)


def _unpad_w_ukv(w):
    s = w.shape[:-1]
    kpart = w[..., :1024].reshape(s + (MLA_HEADS, LANES))[..., :64]
    vpart = w[..., 1024:].reshape(s + (MLA_HEADS, 64))
    return jnp.concatenate([kpart, vpart], axis=-1).reshape(s + (1024,))


def _rope_tables(L, dim, lane_lo, period):
    half = dim // 2
    inv = 1.0 / (ROPE_THETA ** (jnp.arange(0, dim, 2, dtype=F32) / dim))
    ang = jnp.arange(L, dtype=F32)[:, None] * inv[None, :]
    cos, sin = jnp.cos(ang), jnp.sin(ang)
    one = lambda n: jnp.ones((L, n), F32)
    zero = lambda n: jnp.zeros((L, n), F32)
    rest = period - lane_lo - dim
    rep = LANES // period
    c = jnp.tile(jnp.concatenate([one(lane_lo), cos, cos, one(rest)], axis=1), (1, rep))
    a = jnp.tile(jnp.concatenate([zero(lane_lo), -sin, zero(half), zero(rest)], axis=1), (1, rep))
    b = jnp.tile(jnp.concatenate([zero(lane_lo + half), sin, zero(rest)], axis=1), (1, rep))
    return c, a, b


def _to_strided(t, d):
    L, w = t.shape
    return t.reshape(L // d, d, w).transpose(1, 0, 2)


def _from_strided(t):
    d, ld, w = t.shape
    return t.transpose(1, 0, 2).reshape(d * ld, w)


def _head_rows(t):
    return t.T.reshape(MLA_HEADS // 2, 2, t.shape[0])


def _head_rows_strided(t, d):
    s = _to_strided(t, d)
    return s.transpose(0, 2, 1).reshape(d, MLA_HEADS // 2, 2, s.shape[1])

def _local_grads(x, target, norm_g, w_in_p, q_norm_g, kv_norm_g, w_uq_p, w_ukv_p, w_out, final_g):
    L = x.shape[0]
    tabs_m = _rope_tables(L, MLA_ROPE, MLA_NOPE, LANES)
    tabs_d = _rope_tables(L, ROT_DIM, 0, DIL_HD)
    tabs_m_t = (tabs_m[0], -tabs_m[1], -tabs_m[2])
    tabs_d_t = (tabs_d[0], -tabs_d[1], -tabs_d[2])
    w_in_t = jnp.swapaxes(w_in_p, 1, 2)
    w_uq_t = jnp.swapaxes(w_uq_p, 1, 2)
    w_ukv_t = jnp.swapaxes(w_ukv_p, 1, 2)
    w_out_t = jnp.swapaxes(w_out, 1, 2)

    saved = []
    for l in range(DEPTH):
        h = _rms_fwd(x, norm_g[l:l + 1], "rms_fwd")
        p = _mm(h, w_in_p[l], tm=512, tn=3328, tk=1024, out_dtype=BF16, name="in_proj")
        q, k, v, cqn, ckvn = _mla_prep(p, q_norm_g[l:l + 1], kv_norm_g[l:l + 1], w_uq_p[l], w_ukv_p[l], tabs_m,
                                       "mla_prep")
        oa, lse_a = _mla_fwd(q, k, v.T, "mla_fwd")
        dil = _dil_prep(p, tabs_d, "dil_prep")
        dil_s, o_g, lse_g = [], [], []
        for g, (_, dd) in enumerate(DIL_PAIRS):
            qs, ks, vs = (_to_strided(t, dd) for t in dil[3 * g:3 * g + 3])
            pad = ((0, 0), (DIL_HALF, DIL_HALF), (0, 0))
            vp = jnp.pad(vs, pad)
            v_t4 = vp.reshape(dd, vp.shape[1] // LANES, LANES, HEAD_W).transpose(0, 1, 3, 2)
            og, lg = _dil_fwd(qs, jnp.pad(ks, pad), v_t4, "dil_fwd_%d" % dd)
            dil_s.append((qs, ks, vs))
            o_g.append(_from_strided(og))
            lse_g.append(_from_strided(lg))
        ab, bm, lt = _merge_gate(oa, p, o_g, lse_g, "merge_gate")
        x_next = _mm(ab, w_out[l], tm=1024, tn=1024, tk=1024, out_dtype=F32, name="out_proj", add=x)
        saved.append((x, h, p, q, k, v, cqn, ckvn, oa, lse_a, dil_s, bm, lt, ab))
        x = x_next

    loss_b, dx, d_final = _loss_head(x, final_g[None, :], target, "loss_head")
    loss = loss_b[0, 0]

    d_norm, d_qn, d_kvn, d_win, d_wuq, d_wukv, d_wout = [], [], [], [], [], [], []
    for l in reversed(range(DEPTH)):
        x_l, h, p, q, k, v, cqn, ckvn, oa, lse_a, dil_s, bm, lt, ab = saved[l]
        dab = _mm(dx, w_out_t[l], tm=1024, tn=1024, tk=1024, out_dtype=F32, name="out_proj_dgrad")
        d_wout.append(_mm(ab, dx, tm=1024, tn=1024, tk=1024, out_dtype=F32, name="out_proj_wgrad", a_is_kxm=True))
        doa, dbm, D_a, D_b, dgates = _gate_bwd(dab, p, oa, bm, "gate_bwd")
        dq, dk_t, dv_t = _mla_bwd(q, k, v, q.T, doa, doa.T, lse_a.transpose(0, 2, 1),
                                  D_a.reshape(L, MLA_HEADS // 2, 2).transpose(1, 0, 2), "mla_bwd")
        dk, dv = dk_t.T, dv_t.T
        dp_mla, dq_pre, dkv, dqg, dkvg = _mla_prep_bwd(dq, dk, dv, p, q_norm_g[l:l + 1], kv_norm_g[l:l + 1],
                                                       w_uq_t[l], w_ukv_t[l], tabs_m_t, "mla_prep_bwd")
        d_wuq.append(_mm(cqn, dq_pre, tm=Q_LORA, tn=1024, tk=2048, out_dtype=F32, name="w_uq_wgrad", a_is_kxm=True))
        d_wukv.append(_mm(ckvn, dkv, tm=KV_LORA, tn=1536, tk=2048, out_dtype=F32, name="w_ukv_wgrad", a_is_kxm=True))
        dgr = []
        for g, (_, dd) in enumerate(DIL_PAIRS):
            qs, ks, vs = dil_s[g]
            dqs, dks, dvs = _dil_bwd(qs, ks, vs, _to_strided(dbm, dd), _head_rows_strided(lt, dd),
                                     _head_rows_strided(D_b, dd), "dil_bwd_%d" % dd)
            dgr += [_from_strided(dqs), _from_strided(dks), _from_strided(dvs)]
        dp_dil = _dil_prep_bwd(dgr, tabs_d_t, "dil_prep_bwd")
        dp = jnp.concatenate([dp_mla, dgates, dp_dil], axis=1)
        dh = _mm(dp, w_in_t[l], tm=1024, tn=1024, tk=1664, out_dtype=F32, name="in_proj_dgrad")
        d_win.append(_mm(h, dp, tm=1024, tn=1664, tk=1024, out_dtype=F32, name="in_proj_wgrad", a_is_kxm=True))
        dx, dng = _rms_bwd(dh, x_l, norm_g[l:l + 1], dx, "rms_bwd")
        d_norm.append(dng[0])
        d_qn.append(dqg[0])
        d_kvn.append(dkvg[0])

    rev = lambda xs: jnp.stack(xs[::-1])
    return (loss, dx, rev(d_norm), rev(d_win), rev(d_qn), rev(d_kvn), rev(d_wuq), rev(d_wukv), rev(d_wout),
            d_final[0])


def kernel(x, norm_g, w_in, q_norm_g, kv_norm_g, w_uq, w_ukv, w_out, final_g, loss_target, m_norm_g, m_w_in, m_q_norm_g, m_kv_norm_g, m_w_uq, m_w_ukv, m_w_out, m_final_g, v_norm_g, v_w_in, v_q_norm_g, v_kv_norm_g, v_w_uq, v_w_ukv, v_w_out, v_final_g):
    c = lax.axis_index("c")

    def families(a_in, a_uq, a_ukv, a_out):
        return [t.reshape(shape) for t, shape in zip((a_in, a_uq, a_ukv, a_out), FAM_SHAPES)]

    g_in, g_uq, g_ukv, g_out = _allgather_weights([t.astype(BF16) for t in families(w_in, w_uq, w_ukv, w_out)])
    w_in_p = _assemble_w_in(g_in.reshape(N_SHARD, DEPTH, D_MODEL, SHARD_COLS_IN))
    w_uq_p = _pad_w_uq(_from_col_shards(g_uq, Q_LORA))
    w_ukv_p = _pad_w_ukv(_from_col_shards(g_ukv, KV_LORA))
    w_out_f = g_out.reshape(N_SHARD, DEPTH, 1024 // N_SHARD, D_MODEL).transpose(1, 0, 2, 3).reshape(DEPTH, 1024, D_MODEL)

    (loss, dx, d_norm, d_win_p, d_qn, d_kvn, d_wuq_p, d_wukv_p, d_wout, d_final) = _local_grads(
        x[0], loss_target[0], norm_g, w_in_p, q_norm_g, kv_norm_g, w_uq_p, w_ukv_p, w_out_f, final_g)
    loss = lax.psum(loss, ("x", "y", "c"))

    small = _pack_small(d_norm, d_qn, d_kvn, d_final)
    grads = [_split_w_in_grad(d_win_p), _col_shards(_unpad_w_uq(d_wuq_p)), _col_shards(_unpad_w_ukv(d_wukv_p)),
             d_wout.reshape(DEPTH, N_SHARD, 1024 // N_SHARD, D_MODEL).transpose(1, 0, 2, 3).reshape(
                 N_SHARD, DEPTH * (1024 // N_SHARD), D_MODEL),
             jnp.broadcast_to(small[None], (N_SHARD, SMALL_ROWS, LANES))]
    halves = [g.reshape(N_SHARD, 2, g.shape[1] // 2, g.shape[2]) for g in grads]
    from_sib = _sibling_send_halves(halves)
    chip_sum = [_add2(lax.dynamic_index_in_dim(h, c, axis=1, keepdims=False), f, "grad_add_pair", BF16)
                for h, f in zip(halves, from_sib)]
    red_half = [_add4_ordered(t, "grad_add_chips") for t in _chip_scatter(chip_sum)]
    other_half = _sibling_swap(red_half)
    gred = []
    for mine, other in zip(red_half, other_half):
        both = jnp.stack([mine, other])
        gred.append(jnp.concatenate([lax.dynamic_index_in_dim(both, c, axis=0, keepdims=False),
                                     lax.dynamic_index_in_dim(both, 1 - c, axis=0, keepdims=False)], axis=0))

    wf = families(w_in, w_uq, w_ukv, w_out) + [_pack_small(norm_g, q_norm_g, kv_norm_g, final_g)]
    mf = families(m_w_in, m_w_uq, m_w_ukv, m_w_out) + [_pack_small(m_norm_g, m_q_norm_g, m_kv_norm_g, m_final_g)]
    vf = families(v_w_in, v_w_uq, v_w_ukv, v_w_out) + [_pack_small(v_norm_g, v_q_norm_g, v_kv_norm_g, v_final_g)]
    upd = [_adamw(w, g, m, v, "adamw") for w, g, m, v in zip(wf, gred, mf, vf)]

    def leaves(fams):
        a_in, a_uq, a_ukv, a_out, s = fams
        s_norm, s_qn, s_kvn, s_final = _split_small(s)
        return [s_norm, a_in.reshape(w_in.shape), s_qn, s_kvn, a_uq.reshape(w_uq.shape), a_ukv.reshape(w_ukv.shape),
                a_out.reshape(w_out.shape), s_final]

    return (loss, dx[None], *leaves(gred), *leaves([u[0] for u in upd]), *leaves([u[1] for u in upd]),
            *leaves([u[2] for u in upd]))
```

```python
import functools

import jax
import jax.numpy as jnp
from jax import lax
from jax.experimental import pallas as pl
from jax.experimental.pallas import tpu as pltpu

F32 = jnp.float32
BF16 = jnp.bfloat16
MESH = pl.DeviceIdType.MESH

D_MODEL = 1024
DEPTH = 4
MLA_HEADS = 8
MLA_NOPE = 64
MLA_ROPE = 32
Q_LORA = 384
KV_LORA = 256
DIL_PAIRS = ((128, 1), (512, 4), (2048, 16))
DIL_HD = 64
DIL_HALF = 64
ROT_DIM = 16
ROPE_THETA = 500000.0
EPS = 1e-6
IN_WIDTH = 6304
N_SHARD = 4

P_WIDTH = 6656
P_MLA = 1024
P_GATE = 1024
P_DIL0 = 2048
LANES = 128
HEAD_W = 512

ADAM_LR = 0.001
ADAM_B1 = 0.9
ADAM_B2 = 0.999
ADAM_EPS = 1e-08
ADAM_WD = 0.01
ADAM_STEP = 10

SHARD_COLS_IN = IN_WIDTH // N_SHARD
FAM_SHAPES = ((DEPTH * D_MODEL, SHARD_COLS_IN), (DEPTH * Q_LORA, 768 // N_SHARD), (DEPTH * KV_LORA, 1024 // N_SHARD),
              (DEPTH * (1024 // N_SHARD), D_MODEL))
N_SMALL = DEPTH * (D_MODEL + Q_LORA + KV_LORA) + D_MODEL
SMALL_ROWS = 64
VMEM_BIG_MB = 48


def _cparams(vmem_mb=None):
    if vmem_mb is None:
        return None
    return pltpu.CompilerParams(vmem_limit_bytes=vmem_mb << 20)


def _sigmoid(x):
    return 1.0 / (1.0 + jnp.exp(-x))


def _rope(x, c, a, b, sh):
    return x * c + pltpu.roll(x, LANES - sh, 1) * a + pltpu.roll(x, sh, 1) * b


def _per_head8(x, pick_first):
    r = lax.broadcasted_iota(jnp.int32, (HEAD_W, MLA_HEADS), 0)
    c = lax.broadcasted_iota(jnp.int32, (HEAD_W, MLA_HEADS), 1)
    sel = (r == c * DIL_HD) if pick_first else (r // DIL_HD == c)
    mat = jnp.where(sel, 1.0, 0.0).astype(BF16)
    out = jnp.zeros((x.shape[0], MLA_HEADS), F32)
    for _ in range(3):
        part = x.astype(BF16)
        out = out + jnp.dot(part, mat, preferred_element_type=F32)
        x = x - part.astype(F32)
    return out


def _mm(a, b, *, tm, tn, tk, out_dtype, name, add=None, a_is_kxm=False):
    K, M = a.shape if a_is_kxm else a.shape[::-1]
    N = b.shape[1]
    tm, tn, tk = min(tm, M), min(tn, N), min(tk, K)
    assert M % tm == 0 and N % tn == 0 and K % tk == 0, (a.shape, b.shape)
    nk = K // tk
    has_add = add is not None

    def body(*refs):
        if has_add:
            a_ref, b_ref, add_ref, o_ref, acc = refs
        else:
            a_ref, b_ref, o_ref, acc = refs
        k = pl.program_id(2)
        if a_is_kxm:
            part = lax.dot_general(a_ref[...].astype(BF16), b_ref[...].astype(BF16), (((0,), (0,)), ((), ())),
                                   preferred_element_type=F32)
        else:
            part = jnp.dot(a_ref[...].astype(BF16), b_ref[...].astype(BF16), preferred_element_type=F32)

        @pl.when(k == 0)
        def _():
            acc[...] = part

        @pl.when(k > 0)
        def _():
            acc[...] += part

        @pl.when(k == nk - 1)
        def _():
            r = acc[...]
            if has_add:
                r = r + add_ref[...]
            o_ref[...] = r.astype(out_dtype)

    a_spec = pl.BlockSpec((tk, tm), lambda i, j, k: (k, i)) if a_is_kxm else pl.BlockSpec((tm, tk), lambda i, j, k: (i, k))
    in_specs = [a_spec, pl.BlockSpec((tk, tn), lambda i, j, k: (k, j))]
    args = [a, b]
    if has_add:
        in_specs.append(pl.BlockSpec((tm, tn), lambda i, j, k: (i, j)))
        args.append(add)
    return pl.pallas_call(
        body, name=name, grid=(M // tm, N // tn, nk), in_specs=in_specs,
        out_specs=pl.BlockSpec((tm, tn), lambda i, j, k: (i, j)),
        out_shape=jax.ShapeDtypeStruct((M, N), out_dtype),
        scratch_shapes=[pltpu.VMEM((tm, tn), F32)], compiler_params=_cparams(VMEM_BIG_MB))(*args)


def _row_spec(tm, w, cb=0):
    return pl.BlockSpec((tm, w), lambda i: (i, cb))


def _const_spec(arr):
    nd = arr.ndim
    return pl.BlockSpec(arr.shape, lambda i: (0,) * nd)


def _rms_fwd(x, g, name):
    L, D = x.shape
    tm = min(512, L)

    def body(x_ref, g_ref, o_ref):
        xv = x_ref[...]
        r = lax.rsqrt(jnp.mean(xv * xv, axis=-1, keepdims=True) + EPS)
        o_ref[...] = (xv * r * g_ref[...]).astype(BF16)

    return pl.pallas_call(
        body, name=name, grid=(L // tm,), in_specs=[_row_spec(tm, D), _const_spec(g)],
        out_specs=_row_spec(tm, D), out_shape=jax.ShapeDtypeStruct((L, D), BF16))(x, g)


def _rms_bwd(dh, x, g, dres, name):
    L, D = x.shape
    tm = min(512, L)

    def body(dh_ref, x_ref, g_ref, dres_ref, dx_ref, dg_ref):
        xv = x_ref[...]
        dy = dh_ref[...]
        r = lax.rsqrt(jnp.mean(xv * xv, axis=-1, keepdims=True) + EPS)
        dyg = dy * g_ref[...]
        dx_ref[...] = dres_ref[...] + r * dyg - xv * (r * r * r) * jnp.mean(dyg * xv, axis=-1, keepdims=True)
        part = jnp.sum(dy * xv * r, axis=0, keepdims=True)

        @pl.when(pl.program_id(0) == 0)
        def _():
            dg_ref[...] = part

        @pl.when(pl.program_id(0) > 0)
        def _():
            dg_ref[...] += part

    return pl.pallas_call(
        body, name=name, grid=(L // tm,),
        in_specs=[_row_spec(tm, D), _row_spec(tm, D), _const_spec(g), _row_spec(tm, D)],
        out_specs=[_row_spec(tm, D), pl.BlockSpec((1, D), lambda i: (0, 0))],
        out_shape=[jax.ShapeDtypeStruct((L, D), F32), jax.ShapeDtypeStruct((1, D), F32)])(dh, x, g, dres)


def _loss_head(x, g, target, name):
    L, D = x.shape
    tm = min(512, L)

    def body(x_ref, g_ref, t_ref, loss_ref, dx_ref, dg_ref):
        xv = x_ref[...]
        gv = g_ref[...]
        r = lax.rsqrt(jnp.mean(xv * xv, axis=-1, keepdims=True) + EPS)
        xr = xv * r
        err = xr * gv - t_ref[...]
        lp = 0.5 * jnp.sum(jnp.mean(err * err, axis=-1, keepdims=True))
        dy = err * (1.0 / D)
        dyg = dy * gv
        dx_ref[...] = r * dyg - xv * (r * r * r) * jnp.mean(dyg * xv, axis=-1, keepdims=True)
        part = jnp.sum(dy * xr, axis=0, keepdims=True)

        @pl.when(pl.program_id(0) == 0)
        def _():
            dg_ref[...] = part
            loss_ref[...] = jnp.zeros(loss_ref.shape, F32) + lp

        @pl.when(pl.program_id(0) > 0)
        def _():
            dg_ref[...] += part
            loss_ref[...] += lp

    return pl.pallas_call(
        body, name=name, grid=(L // tm,),
        in_specs=[_row_spec(tm, D), _const_spec(g), _row_spec(tm, D)],
        out_specs=[pl.BlockSpec((8, LANES), lambda i: (0, 0)), _row_spec(tm, D), pl.BlockSpec((1, D), lambda i: (0, 0))],
        out_shape=[jax.ShapeDtypeStruct((8, LANES), F32), jax.ShapeDtypeStruct((L, D), F32),
                   jax.ShapeDtypeStruct((1, D), F32)])(x, g, target)


def _mla_prep(p, qg, kvg, wuq, wukv, tabs, name):
    L = p.shape[0]
    tm = min(512, L)
    scale = (MLA_NOPE + MLA_ROPE) ** -0.5
    tc, ta, tb = tabs

    def body(p_ref, qg_ref, kvg_ref, wuq_ref, wukv_ref, c_ref, a_ref, b_ref, q_ref, k_ref, v_ref, cqn_ref, ckvn_ref):
        c, a, b = c_ref[...], a_ref[...], b_ref[...]
        cq = p_ref[:, 0:Q_LORA].astype(F32)
        ckv = p_ref[:, Q_LORA:Q_LORA + KV_LORA].astype(F32)
        kr = p_ref[:, 640:768].astype(F32)
        cqn = (cq * lax.rsqrt(jnp.mean(cq * cq, axis=-1, keepdims=True) + EPS) * qg_ref[...]).astype(BF16)
        ckvn = (ckv * lax.rsqrt(jnp.mean(ckv * ckv, axis=-1, keepdims=True) + EPS) * kvg_ref[...]).astype(BF16)
        cqn_ref[...] = cqn
        ckvn_ref[...] = ckvn
        q = jnp.dot(cqn, wuq_ref[...], preferred_element_type=F32)
        kv = jnp.dot(ckvn, wukv_ref[...], preferred_element_type=F32)
        krr = _rope(kr, c, a, b, MLA_ROPE // 2)
        for h in range(MLA_HEADS):
            sl = slice(h * LANES, (h + 1) * LANES)
            q_ref[:, sl] = (_rope(q[:, sl], c, a, b, MLA_ROPE // 2) * (scale * LOG2E)).astype(BF16)
            k_ref[:, sl] = (kv[:, sl] + krr).astype(BF16)
        v_ref[...] = kv[:, 1024:1536].astype(BF16)

    return pl.pallas_call(
        body, name=name, grid=(L // tm,),
        in_specs=[_row_spec(tm, P_MLA, 0), _const_spec(qg), _const_spec(kvg), _const_spec(wuq), _const_spec(wukv),
                  _row_spec(tm, LANES), _row_spec(tm, LANES), _row_spec(tm, LANES)],
        out_specs=[_row_spec(tm, 1024), _row_spec(tm, 1024), _row_spec(tm, HEAD_W), _row_spec(tm, Q_LORA),
                   _row_spec(tm, KV_LORA)],
        out_shape=[jax.ShapeDtypeStruct((L, 1024), BF16), jax.ShapeDtypeStruct((L, 1024), BF16),
                   jax.ShapeDtypeStruct((L, HEAD_W), BF16), jax.ShapeDtypeStruct((L, Q_LORA), BF16),
                   jax.ShapeDtypeStruct((L, KV_LORA), BF16)],
        compiler_params=_cparams(VMEM_BIG_MB))(p, qg, kvg, wuq, wukv, tc, ta, tb)


def _mla_prep_bwd(dq, dk, dv, p, qg, kvg, wuq_t, wukv_t, tabs_t, name):
    L = p.shape[0]
    tm = min(512, L)
    scale = (MLA_NOPE + MLA_ROPE) ** -0.5
    tc, ta, tb = tabs_t

    def body(dq_ref, dk_ref, dv_ref, p_ref, qg_ref, kvg_ref, wuqt_ref, wukvt_ref, c_ref, a_ref, b_ref,
             dp_ref, dqp_ref, dkv_ref, dqg_ref, dkvg_ref):
        c, a, b = c_ref[...], a_ref[...], b_ref[...]
        dkr = jnp.zeros((tm, LANES), F32)
        for h in range(MLA_HEADS):
            sl = slice(h * LANES, (h + 1) * LANES)
            dqp_ref[:, sl] = (_rope(dq_ref[:, sl].astype(F32), c, a, b, MLA_ROPE // 2) * scale).astype(BF16)
            dkh = dk_ref[:, sl].astype(F32) * LN2
            dkv_ref[:, sl] = dkh.astype(BF16)
            dkr = dkr + dkh
        dkv_ref[:, 1024:1536] = dv_ref[...].astype(BF16)
        lane = lax.broadcasted_iota(jnp.int32, (tm, LANES), 1)
        dkr = jnp.where((lane >= MLA_NOPE) & (lane < MLA_NOPE + MLA_ROPE), _rope(dkr, c, a, b, MLA_ROPE // 2), 0.0)

        d_cqn = jnp.dot(dqp_ref[...], wuqt_ref[...], preferred_element_type=F32)
        d_ckvn = jnp.dot(dkv_ref[...], wukvt_ref[...], preferred_element_type=F32)

        def norm_bwd(xv, gv, dy):
            r = lax.rsqrt(jnp.mean(xv * xv, axis=-1, keepdims=True) + EPS)
            dyg = dy * gv
            dx = r * dyg - xv * (r * r * r) * jnp.mean(dyg * xv, axis=-1, keepdims=True)
            return dx, jnp.sum(dy * xv * r, axis=0, keepdims=True)

        d_cq, dqg = norm_bwd(p_ref[:, 0:Q_LORA].astype(F32), qg_ref[...], d_cqn)
        d_ckv, dkvg = norm_bwd(p_ref[:, Q_LORA:Q_LORA + KV_LORA].astype(F32), kvg_ref[...], d_ckvn)
        dp_ref[:, 0:Q_LORA] = d_cq.astype(BF16)
        dp_ref[:, Q_LORA:Q_LORA + KV_LORA] = d_ckv.astype(BF16)
        dp_ref[:, 640:768] = dkr.astype(BF16)
        dp_ref[:, 768:1024] = jnp.zeros((tm, 256), BF16)

        @pl.when(pl.program_id(0) == 0)
        def _():
            dqg_ref[...] = dqg
            dkvg_ref[...] = dkvg

        @pl.when(pl.program_id(0) > 0)
        def _():
            dqg_ref[...] += dqg
            dkvg_ref[...] += dkvg

    return pl.pallas_call(
        body, name=name, grid=(L // tm,),
        in_specs=[_row_spec(tm, 1024), _row_spec(tm, 1024), _row_spec(tm, HEAD_W), _row_spec(tm, P_MLA, 0),
                  _const_spec(qg), _const_spec(kvg), _const_spec(wuq_t), _const_spec(wukv_t),
                  _row_spec(tm, LANES), _row_spec(tm, LANES), _row_spec(tm, LANES)],
        out_specs=[_row_spec(tm, P_MLA), _row_spec(tm, 1024), _row_spec(tm, 1536),
                   pl.BlockSpec((1, Q_LORA), lambda i: (0, 0)), pl.BlockSpec((1, KV_LORA), lambda i: (0, 0))],
        out_shape=[jax.ShapeDtypeStruct((L, P_MLA), BF16), jax.ShapeDtypeStruct((L, 1024), BF16),
                   jax.ShapeDtypeStruct((L, 1536), BF16), jax.ShapeDtypeStruct((1, Q_LORA), F32),
                   jax.ShapeDtypeStruct((1, KV_LORA), F32)],
        compiler_params=_cparams(VMEM_BIG_MB))(dq, dk, dv, p, qg, kvg, wuq_t, wukv_t, tc, ta, tb)


def _dil_prep(p, tabs, name):
    L = p.shape[0]
    tm = min(512, L)
    tc, ta, tb = tabs
    scale = DIL_HD ** -0.5

    def body(*refs):
        ins, (c_ref, a_ref, b_ref), outs = refs[:9], refs[9:12], refs[12:]
        c, a, b = c_ref[...], a_ref[...], b_ref[...]
        for n in range(9):
            t = n % 3
            for cb in range(HEAD_W // LANES):
                sl = slice(cb * LANES, (cb + 1) * LANES)
                xv = ins[n][:, sl].astype(F32)
                if t == 0:
                    xv = _rope(xv, c, a, b, ROT_DIM // 2) * (scale * LOG2E)
                elif t == 1:
                    xv = _rope(xv, c, a, b, ROT_DIM // 2)
                outs[n][:, sl] = xv.astype(BF16)

    in_specs = [_row_spec(tm, HEAD_W, P_DIL0 // HEAD_W + n) for n in range(9)] + [_row_spec(tm, LANES)] * 3
    return pl.pallas_call(
        body, name=name, grid=(L // tm,), in_specs=in_specs,
        out_specs=[_row_spec(tm, HEAD_W)] * 9,
        out_shape=[jax.ShapeDtypeStruct((L, HEAD_W), BF16)] * 9)(*([p] * 9), tc, ta, tb)


def _dil_prep_bwd(grads, tabs_t, name):
    L = grads[0].shape[0]
    tm = min(512, L)
    tc, ta, tb = tabs_t
    scale = DIL_HD ** -0.5

    def body(*refs):
        ins, (c_ref, a_ref, b_ref), o_ref = refs[:9], refs[9:12], refs[12]
        c, a, b = c_ref[...], a_ref[...], b_ref[...]
        for n in range(9):
            t = n % 3
            for cb in range(HEAD_W // LANES):
                sl = slice(cb * LANES, (cb + 1) * LANES)
                xv = ins[n][:, sl].astype(F32)
                if t == 0:
                    xv = _rope(xv, c, a, b, ROT_DIM // 2) * scale
                elif t == 1:
                    xv = _rope(xv, c, a, b, ROT_DIM // 2) * LN2
                o_ref[:, n * HEAD_W + cb * LANES:n * HEAD_W + (cb + 1) * LANES] = xv.astype(BF16)

    return pl.pallas_call(
        body, name=name, grid=(L // tm,), in_specs=[_row_spec(tm, HEAD_W)] * 9 + [_row_spec(tm, LANES)] * 3,
        out_specs=_row_spec(tm, 9 * HEAD_W), out_shape=jax.ShapeDtypeStruct((L, 9 * HEAD_W), BF16),
        compiler_params=_cparams(VMEM_BIG_MB))(*grads, tc, ta, tb)


def _merge_gate(oa, p, o_g, lse_g, name):
    L = oa.shape[0]
    tm = min(512, L)

    def body(oa_ref, ga_ref, gb_ref, o1, o2, o3, l1, l2, l3, ab_ref, bm_ref, lt_ref):
        la, lb, lc = l1[...], l2[...], l3[...]
        m = jnp.maximum(jnp.maximum(la, lb), lc)
        ea, eb, ec = jnp.exp2(la - m), jnp.exp2(lb - m), jnp.exp2(lc - m)
        den = ea + eb + ec
        bm = (ea * o1[...] + eb * o2[...] + ec * o3[...]) / den
        bm_ref[...] = bm
        lt_ref[...] = _per_head8(m + jnp.log2(den), True)
        ga, gb = ga_ref[...].astype(F32), gb_ref[...].astype(F32)
        ab_ref[:, 0:HEAD_W] = (oa_ref[...] * (ga * _sigmoid(ga))).astype(BF16)
        ab_ref[:, HEAD_W:2 * HEAD_W] = (bm * (gb * _sigmoid(gb))).astype(BF16)

    w = _row_spec(tm, HEAD_W)
    return pl.pallas_call(
        body, name=name, grid=(L // tm,),
        in_specs=[w, _row_spec(tm, HEAD_W, 2), _row_spec(tm, HEAD_W, 3), w, w, w, w, w, w],
        out_specs=[_row_spec(tm, 2 * HEAD_W), w, _row_spec(tm, MLA_HEADS)],
        out_shape=[jax.ShapeDtypeStruct((L, 2 * HEAD_W), BF16), jax.ShapeDtypeStruct((L, HEAD_W), F32),
                   jax.ShapeDtypeStruct((L, MLA_HEADS), F32)])(oa, p, p, *o_g, *lse_g)


def _gate_bwd(dab, p, oa, bm, name):
    L = oa.shape[0]
    tm = min(512, L)

    def body(da_ref, db_ref, ga_ref, gb_ref, oa_ref, bm_ref, doa_ref, dbm_ref, Da_ref, Db_ref, dg_ref):
        def one(d, g, o, do_ref, D_ref, col):
            sg = _sigmoid(g)
            do = d * (g * sg)
            do_ref[...] = do.astype(BF16)
            dg_ref[:, col:col + HEAD_W] = (d * o * (sg * (1.0 + g * (1.0 - sg)))).astype(BF16)
            D_ref[...] = _per_head8(do * o, False)

        one(da_ref[...], ga_ref[...].astype(F32), oa_ref[...], doa_ref, Da_ref, 0)
        one(db_ref[...], gb_ref[...].astype(F32), bm_ref[...], dbm_ref, Db_ref, HEAD_W)

    w = _row_spec(tm, HEAD_W)
    w8 = _row_spec(tm, MLA_HEADS)
    return pl.pallas_call(
        body, name=name, grid=(L // tm,),
        in_specs=[_row_spec(tm, HEAD_W, 0), _row_spec(tm, HEAD_W, 1), _row_spec(tm, HEAD_W, 2),
                  _row_spec(tm, HEAD_W, 3), w, w],
        out_specs=[w, w, w8, w8, _row_spec(tm, 2 * HEAD_W)],
        out_shape=[jax.ShapeDtypeStruct((L, HEAD_W), BF16), jax.ShapeDtypeStruct((L, HEAD_W), BF16),
                   jax.ShapeDtypeStruct((L, MLA_HEADS), F32), jax.ShapeDtypeStruct((L, MLA_HEADS), F32),
                   jax.ShapeDtypeStruct((L, 2 * HEAD_W), BF16)])(dab, dab, p, p, oa, bm)


NT = (((1,), (1,)), ((), ()))
TN = (((0,), (0,)), ((), ()))
NEG = -1e30


MLA_TQ = 512
MLA_TK = 2048
MLA_BWD_TK = 1024
LOG2E = 1.4426950408889634
LN2 = 0.6931471805599453


def _mla_fwd(q, k, v_t, name):
    L = q.shape[0]
    tq, tk = min(MLA_TQ, L), min(MLA_TK, L)
    nq, nk = L // tq, L // tk
    npair = MLA_HEADS // 2

    def body(q_ref, k_ref, vt_ref, o_ref, lse_ref, m0, l0, a0, m1, l1, a1):
        j = pl.program_id(2)
        stats = ((m0, l0, a0), (m1, l1, a1))

        @pl.when(j == 0)
        def _():
            for m_sc, l_sc, acc_sc in stats:
                m_sc[...] = jnp.full(m_sc.shape, NEG, F32)
                l_sc[...] = jnp.zeros(l_sc.shape, F32)
                acc_sc[...] = jnp.zeros(acc_sc.shape, F32)

        s_ts = [lax.dot_general(k_ref[:, hh * LANES:(hh + 1) * LANES], q_ref[:, hh * LANES:(hh + 1) * LANES], NT,
                                preferred_element_type=F32) for hh in range(2)]
        for hh in range(2):
            m_sc, l_sc, acc_sc = stats[hh]
            s_t = s_ts[hh]
            m_prev = m_sc[...]
            m_new = jnp.maximum(m_prev, jnp.max(s_t, axis=0, keepdims=True))
            alpha = jnp.exp2(m_prev - m_new)
            p_t = jnp.exp2(s_t - m_new)
            l_sc[...] = alpha * l_sc[...] + jnp.sum(p_t, axis=0, keepdims=True)
            m_sc[...] = m_new
            pv = jnp.dot(vt_ref[hh * DIL_HD:(hh + 1) * DIL_HD, :], p_t.astype(BF16),
                         preferred_element_type=F32)
            acc_sc[...] = alpha * acc_sc[...] + pv

        @pl.when(j == nk - 1)
        def _():
            o_ref[...] = jnp.concatenate([a0[...] / l0[...], a1[...] / l1[...]], axis=0).T
            lse_ref[...] = jnp.concatenate([m0[...] + jnp.log2(l0[...]), m1[...] + jnp.log2(l1[...])], axis=0)

    stat = [pltpu.VMEM((1, tq), F32), pltpu.VMEM((1, tq), F32), pltpu.VMEM((DIL_HD, tq), F32)]
    return pl.pallas_call(
        body, name=name, grid=(npair, nq, nk),
        in_specs=[pl.BlockSpec((tq, 2 * LANES), lambda pr, i, j: (i, pr)),
                  pl.BlockSpec((tk, 2 * LANES), lambda pr, i, j: (j, pr)),
                  pl.BlockSpec((LANES, tk), lambda pr, i, j: (pr, j))],
        out_specs=[pl.BlockSpec((tq, LANES), lambda pr, i, j: (i, pr)),
                   pl.BlockSpec((None, 2, tq), lambda pr, i, j: (pr, 0, i))],
        out_shape=[jax.ShapeDtypeStruct((L, HEAD_W), F32), jax.ShapeDtypeStruct((npair, 2, L), F32)],
        scratch_shapes=stat + stat, compiler_params=_cparams(VMEM_BIG_MB))(q, k, v_t)


def _mla_bwd(q, k, v, q_t, do, do_t, lse_cols, d_cols, name):
    L = q.shape[0]
    tq, tk = min(512, L), min(MLA_BWD_TK, L)
    nq, nk = L // tq, L // tk
    npair = MLA_HEADS // 2

    def body(q_ref, k_ref, v_ref, qt_ref, do_ref, dot_ref, lse_ref, d_ref, dq_out, dkt_out, dvt_out,
             dq_ref, dkt_ref, dvt_ref):
        j, i = pl.program_id(1), pl.program_id(2)

        @pl.when((j == 0) & (i == 0))
        def _():
            dq_ref[...] = jnp.zeros(dq_ref.shape, F32)

        @pl.when(i == 0)
        def _():
            dkt_ref[...] = jnp.zeros(dkt_ref.shape, F32)
            dvt_ref[...] = jnp.zeros(dvt_ref.shape, F32)

        first = lax.broadcasted_iota(jnp.int32, (tq, LANES), 1) < DIL_HD
        dov = do_ref[...]
        vv = v_ref[...]
        rows = pl.ds(pl.multiple_of(i * tq, tq), tq)
        for hh in range(2):
            sl = slice(hh * LANES, (hh + 1) * LANES)
            hrows = slice(hh * DIL_HD, (hh + 1) * DIL_HD)
            qh, kh = q_ref[:, sl], k_ref[:, sl]
            do_h = jnp.where(first if hh == 0 else ~first, dov, jnp.zeros_like(dov))
            s = lax.dot_general(qh, kh, NT, preferred_element_type=F32)
            p = jnp.exp2(s - lse_ref[:, hh:hh + 1])
            dvt_ref[hrows, :] += jnp.dot(dot_ref[hrows, :], p.astype(BF16), preferred_element_type=F32)
            dp = lax.dot_general(do_h, vv, NT, preferred_element_type=F32)
            ds = (p * (dp - d_ref[:, hh:hh + 1])).astype(BF16)
            dq_ref[rows, sl] += jnp.dot(ds, kh, preferred_element_type=F32)
            dkt_ref[sl, :] += jnp.dot(qt_ref[sl, :], ds, preferred_element_type=F32)

        @pl.when(i == nq - 1)
        def _():
            dkt_out[...] = dkt_ref[...].astype(BF16)
            dvt_out[...] = dvt_ref[...].astype(BF16)

        @pl.when((j == nk - 1) & (i == nq - 1))
        def _():
            dq_out[...] = dq_ref[...].astype(BF16)

    colspec = pl.BlockSpec((None, tq, 2), lambda pr, j, i: (pr, i, 0))
    return pl.pallas_call(
        body, name=name, grid=(npair, nk, nq),
        in_specs=[pl.BlockSpec((tq, 2 * LANES), lambda pr, j, i: (i, pr)),
                  pl.BlockSpec((tk, 2 * LANES), lambda pr, j, i: (j, pr)),
                  pl.BlockSpec((tk, LANES), lambda pr, j, i: (j, pr)),
                  pl.BlockSpec((2 * LANES, tq), lambda pr, j, i: (pr, i)),
                  pl.BlockSpec((tq, LANES), lambda pr, j, i: (i, pr)),
                  pl.BlockSpec((LANES, tq), lambda pr, j, i: (pr, i)),
                  colspec, colspec],
        out_specs=[pl.BlockSpec((L, 2 * LANES), lambda pr, j, i: (0, pr)),
                   pl.BlockSpec((2 * LANES, tk), lambda pr, j, i: (pr, j)),
                   pl.BlockSpec((LANES, tk), lambda pr, j, i: (pr, j))],
        out_shape=[jax.ShapeDtypeStruct((L, 1024), BF16), jax.ShapeDtypeStruct((1024, L), BF16),
                   jax.ShapeDtypeStruct((HEAD_W, L), BF16)],
        scratch_shapes=[pltpu.VMEM((L, 2 * LANES), F32), pltpu.VMEM((2 * LANES, tk), F32),
                        pltpu.VMEM((LANES, tk), F32)],
        compiler_params=_cparams(VMEM_BIG_MB))(q, k, v, q_t, do, do_t, lse_cols, d_cols)


DIL_SQ = 128
DIL_SW = DIL_SQ + 2 * DIL_HALF


def _dil_window(a_sub, ld):
    return pl.multiple_of(jnp.clip(a_sub - DIL_HALF, 0, ld - DIL_SW), DIL_HALF)


def _dil_band_mask(shift, heads):
    kidx = lax.broadcasted_iota(jnp.int32, (DIL_SW, heads * DIL_SQ), 0)
    qidx = lax.broadcasted_iota(jnp.int32, (DIL_SW, heads * DIL_SQ), 1) % DIL_SQ
    return jnp.abs(shift + kidx - qidx) <= DIL_HALF


def _pair_rows(x, first):
    zero = jnp.zeros_like(x)
    return jnp.concatenate([jnp.where(first, x, zero), jnp.where(first, zero, x)], axis=0)


def _dil_fwd(q, kp, vp_t4, name):
    d, ld, _ = q.shape
    assert ld % DIL_SQ == 0
    tq = min(512, ld)
    nq = ld // tq
    npair = HEAD_W // LANES
    nb = (ld + 2 * DIL_HALF) // LANES

    def body(q_ref, k_ref, vt_ref, o_ref, lse_ref):
        i = pl.program_id(2)
        first = lax.broadcasted_iota(jnp.int32, (DIL_SQ, LANES), 1) < DIL_HD
        kidx = lax.broadcasted_iota(jnp.int32, (DIL_SW, DIL_SQ), 0)
        for u in range(tq // DIL_SQ):
            a_sub = pl.multiple_of(i * tq + u * DIL_SQ, DIL_SQ)
            kk = a_sub // LANES
            rows = slice(u * DIL_SQ, (u + 1) * DIL_SQ)
            kwin = k_ref[pl.ds(a_sub, DIL_SW), :]
            valid = _dil_band_mask(-DIL_HALF, 1) & (kidx >= DIL_HALF - a_sub) & (kidx < ld + DIL_HALF - a_sub)
            qv = q_ref[rows, :]
            outs, lses = [], []
            for hh in range(2):
                qh = jnp.where(first if hh == 0 else ~first, qv, jnp.zeros_like(qv))
                s_t = jnp.where(valid, lax.dot_general(kwin, qh, NT, preferred_element_type=F32), NEG)
                m = jnp.max(s_t, axis=0, keepdims=True)
                p32 = jnp.exp2(s_t - m)
                l = jnp.sum(p32, axis=0, keepdims=True)
                p_t = p32.astype(BF16)
                hrows = slice(hh * DIL_HD, (hh + 1) * DIL_HD)
                pv = (jnp.dot(vt_ref[kk, hrows, :], p_t[0:LANES, :], preferred_element_type=F32)
                      + jnp.dot(vt_ref[kk + 1, hrows, :], p_t[LANES:DIL_SW, :], preferred_element_type=F32))
                outs.append(pv / l)
                lses.append(jnp.broadcast_to(m + jnp.log2(l), (DIL_HD, DIL_SQ)))
            o_ref[rows, :] = jnp.concatenate(outs, axis=0).T
            lse_ref[rows, :] = jnp.concatenate(lses, axis=0).T

    blk = pl.BlockSpec((None, tq, LANES), lambda r, pr, i: (r, i, pr))
    full = pl.BlockSpec((None, ld + 2 * DIL_HALF, LANES), lambda r, pr, i: (r, 0, pr))
    vspec = pl.BlockSpec((None, nb, LANES, LANES), lambda r, pr, i: (r, 0, pr, 0))
    return pl.pallas_call(
        body, name=name, grid=(d, npair, nq), in_specs=[blk, full, vspec], out_specs=[blk, blk],
        out_shape=[jax.ShapeDtypeStruct((d, ld, HEAD_W), F32), jax.ShapeDtypeStruct((d, ld, HEAD_W), F32)],
        compiler_params=_cparams(VMEM_BIG_MB))(q, kp, vp_t4)


def _dil_bwd(q, k, v, do, lse_rows, d_rows, name):
    d, ld, _ = q.shape
    assert ld % DIL_SQ == 0 and ld >= DIL_SW
    tq = min(512, ld)
    nq = ld // tq
    npair = HEAD_W // LANES
    span = min(tq + 2 * DIL_HALF, ld)

    def body(q_ref, k_ref, v_ref, do_ref, lse_ref, d_ref, dq_ref, dk_ref, dv_ref, dk_sc, dv_sc, dk_acc, dv_acc):
        i = pl.program_id(2)

        @pl.when(i == 0)
        def _():
            dk_acc[...] = jnp.zeros(dk_acc.shape, F32)
            dv_acc[...] = jnp.zeros(dv_acc.shape, F32)

        dk_sc[...] = jnp.zeros(dk_sc.shape, F32)
        dv_sc[...] = jnp.zeros(dv_sc.shape, F32)
        first = lax.broadcasted_iota(jnp.int32, (DIL_SQ, LANES), 1) < DIL_HD
        base = pl.multiple_of(jnp.clip(i * tq - DIL_HALF, 0, ld - span), DIL_HALF)
        for u in range(tq // DIL_SQ):
            a_sub = i * tq + u * DIL_SQ
            ws = _dil_window(a_sub, ld)
            rows = slice(u * DIL_SQ, (u + 1) * DIL_SQ)
            win = pl.ds(pl.multiple_of(ws - base, DIL_HALF), DIL_SW)
            kwin = k_ref[pl.ds(ws, DIL_SW), :]
            vwin = v_ref[pl.ds(ws, DIL_SW), :]
            q2 = _pair_rows(q_ref[rows, :], first)
            do2 = _pair_rows(do_ref[rows, :], first)
            lse2 = jnp.concatenate([lse_ref[0:1, rows], lse_ref[1:2, rows]], axis=1)
            dd2 = jnp.concatenate([d_ref[0:1, rows], d_ref[1:2, rows]], axis=1)
            s_t = lax.dot_general(kwin, q2, NT, preferred_element_type=F32)
            p_t = jnp.exp2(jnp.where(_dil_band_mask(ws - a_sub, 2), s_t, NEG) - lse2)
            dv_sc[win, :] += jnp.dot(p_t.astype(BF16), do2, preferred_element_type=F32)
            dp_t = lax.dot_general(vwin, do2, NT, preferred_element_type=F32)
            ds_t = (p_t * (dp_t - dd2)).astype(BF16)
            dk_sc[win, :] += jnp.dot(ds_t, q2, preferred_element_type=F32)
            dq2 = lax.dot_general(ds_t, kwin, TN, preferred_element_type=F32)
            dq_ref[rows, :] = jnp.where(first, dq2[0:DIL_SQ, :], dq2[DIL_SQ:2 * DIL_SQ, :]).astype(BF16)
        dk_acc[pl.ds(base, span), :] += dk_sc[...]
        dv_acc[pl.ds(base, span), :] += dv_sc[...]

        @pl.when(i == nq - 1)
        def _():
            dk_ref[...] = dk_acc[...].astype(BF16)
            dv_ref[...] = dv_acc[...].astype(BF16)

    blk = pl.BlockSpec((None, tq, LANES), lambda r, pr, i: (r, i, pr))
    full = pl.BlockSpec((None, ld, LANES), lambda r, pr, i: (r, 0, pr))
    rowspec = pl.BlockSpec((None, None, 2, tq), lambda r, pr, i: (r, pr, 0, i))
    return pl.pallas_call(
        body, name=name, grid=(d, npair, nq), in_specs=[blk, full, full, blk, rowspec, rowspec],
        out_specs=[blk, full, full],
        out_shape=[jax.ShapeDtypeStruct((d, ld, HEAD_W), BF16)] * 3,
        scratch_shapes=[pltpu.VMEM((span, LANES), F32), pltpu.VMEM((span, LANES), F32),
                        pltpu.VMEM((ld, LANES), F32), pltpu.VMEM((ld, LANES), F32)],
        compiler_params=_cparams(VMEM_BIG_MB))(q, k, v, do, lse_rows, d_rows)


TILE_BYTES = 1 << 21


def _row_tile(rows, cols, budget=TILE_BYTES):
    for parts in range(1, rows + 1):
        tr = rows // parts
        if rows % parts == 0 and tr % 8 == 0 and tr * cols * 4 <= budget:
            return tr
    return rows


def _add2(a, b, name, out_dtype):
    n, rows, cols = a.shape
    tr = _row_tile(rows, cols)

    def body(a_ref, b_ref, o_ref):
        o_ref[...] = (a_ref[...] + b_ref[...]).astype(out_dtype)

    spec = pl.BlockSpec((None, tr, cols), lambda t, i: (t, i, 0))
    return pl.pallas_call(body, name=name, grid=(n, rows // tr), in_specs=[spec, spec], out_specs=spec,
                          out_shape=jax.ShapeDtypeStruct(a.shape, out_dtype))(a, b)


def _add4_ordered(a, name):
    _, rows, cols = a.shape
    tr = _row_tile(rows, cols, TILE_BYTES // 4)

    def body(a_ref, o_ref):
        o_ref[...] = ((a_ref[0].astype(F32) + a_ref[1].astype(F32)) + a_ref[2].astype(F32)) + a_ref[3].astype(F32)

    return pl.pallas_call(
        body, name=name, grid=(rows // tr,), in_specs=[pl.BlockSpec((4, tr, cols), lambda i: (0, i, 0))],
        out_specs=pl.BlockSpec((tr, cols), lambda i: (i, 0)),
        out_shape=jax.ShapeDtypeStruct((rows, cols), F32))(a)


def _adamw(w, g, m, v, name):
    rows, cols = w.shape
    tr = _row_tile(rows, cols)
    bc1 = 1.0 - ADAM_B1 ** ADAM_STEP
    bc2 = 1.0 - ADAM_B2 ** ADAM_STEP

    def body(w_ref, g_ref, m_ref, v_ref, d_ref, nm_ref, nv_ref):
        gv = g_ref[...]
        nm = ADAM_B1 * m_ref[...] + (1.0 - ADAM_B1) * gv
        nv = ADAM_B2 * v_ref[...] + (1.0 - ADAM_B2) * (gv * gv)
        d_ref[...] = -ADAM_LR * ((nm / bc1) / (jnp.sqrt(nv / bc2) + ADAM_EPS) + ADAM_WD * w_ref[...])
        nm_ref[...] = nm
        nv_ref[...] = nv

    spec = pl.BlockSpec((tr, cols), lambda i: (i, 0))
    return pl.pallas_call(body, name=name, grid=(rows // tr,), in_specs=[spec] * 4, out_specs=[spec] * 3,
                          out_shape=[jax.ShapeDtypeStruct(w.shape, F32)] * 3,
                          compiler_params=_cparams(VMEM_BIG_MB))(w, g, m, v)


ANY = pl.BlockSpec(memory_space=pl.ANY)


def _place():
    return lax.axis_index("x"), lax.axis_index("y"), lax.axis_index("c")


def _rcopy(send_sems, recv_sems, n, src, dst, to):
    return pltpu.make_async_remote_copy(src_ref=src, dst_ref=dst, send_sem=send_sems.at[n], recv_sem=recv_sems.at[n],
                                        device_id=to, device_id_type=MESH)


def _allgather_weights(shards):
    na = len(shards)

    def body(*refs):
        w_refs, g_refs = refs[:na], refs[na:2 * na]
        send_sems, recv_sems, local_sems = refs[2 * na:]
        x, y, c = _place()
        s = 2 * x + y
        chips = [(1 - x, y), (x, 1 - y), (1 - x, 1 - y)]

        def half(a, shard, h):
            hr = shards[a].shape[0] // 2
            return g_refs[a].at[shard, pl.ds(h * hr, hr), :]

        started = []
        for a in range(na):
            hr = shards[a].shape[0] // 2
            mine = pltpu.make_async_copy(w_refs[a], g_refs[a].at[s], local_sems.at[a])
            mine.start()
            started.append(mine)
        sends = []
        for a in range(na):
            hr = shards[a].shape[0] // 2
            for n, (cx, cy) in enumerate(chips):
                cp = _rcopy(send_sems, recv_sems, 6 * a + n, w_refs[a].at[pl.ds(c * hr, hr), :], half(a, s, c),
                            (cx, cy, c))
                cp.start()
                sends.append(cp)
        for a in range(na):
            for n, (cx, cy) in enumerate(chips):
                sj = 2 * cx + cy
                _rcopy(send_sems, recv_sems, 6 * a + n, half(a, sj, c), half(a, sj, c), (cx, cy, c)).wait_recv()
                fw = _rcopy(send_sems, recv_sems, 6 * a + 3 + n, half(a, sj, c), half(a, sj, c), (x, y, 1 - c))
                fw.start()
                sends.append(fw)
        for a in range(na):
            for n, (cx, cy) in enumerate(chips):
                sj = 2 * cx + cy
                _rcopy(send_sems, recv_sems, 6 * a + 3 + n, half(a, sj, 1 - c), half(a, sj, 1 - c),
                       (x, y, 1 - c)).wait_recv()
        for cp in sends:
            cp.wait_send()
        for mine in started:
            mine.wait()

    return pl.pallas_call(
        body, name="allgather_weights", in_specs=[ANY] * na, out_specs=[ANY] * na,
        out_shape=[jax.ShapeDtypeStruct((N_SHARD,) + t.shape, t.dtype) for t in shards],
        scratch_shapes=[pltpu.SemaphoreType.DMA((6 * na,)), pltpu.SemaphoreType.DMA((6 * na,)),
                        pltpu.SemaphoreType.DMA((na,))])(*shards)


def _sibling_send_halves(gs):
    na = len(gs)

    def body(*refs):
        g_refs, o_refs = refs[:na], refs[na:2 * na]
        send_sems, recv_sems = refs[2 * na:]
        x, y, c = _place()
        cps = []
        for a in range(na):
            for t in range(N_SHARD):
                cp = _rcopy(send_sems, recv_sems, N_SHARD * a + t, g_refs[a].at[t, 1 - c], o_refs[a].at[t],
                            (x, y, 1 - c))
                cp.start()
                cps.append(cp)
        for cp in cps:
            cp.wait()

    return pl.pallas_call(
        body, name="grad_sibling_exchange", in_specs=[ANY] * na, out_specs=[ANY] * na,
        out_shape=[jax.ShapeDtypeStruct((N_SHARD,) + g.shape[2:], g.dtype) for g in gs],
        scratch_shapes=[pltpu.SemaphoreType.DMA((N_SHARD * na,)), pltpu.SemaphoreType.DMA((N_SHARD * na,))])(*gs)


def _chip_scatter(parts):
    na = len(parts)

    def body(*refs):
        a_refs, o_refs = refs[:na], refs[na:2 * na]
        send_sems, recv_sems, local_sems = refs[2 * na:]
        x, y, c = _place()
        s = 2 * x + y
        chips = [(1 - x, y), (x, 1 - y), (1 - x, 1 - y)]
        started, cps = [], []
        for a in range(na):
            mine = pltpu.make_async_copy(a_refs[a].at[s], o_refs[a].at[s], local_sems.at[a])
            mine.start()
            started.append(mine)
            for n, (cx, cy) in enumerate(chips):
                cp = _rcopy(send_sems, recv_sems, 3 * a + n, a_refs[a].at[2 * cx + cy], o_refs[a].at[s], (cx, cy, c))
                cp.start()
                cps.append(cp)
        for a in range(na):
            for n, (cx, cy) in enumerate(chips):
                sj = 2 * cx + cy
                _rcopy(send_sems, recv_sems, 3 * a + n, a_refs[a].at[sj], o_refs[a].at[sj], (cx, cy, c)).wait_recv()
        for cp in cps:
            cp.wait_send()
        for mine in started:
            mine.wait()

    return pl.pallas_call(
        body, name="grad_chip_scatter", in_specs=[ANY] * na, out_specs=[ANY] * na,
        out_shape=[jax.ShapeDtypeStruct(t.shape, t.dtype) for t in parts],
        scratch_shapes=[pltpu.SemaphoreType.DMA((3 * na,)), pltpu.SemaphoreType.DMA((3 * na,)),
                        pltpu.SemaphoreType.DMA((na,))])(*parts)


def _sibling_swap(rs):
    na = len(rs)

    def body(*refs):
        r_refs, o_refs = refs[:na], refs[na:2 * na]
        send_sems, recv_sems = refs[2 * na:]
        x, y, c = _place()
        cps = []
        for a in range(na):
            cp = _rcopy(send_sems, recv_sems, a, r_refs[a], o_refs[a], (x, y, 1 - c))
            cp.start()
            cps.append(cp)
        for cp in cps:
            cp.wait()

    return pl.pallas_call(
        body, name="grad_sibling_swap", in_specs=[ANY] * na, out_specs=[ANY] * na,
        out_shape=[jax.ShapeDtypeStruct(t.shape, t.dtype) for t in rs],
        scratch_shapes=[pltpu.SemaphoreType.DMA((na,)), pltpu.SemaphoreType.DMA((na,))])(*rs)


def _pack_small(norm_g, q_norm_g, kv_norm_g, final_g):
    flat = jnp.concatenate([norm_g.reshape(-1), q_norm_g.reshape(-1), kv_norm_g.reshape(-1), final_g.reshape(-1),
                            jnp.zeros((SMALL_ROWS * LANES - N_SMALL,), F32)])
    return flat.reshape(SMALL_ROWS, LANES)


def _split_small(s):
    s = s.reshape(-1)
    o = 0
    out = []
    for n, shape in ((DEPTH * D_MODEL, (DEPTH, D_MODEL)), (DEPTH * Q_LORA, (DEPTH, Q_LORA)),
                     (DEPTH * KV_LORA, (DEPTH, KV_LORA)), (D_MODEL, (D_MODEL,))):
        out.append(s[o:o + n].reshape(shape))
        o += n
    return out


def _assemble_w_in(sh):
    z = lambda n: jnp.zeros(sh.shape[1:3] + (n,), sh.dtype)
    s0, s1, s2, s3 = sh[0], sh[1], sh[2], sh[3]
    return jnp.concatenate([s0[..., 0:640], z(64), s0[..., 640:672], z(32), z(256), s0[..., 672:1184],
                            s3[..., 1064:1576], s0[..., 1184:1576], s1, s2, s3[..., 0:1064]], axis=-1)


def _split_w_in_grad(parts):
    def shard(s, a, b):
        if s == 0:
            return jnp.concatenate([a[:, 0:640], a[:, 704:736], a[:, 1024:1536], b[:, 0:392]], axis=1)
        if s == 3:
            return jnp.concatenate([b[:, 3544:4608], a[:, 1536:2048]], axis=1)
        return b[:, 392 + (s - 1) * SHARD_COLS_IN:392 + s * SHARD_COLS_IN]

    rows = jnp.concatenate([shard(s, a, b) for s in range(N_SHARD) for a, b in parts], axis=0)
    return rows.reshape(N_SHARD, DEPTH * D_MODEL, SHARD_COLS_IN)


def _col_shards(w):
    dl, r, cc = w.shape
    return w.reshape(dl, r, N_SHARD, cc // N_SHARD).transpose(2, 0, 1, 3).reshape(N_SHARD, dl * r, cc // N_SHARD)


def _from_col_shards(g, rows):
    cc = g.shape[-1]
    return g.reshape(N_SHARD, DEPTH, rows, cc).transpose(1, 2, 0, 3).reshape(DEPTH, rows, N_SHARD * cc)


def _pad_w_in(w):
    z = lambda n: jnp.zeros(w.shape[:-1] + (n,), w.dtype)
    return jnp.concatenate([w[..., 0:640], z(64), w[..., 640:672], z(32), z(256), w[..., 672:1184],
                            w[..., 5792:6304], w[..., 1184:5792]], axis=-1)


def _unpad_w_in(w):
    return jnp.concatenate([w[..., 0:640], w[..., 704:736], w[..., 1024:1536], w[..., 2048:6656],
                            w[..., 1536:2048]], axis=-1)


def _pad_w_uq(w):
    s = w.shape[:-1]
    w = w.reshape(s + (MLA_HEADS, 96))
    return jnp.pad(w, [(0, 0)] * (w.ndim - 1) + [(0, 32)]).reshape(s + (1024,))


def _unpad_w_uq(w):
    s = w.shape[:-1]
    return w.reshape(s + (MLA_HEADS, LANES))[..., :96].reshape(s + (768,))


def _pad_w_ukv(w):
    s = w.shape[:-1]
    w = w.reshape(s + (MLA_HEADS, 128))
    kpart = jnp.pad(w[..., :64], [(0, 0)] * (w.ndim - 1) + [(0, 64)]).reshape(s + (1024,))
    vpart = w[..., 64:].reshape(s + (512,))
    return jnp.concatenate([kpart, vpart], axis=-1)


def _unpad_w_ukv(w):
    s = w.shape[:-1]
    kpart = w[..., :1024].reshape(s + (MLA_HEADS, LANES))[..., :64]
    vpart = w[..., 1024:].reshape(s + (MLA_HEADS, 64))
    return jnp.concatenate([kpart, vpart], axis=-1).reshape(s + (1024,))


def _rope_tables(L, dim, lane_lo, period):
    half = dim // 2
    inv = 1.0 / (ROPE_THETA ** (jnp.arange(0, dim, 2, dtype=F32) / dim))
    ang = jnp.arange(L, dtype=F32)[:, None] * inv[None, :]
    cos, sin = jnp.cos(ang), jnp.sin(ang)
    one = lambda n: jnp.ones((L, n), F32)
    zero = lambda n: jnp.zeros((L, n), F32)
    rest = period - lane_lo - dim
    rep = LANES // period
    c = jnp.tile(jnp.concatenate([one(lane_lo), cos, cos, one(rest)], axis=1), (1, rep))
    a = jnp.tile(jnp.concatenate([zero(lane_lo), -sin, zero(half), zero(rest)], axis=1), (1, rep))
    b = jnp.tile(jnp.concatenate([zero(lane_lo + half), sin, zero(rest)], axis=1), (1, rep))
    return c, a, b


def _to_strided(t, d):
    L, w = t.shape
    return t.reshape(L // d, d, w).transpose(1, 0, 2)


def _from_strided(t):
    d, ld, w = t.shape
    return t.transpose(1, 0, 2).reshape(d * ld, w)


def _head_rows(t):
    return t.T.reshape(MLA_HEADS // 2, 2, t.shape[0])


def _head_rows_strided(t, d):
    s = _to_strided(t, d)
    return s.transpose(0, 2, 1).reshape(d, MLA_HEADS // 2, 2, s.shape[1])

def _local_grads(x, target, norm_g, w_in_p, q_norm_g, kv_norm_g, w_uq_p, w_ukv_p, w_out, final_g):
    L = x.shape[0]
    tabs_m = _rope_tables(L, MLA_ROPE, MLA_NOPE, LANES)
    tabs_d = _rope_tables(L, ROT_DIM, 0, DIL_HD)
    tabs_m_t = (tabs_m[0], -tabs_m[1], -tabs_m[2])
    tabs_d_t = (tabs_d[0], -tabs_d[1], -tabs_d[2])
    w_in_t = jnp.swapaxes(w_in_p, 1, 2)
    w_uq_t = jnp.swapaxes(w_uq_p, 1, 2)
    w_ukv_t = jnp.swapaxes(w_ukv_p, 1, 2)
    w_out_t = jnp.swapaxes(w_out, 1, 2)

    saved = []
    for l in range(DEPTH):
        h = _rms_fwd(x, norm_g[l:l + 1], "rms_fwd")
        p = _mm(h, w_in_p[l], tm=512, tn=3328, tk=1024, out_dtype=BF16, name="in_proj")
        q, k, v, cqn, ckvn = _mla_prep(p, q_norm_g[l:l + 1], kv_norm_g[l:l + 1], w_uq_p[l], w_ukv_p[l], tabs_m,
                                       "mla_prep")
        oa, lse_a = _mla_fwd(q, k, v.T, "mla_fwd")
        dil = _dil_prep(p, tabs_d, "dil_prep")
        dil_s, o_g, lse_g = [], [], []
        for g, (_, dd) in enumerate(DIL_PAIRS):
            qs, ks, vs = (_to_strided(t, dd) for t in dil[3 * g:3 * g + 3])
            pad = ((0, 0), (DIL_HALF, DIL_HALF), (0, 0))
            vp = jnp.pad(vs, pad)
            v_t4 = vp.reshape(dd, vp.shape[1] // LANES, LANES, HEAD_W).transpose(0, 1, 3, 2)
            og, lg = _dil_fwd(qs, jnp.pad(ks, pad), v_t4, "dil_fwd_%d" % dd)
            dil_s.append((qs, ks, vs))
            o_g.append(_from_strided(og))
            lse_g.append(_from_strided(lg))
        ab, bm, lt = _merge_gate(oa, p, o_g, lse_g, "merge_gate")
        x_next = _mm(ab, w_out[l], tm=1024, tn=1024, tk=1024, out_dtype=F32, name="out_proj", add=x)
        saved.append((x, h, p, q, k, v, cqn, ckvn, oa, lse_a, dil_s, bm, lt, ab))
        x = x_next

    loss_b, dx, d_final = _loss_head(x, final_g[None, :], target, "loss_head")
    loss = loss_b[0, 0]

    d_norm, d_qn, d_kvn, d_win, d_wuq, d_wukv, d_wout = [], [], [], [], [], [], []
    for l in reversed(range(DEPTH)):
        x_l, h, p, q, k, v, cqn, ckvn, oa, lse_a, dil_s, bm, lt, ab = saved[l]
        dab = _mm(dx, w_out_t[l], tm=1024, tn=1024, tk=1024, out_dtype=F32, name="out_proj_dgrad")
        d_wout.append(_mm(ab, dx, tm=1024, tn=1024, tk=1024, out_dtype=F32, name="out_proj_wgrad", a_is_kxm=True))
        doa, dbm, D_a, D_b, dgates = _gate_bwd(dab, p, oa, bm, "gate_bwd")
        dq, dk_t, dv_t = _mla_bwd(q, k, v, q.T, doa, doa.T, lse_a.transpose(0, 2, 1),
                                  D_a.reshape(L, MLA_HEADS // 2, 2).transpose(1, 0, 2), "mla_bwd")
        dk, dv = dk_t.T, dv_t.T
        dp_mla, dq_pre, dkv, dqg, dkvg = _mla_prep_bwd(dq, dk, dv, p, q_norm_g[l:l + 1], kv_norm_g[l:l + 1],
                                                       w_uq_t[l], w_ukv_t[l], tabs_m_t, "mla_prep_bwd")
        d_wuq.append(_mm(cqn, dq_pre, tm=Q_LORA, tn=1024, tk=2048, out_dtype=F32, name="w_uq_wgrad", a_is_kxm=True))
        d_wukv.append(_mm(ckvn, dkv, tm=KV_LORA, tn=1536, tk=2048, out_dtype=F32, name="w_ukv_wgrad", a_is_kxm=True))
        dgr = []
        for g, (_, dd) in enumerate(DIL_PAIRS):
            qs, ks, vs = dil_s[g]
            dqs, dks, dvs = _dil_bwd(qs, ks, vs, _to_strided(dbm, dd), _head_rows_strided(lt, dd),
                                     _head_rows_strided(D_b, dd), "dil_bwd_%d" % dd)
            dgr += [_from_strided(dqs), _from_strided(dks), _from_strided(dvs)]
        dp_dil = _dil_prep_bwd(dgr, tabs_d_t, "dil_prep_bwd")
        dp_a = jnp.concatenate([dp_mla, dgates], axis=1)
        dh = _mm(dp_a, w_in_t[l][0:P_DIL0], tm=1024, tn=1024, tk=2048, out_dtype=F32, name="in_proj_dgrad_a")
        dh = _mm(dp_dil, w_in_t[l][P_DIL0:], tm=512, tn=1024, tk=2304, out_dtype=F32, name="in_proj_dgrad_b",
                 add=dh)
        d_win.append((_mm(h, dp_a, tm=512, tn=2048, tk=1024, out_dtype=F32, name="in_proj_wgrad_a", a_is_kxm=True),
                      _mm(h, dp_dil, tm=512, tn=1536, tk=2048, out_dtype=F32, name="in_proj_wgrad_b",
                          a_is_kxm=True)))
        dx, dng = _rms_bwd(dh, x_l, norm_g[l:l + 1], dx, "rms_bwd")
        d_norm.append(dng[0])
        d_qn.append(dqg[0])
        d_kvn.append(dkvg[0])

    rev = lambda xs: jnp.stack(xs[::-1])
    return (loss, dx, rev(d_norm), d_win[::-1], rev(d_qn), rev(d_kvn), rev(d_wuq), rev(d_wukv), rev(d_wout),
            d_final[0])


def kernel(x, norm_g, w_in, q_norm_g, kv_norm_g, w_uq, w_ukv, w_out, final_g, loss_target, m_norm_g, m_w_in, m_q_norm_g, m_kv_norm_g, m_w_uq, m_w_ukv, m_w_out, m_final_g, v_norm_g, v_w_in, v_q_norm_g, v_kv_norm_g, v_w_uq, v_w_ukv, v_w_out, v_final_g):
    c = lax.axis_index("c")

    def families(a_in, a_uq, a_ukv, a_out):
        return [t.reshape(shape) for t, shape in zip((a_in, a_uq, a_ukv, a_out), FAM_SHAPES)]

    g_in, g_uq, g_ukv, g_out = _allgather_weights([t.astype(BF16) for t in families(w_in, w_uq, w_ukv, w_out)])
    w_in_p = _assemble_w_in(g_in.reshape(N_SHARD, DEPTH, D_MODEL, SHARD_COLS_IN))
    w_uq_p = _pad_w_uq(_from_col_shards(g_uq, Q_LORA))
    w_ukv_p = _pad_w_ukv(_from_col_shards(g_ukv, KV_LORA))
    w_out_f = g_out.reshape(N_SHARD, DEPTH, 1024 // N_SHARD, D_MODEL).transpose(1, 0, 2, 3).reshape(DEPTH, 1024, D_MODEL)

    (loss, dx, d_norm, d_win_p, d_qn, d_kvn, d_wuq_p, d_wukv_p, d_wout, d_final) = _local_grads(
        x[0], loss_target[0], norm_g, w_in_p, q_norm_g, kv_norm_g, w_uq_p, w_ukv_p, w_out_f, final_g)
    loss = lax.psum(loss, ("x", "y", "c"))

    small = _pack_small(d_norm, d_qn, d_kvn, d_final)
    grads = [_split_w_in_grad(d_win_p), _col_shards(_unpad_w_uq(d_wuq_p)), _col_shards(_unpad_w_ukv(d_wukv_p)),
             d_wout.reshape(DEPTH, N_SHARD, 1024 // N_SHARD, D_MODEL).transpose(1, 0, 2, 3).reshape(
                 N_SHARD, DEPTH * (1024 // N_SHARD), D_MODEL),
             jnp.broadcast_to(small[None], (N_SHARD, SMALL_ROWS, LANES))]
    halves = [g.reshape(N_SHARD, 2, g.shape[1] // 2, g.shape[2]) for g in grads]
    from_sib = _sibling_send_halves(halves)
    chip_sum = [_add2(lax.dynamic_index_in_dim(h, c, axis=1, keepdims=False), f, "grad_add_pair", BF16)
                for h, f in zip(halves, from_sib)]
    red_half = [_add4_ordered(t, "grad_add_chips") for t in _chip_scatter(chip_sum)]
    other_half = _sibling_swap(red_half)
    gred = []
    for mine, other in zip(red_half, other_half):
        both = jnp.stack([mine, other])
        gred.append(jnp.concatenate([lax.dynamic_index_in_dim(both, c, axis=0, keepdims=False),
                                     lax.dynamic_index_in_dim(both, 1 - c, axis=0, keepdims=False)], axis=0))

    wf = families(w_in, w_uq, w_ukv, w_out) + [_pack_small(norm_g, q_norm_g, kv_norm_g, final_g)]
    mf = families(m_w_in, m_w_uq, m_w_ukv, m_w_out) + [_pack_small(m_norm_g, m_q_norm_g, m_kv_norm_g, m_final_g)]
    vf = families(v_w_in, v_w_uq, v_w_ukv, v_w_out) + [_pack_small(v_norm_g, v_q_norm_g, v_kv_norm_g, v_final_g)]
    upd = [_adamw(w, g, m, v, "adamw") for w, g, m, v in zip(wf, gred, mf, vf)]

    def leaves(fams):
        a_in, a_uq, a_ukv, a_out, s = fams
        s_norm, s_qn, s_kvn, s_final = _split_small(s)
        return [s_norm, a_in.reshape(w_in.shape), s_qn, s_kvn, a_uq.reshape(w_uq.shape), a_ukv.reshape(w_ukv.shape),
                a_out.reshape(w_out.shape), s_final]

    return (loss, dx[None], *leaves(gred), *leaves([u[0] for u in upd]), *leaves([u[1] for u in upd]),
            *leaves([u[2] for u in upd]))
```

```python
import functools

import jax
import jax.numpy as jnp
from jax import lax
from jax.experimental import pallas as pl
from jax.experimental.pallas import tpu as pltpu

F32 = jnp.float32
BF16 = jnp.bfloat16
MESH = pl.DeviceIdType.MESH

D_MODEL = 1024
DEPTH = 4
MLA_HEADS = 8
MLA_NOPE = 64
MLA_ROPE = 32
Q_LORA = 384
KV_LORA = 256
DIL_PAIRS = ((128, 1), (512, 4), (2048, 16))
DIL_HD = 64
DIL_HALF = 64
ROT_DIM = 16
ROPE_THETA = 500000.0
EPS = 1e-6
IN_WIDTH = 6304
N_SHARD = 4

P_WIDTH = 6656
P_MLA = 1024
P_GATE = 1024
P_DIL0 = 2048
LANES = 128
HEAD_W = 512

ADAM_LR = 0.001
ADAM_B1 = 0.9
ADAM_B2 = 0.999
ADAM_EPS = 1e-08
ADAM_WD = 0.01
ADAM_STEP = 10

SHARD_COLS_IN = IN_WIDTH // N_SHARD
FAM_SHAPES = ((DEPTH * D_MODEL, SHARD_COLS_IN), (DEPTH * Q_LORA, 768 // N_SHARD), (DEPTH * KV_LORA, 1024 // N_SHARD),
              (DEPTH * (1024 // N_SHARD), D_MODEL))
N_SMALL = DEPTH * (D_MODEL + Q_LORA + KV_LORA) + D_MODEL
SMALL_ROWS = 64
VMEM_BIG_MB = 48


def _cparams(vmem_mb=None):
    if vmem_mb is None:
        return None
    return pltpu.CompilerParams(vmem_limit_bytes=vmem_mb << 20)


def _sigmoid(x):
    return 1.0 / (1.0 + jnp.exp(-x))


def _rope(x, c, a, b, sh):
    return x * c + pltpu.roll(x, LANES - sh, 1) * a + pltpu.roll(x, sh, 1) * b


def _per_head8(x, pick_first):
    r = lax.broadcasted_iota(jnp.int32, (HEAD_W, MLA_HEADS), 0)
    c = lax.broadcasted_iota(jnp.int32, (HEAD_W, MLA_HEADS), 1)
    sel = (r == c * DIL_HD) if pick_first else (r // DIL_HD == c)
    mat = jnp.where(sel, 1.0, 0.0).astype(BF16)
    out = jnp.zeros((x.shape[0], MLA_HEADS), F32)
    for _ in range(3):
        part = x.astype(BF16)
        out = out + jnp.dot(part, mat, preferred_element_type=F32)
        x = x - part.astype(F32)
    return out


def _mm(a, b, *, tm, tn, tk, out_dtype, name, add=None, a_is_kxm=False):
    K, M = a.shape if a_is_kxm else a.shape[::-1]
    N = b.shape[1]
    tm, tn, tk = min(tm, M), min(tn, N), min(tk, K)
    assert M % tm == 0 and N % tn == 0 and K % tk == 0, (a.shape, b.shape)
    nk = K // tk
    has_add = add is not None

    def body(*refs):
        a_ref, b_ref = refs[0], refs[1]
        add_ref = refs[2] if has_add else None
        o_ref = refs[3] if has_add else refs[2]
        k = pl.program_id(2)
        if a_is_kxm:
            part = lax.dot_general(a_ref[...].astype(BF16), b_ref[...].astype(BF16), (((0,), (0,)), ((), ())),
                                   preferred_element_type=F32)
        else:
            part = jnp.dot(a_ref[...].astype(BF16), b_ref[...].astype(BF16), preferred_element_type=F32)
        if nk == 1:
            o_ref[...] = (part + add_ref[...] if has_add else part).astype(out_dtype)
            return
        acc = refs[-1]

        @pl.when(k == 0)
        def _():
            acc[...] = part

        @pl.when(k > 0)
        def _():
            acc[...] += part

        @pl.when(k == nk - 1)
        def _():
            r = acc[...]
            if has_add:
                r = r + add_ref[...]
            o_ref[...] = r.astype(out_dtype)

    a_spec = pl.BlockSpec((tk, tm), lambda i, j, k: (k, i)) if a_is_kxm else pl.BlockSpec((tm, tk), lambda i, j, k: (i, k))
    in_specs = [a_spec, pl.BlockSpec((tk, tn), lambda i, j, k: (k, j))]
    args = [a, b]
    if has_add:
        in_specs.append(pl.BlockSpec((tm, tn), lambda i, j, k: (i, j)))
        args.append(add)
    return pl.pallas_call(
        body, name=name, grid=(M // tm, N // tn, nk), in_specs=in_specs,
        out_specs=pl.BlockSpec((tm, tn), lambda i, j, k: (i, j)),
        out_shape=jax.ShapeDtypeStruct((M, N), out_dtype),
        scratch_shapes=[pltpu.VMEM((tm, tn), F32)] if nk > 1 else [],
        compiler_params=_cparams(VMEM_BIG_MB))(*args)


def _row_spec(tm, w, cb=0):
    return pl.BlockSpec((tm, w), lambda i: (i, cb))


def _const_spec(arr):
    nd = arr.ndim
    return pl.BlockSpec(arr.shape, lambda i: (0,) * nd)


def _rms_fwd(x, g, name):
    L, D = x.shape
    tm = min(512, L)

    def body(x_ref, g_ref, o_ref):
        xv = x_ref[...]
        r = lax.rsqrt(jnp.mean(xv * xv, axis=-1, keepdims=True) + EPS)
        o_ref[...] = (xv * r * g_ref[...]).astype(BF16)

    return pl.pallas_call(
        body, name=name, grid=(L // tm,), in_specs=[_row_spec(tm, D), _const_spec(g)],
        out_specs=_row_spec(tm, D), out_shape=jax.ShapeDtypeStruct((L, D), BF16))(x, g)


def _rms_bwd(dh, x, g, dres, name):
    L, D = x.shape
    tm = min(512, L)

    def body(dh_ref, x_ref, g_ref, dres_ref, dx_ref, dg_ref):
        xv = x_ref[...]
        dy = dh_ref[...]
        r = lax.rsqrt(jnp.mean(xv * xv, axis=-1, keepdims=True) + EPS)
        dyg = dy * g_ref[...]
        dx_ref[...] = dres_ref[...] + r * dyg - xv * (r * r * r) * jnp.mean(dyg * xv, axis=-1, keepdims=True)
        part = jnp.sum(dy * xv * r, axis=0, keepdims=True)

        @pl.when(pl.program_id(0) == 0)
        def _():
            dg_ref[...] = part

        @pl.when(pl.program_id(0) > 0)
        def _():
            dg_ref[...] += part

    return pl.pallas_call(
        body, name=name, grid=(L // tm,),
        in_specs=[_row_spec(tm, D), _row_spec(tm, D), _const_spec(g), _row_spec(tm, D)],
        out_specs=[_row_spec(tm, D), pl.BlockSpec((1, D), lambda i: (0, 0))],
        out_shape=[jax.ShapeDtypeStruct((L, D), F32), jax.ShapeDtypeStruct((1, D), F32)])(dh, x, g, dres)


def _loss_head(x, g, target, name):
    L, D = x.shape
    tm = min(512, L)

    def body(x_ref, g_ref, t_ref, loss_ref, dx_ref, dg_ref):
        xv = x_ref[...]
        gv = g_ref[...]
        r = lax.rsqrt(jnp.mean(xv * xv, axis=-1, keepdims=True) + EPS)
        xr = xv * r
        err = xr * gv - t_ref[...]
        lp = 0.5 * jnp.sum(jnp.mean(err * err, axis=-1, keepdims=True))
        dy = err * (1.0 / D)
        dyg = dy * gv
        dx_ref[...] = r * dyg - xv * (r * r * r) * jnp.mean(dyg * xv, axis=-1, keepdims=True)
        part = jnp.sum(dy * xr, axis=0, keepdims=True)

        @pl.when(pl.program_id(0) == 0)
        def _():
            dg_ref[...] = part
            loss_ref[...] = jnp.zeros(loss_ref.shape, F32) + lp

        @pl.when(pl.program_id(0) > 0)
        def _():
            dg_ref[...] += part
            loss_ref[...] += lp

    return pl.pallas_call(
        body, name=name, grid=(L // tm,),
        in_specs=[_row_spec(tm, D), _const_spec(g), _row_spec(tm, D)],
        out_specs=[pl.BlockSpec((8, LANES), lambda i: (0, 0)), _row_spec(tm, D), pl.BlockSpec((1, D), lambda i: (0, 0))],
        out_shape=[jax.ShapeDtypeStruct((8, LANES), F32), jax.ShapeDtypeStruct((L, D), F32),
                   jax.ShapeDtypeStruct((1, D), F32)])(x, g, target)


def _mla_prep(p, qg, kvg, wuq, wukv, tabs, name):
    L = p.shape[0]
    tm = min(512, L)
    scale = (MLA_NOPE + MLA_ROPE) ** -0.5
    tc, ta, tb = tabs

    def body(p_ref, qg_ref, kvg_ref, wuq_ref, wukv_ref, c_ref, a_ref, b_ref, q_ref, k_ref, v_ref, cqn_ref, ckvn_ref):
        c, a, b = c_ref[...], a_ref[...], b_ref[...]
        cq = p_ref[:, 0:Q_LORA].astype(F32)
        ckv = p_ref[:, Q_LORA:Q_LORA + KV_LORA].astype(F32)
        kr = p_ref[:, 640:768].astype(F32)
        cqn = (cq * lax.rsqrt(jnp.mean(cq * cq, axis=-1, keepdims=True) + EPS) * qg_ref[...]).astype(BF16)
        ckvn = (ckv * lax.rsqrt(jnp.mean(ckv * ckv, axis=-1, keepdims=True) + EPS) * kvg_ref[...]).astype(BF16)
        cqn_ref[...] = cqn
        ckvn_ref[...] = ckvn
        q = jnp.dot(cqn, wuq_ref[...], preferred_element_type=F32)
        kv = jnp.dot(ckvn, wukv_ref[...], preferred_element_type=F32)
        krr = _rope(kr, c, a, b, MLA_ROPE // 2)
        for h in range(MLA_HEADS):
            sl = slice(h * LANES, (h + 1) * LANES)
            q_ref[:, sl] = (_rope(q[:, sl], c, a, b, MLA_ROPE // 2) * (scale * LOG2E)).astype(BF16)
            k_ref[:, sl] = (kv[:, sl] + krr).astype(BF16)
        v_ref[...] = kv[:, 1024:1536].astype(BF16)

    return pl.pallas_call(
        body, name=name, grid=(L // tm,),
        in_specs=[_row_spec(tm, P_MLA, 0), _const_spec(qg), _const_spec(kvg), _const_spec(wuq), _const_spec(wukv),
                  _row_spec(tm, LANES), _row_spec(tm, LANES), _row_spec(tm, LANES)],
        out_specs=[_row_spec(tm, 1024), _row_spec(tm, 1024), _row_spec(tm, HEAD_W), _row_spec(tm, Q_LORA),
                   _row_spec(tm, KV_LORA)],
        out_shape=[jax.ShapeDtypeStruct((L, 1024), BF16), jax.ShapeDtypeStruct((L, 1024), BF16),
                   jax.ShapeDtypeStruct((L, HEAD_W), BF16), jax.ShapeDtypeStruct((L, Q_LORA), BF16),
                   jax.ShapeDtypeStruct((L, KV_LORA), BF16)],
        compiler_params=_cparams(VMEM_BIG_MB))(p, qg, kvg, wuq, wukv, tc, ta, tb)


def _mla_prep_bwd(dq, dk, dv, p, qg, kvg, wuq_t, wukv_t, tabs_t, name):
    L = p.shape[0]
    tm = min(512, L)
    scale = (MLA_NOPE + MLA_ROPE) ** -0.5
    tc, ta, tb = tabs_t

    def body(dq_ref, dk_ref, dv_ref, p_ref, qg_ref, kvg_ref, wuqt_ref, wukvt_ref, c_ref, a_ref, b_ref,
             dp_ref, dqp_ref, dkv_ref, dqg_ref, dkvg_ref):
        c, a, b = c_ref[...], a_ref[...], b_ref[...]
        dkr = jnp.zeros((tm, LANES), F32)
        for h in range(MLA_HEADS):
            sl = slice(h * LANES, (h + 1) * LANES)
            dqp_ref[:, sl] = (_rope(dq_ref[:, sl].astype(F32), c, a, b, MLA_ROPE // 2) * scale).astype(BF16)
            dkh = dk_ref[:, sl].astype(F32) * LN2
            dkv_ref[:, sl] = dkh.astype(BF16)
            dkr = dkr + dkh
        dkv_ref[:, 1024:1536] = dv_ref[...].astype(BF16)
        lane = lax.broadcasted_iota(jnp.int32, (tm, LANES), 1)
        dkr = jnp.where((lane >= MLA_NOPE) & (lane < MLA_NOPE + MLA_ROPE), _rope(dkr, c, a, b, MLA_ROPE // 2), 0.0)

        d_cqn = jnp.dot(dqp_ref[...], wuqt_ref[...], preferred_element_type=F32)
        d_ckvn = jnp.dot(dkv_ref[...], wukvt_ref[...], preferred_element_type=F32)

        def norm_bwd(xv, gv, dy):
            r = lax.rsqrt(jnp.mean(xv * xv, axis=-1, keepdims=True) + EPS)
            dyg = dy * gv
            dx = r * dyg - xv * (r * r * r) * jnp.mean(dyg * xv, axis=-1, keepdims=True)
            return dx, jnp.sum(dy * xv * r, axis=0, keepdims=True)

        d_cq, dqg = norm_bwd(p_ref[:, 0:Q_LORA].astype(F32), qg_ref[...], d_cqn)
        d_ckv, dkvg = norm_bwd(p_ref[:, Q_LORA:Q_LORA + KV_LORA].astype(F32), kvg_ref[...], d_ckvn)
        dp_ref[:, 0:Q_LORA] = d_cq.astype(BF16)
        dp_ref[:, Q_LORA:Q_LORA + KV_LORA] = d_ckv.astype(BF16)
        dp_ref[:, 640:768] = dkr.astype(BF16)
        dp_ref[:, 768:1024] = jnp.zeros((tm, 256), BF16)

        @pl.when(pl.program_id(0) == 0)
        def _():
            dqg_ref[...] = dqg
            dkvg_ref[...] = dkvg

        @pl.when(pl.program_id(0) > 0)
        def _():
            dqg_ref[...] += dqg
            dkvg_ref[...] += dkvg

    return pl.pallas_call(
        body, name=name, grid=(L // tm,),
        in_specs=[_row_spec(tm, 1024), _row_spec(tm, 1024), _row_spec(tm, HEAD_W), _row_spec(tm, P_MLA, 0),
                  _const_spec(qg), _const_spec(kvg), _const_spec(wuq_t), _const_spec(wukv_t),
                  _row_spec(tm, LANES), _row_spec(tm, LANES), _row_spec(tm, LANES)],
        out_specs=[_row_spec(tm, P_MLA), _row_spec(tm, 1024), _row_spec(tm, 1536),
                   pl.BlockSpec((1, Q_LORA), lambda i: (0, 0)), pl.BlockSpec((1, KV_LORA), lambda i: (0, 0))],
        out_shape=[jax.ShapeDtypeStruct((L, P_MLA), BF16), jax.ShapeDtypeStruct((L, 1024), BF16),
                   jax.ShapeDtypeStruct((L, 1536), BF16), jax.ShapeDtypeStruct((1, Q_LORA), F32),
                   jax.ShapeDtypeStruct((1, KV_LORA), F32)],
        compiler_params=_cparams(VMEM_BIG_MB))(dq, dk, dv, p, qg, kvg, wuq_t, wukv_t, tc, ta, tb)


def _dil_prep(p, tabs, name):
    L = p.shape[0]
    tm = min(512, L)
    tc, ta, tb = tabs
    scale = DIL_HD ** -0.5

    def body(*refs):
        ins, (c_ref, a_ref, b_ref), outs = refs[:9], refs[9:12], refs[12:]
        c, a, b = c_ref[...], a_ref[...], b_ref[...]
        for n in range(9):
            t = n % 3
            for cb in range(HEAD_W // LANES):
                sl = slice(cb * LANES, (cb + 1) * LANES)
                xv = ins[n][:, sl].astype(F32)
                if t == 0:
                    xv = _rope(xv, c, a, b, ROT_DIM // 2) * (scale * LOG2E)
                elif t == 1:
                    xv = _rope(xv, c, a, b, ROT_DIM // 2)
                outs[n][:, sl] = xv.astype(BF16)

    in_specs = [_row_spec(tm, HEAD_W, P_DIL0 // HEAD_W + n) for n in range(9)] + [_row_spec(tm, LANES)] * 3
    return pl.pallas_call(
        body, name=name, grid=(L // tm,), in_specs=in_specs,
        out_specs=[_row_spec(tm, HEAD_W)] * 9,
        out_shape=[jax.ShapeDtypeStruct((L, HEAD_W), BF16)] * 9)(*([p] * 9), tc, ta, tb)


def _dil_prep_bwd(grads, tabs_t, name):
    L = grads[0].shape[0]
    tm = min(512, L)
    tc, ta, tb = tabs_t
    scale = DIL_HD ** -0.5

    def body(*refs):
        ins, (c_ref, a_ref, b_ref), o_ref = refs[:9], refs[9:12], refs[12]
        c, a, b = c_ref[...], a_ref[...], b_ref[...]
        for n in range(9):
            t = n % 3
            for cb in range(HEAD_W // LANES):
                sl = slice(cb * LANES, (cb + 1) * LANES)
                xv = ins[n][:, sl].astype(F32)
                if t == 0:
                    xv = _rope(xv, c, a, b, ROT_DIM // 2) * scale
                elif t == 1:
                    xv = _rope(xv, c, a, b, ROT_DIM // 2) * LN2
                o_ref[:, n * HEAD_W + cb * LANES:n * HEAD_W + (cb + 1) * LANES] = xv.astype(BF16)

    return pl.pallas_call(
        body, name=name, grid=(L // tm,), in_specs=[_row_spec(tm, HEAD_W)] * 9 + [_row_spec(tm, LANES)] * 3,
        out_specs=_row_spec(tm, 9 * HEAD_W), out_shape=jax.ShapeDtypeStruct((L, 9 * HEAD_W), BF16),
        compiler_params=_cparams(VMEM_BIG_MB))(*grads, tc, ta, tb)


def _merge_gate(oa, p, o_g, lse_g, name):
    L = oa.shape[0]
    tm = min(512, L)

    def body(oa_ref, ga_ref, gb_ref, o1, o2, o3, l1, l2, l3, ab_ref, bm_ref, lt_ref):
        la, lb, lc = l1[...], l2[...], l3[...]
        m = jnp.maximum(jnp.maximum(la, lb), lc)
        ea, eb, ec = jnp.exp2(la - m), jnp.exp2(lb - m), jnp.exp2(lc - m)
        den = ea + eb + ec
        bm = (ea * o1[...] + eb * o2[...] + ec * o3[...]) / den
        bm_ref[...] = bm
        lt_ref[...] = _per_head8(m + jnp.log2(den), True)
        ga, gb = ga_ref[...].astype(F32), gb_ref[...].astype(F32)
        ab_ref[:, 0:HEAD_W] = (oa_ref[...] * (ga * _sigmoid(ga))).astype(BF16)
        ab_ref[:, HEAD_W:2 * HEAD_W] = (bm * (gb * _sigmoid(gb))).astype(BF16)

    w = _row_spec(tm, HEAD_W)
    return pl.pallas_call(
        body, name=name, grid=(L // tm,),
        in_specs=[w, _row_spec(tm, HEAD_W, 2), _row_spec(tm, HEAD_W, 3), w, w, w, w, w, w],
        out_specs=[_row_spec(tm, 2 * HEAD_W), w, _row_spec(tm, MLA_HEADS)],
        out_shape=[jax.ShapeDtypeStruct((L, 2 * HEAD_W), BF16), jax.ShapeDtypeStruct((L, HEAD_W), F32),
                   jax.ShapeDtypeStruct((L, MLA_HEADS), F32)])(oa, p, p, *o_g, *lse_g)


def _gate_bwd(dab, p, oa, bm, name):
    L = oa.shape[0]
    tm = min(512, L)

    def body(da_ref, db_ref, ga_ref, gb_ref, oa_ref, bm_ref, doa_ref, dbm_ref, Da_ref, Db_ref, dg_ref):
        def one(d, g, o, do_ref, D_ref, col):
            sg = _sigmoid(g)
            do = d * (g * sg)
            do_ref[...] = do.astype(BF16)
            dg_ref[:, col:col + HEAD_W] = (d * o * (sg * (1.0 + g * (1.0 - sg)))).astype(BF16)
            D_ref[...] = _per_head8(do * o, False)

        one(da_ref[...], ga_ref[...].astype(F32), oa_ref[...], doa_ref, Da_ref, 0)
        one(db_ref[...], gb_ref[...].astype(F32), bm_ref[...], dbm_ref, Db_ref, HEAD_W)

    w = _row_spec(tm, HEAD_W)
    w8 = _row_spec(tm, MLA_HEADS)
    return pl.pallas_call(
        body, name=name, grid=(L // tm,),
        in_specs=[_row_spec(tm, HEAD_W, 0), _row_spec(tm, HEAD_W, 1), _row_spec(tm, HEAD_W, 2),
                  _row_spec(tm, HEAD_W, 3), w, w],
        out_specs=[w, w, w8, w8, _row_spec(tm, 2 * HEAD_W)],
        out_shape=[jax.ShapeDtypeStruct((L, HEAD_W), BF16), jax.ShapeDtypeStruct((L, HEAD_W), BF16),
                   jax.ShapeDtypeStruct((L, MLA_HEADS), F32), jax.ShapeDtypeStruct((L, MLA_HEADS), F32),
                   jax.ShapeDtypeStruct((L, 2 * HEAD_W), BF16)])(dab, dab, p, p, oa, bm)


NT = (((1,), (1,)), ((), ()))
TN = (((0,), (0,)), ((), ()))
NEG = -1e30


MLA_TQ = 512
MLA_TK = 2048
MLA_BWD_TQ = 1024
MLA_BWD_TK = 1024
LOG2E = 1.4426950408889634
LN2 = 0.6931471805599453


def _mla_fwd(q, k, v_t, name):
    L = q.shape[0]
    tq, tk = min(MLA_TQ, L), min(MLA_TK, L)
    nq, nk = L // tq, L // tk
    npair = MLA_HEADS // 2

    def body(q_ref, k_ref, vt_ref, o_ref, lse_ref, m0, l0, a0, m1, l1, a1):
        j = pl.program_id(2)
        stats = ((m0, l0, a0), (m1, l1, a1))

        @pl.when(j == 0)
        def _():
            for m_sc, l_sc, acc_sc in stats:
                m_sc[...] = jnp.full(m_sc.shape, NEG, F32)
                l_sc[...] = jnp.zeros(l_sc.shape, F32)
                acc_sc[...] = jnp.zeros(acc_sc.shape, F32)

        s_ts = [lax.dot_general(k_ref[:, hh * LANES:(hh + 1) * LANES], q_ref[:, hh * LANES:(hh + 1) * LANES], NT,
                                preferred_element_type=F32) for hh in range(2)]
        for hh in range(2):
            m_sc, l_sc, acc_sc = stats[hh]
            s_t = s_ts[hh]
            m_prev = m_sc[...]
            m_new = jnp.maximum(m_prev, jnp.max(s_t, axis=0, keepdims=True))
            alpha = jnp.exp2(m_prev - m_new)
            p_t = jnp.exp2(s_t - m_new)
            l_sc[...] = alpha * l_sc[...] + jnp.sum(p_t, axis=0, keepdims=True)
            m_sc[...] = m_new
            pv = jnp.dot(vt_ref[hh * DIL_HD:(hh + 1) * DIL_HD, :], p_t.astype(BF16),
                         preferred_element_type=F32)
            acc_sc[...] = alpha * acc_sc[...] + pv

        @pl.when(j == nk - 1)
        def _():
            o_ref[...] = jnp.concatenate([a0[...] / l0[...], a1[...] / l1[...]], axis=0).T
            lse_ref[...] = jnp.concatenate([m0[...] + jnp.log2(l0[...]), m1[...] + jnp.log2(l1[...])], axis=0)

    stat = [pltpu.VMEM((1, tq), F32), pltpu.VMEM((1, tq), F32), pltpu.VMEM((DIL_HD, tq), F32)]
    return pl.pallas_call(
        body, name=name, grid=(npair, nq, nk),
        in_specs=[pl.BlockSpec((tq, 2 * LANES), lambda pr, i, j: (i, pr)),
                  pl.BlockSpec((tk, 2 * LANES), lambda pr, i, j: (j, pr)),
                  pl.BlockSpec((LANES, tk), lambda pr, i, j: (pr, j))],
        out_specs=[pl.BlockSpec((tq, LANES), lambda pr, i, j: (i, pr)),
                   pl.BlockSpec((None, 2, tq), lambda pr, i, j: (pr, 0, i))],
        out_shape=[jax.ShapeDtypeStruct((L, HEAD_W), F32), jax.ShapeDtypeStruct((npair, 2, L), F32)],
        scratch_shapes=stat + stat, compiler_params=_cparams(VMEM_BIG_MB))(q, k, v_t)


def _mla_bwd(q, k, v, q_t, do, do_t, lse_cols, d_cols, name):
    L = q.shape[0]
    tq, tk = min(MLA_BWD_TQ, L), min(MLA_BWD_TK, L)
    nq, nk = L // tq, L // tk
    npair = MLA_HEADS // 2

    def body(q_ref, k_ref, v_ref, qt_ref, do_ref, dot_ref, lse_ref, d_ref, dq_out, dkt_out, dvt_out,
             dq_ref, dkt_ref, dvt_ref):
        j, i = pl.program_id(1), pl.program_id(2)

        @pl.when((j == 0) & (i == 0))
        def _():
            dq_ref[...] = jnp.zeros(dq_ref.shape, F32)

        @pl.when(i == 0)
        def _():
            dkt_ref[...] = jnp.zeros(dkt_ref.shape, F32)
            dvt_ref[...] = jnp.zeros(dvt_ref.shape, F32)

        first = lax.broadcasted_iota(jnp.int32, (tq, LANES), 1) < DIL_HD
        dov = do_ref[...]
        vv = v_ref[...]
        rows = pl.ds(pl.multiple_of(i * tq, tq), tq)
        for hh in range(2):
            sl = slice(hh * LANES, (hh + 1) * LANES)
            hrows = slice(hh * DIL_HD, (hh + 1) * DIL_HD)
            qh, kh = q_ref[:, sl], k_ref[:, sl]
            do_h = jnp.where(first if hh == 0 else ~first, dov, jnp.zeros_like(dov))
            s = lax.dot_general(qh, kh, NT, preferred_element_type=F32)
            p = jnp.exp2(s - lse_ref[:, hh:hh + 1])
            dvt_ref[hrows, :] += jnp.dot(dot_ref[hrows, :], p.astype(BF16), preferred_element_type=F32)
            dp = lax.dot_general(do_h, vv, NT, preferred_element_type=F32)
            ds = (p * (dp - d_ref[:, hh:hh + 1])).astype(BF16)
            dq_ref[rows, sl] += jnp.dot(ds, kh, preferred_element_type=F32)
            dkt_ref[sl, :] += jnp.dot(qt_ref[sl, :], ds, preferred_element_type=F32)

        @pl.when(i == nq - 1)
        def _():
            dkt_out[...] = dkt_ref[...].astype(BF16)
            dvt_out[...] = dvt_ref[...].astype(BF16)

        @pl.when((j == nk - 1) & (i == nq - 1))
        def _():
            dq_out[...] = dq_ref[...].astype(BF16)

    colspec = pl.BlockSpec((None, tq, 2), lambda pr, j, i: (pr, i, 0))
    return pl.pallas_call(
        body, name=name, grid=(npair, nk, nq),
        in_specs=[pl.BlockSpec((tq, 2 * LANES), lambda pr, j, i: (i, pr)),
                  pl.BlockSpec((tk, 2 * LANES), lambda pr, j, i: (j, pr)),
                  pl.BlockSpec((tk, LANES), lambda pr, j, i: (j, pr)),
                  pl.BlockSpec((2 * LANES, tq), lambda pr, j, i: (pr, i)),
                  pl.BlockSpec((tq, LANES), lambda pr, j, i: (i, pr)),
                  pl.BlockSpec((LANES, tq), lambda pr, j, i: (pr, i)),
                  colspec, colspec],
        out_specs=[pl.BlockSpec((L, 2 * LANES), lambda pr, j, i: (0, pr)),
                   pl.BlockSpec((2 * LANES, tk), lambda pr, j, i: (pr, j)),
                   pl.BlockSpec((LANES, tk), lambda pr, j, i: (pr, j))],
        out_shape=[jax.ShapeDtypeStruct((L, 1024), BF16), jax.ShapeDtypeStruct((1024, L), BF16),
                   jax.ShapeDtypeStruct((HEAD_W, L), BF16)],
        scratch_shapes=[pltpu.VMEM((L, 2 * LANES), F32), pltpu.VMEM((2 * LANES, tk), F32),
                        pltpu.VMEM((LANES, tk), F32)],
        compiler_params=_cparams(VMEM_BIG_MB))(q, k, v, q_t, do, do_t, lse_cols, d_cols)


DIL_SQ = 128
DIL_SW = DIL_SQ + 2 * DIL_HALF


def _dil_window(a_sub, ld):
    return pl.multiple_of(jnp.clip(a_sub - DIL_HALF, 0, ld - DIL_SW), DIL_HALF)


def _dil_band_mask(shift, heads):
    kidx = lax.broadcasted_iota(jnp.int32, (DIL_SW, heads * DIL_SQ), 0)
    qidx = lax.broadcasted_iota(jnp.int32, (DIL_SW, heads * DIL_SQ), 1) % DIL_SQ
    return jnp.abs(shift + kidx - qidx) <= DIL_HALF


def _pair_rows(x, first):
    zero = jnp.zeros_like(x)
    return jnp.concatenate([jnp.where(first, x, zero), jnp.where(first, zero, x)], axis=0)


def _dil_fwd(q, kp, vp_t4, name):
    d, ld, _ = q.shape
    assert ld % DIL_SQ == 0
    tq = min(512, ld)
    nq = ld // tq
    npair = HEAD_W // LANES
    nb = (ld + 2 * DIL_HALF) // LANES

    def body(q_ref, k_ref, vt_ref, o_ref, lse_ref):
        i = pl.program_id(2)
        first = lax.broadcasted_iota(jnp.int32, (DIL_SQ, LANES), 1) < DIL_HD
        kidx = lax.broadcasted_iota(jnp.int32, (DIL_SW, DIL_SQ), 0)
        for u in range(tq // DIL_SQ):
            a_sub = pl.multiple_of(i * tq + u * DIL_SQ, DIL_SQ)
            kk = a_sub // LANES
            rows = slice(u * DIL_SQ, (u + 1) * DIL_SQ)
            kwin = k_ref[pl.ds(a_sub, DIL_SW), :]
            valid = _dil_band_mask(-DIL_HALF, 1) & (kidx >= DIL_HALF - a_sub) & (kidx < ld + DIL_HALF - a_sub)
            qv = q_ref[rows, :]
            outs, lses = [], []
            for hh in range(2):
                qh = jnp.where(first if hh == 0 else ~first, qv, jnp.zeros_like(qv))
                s_t = jnp.where(valid, lax.dot_general(kwin, qh, NT, preferred_element_type=F32), NEG)
                m = jnp.max(s_t, axis=0, keepdims=True)
                p32 = jnp.exp2(s_t - m)
                l = jnp.sum(p32, axis=0, keepdims=True)
                p_t = p32.astype(BF16)
                hrows = slice(hh * DIL_HD, (hh + 1) * DIL_HD)
                pv = (jnp.dot(vt_ref[kk, hrows, :], p_t[0:LANES, :], preferred_element_type=F32)
                      + jnp.dot(vt_ref[kk + 1, hrows, :], p_t[LANES:DIL_SW, :], preferred_element_type=F32))
                outs.append(pv / l)
                lses.append(jnp.broadcast_to(m + jnp.log2(l), (DIL_HD, DIL_SQ)))
            o_ref[rows, :] = jnp.concatenate(outs, axis=0).T
            lse_ref[rows, :] = jnp.concatenate(lses, axis=0).T

    blk = pl.BlockSpec((None, tq, LANES), lambda r, pr, i: (r, i, pr))
    full = pl.BlockSpec((None, ld + 2 * DIL_HALF, LANES), lambda r, pr, i: (r, 0, pr))
    vspec = pl.BlockSpec((None, nb, LANES, LANES), lambda r, pr, i: (r, 0, pr, 0))
    return pl.pallas_call(
        body, name=name, grid=(d, npair, nq), in_specs=[blk, full, vspec], out_specs=[blk, blk],
        out_shape=[jax.ShapeDtypeStruct((d, ld, HEAD_W), F32), jax.ShapeDtypeStruct((d, ld, HEAD_W), F32)],
        compiler_params=_cparams(VMEM_BIG_MB))(q, kp, vp_t4)


def _dil_bwd(q, k, v, do, lse_rows, d_rows, name):
    d, ld, _ = q.shape
    assert ld % DIL_SQ == 0 and ld >= DIL_SW
    tq = min(512, ld)
    nq = ld // tq
    npair = HEAD_W // LANES
    span = min(tq + 2 * DIL_HALF, ld)

    def body(q_ref, k_ref, v_ref, do_ref, lse_ref, d_ref, dq_ref, dk_ref, dv_ref, dk_sc, dv_sc, dk_acc, dv_acc):
        i = pl.program_id(2)

        @pl.when(i == 0)
        def _():
            dk_acc[...] = jnp.zeros(dk_acc.shape, F32)
            dv_acc[...] = jnp.zeros(dv_acc.shape, F32)

        dk_sc[...] = jnp.zeros(dk_sc.shape, F32)
        dv_sc[...] = jnp.zeros(dv_sc.shape, F32)
        first = lax.broadcasted_iota(jnp.int32, (DIL_SQ, LANES), 1) < DIL_HD
        base = pl.multiple_of(jnp.clip(i * tq - DIL_HALF, 0, ld - span), DIL_HALF)
        for u in range(tq // DIL_SQ):
            a_sub = i * tq + u * DIL_SQ
            ws = _dil_window(a_sub, ld)
            rows = slice(u * DIL_SQ, (u + 1) * DIL_SQ)
            win = pl.ds(pl.multiple_of(ws - base, DIL_HALF), DIL_SW)
            kwin = k_ref[pl.ds(ws, DIL_SW), :]
            vwin = v_ref[pl.ds(ws, DIL_SW), :]
            q2 = _pair_rows(q_ref[rows, :], first)
            do2 = _pair_rows(do_ref[rows, :], first)
            lse2 = jnp.concatenate([lse_ref[0:1, rows], lse_ref[1:2, rows]], axis=1)
            dd2 = jnp.concatenate([d_ref[0:1, rows], d_ref[1:2, rows]], axis=1)
            s_t = lax.dot_general(kwin, q2, NT, preferred_element_type=F32)
            p_t = jnp.exp2(jnp.where(_dil_band_mask(ws - a_sub, 2), s_t, NEG) - lse2)
            dv_sc[win, :] += jnp.dot(p_t.astype(BF16), do2, preferred_element_type=F32)
            dp_t = lax.dot_general(vwin, do2, NT, preferred_element_type=F32)
            ds_t = (p_t * (dp_t - dd2)).astype(BF16)
            dk_sc[win, :] += jnp.dot(ds_t, q2, preferred_element_type=F32)
            dq2 = lax.dot_general(ds_t, kwin, TN, preferred_element_type=F32)
            dq_ref[rows, :] = jnp.where(first, dq2[0:DIL_SQ, :], dq2[DIL_SQ:2 * DIL_SQ, :]).astype(BF16)
        dk_acc[pl.ds(base, span), :] += dk_sc[...]
        dv_acc[pl.ds(base, span), :] += dv_sc[...]

        @pl.when(i == nq - 1)
        def _():
            dk_ref[...] = dk_acc[...].astype(BF16)
            dv_ref[...] = dv_acc[...].astype(BF16)

    blk = pl.BlockSpec((None, tq, LANES), lambda r, pr, i: (r, i, pr))
    full = pl.BlockSpec((None, ld, LANES), lambda r, pr, i: (r, 0, pr))
    rowspec = pl.BlockSpec((None, None, 2, tq), lambda r, pr, i: (r, pr, 0, i))
    return pl.pallas_call(
        body, name=name, grid=(d, npair, nq), in_specs=[blk, full, full, blk, rowspec, rowspec],
        out_specs=[blk, full, full],
        out_shape=[jax.ShapeDtypeStruct((d, ld, HEAD_W), BF16)] * 3,
        scratch_shapes=[pltpu.VMEM((span, LANES), F32), pltpu.VMEM((span, LANES), F32),
                        pltpu.VMEM((ld, LANES), F32), pltpu.VMEM((ld, LANES), F32)],
        compiler_params=_cparams(VMEM_BIG_MB))(q, k, v, do, lse_rows, d_rows)


TILE_BYTES = 1 << 21


def _row_tile(rows, cols, budget=TILE_BYTES):
    for parts in range(1, rows + 1):
        tr = rows // parts
        if rows % parts == 0 and tr % 8 == 0 and tr * cols * 4 <= budget:
            return tr
    return rows


def _add2(a, b, name, out_dtype):
    n, rows, cols = a.shape
    tr = _row_tile(rows, cols)

    def body(a_ref, b_ref, o_ref):
        o_ref[...] = (a_ref[...] + b_ref[...]).astype(out_dtype)

    spec = pl.BlockSpec((None, tr, cols), lambda t, i: (t, i, 0))
    return pl.pallas_call(body, name=name, grid=(n, rows // tr), in_specs=[spec, spec], out_specs=spec,
                          out_shape=jax.ShapeDtypeStruct(a.shape, out_dtype))(a, b)


def _add4_ordered(a, name):
    _, rows, cols = a.shape
    tr = _row_tile(rows, cols, TILE_BYTES // 4)

    def body(a_ref, o_ref):
        o_ref[...] = ((a_ref[0].astype(F32) + a_ref[1].astype(F32)) + a_ref[2].astype(F32)) + a_ref[3].astype(F32)

    return pl.pallas_call(
        body, name=name, grid=(rows // tr,), in_specs=[pl.BlockSpec((4, tr, cols), lambda i: (0, i, 0))],
        out_specs=pl.BlockSpec((tr, cols), lambda i: (i, 0)),
        out_shape=jax.ShapeDtypeStruct((rows, cols), F32))(a)


def _adamw(w, g, m, v, name):
    rows, cols = w.shape
    tr = _row_tile(rows, cols)
    bc1 = 1.0 - ADAM_B1 ** ADAM_STEP
    bc2 = 1.0 - ADAM_B2 ** ADAM_STEP

    def body(w_ref, g_ref, m_ref, v_ref, d_ref, nm_ref, nv_ref):
        gv = g_ref[...]
        nm = ADAM_B1 * m_ref[...] + (1.0 - ADAM_B1) * gv
        nv = ADAM_B2 * v_ref[...] + (1.0 - ADAM_B2) * (gv * gv)
        d_ref[...] = -ADAM_LR * ((nm / bc1) / (jnp.sqrt(nv / bc2) + ADAM_EPS) + ADAM_WD * w_ref[...])
        nm_ref[...] = nm
        nv_ref[...] = nv

    spec = pl.BlockSpec((tr, cols), lambda i: (i, 0))
    return pl.pallas_call(body, name=name, grid=(rows // tr,), in_specs=[spec] * 4, out_specs=[spec] * 3,
                          out_shape=[jax.ShapeDtypeStruct(w.shape, F32)] * 3,
                          compiler_params=_cparams(VMEM_BIG_MB))(w, g, m, v)


ANY = pl.BlockSpec(memory_space=pl.ANY)


def _place():
    return lax.axis_index("x"), lax.axis_index("y"), lax.axis_index("c")


def _rcopy(send_sems, recv_sems, n, src, dst, to):
    return pltpu.make_async_remote_copy(src_ref=src, dst_ref=dst, send_sem=send_sems.at[n], recv_sem=recv_sems.at[n],
                                        device_id=to, device_id_type=MESH)


def _allgather_weights(shards):
    na = len(shards)

    def body(*refs):
        w_refs, g_refs = refs[:na], refs[na:2 * na]
        send_sems, recv_sems, local_sems = refs[2 * na:]
        x, y, c = _place()
        s = 2 * x + y
        chips = [(1 - x, y), (x, 1 - y), (1 - x, 1 - y)]

        def half(a, shard, h):
            hr = shards[a].shape[0] // 2
            return g_refs[a].at[shard, pl.ds(h * hr, hr), :]

        started = []
        for a in range(na):
            hr = shards[a].shape[0] // 2
            mine = pltpu.make_async_copy(w_refs[a], g_refs[a].at[s], local_sems.at[a])
            mine.start()
            started.append(mine)
        sends = []
        for a in range(na):
            hr = shards[a].shape[0] // 2
            for n, (cx, cy) in enumerate(chips):
                cp = _rcopy(send_sems, recv_sems, 6 * a + n, w_refs[a].at[pl.ds(c * hr, hr), :], half(a, s, c),
                            (cx, cy, c))
                cp.start()
                sends.append(cp)
        for a in range(na):
            for n, (cx, cy) in enumerate(chips):
                sj = 2 * cx + cy
                _rcopy(send_sems, recv_sems, 6 * a + n, half(a, sj, c), half(a, sj, c), (cx, cy, c)).wait_recv()
                fw = _rcopy(send_sems, recv_sems, 6 * a + 3 + n, half(a, sj, c), half(a, sj, c), (x, y, 1 - c))
                fw.start()
                sends.append(fw)
        for a in range(na):
            for n, (cx, cy) in enumerate(chips):
                sj = 2 * cx + cy
                _rcopy(send_sems, recv_sems, 6 * a + 3 + n, half(a, sj, 1 - c), half(a, sj, 1 - c),
                       (x, y, 1 - c)).wait_recv()
        for cp in sends:
            cp.wait_send()
        for mine in started:
            mine.wait()

    return pl.pallas_call(
        body, name="allgather_weights", in_specs=[ANY] * na, out_specs=[ANY] * na,
        out_shape=[jax.ShapeDtypeStruct((N_SHARD,) + t.shape, t.dtype) for t in shards],
        scratch_shapes=[pltpu.SemaphoreType.DMA((6 * na,)), pltpu.SemaphoreType.DMA((6 * na,)),
                        pltpu.SemaphoreType.DMA((na,))])(*shards)


def _sibling_send_halves(gs):
    na = len(gs)

    def body(*refs):
        g_refs, o_refs = refs[:na], refs[na:2 * na]
        send_sems, recv_sems = refs[2 * na:]
        x, y, c = _place()
        cps = []
        for a in range(na):
            for t in range(N_SHARD):
                cp = _rcopy(send_sems, recv_sems, N_SHARD * a + t, g_refs[a].at[t, 1 - c], o_refs[a].at[t],
                            (x, y, 1 - c))
                cp.start()
                cps.append(cp)
        for cp in cps:
            cp.wait()

    return pl.pallas_call(
        body, name="grad_sibling_exchange", in_specs=[ANY] * na, out_specs=[ANY] * na,
        out_shape=[jax.ShapeDtypeStruct((N_SHARD,) + g.shape[2:], g.dtype) for g in gs],
        scratch_shapes=[pltpu.SemaphoreType.DMA((N_SHARD * na,)), pltpu.SemaphoreType.DMA((N_SHARD * na,))])(*gs)


def _chip_scatter(parts):
    na = len(parts)

    def body(*refs):
        a_refs, o_refs = refs[:na], refs[na:2 * na]
        send_sems, recv_sems, local_sems = refs[2 * na:]
        x, y, c = _place()
        s = 2 * x + y
        chips = [(1 - x, y), (x, 1 - y), (1 - x, 1 - y)]
        started, cps = [], []
        for a in range(na):
            mine = pltpu.make_async_copy(a_refs[a].at[s], o_refs[a].at[s], local_sems.at[a])
            mine.start()
            started.append(mine)
            for n, (cx, cy) in enumerate(chips):
                cp = _rcopy(send_sems, recv_sems, 3 * a + n, a_refs[a].at[2 * cx + cy], o_refs[a].at[s], (cx, cy, c))
                cp.start()
                cps.append(cp)
        for a in range(na):
            for n, (cx, cy) in enumerate(chips):
                sj = 2 * cx + cy
                _rcopy(send_sems, recv_sems, 3 * a + n, a_refs[a].at[sj], o_refs[a].at[sj], (cx, cy, c)).wait_recv()
        for cp in cps:
            cp.wait_send()
        for mine in started:
            mine.wait()

    return pl.pallas_call(
        body, name="grad_chip_scatter", in_specs=[ANY] * na, out_specs=[ANY] * na,
        out_shape=[jax.ShapeDtypeStruct(t.shape, t.dtype) for t in parts],
        scratch_shapes=[pltpu.SemaphoreType.DMA((3 * na,)), pltpu.SemaphoreType.DMA((3 * na,)),
                        pltpu.SemaphoreType.DMA((na,))])(*parts)


def _sibling_swap(rs):
    na = len(rs)

    def body(*refs):
        r_refs, o_refs = refs[:na], refs[na:2 * na]
        send_sems, recv_sems = refs[2 * na:]
        x, y, c = _place()
        cps = []
        for a in range(na):
            cp = _rcopy(send_sems, recv_sems, a, r_refs[a], o_refs[a], (x, y, 1 - c))
            cp.start()
            cps.append(cp)
        for cp in cps:
            cp.wait()

    return pl.pallas_call(
        body, name="grad_sibling_swap", in_specs=[ANY] * na, out_specs=[ANY] * na,
        out_shape=[jax.ShapeDtypeStruct(t.shape, t.dtype) for t in rs],
        scratch_shapes=[pltpu.SemaphoreType.DMA((na,)), pltpu.SemaphoreType.DMA((na,))])(*rs)


def _pack_small(norm_g, q_norm_g, kv_norm_g, final_g):
    flat = jnp.concatenate([norm_g.reshape(-1), q_norm_g.reshape(-1), kv_norm_g.reshape(-1), final_g.reshape(-1),
                            jnp.zeros((SMALL_ROWS * LANES - N_SMALL,), F32)])
    return flat.reshape(SMALL_ROWS, LANES)


def _split_small(s):
    s = s.reshape(-1)
    o = 0
    out = []
    for n, shape in ((DEPTH * D_MODEL, (DEPTH, D_MODEL)), (DEPTH * Q_LORA, (DEPTH, Q_LORA)),
                     (DEPTH * KV_LORA, (DEPTH, KV_LORA)), (D_MODEL, (D_MODEL,))):
        out.append(s[o:o + n].reshape(shape))
        o += n
    return out


def _assemble_w_in(sh):
    z = lambda n: jnp.zeros(sh.shape[1:3] + (n,), sh.dtype)
    s0, s1, s2, s3 = sh[0], sh[1], sh[2], sh[3]
    return jnp.concatenate([s0[..., 0:640], z(64), s0[..., 640:672], z(32), z(256), s0[..., 672:1184],
                            s3[..., 1064:1576], s0[..., 1184:1576], s1, s2, s3[..., 0:1064]], axis=-1)


def _split_w_in_grad(parts):
    def shard(s, a, b):
        if s == 0:
            return jnp.concatenate([a[:, 0:640], a[:, 704:736], a[:, 1024:1536], b[:, 0:392]], axis=1)
        if s == 3:
            return jnp.concatenate([b[:, 3544:4608], a[:, 1536:2048]], axis=1)
        return b[:, 392 + (s - 1) * SHARD_COLS_IN:392 + s * SHARD_COLS_IN]

    rows = jnp.concatenate([shard(s, a, b) for s in range(N_SHARD) for a, b in parts], axis=0)
    return rows.reshape(N_SHARD, DEPTH * D_MODEL, SHARD_COLS_IN)


def _col_shards(w):
    dl, r, cc = w.shape
    return w.reshape(dl, r, N_SHARD, cc // N_SHARD).transpose(2, 0, 1, 3).reshape(N_SHARD, dl * r, cc // N_SHARD)


def _from_col_shards(g, rows):
    cc = g.shape[-1]
    return g.reshape(N_SHARD, DEPTH, rows, cc).transpose(1, 2, 0, 3).reshape(DEPTH, rows, N_SHARD * cc)


def _pad_w_in(w):
    z = lambda n: jnp.zeros(w.shape[:-1] + (n,), w.dtype)
    return jnp.concatenate([w[..., 0:640], z(64), w[..., 640:672], z(32), z(256), w[..., 672:1184],
                            w[..., 5792:6304], w[..., 1184:5792]], axis=-1)


def _unpad_w_in(w):
    return jnp.concatenate([w[..., 0:640], w[..., 704:736], w[..., 1024:1536], w[..., 2048:6656],
                            w[..., 1536:2048]], axis=-1)


def _pad_w_uq(w):
    s = w.shape[:-1]
    w = w.reshape(s + (MLA_HEADS, 96))
    return jnp.pad(w, [(0, 0)] * (w.ndim - 1) + [(0, 32)]).reshape(s + (1024,))


def _unpad_w_uq(w):
    s = w.shape[:-1]
    return w.reshape(s + (MLA_HEADS, LANES))[..., :96].reshape(s + (768,))


def _pad_w_ukv(w):
    s = w.shape[:-1]
    w = w.reshape(s + (MLA_HEADS, 128))
    kpart = jnp.pad(w[..., :64], [(0, 0)] * (w.ndim - 1) + [(0, 64)]).reshape(s + (1024,))
    vpart = w[..., 64:].reshape(s + (512,))
    return jnp.concatenate([kpart, vpart], axis=-1)


def _unpad_w_ukv(w):
    s = w.shape[:-1]
    kpart = w[..., :1024].reshape(s + (MLA_HEADS, LANES))[..., :64]
    vpart = w[..., 1024:].reshape(s + (MLA_HEADS, 64))
    return jnp.concatenate([kpart, vpart], axis=-1).reshape(s + (1024,))


def _rope_tables(L, dim, lane_lo, period):
    half = dim // 2
    inv = 1.0 / (ROPE_THETA ** (jnp.arange(0, dim, 2, dtype=F32) / dim))
    ang = jnp.arange(L, dtype=F32)[:, None] * inv[None, :]
    cos, sin = jnp.cos(ang), jnp.sin(ang)
    one = lambda n: jnp.ones((L, n), F32)
    zero = lambda n: jnp.zeros((L, n), F32)
    rest = period - lane_lo - dim
    rep = LANES // period
    c = jnp.tile(jnp.concatenate([one(lane_lo), cos, cos, one(rest)], axis=1), (1, rep))
    a = jnp.tile(jnp.concatenate([zero(lane_lo), -sin, zero(half), zero(rest)], axis=1), (1, rep))
    b = jnp.tile(jnp.concatenate([zero(lane_lo + half), sin, zero(rest)], axis=1), (1, rep))
    return c, a, b


def _to_strided(t, d):
    L, w = t.shape
    return t.reshape(L // d, d, w).transpose(1, 0, 2)


def _from_strided(t):
    d, ld, w = t.shape
    return t.transpose(1, 0, 2).reshape(d * ld, w)


def _head_rows(t):
    return t.T.reshape(MLA_HEADS // 2, 2, t.shape[0])


def _head_rows_strided(t, d):
    s = _to_strided(t, d)
    return s.transpose(0, 2, 1).reshape(d, MLA_HEADS // 2, 2, s.shape[1])

def _local_grads(x, target, norm_g, w_in_p, q_norm_g, kv_norm_g, w_uq_p, w_ukv_p, w_out, final_g):
    L = x.shape[0]
    tabs_m = _rope_tables(L, MLA_ROPE, MLA_NOPE, LANES)
    tabs_d = _rope_tables(L, ROT_DIM, 0, DIL_HD)
    tabs_m_t = (tabs_m[0], -tabs_m[1], -tabs_m[2])
    tabs_d_t = (tabs_d[0], -tabs_d[1], -tabs_d[2])
    w_in_t = jnp.swapaxes(w_in_p, 1, 2)
    w_uq_t = jnp.swapaxes(w_uq_p, 1, 2)
    w_ukv_t = jnp.swapaxes(w_ukv_p, 1, 2)
    w_out_t = jnp.swapaxes(w_out, 1, 2)

    saved = []
    for l in range(DEPTH):
        h = _rms_fwd(x, norm_g[l:l + 1], "rms_fwd")
        p = _mm(h, w_in_p[l], tm=1024, tn=3328, tk=1024, out_dtype=BF16, name="in_proj")
        q, k, v, cqn, ckvn = _mla_prep(p, q_norm_g[l:l + 1], kv_norm_g[l:l + 1], w_uq_p[l], w_ukv_p[l], tabs_m,
                                       "mla_prep")
        oa, lse_a = _mla_fwd(q, k, v.T, "mla_fwd")
        dil = _dil_prep(p, tabs_d, "dil_prep")
        dil_s, o_g, lse_g = [], [], []
        for g, (_, dd) in enumerate(DIL_PAIRS):
            qs, ks, vs = (_to_strided(t, dd) for t in dil[3 * g:3 * g + 3])
            pad = ((0, 0), (DIL_HALF, DIL_HALF), (0, 0))
            vp = jnp.pad(vs, pad)
            v_t4 = vp.reshape(dd, vp.shape[1] // LANES, LANES, HEAD_W).transpose(0, 1, 3, 2)
            og, lg = _dil_fwd(qs, jnp.pad(ks, pad), v_t4, "dil_fwd_%d" % dd)
            dil_s.append((qs, ks, vs))
            o_g.append(_from_strided(og))
            lse_g.append(_from_strided(lg))
        ab, bm, lt = _merge_gate(oa, p, o_g, lse_g, "merge_gate")
        x_next = _mm(ab, w_out[l], tm=1024, tn=1024, tk=1024, out_dtype=F32, name="out_proj", add=x)
        saved.append((x, h, p, q, k, v, cqn, ckvn, oa, lse_a, dil_s, bm, lt, ab))
        x = x_next

    loss_b, dx, d_final = _loss_head(x, final_g[None, :], target, "loss_head")
    loss = loss_b[0, 0]

    d_norm, d_qn, d_kvn, d_win, d_wuq, d_wukv, d_wout = [], [], [], [], [], [], []
    for l in reversed(range(DEPTH)):
        x_l, h, p, q, k, v, cqn, ckvn, oa, lse_a, dil_s, bm, lt, ab = saved[l]
        dab = _mm(dx, w_out_t[l], tm=1024, tn=1024, tk=1024, out_dtype=F32, name="out_proj_dgrad")
        d_wout.append(_mm(ab, dx, tm=1024, tn=1024, tk=1024, out_dtype=F32, name="out_proj_wgrad", a_is_kxm=True))
        doa, dbm, D_a, D_b, dgates = _gate_bwd(dab, p, oa, bm, "gate_bwd")
        dq, dk_t, dv_t = _mla_bwd(q, k, v, q.T, doa, doa.T, lse_a.transpose(0, 2, 1),
                                  D_a.reshape(L, MLA_HEADS // 2, 2).transpose(1, 0, 2), "mla_bwd")
        dk, dv = dk_t.T, dv_t.T
        dp_mla, dq_pre, dkv, dqg, dkvg = _mla_prep_bwd(dq, dk, dv, p, q_norm_g[l:l + 1], kv_norm_g[l:l + 1],
                                                       w_uq_t[l], w_ukv_t[l], tabs_m_t, "mla_prep_bwd")
        d_wuq.append(_mm(cqn, dq_pre, tm=Q_LORA, tn=1024, tk=2048, out_dtype=F32, name="w_uq_wgrad", a_is_kxm=True))
        d_wukv.append(_mm(ckvn, dkv, tm=KV_LORA, tn=1536, tk=2048, out_dtype=F32, name="w_ukv_wgrad", a_is_kxm=True))
        dgr = []
        for g, (_, dd) in enumerate(DIL_PAIRS):
            qs, ks, vs = dil_s[g]
            dqs, dks, dvs = _dil_bwd(qs, ks, vs, _to_strided(dbm, dd), _head_rows_strided(lt, dd),
                                     _head_rows_strided(D_b, dd), "dil_bwd_%d" % dd)
            dgr += [_from_strided(dqs), _from_strided(dks), _from_strided(dvs)]
        dp_dil = _dil_prep_bwd(dgr, tabs_d_t, "dil_prep_bwd")
        dp_a = jnp.concatenate([dp_mla, dgates], axis=1)
        dh = _mm(dp_a, w_in_t[l][0:P_DIL0], tm=1024, tn=1024, tk=2048, out_dtype=F32, name="in_proj_dgrad_a")
        dh = _mm(dp_dil, w_in_t[l][P_DIL0:], tm=512, tn=1024, tk=2304, out_dtype=F32, name="in_proj_dgrad_b",
                 add=dh)
        d_win.append((_mm(h, dp_a, tm=512, tn=2048, tk=1024, out_dtype=F32, name="in_proj_wgrad_a", a_is_kxm=True),
                      _mm(h, dp_dil, tm=512, tn=1536, tk=2048, out_dtype=F32, name="in_proj_wgrad_b",
                          a_is_kxm=True)))
        dx, dng = _rms_bwd(dh, x_l, norm_g[l:l + 1], dx, "rms_bwd")
        d_norm.append(dng[0])
        d_qn.append(dqg[0])
        d_kvn.append(dkvg[0])

    rev = lambda xs: jnp.stack(xs[::-1])
    return (loss, dx, rev(d_norm), d_win[::-1], rev(d_qn), rev(d_kvn), rev(d_wuq), rev(d_wukv), rev(d_wout),
            d_final[0])


def kernel(x, norm_g, w_in, q_norm_g, kv_norm_g, w_uq, w_ukv, w_out, final_g, loss_target, m_norm_g, m_w_in, m_q_norm_g, m_kv_norm_g, m_w_uq, m_w_ukv, m_w_out, m_final_g, v_norm_g, v_w_in, v_q_norm_g, v_kv_norm_g, v_w_uq, v_w_ukv, v_w_out, v_final_g):
    c = lax.axis_index("c")

    def families(a_in, a_uq, a_ukv, a_out):
        return [t.reshape(shape) for t, shape in zip((a_in, a_uq, a_ukv, a_out), FAM_SHAPES)]

    g_in, g_uq, g_ukv, g_out = _allgather_weights([t.astype(BF16) for t in families(w_in, w_uq, w_ukv, w_out)])
    w_in_p = _assemble_w_in(g_in.reshape(N_SHARD, DEPTH, D_MODEL, SHARD_COLS_IN))
    w_uq_p = _pad_w_uq(_from_col_shards(g_uq, Q_LORA))
    w_ukv_p = _pad_w_ukv(_from_col_shards(g_ukv, KV_LORA))
    w_out_f = g_out.reshape(N_SHARD, DEPTH, 1024 // N_SHARD, D_MODEL).transpose(1, 0, 2, 3).reshape(DEPTH, 1024, D_MODEL)

    (loss, dx, d_norm, d_win_p, d_qn, d_kvn, d_wuq_p, d_wukv_p, d_wout, d_final) = _local_grads(
        x[0], loss_target[0], norm_g, w_in_p, q_norm_g, kv_norm_g, w_uq_p, w_ukv_p, w_out_f, final_g)
    loss = lax.psum(loss, ("x", "y", "c"))

    small = _pack_small(d_norm, d_qn, d_kvn, d_final)
    grads = [_split_w_in_grad(d_win_p), _col_shards(_unpad_w_uq(d_wuq_p)), _col_shards(_unpad_w_ukv(d_wukv_p)),
             d_wout.reshape(DEPTH, N_SHARD, 1024 // N_SHARD, D_MODEL).transpose(1, 0, 2, 3).reshape(
                 N_SHARD, DEPTH * (1024 // N_SHARD), D_MODEL),
             jnp.broadcast_to(small[None], (N_SHARD, SMALL_ROWS, LANES))]
    halves = [g.reshape(N_SHARD, 2, g.shape[1] // 2, g.shape[2]) for g in grads]
    from_sib = _sibling_send_halves(halves)
    chip_sum = [_add2(lax.dynamic_index_in_dim(h, c, axis=1, keepdims=False), f, "grad_add_pair", BF16)
                for h, f in zip(halves, from_sib)]
    red_half = [_add4_ordered(t, "grad_add_chips") for t in _chip_scatter(chip_sum)]
    other_half = _sibling_swap(red_half)
    gred = []
    for mine, other in zip(red_half, other_half):
        both = jnp.stack([mine, other])
        gred.append(jnp.concatenate([lax.dynamic_index_in_dim(both, c, axis=0, keepdims=False),
                                     lax.dynamic_index_in_dim(both, 1 - c, axis=0, keepdims=False)], axis=0))

    wf = families(w_in, w_uq, w_ukv, w_out) + [_pack_small(norm_g, q_norm_g, kv_norm_g, final_g)]
    mf = families(m_w_in, m_w_uq, m_w_ukv, m_w_out) + [_pack_small(m_norm_g, m_q_norm_g, m_kv_norm_g, m_final_g)]
    vf = families(v_w_in, v_w_uq, v_w_ukv, v_w_out) + [_pack_small(v_norm_g, v_q_norm_g, v_kv_norm_g, v_final_g)]
    upd = [_adamw(w, g, m, v, "adamw") for w, g, m, v in zip(wf, gred, mf, vf)]

    def leaves(fams):
        a_in, a_uq, a_ukv, a_out, s = fams
        s_norm, s_qn, s_kvn, s_final = _split_small(s)
        return [s_norm, a_in.reshape(w_in.shape), s_qn, s_kvn, a_uq.reshape(w_uq.shape), a_ukv.reshape(w_ukv.shape),
                a_out.reshape(w_out.shape), s_final]

    return (loss, dx[None], *leaves(gred), *leaves([u[0] for u in upd]), *leaves([u[1] for u in upd]),
            *leaves([u[2] for u in upd]))
```

```python
import functools

import jax
import jax.numpy as jnp
from jax import lax
from jax.experimental import pallas as pl
from jax.experimental.pallas import tpu as pltpu

F32 = jnp.float32
BF16 = jnp.bfloat16
MESH = pl.DeviceIdType.MESH

D_MODEL = 1024
DEPTH = 4
MLA_HEADS = 8
MLA_NOPE = 64
MLA_ROPE = 32
Q_LORA = 384
KV_LORA = 256
DIL_PAIRS = ((128, 1), (512, 4), (2048, 16))
DIL_HD = 64
DIL_HALF = 64
ROT_DIM = 16
ROPE_THETA = 500000.0
EPS = 1e-6
IN_WIDTH = 6304
N_SHARD = 4

P_WIDTH = 6656
P_MLA = 1024
P_GATE = 1024
P_DIL0 = 2048
LANES = 128
HEAD_W = 512

ADAM_LR = 0.001
ADAM_B1 = 0.9
ADAM_B2 = 0.999
ADAM_EPS = 1e-08
ADAM_WD = 0.01
ADAM_STEP = 10

SHARD_COLS_IN = IN_WIDTH // N_SHARD
FAM_SHAPES = ((DEPTH * D_MODEL, SHARD_COLS_IN), (DEPTH * Q_LORA, 768 // N_SHARD), (DEPTH * KV_LORA, 1024 // N_SHARD),
              (DEPTH * (1024 // N_SHARD), D_MODEL))
N_SMALL = DEPTH * (D_MODEL + Q_LORA + KV_LORA) + D_MODEL
SMALL_ROWS = 64
VMEM_BIG_MB = 48


def _cparams(vmem_mb=None):
    if vmem_mb is None:
        return None
    return pltpu.CompilerParams(vmem_limit_bytes=vmem_mb << 20)


def _sigmoid(x):
    return 1.0 / (1.0 + jnp.exp(-x))


def _rope(x, c, a, b, sh):
    return x * c + pltpu.roll(x, LANES - sh, 1) * a + pltpu.roll(x, sh, 1) * b


def _per_head8(x, pick_first):
    r = lax.broadcasted_iota(jnp.int32, (HEAD_W, MLA_HEADS), 0)
    c = lax.broadcasted_iota(jnp.int32, (HEAD_W, MLA_HEADS), 1)
    sel = (r == c * DIL_HD) if pick_first else (r // DIL_HD == c)
    mat = jnp.where(sel, 1.0, 0.0).astype(BF16)
    out = jnp.zeros((x.shape[0], MLA_HEADS), F32)
    for _ in range(3):
        part = x.astype(BF16)
        out = out + jnp.dot(part, mat, preferred_element_type=F32)
        x = x - part.astype(F32)
    return out


def _mm(a, b, *, tm, tn, tk, out_dtype, name, add=None, a_is_kxm=False):
    K, M = a.shape if a_is_kxm else a.shape[::-1]
    N = b.shape[1]
    tm, tn, tk = min(tm, M), min(tn, N), min(tk, K)
    assert M % tm == 0 and N % tn == 0 and K % tk == 0, (a.shape, b.shape)
    nk = K // tk
    has_add = add is not None

    def body(*refs):
        a_ref, b_ref = refs[0], refs[1]
        add_ref = refs[2] if has_add else None
        o_ref = refs[3] if has_add else refs[2]
        k = pl.program_id(2)
        if a_is_kxm:
            part = lax.dot_general(a_ref[...].astype(BF16), b_ref[...].astype(BF16), (((0,), (0,)), ((), ())),
                                   preferred_element_type=F32)
        else:
            part = jnp.dot(a_ref[...].astype(BF16), b_ref[...].astype(BF16), preferred_element_type=F32)
        if nk == 1:
            o_ref[...] = (part + add_ref[...] if has_add else part).astype(out_dtype)
            return
        acc = refs[-1]

        @pl.when(k == 0)
        def _():
            acc[...] = part

        @pl.when(k > 0)
        def _():
            acc[...] += part

        @pl.when(k == nk - 1)
        def _():
            r = acc[...]
            if has_add:
                r = r + add_ref[...]
            o_ref[...] = r.astype(out_dtype)

    a_spec = pl.BlockSpec((tk, tm), lambda i, j, k: (k, i)) if a_is_kxm else pl.BlockSpec((tm, tk), lambda i, j, k: (i, k))
    in_specs = [a_spec, pl.BlockSpec((tk, tn), lambda i, j, k: (k, j))]
    args = [a, b]
    if has_add:
        in_specs.append(pl.BlockSpec((tm, tn), lambda i, j, k: (i, j)))
        args.append(add)
    return pl.pallas_call(
        body, name=name, grid=(M // tm, N // tn, nk), in_specs=in_specs,
        out_specs=pl.BlockSpec((tm, tn), lambda i, j, k: (i, j)),
        out_shape=jax.ShapeDtypeStruct((M, N), out_dtype),
        scratch_shapes=[pltpu.VMEM((tm, tn), F32)] if nk > 1 else [],
        compiler_params=_cparams(VMEM_BIG_MB))(*args)


def _row_spec(tm, w, cb=0):
    return pl.BlockSpec((tm, w), lambda i: (i, cb))


def _const_spec(arr):
    nd = arr.ndim
    return pl.BlockSpec(arr.shape, lambda i: (0,) * nd)


def _rms_fwd(x, g, name):
    L, D = x.shape
    tm = min(512, L)

    def body(x_ref, g_ref, o_ref):
        xv = x_ref[...]
        r = lax.rsqrt(jnp.mean(xv * xv, axis=-1, keepdims=True) + EPS)
        o_ref[...] = (xv * r * g_ref[...]).astype(BF16)

    return pl.pallas_call(
        body, name=name, grid=(L // tm,), in_specs=[_row_spec(tm, D), _const_spec(g)],
        out_specs=_row_spec(tm, D), out_shape=jax.ShapeDtypeStruct((L, D), BF16))(x, g)


def _rms_bwd(dh, x, g, dres, name):
    L, D = x.shape
    tm = min(512, L)

    def body(dh_ref, x_ref, g_ref, dres_ref, dx_ref, dg_ref):
        xv = x_ref[...]
        dy = dh_ref[...]
        r = lax.rsqrt(jnp.mean(xv * xv, axis=-1, keepdims=True) + EPS)
        dyg = dy * g_ref[...]
        dx_ref[...] = dres_ref[...] + r * dyg - xv * (r * r * r) * jnp.mean(dyg * xv, axis=-1, keepdims=True)
        part = jnp.sum(dy * xv * r, axis=0, keepdims=True)

        @pl.when(pl.program_id(0) == 0)
        def _():
            dg_ref[...] = part

        @pl.when(pl.program_id(0) > 0)
        def _():
            dg_ref[...] += part

    return pl.pallas_call(
        body, name=name, grid=(L // tm,),
        in_specs=[_row_spec(tm, D), _row_spec(tm, D), _const_spec(g), _row_spec(tm, D)],
        out_specs=[_row_spec(tm, D), pl.BlockSpec((1, D), lambda i: (0, 0))],
        out_shape=[jax.ShapeDtypeStruct((L, D), F32), jax.ShapeDtypeStruct((1, D), F32)])(dh, x, g, dres)


def _loss_head(x, g, target, name):
    L, D = x.shape
    tm = min(512, L)

    def body(x_ref, g_ref, t_ref, loss_ref, dx_ref, dg_ref):
        xv = x_ref[...]
        gv = g_ref[...]
        r = lax.rsqrt(jnp.mean(xv * xv, axis=-1, keepdims=True) + EPS)
        xr = xv * r
        err = xr * gv - t_ref[...]
        lp = 0.5 * jnp.sum(jnp.mean(err * err, axis=-1, keepdims=True))
        dy = err * (1.0 / D)
        dyg = dy * gv
        dx_ref[...] = r * dyg - xv * (r * r * r) * jnp.mean(dyg * xv, axis=-1, keepdims=True)
        part = jnp.sum(dy * xr, axis=0, keepdims=True)

        @pl.when(pl.program_id(0) == 0)
        def _():
            dg_ref[...] = part
            loss_ref[...] = jnp.zeros(loss_ref.shape, F32) + lp

        @pl.when(pl.program_id(0) > 0)
        def _():
            dg_ref[...] += part
            loss_ref[...] += lp

    return pl.pallas_call(
        body, name=name, grid=(L // tm,),
        in_specs=[_row_spec(tm, D), _const_spec(g), _row_spec(tm, D)],
        out_specs=[pl.BlockSpec((8, LANES), lambda i: (0, 0)), _row_spec(tm, D), pl.BlockSpec((1, D), lambda i: (0, 0))],
        out_shape=[jax.ShapeDtypeStruct((8, LANES), F32), jax.ShapeDtypeStruct((L, D), F32),
                   jax.ShapeDtypeStruct((1, D), F32)])(x, g, target)


def _mla_prep(p, qg, kvg, wuq, wukv, tabs, name):
    L = p.shape[0]
    tm = min(512, L)
    scale = (MLA_NOPE + MLA_ROPE) ** -0.5
    tc, ta, tb = tabs

    def body(p_ref, qg_ref, kvg_ref, wuq_ref, wukv_ref, c_ref, a_ref, b_ref, q_ref, k_ref, v_ref, cqn_ref, ckvn_ref):
        c, a, b = c_ref[...], a_ref[...], b_ref[...]
        cq = p_ref[:, 0:Q_LORA].astype(F32)
        ckv = p_ref[:, Q_LORA:Q_LORA + KV_LORA].astype(F32)
        kr = p_ref[:, 640:768].astype(F32)
        cqn = (cq * lax.rsqrt(jnp.mean(cq * cq, axis=-1, keepdims=True) + EPS) * qg_ref[...]).astype(BF16)
        ckvn = (ckv * lax.rsqrt(jnp.mean(ckv * ckv, axis=-1, keepdims=True) + EPS) * kvg_ref[...]).astype(BF16)
        cqn_ref[...] = cqn
        ckvn_ref[...] = ckvn
        q = jnp.dot(cqn, wuq_ref[...], preferred_element_type=F32)
        kv = jnp.dot(ckvn, wukv_ref[...], preferred_element_type=F32)
        krr = _rope(kr, c, a, b, MLA_ROPE // 2)
        for h in range(MLA_HEADS):
            sl = slice(h * LANES, (h + 1) * LANES)
            q_ref[:, sl] = (_rope(q[:, sl], c, a, b, MLA_ROPE // 2) * (scale * LOG2E)).astype(BF16)
            k_ref[:, sl] = (kv[:, sl] + krr).astype(BF16)
        v_ref[...] = kv[:, 1024:1536].astype(BF16)

    return pl.pallas_call(
        body, name=name, grid=(L // tm,),
        in_specs=[_row_spec(tm, P_MLA, 0), _const_spec(qg), _const_spec(kvg), _const_spec(wuq), _const_spec(wukv),
                  _row_spec(tm, LANES), _row_spec(tm, LANES), _row_spec(tm, LANES)],
        out_specs=[_row_spec(tm, 1024), _row_spec(tm, 1024), _row_spec(tm, HEAD_W), _row_spec(tm, Q_LORA),
                   _row_spec(tm, KV_LORA)],
        out_shape=[jax.ShapeDtypeStruct((L, 1024), BF16), jax.ShapeDtypeStruct((L, 1024), BF16),
                   jax.ShapeDtypeStruct((L, HEAD_W), BF16), jax.ShapeDtypeStruct((L, Q_LORA), BF16),
                   jax.ShapeDtypeStruct((L, KV_LORA), BF16)],
        compiler_params=_cparams(VMEM_BIG_MB))(p, qg, kvg, wuq, wukv, tc, ta, tb)


def _mla_prep_bwd(dq, dk, dv, p, qg, kvg, wuq_t, wukv_t, tabs_t, name):
    L = p.shape[0]
    tm = min(512, L)
    scale = (MLA_NOPE + MLA_ROPE) ** -0.5
    tc, ta, tb = tabs_t

    def body(dq_ref, dk_ref, dv_ref, p_ref, qg_ref, kvg_ref, wuqt_ref, wukvt_ref, c_ref, a_ref, b_ref,
             dp_ref, dqp_ref, dkv_ref, dqg_ref, dkvg_ref):
        c, a, b = c_ref[...], a_ref[...], b_ref[...]
        dkr = jnp.zeros((tm, LANES), F32)
        for h in range(MLA_HEADS):
            sl = slice(h * LANES, (h + 1) * LANES)
            dqp_ref[:, sl] = (_rope(dq_ref[:, sl].astype(F32), c, a, b, MLA_ROPE // 2) * scale).astype(BF16)
            dkh = dk_ref[:, sl].astype(F32) * LN2
            dkv_ref[:, sl] = dkh.astype(BF16)
            dkr = dkr + dkh
        dkv_ref[:, 1024:1536] = dv_ref[...].astype(BF16)
        lane = lax.broadcasted_iota(jnp.int32, (tm, LANES), 1)
        dkr = jnp.where((lane >= MLA_NOPE) & (lane < MLA_NOPE + MLA_ROPE), _rope(dkr, c, a, b, MLA_ROPE // 2), 0.0)

        d_cqn = jnp.dot(dqp_ref[...], wuqt_ref[...], preferred_element_type=F32)
        d_ckvn = jnp.dot(dkv_ref[...], wukvt_ref[...], preferred_element_type=F32)

        def norm_bwd(xv, gv, dy):
            r = lax.rsqrt(jnp.mean(xv * xv, axis=-1, keepdims=True) + EPS)
            dyg = dy * gv
            dx = r * dyg - xv * (r * r * r) * jnp.mean(dyg * xv, axis=-1, keepdims=True)
            return dx, jnp.sum(dy * xv * r, axis=0, keepdims=True)

        d_cq, dqg = norm_bwd(p_ref[:, 0:Q_LORA].astype(F32), qg_ref[...], d_cqn)
        d_ckv, dkvg = norm_bwd(p_ref[:, Q_LORA:Q_LORA + KV_LORA].astype(F32), kvg_ref[...], d_ckvn)
        dp_ref[:, 0:Q_LORA] = d_cq.astype(BF16)
        dp_ref[:, Q_LORA:Q_LORA + KV_LORA] = d_ckv.astype(BF16)
        dp_ref[:, 640:768] = dkr.astype(BF16)
        dp_ref[:, 768:1024] = jnp.zeros((tm, 256), BF16)

        @pl.when(pl.program_id(0) == 0)
        def _():
            dqg_ref[...] = dqg
            dkvg_ref[...] = dkvg

        @pl.when(pl.program_id(0) > 0)
        def _():
            dqg_ref[...] += dqg
            dkvg_ref[...] += dkvg

    return pl.pallas_call(
        body, name=name, grid=(L // tm,),
        in_specs=[_row_spec(tm, 1024), _row_spec(tm, 1024), _row_spec(tm, HEAD_W), _row_spec(tm, P_MLA, 0),
                  _const_spec(qg), _const_spec(kvg), _const_spec(wuq_t), _const_spec(wukv_t),
                  _row_spec(tm, LANES), _row_spec(tm, LANES), _row_spec(tm, LANES)],
        out_specs=[_row_spec(tm, P_MLA), _row_spec(tm, 1024), _row_spec(tm, 1536),
                   pl.BlockSpec((1, Q_LORA), lambda i: (0, 0)), pl.BlockSpec((1, KV_LORA), lambda i: (0, 0))],
        out_shape=[jax.ShapeDtypeStruct((L, P_MLA), BF16), jax.ShapeDtypeStruct((L, 1024), BF16),
                   jax.ShapeDtypeStruct((L, 1536), BF16), jax.ShapeDtypeStruct((1, Q_LORA), F32),
                   jax.ShapeDtypeStruct((1, KV_LORA), F32)],
        compiler_params=_cparams(VMEM_BIG_MB))(dq, dk, dv, p, qg, kvg, wuq_t, wukv_t, tc, ta, tb)


def _dil_prep(p, tabs, name):
    L = p.shape[0]
    tm = min(512, L)
    tc, ta, tb = tabs
    scale = DIL_HD ** -0.5

    def body(*refs):
        ins, (c_ref, a_ref, b_ref), outs = refs[:9], refs[9:12], refs[12:]
        c, a, b = c_ref[...], a_ref[...], b_ref[...]
        for n in range(9):
            t = n % 3
            for cb in range(HEAD_W // LANES):
                sl = slice(cb * LANES, (cb + 1) * LANES)
                xv = ins[n][:, sl].astype(F32)
                if t == 0:
                    xv = _rope(xv, c, a, b, ROT_DIM // 2) * (scale * LOG2E)
                elif t == 1:
                    xv = _rope(xv, c, a, b, ROT_DIM // 2)
                outs[n][:, sl] = xv.astype(BF16)

    in_specs = [_row_spec(tm, HEAD_W, P_DIL0 // HEAD_W + n) for n in range(9)] + [_row_spec(tm, LANES)] * 3
    return pl.pallas_call(
        body, name=name, grid=(L // tm,), in_specs=in_specs,
        out_specs=[_row_spec(tm, HEAD_W)] * 9,
        out_shape=[jax.ShapeDtypeStruct((L, HEAD_W), BF16)] * 9)(*([p] * 9), tc, ta, tb)


def _dil_prep_bwd(grads, tabs_t, name):
    L = grads[0].shape[0]
    tm = min(512, L)
    tc, ta, tb = tabs_t
    scale = DIL_HD ** -0.5

    def body(*refs):
        ins, (c_ref, a_ref, b_ref), o_ref = refs[:9], refs[9:12], refs[12]
        c, a, b = c_ref[...], a_ref[...], b_ref[...]
        for n in range(9):
            t = n % 3
            for cb in range(HEAD_W // LANES):
                sl = slice(cb * LANES, (cb + 1) * LANES)
                xv = ins[n][:, sl].astype(F32)
                if t == 0:
                    xv = _rope(xv, c, a, b, ROT_DIM // 2) * scale
                elif t == 1:
                    xv = _rope(xv, c, a, b, ROT_DIM // 2) * LN2
                o_ref[:, n * HEAD_W + cb * LANES:n * HEAD_W + (cb + 1) * LANES] = xv.astype(BF16)

    return pl.pallas_call(
        body, name=name, grid=(L // tm,), in_specs=[_row_spec(tm, HEAD_W)] * 9 + [_row_spec(tm, LANES)] * 3,
        out_specs=_row_spec(tm, 9 * HEAD_W), out_shape=jax.ShapeDtypeStruct((L, 9 * HEAD_W), BF16),
        compiler_params=_cparams(VMEM_BIG_MB))(*grads, tc, ta, tb)


def _merge_gate(oa, p, o_g, lse_g, name):
    L = oa.shape[0]
    tm = min(512, L)

    def body(oa_ref, ga_ref, gb_ref, o1, o2, o3, l1, l2, l3, ab_ref, bm_ref, lt_ref):
        la, lb, lc = l1[...], l2[...], l3[...]
        m = jnp.maximum(jnp.maximum(la, lb), lc)
        ea, eb, ec = jnp.exp2(la - m), jnp.exp2(lb - m), jnp.exp2(lc - m)
        den = ea + eb + ec
        bm = (ea * o1[...] + eb * o2[...] + ec * o3[...]) / den
        bm_ref[...] = bm
        lt_ref[...] = _per_head8(m + jnp.log2(den), True)
        ga, gb = ga_ref[...].astype(F32), gb_ref[...].astype(F32)
        ab_ref[:, 0:HEAD_W] = (oa_ref[...] * (ga * _sigmoid(ga))).astype(BF16)
        ab_ref[:, HEAD_W:2 * HEAD_W] = (bm * (gb * _sigmoid(gb))).astype(BF16)

    w = _row_spec(tm, HEAD_W)
    return pl.pallas_call(
        body, name=name, grid=(L // tm,),
        in_specs=[w, _row_spec(tm, HEAD_W, 2), _row_spec(tm, HEAD_W, 3), w, w, w, w, w, w],
        out_specs=[_row_spec(tm, 2 * HEAD_W), w, _row_spec(tm, MLA_HEADS)],
        out_shape=[jax.ShapeDtypeStruct((L, 2 * HEAD_W), BF16), jax.ShapeDtypeStruct((L, HEAD_W), F32),
                   jax.ShapeDtypeStruct((L, MLA_HEADS), F32)])(oa, p, p, *o_g, *lse_g)


def _gate_bwd(dab, p, oa, bm, name):
    L = oa.shape[0]
    tm = min(512, L)

    def body(da_ref, db_ref, ga_ref, gb_ref, oa_ref, bm_ref, doa_ref, dbm_ref, Da_ref, Db_ref, dg_ref):
        def one(d, g, o, do_ref, D_ref, col):
            sg = _sigmoid(g)
            do = d * (g * sg)
            do_ref[...] = do.astype(BF16)
            dg_ref[:, col:col + HEAD_W] = (d * o * (sg * (1.0 + g * (1.0 - sg)))).astype(BF16)
            D_ref[...] = _per_head8(do * o, False)

        one(da_ref[...], ga_ref[...].astype(F32), oa_ref[...], doa_ref, Da_ref, 0)
        one(db_ref[...], gb_ref[...].astype(F32), bm_ref[...], dbm_ref, Db_ref, HEAD_W)

    w = _row_spec(tm, HEAD_W)
    w8 = _row_spec(tm, MLA_HEADS)
    return pl.pallas_call(
        body, name=name, grid=(L // tm,),
        in_specs=[_row_spec(tm, HEAD_W, 0), _row_spec(tm, HEAD_W, 1), _row_spec(tm, HEAD_W, 2),
                  _row_spec(tm, HEAD_W, 3), w, w],
        out_specs=[w, w, w8, w8, _row_spec(tm, 2 * HEAD_W)],
        out_shape=[jax.ShapeDtypeStruct((L, HEAD_W), BF16), jax.ShapeDtypeStruct((L, HEAD_W), BF16),
                   jax.ShapeDtypeStruct((L, MLA_HEADS), F32), jax.ShapeDtypeStruct((L, MLA_HEADS), F32),
                   jax.ShapeDtypeStruct((L, 2 * HEAD_W), BF16)])(dab, dab, p, p, oa, bm)


NT = (((1,), (1,)), ((), ()))
TN = (((0,), (0,)), ((), ()))
NEG = -1e30


MLA_TQ = 512
MLA_TK = 2048
MLA_BWD_TQ = 1024
MLA_BWD_TK = 1024
LOG2E = 1.4426950408889634
LN2 = 0.6931471805599453


def _mla_fwd(q, k, v_t, name):
    L = q.shape[0]
    tq, tk = min(MLA_TQ, L), min(MLA_TK, L)
    nq, nk = L // tq, L // tk
    npair = MLA_HEADS // 2

    def body(q_ref, k_ref, vt_ref, o_ref, lse_ref, m0, l0, a0, m1, l1, a1):
        j = pl.program_id(2)
        stats = ((m0, l0, a0), (m1, l1, a1))

        @pl.when(j == 0)
        def _():
            for m_sc, l_sc, acc_sc in stats:
                m_sc[...] = jnp.full(m_sc.shape, NEG, F32)
                l_sc[...] = jnp.zeros(l_sc.shape, F32)
                acc_sc[...] = jnp.zeros(acc_sc.shape, F32)

        s_ts = [lax.dot_general(k_ref[:, hh * LANES:(hh + 1) * LANES], q_ref[:, hh * LANES:(hh + 1) * LANES], NT,
                                preferred_element_type=F32) for hh in range(2)]
        for hh in range(2):
            m_sc, l_sc, acc_sc = stats[hh]
            s_t = s_ts[hh]
            m_prev = m_sc[...]
            m_new = jnp.maximum(m_prev, jnp.max(s_t, axis=0, keepdims=True))
            alpha = jnp.exp2(m_prev - m_new)
            p_t = jnp.exp2(s_t - m_new)
            l_sc[...] = alpha * l_sc[...] + jnp.sum(p_t, axis=0, keepdims=True)
            m_sc[...] = m_new
            pv = jnp.dot(vt_ref[hh * DIL_HD:(hh + 1) * DIL_HD, :], p_t.astype(BF16),
                         preferred_element_type=F32)
            acc_sc[...] = alpha * acc_sc[...] + pv

        @pl.when(j == nk - 1)
        def _():
            o_ref[...] = jnp.concatenate([a0[...] / l0[...], a1[...] / l1[...]], axis=0).T
            lse_ref[...] = jnp.concatenate([m0[...] + jnp.log2(l0[...]), m1[...] + jnp.log2(l1[...])], axis=0)

    stat = [pltpu.VMEM((1, tq), F32), pltpu.VMEM((1, tq), F32), pltpu.VMEM((DIL_HD, tq), F32)]
    return pl.pallas_call(
        body, name=name, grid=(npair, nq, nk),
        in_specs=[pl.BlockSpec((tq, 2 * LANES), lambda pr, i, j: (i, pr)),
                  pl.BlockSpec((tk, 2 * LANES), lambda pr, i, j: (j, pr)),
                  pl.BlockSpec((LANES, tk), lambda pr, i, j: (pr, j))],
        out_specs=[pl.BlockSpec((tq, LANES), lambda pr, i, j: (i, pr)),
                   pl.BlockSpec((None, 2, tq), lambda pr, i, j: (pr, 0, i))],
        out_shape=[jax.ShapeDtypeStruct((L, HEAD_W), F32), jax.ShapeDtypeStruct((npair, 2, L), F32)],
        scratch_shapes=stat + stat, compiler_params=_cparams(VMEM_BIG_MB))(q, k, v_t)


def _mla_bwd(q, k, v, q_t, do, do_t, lse_cols, d_cols, name):
    L = q.shape[0]
    tq, tk = min(MLA_BWD_TQ, L), min(MLA_BWD_TK, L)
    nq, nk = L // tq, L // tk
    npair = MLA_HEADS // 2

    def body(q_ref, k_ref, v_ref, qt_ref, do_ref, dot_ref, lse_ref, d_ref, dq_out, dkt_out, dvt_out,
             dq_ref, dkt_ref, dvt_ref):
        j, i = pl.program_id(1), pl.program_id(2)

        @pl.when((j == 0) & (i == 0))
        def _():
            dq_ref[...] = jnp.zeros(dq_ref.shape, F32)

        @pl.when(i == 0)
        def _():
            dkt_ref[...] = jnp.zeros(dkt_ref.shape, F32)
            dvt_ref[...] = jnp.zeros(dvt_ref.shape, F32)

        first = lax.broadcasted_iota(jnp.int32, (tq, LANES), 1) < DIL_HD
        dov = do_ref[...]
        vv = v_ref[...]
        rows = pl.ds(pl.multiple_of(i * tq, tq), tq)
        for hh in range(2):
            sl = slice(hh * LANES, (hh + 1) * LANES)
            hrows = slice(hh * DIL_HD, (hh + 1) * DIL_HD)
            qh, kh = q_ref[:, sl], k_ref[:, sl]
            do_h = jnp.where(first if hh == 0 else ~first, dov, jnp.zeros_like(dov))
            s = lax.dot_general(qh, kh, NT, preferred_element_type=F32)
            p = jnp.exp2(s - lse_ref[:, hh:hh + 1])
            dvt_ref[hrows, :] += jnp.dot(dot_ref[hrows, :], p.astype(BF16), preferred_element_type=F32)
            dp = lax.dot_general(do_h, vv, NT, preferred_element_type=F32)
            ds = (p * (dp - d_ref[:, hh:hh + 1])).astype(BF16)
            dq_ref[rows, sl] += jnp.dot(ds, kh, preferred_element_type=F32)
            dkt_ref[sl, :] += jnp.dot(qt_ref[sl, :], ds, preferred_element_type=F32)

        @pl.when(i == nq - 1)
        def _():
            dkt_out[...] = dkt_ref[...].astype(BF16)
            dvt_out[...] = dvt_ref[...].astype(BF16)

        @pl.when((j == nk - 1) & (i == nq - 1))
        def _():
            dq_out[...] = dq_ref[...].astype(BF16)

    colspec = pl.BlockSpec((None, tq, 2), lambda pr, j, i: (pr, i, 0))
    return pl.pallas_call(
        body, name=name, grid=(npair, nk, nq),
        in_specs=[pl.BlockSpec((tq, 2 * LANES), lambda pr, j, i: (i, pr)),
                  pl.BlockSpec((tk, 2 * LANES), lambda pr, j, i: (j, pr)),
                  pl.BlockSpec((tk, LANES), lambda pr, j, i: (j, pr)),
                  pl.BlockSpec((2 * LANES, tq), lambda pr, j, i: (pr, i)),
                  pl.BlockSpec((tq, LANES), lambda pr, j, i: (i, pr)),
                  pl.BlockSpec((LANES, tq), lambda pr, j, i: (pr, i)),
                  colspec, colspec],
        out_specs=[pl.BlockSpec((L, 2 * LANES), lambda pr, j, i: (0, pr)),
                   pl.BlockSpec((2 * LANES, tk), lambda pr, j, i: (pr, j)),
                   pl.BlockSpec((LANES, tk), lambda pr, j, i: (pr, j))],
        out_shape=[jax.ShapeDtypeStruct((L, 1024), BF16), jax.ShapeDtypeStruct((1024, L), BF16),
                   jax.ShapeDtypeStruct((HEAD_W, L), BF16)],
        scratch_shapes=[pltpu.VMEM((L, 2 * LANES), F32), pltpu.VMEM((2 * LANES, tk), F32),
                        pltpu.VMEM((LANES, tk), F32)],
        compiler_params=_cparams(VMEM_BIG_MB))(q, k, v, q_t, do, do_t, lse_cols, d_cols)


DIL_TQ = 1024
DIL_SQ = 128
DIL_SW = DIL_SQ + 2 * DIL_HALF


def _dil_window(a_sub, ld):
    return pl.multiple_of(jnp.clip(a_sub - DIL_HALF, 0, ld - DIL_SW), DIL_HALF)


def _dil_band_mask(shift, heads):
    kidx = lax.broadcasted_iota(jnp.int32, (DIL_SW, heads * DIL_SQ), 0)
    qidx = lax.broadcasted_iota(jnp.int32, (DIL_SW, heads * DIL_SQ), 1) % DIL_SQ
    return jnp.abs(shift + kidx - qidx) <= DIL_HALF


def _pair_rows(x, first):
    zero = jnp.zeros_like(x)
    return jnp.concatenate([jnp.where(first, x, zero), jnp.where(first, zero, x)], axis=0)


def _dil_fwd(q, kp, vp_t4, name):
    d, ld, _ = q.shape
    assert ld % DIL_SQ == 0
    tq = min(DIL_TQ, ld)
    nq = ld // tq
    npair = HEAD_W // LANES
    nb = (ld + 2 * DIL_HALF) // LANES

    def body(q_ref, k_ref, vt_ref, o_ref, lse_ref):
        i = pl.program_id(2)
        first = lax.broadcasted_iota(jnp.int32, (DIL_SQ, LANES), 1) < DIL_HD
        kidx = lax.broadcasted_iota(jnp.int32, (DIL_SW, DIL_SQ), 0)
        for u in range(tq // DIL_SQ):
            a_sub = pl.multiple_of(i * tq + u * DIL_SQ, DIL_SQ)
            kk = a_sub // LANES
            rows = slice(u * DIL_SQ, (u + 1) * DIL_SQ)
            kwin = k_ref[pl.ds(a_sub, DIL_SW), :]
            valid = _dil_band_mask(-DIL_HALF, 1) & (kidx >= DIL_HALF - a_sub) & (kidx < ld + DIL_HALF - a_sub)
            qv = q_ref[rows, :]
            outs, lses = [], []
            for hh in range(2):
                qh = jnp.where(first if hh == 0 else ~first, qv, jnp.zeros_like(qv))
                s_t = jnp.where(valid, lax.dot_general(kwin, qh, NT, preferred_element_type=F32), NEG)
                m = jnp.max(s_t, axis=0, keepdims=True)
                p32 = jnp.exp2(s_t - m)
                l = jnp.sum(p32, axis=0, keepdims=True)
                p_t = p32.astype(BF16)
                hrows = slice(hh * DIL_HD, (hh + 1) * DIL_HD)
                pv = (jnp.dot(vt_ref[kk, hrows, :], p_t[0:LANES, :], preferred_element_type=F32)
                      + jnp.dot(vt_ref[kk + 1, hrows, :], p_t[LANES:DIL_SW, :], preferred_element_type=F32))
                outs.append(pv / l)
                lses.append(jnp.broadcast_to(m + jnp.log2(l), (DIL_HD, DIL_SQ)))
            o_ref[rows, :] = jnp.concatenate(outs, axis=0).T
            lse_ref[rows, :] = jnp.concatenate(lses, axis=0).T

    blk = pl.BlockSpec((None, tq, LANES), lambda r, pr, i: (r, i, pr))
    full = pl.BlockSpec((None, ld + 2 * DIL_HALF, LANES), lambda r, pr, i: (r, 0, pr))
    vspec = pl.BlockSpec((None, nb, LANES, LANES), lambda r, pr, i: (r, 0, pr, 0))
    return pl.pallas_call(
        body, name=name, grid=(d, npair, nq), in_specs=[blk, full, vspec], out_specs=[blk, blk],
        out_shape=[jax.ShapeDtypeStruct((d, ld, HEAD_W), F32), jax.ShapeDtypeStruct((d, ld, HEAD_W), F32)],
        compiler_params=_cparams(VMEM_BIG_MB))(q, kp, vp_t4)


def _dil_bwd(q, k, v, do, lse_rows, d_rows, name):
    d, ld, _ = q.shape
    assert ld % DIL_SQ == 0 and ld >= DIL_SW
    tq = min(DIL_TQ, ld)
    nq = ld // tq
    npair = HEAD_W // LANES
    span = min(tq + 2 * DIL_HALF, ld)

    def body(q_ref, k_ref, v_ref, do_ref, lse_ref, d_ref, dq_ref, dk_ref, dv_ref, dk_sc, dv_sc, dk_acc, dv_acc):
        i = pl.program_id(2)

        @pl.when(i == 0)
        def _():
            dk_acc[...] = jnp.zeros(dk_acc.shape, F32)
            dv_acc[...] = jnp.zeros(dv_acc.shape, F32)

        dk_sc[...] = jnp.zeros(dk_sc.shape, F32)
        dv_sc[...] = jnp.zeros(dv_sc.shape, F32)
        first = lax.broadcasted_iota(jnp.int32, (DIL_SQ, LANES), 1) < DIL_HD
        base = pl.multiple_of(jnp.clip(i * tq - DIL_HALF, 0, ld - span), DIL_HALF)
        for u in range(tq // DIL_SQ):
            a_sub = i * tq + u * DIL_SQ
            ws = _dil_window(a_sub, ld)
            rows = slice(u * DIL_SQ, (u + 1) * DIL_SQ)
            win = pl.ds(pl.multiple_of(ws - base, DIL_HALF), DIL_SW)
            kwin = k_ref[pl.ds(ws, DIL_SW), :]
            vwin = v_ref[pl.ds(ws, DIL_SW), :]
            q2 = _pair_rows(q_ref[rows, :], first)
            do2 = _pair_rows(do_ref[rows, :], first)
            lse2 = jnp.concatenate([lse_ref[0:1, rows], lse_ref[1:2, rows]], axis=1)
            dd2 = jnp.concatenate([d_ref[0:1, rows], d_ref[1:2, rows]], axis=1)
            s_t = lax.dot_general(kwin, q2, NT, preferred_element_type=F32)
            p_t = jnp.exp2(jnp.where(_dil_band_mask(ws - a_sub, 2), s_t, NEG) - lse2)
            dv_sc[win, :] += jnp.dot(p_t.astype(BF16), do2, preferred_element_type=F32)
            dp_t = lax.dot_general(vwin, do2, NT, preferred_element_type=F32)
            ds_t = (p_t * (dp_t - dd2)).astype(BF16)
            dk_sc[win, :] += jnp.dot(ds_t, q2, preferred_element_type=F32)
            dq2 = lax.dot_general(ds_t, kwin, TN, preferred_element_type=F32)
            dq_ref[rows, :] = jnp.where(first, dq2[0:DIL_SQ, :], dq2[DIL_SQ:2 * DIL_SQ, :]).astype(BF16)
        dk_acc[pl.ds(base, span), :] += dk_sc[...]
        dv_acc[pl.ds(base, span), :] += dv_sc[...]

        @pl.when(i == nq - 1)
        def _():
            dk_ref[...] = dk_acc[...].astype(BF16)
            dv_ref[...] = dv_acc[...].astype(BF16)

    blk = pl.BlockSpec((None, tq, LANES), lambda r, pr, i: (r, i, pr))
    full = pl.BlockSpec((None, ld, LANES), lambda r, pr, i: (r, 0, pr))
    rowspec = pl.BlockSpec((None, None, 2, tq), lambda r, pr, i: (r, pr, 0, i))
    return pl.pallas_call(
        body, name=name, grid=(d, npair, nq), in_specs=[blk, full, full, blk, rowspec, rowspec],
        out_specs=[blk, full, full],
        out_shape=[jax.ShapeDtypeStruct((d, ld, HEAD_W), BF16)] * 3,
        scratch_shapes=[pltpu.VMEM((span, LANES), F32), pltpu.VMEM((span, LANES), F32),
                        pltpu.VMEM((ld, LANES), F32), pltpu.VMEM((ld, LANES), F32)],
        compiler_params=_cparams(VMEM_BIG_MB))(q, k, v, do, lse_rows, d_rows)


TILE_BYTES = 1 << 21


def _row_tile(rows, cols, budget=TILE_BYTES):
    for parts in range(1, rows + 1):
        tr = rows // parts
        if rows % parts == 0 and tr % 8 == 0 and tr * cols * 4 <= budget:
            return tr
    return rows


def _add2(a, b, name, out_dtype):
    n, rows, cols = a.shape
    tr = _row_tile(rows, cols)

    def body(a_ref, b_ref, o_ref):
        o_ref[...] = (a_ref[...] + b_ref[...]).astype(out_dtype)

    spec = pl.BlockSpec((None, tr, cols), lambda t, i: (t, i, 0))
    return pl.pallas_call(body, name=name, grid=(n, rows // tr), in_specs=[spec, spec], out_specs=spec,
                          out_shape=jax.ShapeDtypeStruct(a.shape, out_dtype))(a, b)


def _add4_ordered(a, name):
    _, rows, cols = a.shape
    tr = _row_tile(rows, cols, TILE_BYTES // 4)

    def body(a_ref, o_ref):
        o_ref[...] = ((a_ref[0].astype(F32) + a_ref[1].astype(F32)) + a_ref[2].astype(F32)) + a_ref[3].astype(F32)

    return pl.pallas_call(
        body, name=name, grid=(rows // tr,), in_specs=[pl.BlockSpec((4, tr, cols), lambda i: (0, i, 0))],
        out_specs=pl.BlockSpec((tr, cols), lambda i: (i, 0)),
        out_shape=jax.ShapeDtypeStruct((rows, cols), F32))(a)


def _adamw(w, g, m, v, name):
    rows, cols = w.shape
    tr = _row_tile(rows, cols)
    bc1 = 1.0 - ADAM_B1 ** ADAM_STEP
    bc2 = 1.0 - ADAM_B2 ** ADAM_STEP

    def body(w_ref, g_ref, m_ref, v_ref, d_ref, nm_ref, nv_ref):
        gv = g_ref[...]
        nm = ADAM_B1 * m_ref[...] + (1.0 - ADAM_B1) * gv
        nv = ADAM_B2 * v_ref[...] + (1.0 - ADAM_B2) * (gv * gv)
        d_ref[...] = -ADAM_LR * ((nm / bc1) / (jnp.sqrt(nv / bc2) + ADAM_EPS) + ADAM_WD * w_ref[...])
        nm_ref[...] = nm
        nv_ref[...] = nv

    spec = pl.BlockSpec((tr, cols), lambda i: (i, 0))
    return pl.pallas_call(body, name=name, grid=(rows // tr,), in_specs=[spec] * 4, out_specs=[spec] * 3,
                          out_shape=[jax.ShapeDtypeStruct(w.shape, F32)] * 3,
                          compiler_params=_cparams(VMEM_BIG_MB))(w, g, m, v)


ANY = pl.BlockSpec(memory_space=pl.ANY)


def _place():
    return lax.axis_index("x"), lax.axis_index("y"), lax.axis_index("c")


def _rcopy(send_sems, recv_sems, n, src, dst, to):
    return pltpu.make_async_remote_copy(src_ref=src, dst_ref=dst, send_sem=send_sems.at[n], recv_sem=recv_sems.at[n],
                                        device_id=to, device_id_type=MESH)


def _allgather_weights(shards):
    na = len(shards)
    ns = 7

    def body(*refs):
        w_refs, g_refs = refs[:na], refs[na:2 * na]
        send_sems, recv_sems, local_sems = refs[2 * na:]
        x, y, c = _place()
        s, sx, sy, sd = 2 * x + y, 2 * (1 - x) + y, 2 * x + (1 - y), 2 * (1 - x) + (1 - y)
        to_x, to_y, sib = (1 - x, y, c), (x, 1 - y, c), (x, y, 1 - c)

        def part(a, shard, h, k=None):
            hr = shards[a].shape[0] // 2
            if k is None:
                return g_refs[a].at[shard, pl.ds(h * hr, hr), :]
            return g_refs[a].at[shard, pl.ds(h * hr + k * (hr // 2), hr // 2), :]

        def cp(a, n, src, dst, to):
            return _rcopy(send_sems, recv_sems, ns * a + n, src, dst, to)

        started, sends = [], []

        def go(copy):
            copy.start()
            sends.append(copy)

        for a in range(na):
            hr = shards[a].shape[0] // 2
            mine = pltpu.make_async_copy(w_refs[a], g_refs[a].at[s], local_sems.at[a])
            mine.start()
            started.append(mine)
            own = w_refs[a].at[pl.ds(c * hr, hr), :]
            go(cp(a, 0, own, part(a, s, c), to_x))
            go(cp(a, 1, own, part(a, s, c), to_y))
        for a in range(na):
            cp(a, 0, part(a, sx, c), part(a, sx, c), to_x).wait_recv()
            go(cp(a, 2, part(a, sx, c, 0), part(a, sx, c, 0), to_y))
            go(cp(a, 4, part(a, sx, c), part(a, sx, c), sib))
            cp(a, 1, part(a, sy, c), part(a, sy, c), to_y).wait_recv()
            go(cp(a, 3, part(a, sy, c, 1), part(a, sy, c, 1), to_x))
            go(cp(a, 5, part(a, sy, c), part(a, sy, c), sib))
        for a in range(na):
            cp(a, 2, part(a, sd, c, 0), part(a, sd, c, 0), to_y).wait_recv()
            cp(a, 3, part(a, sd, c, 1), part(a, sd, c, 1), to_x).wait_recv()
            go(cp(a, 6, part(a, sd, c), part(a, sd, c), sib))
        for a in range(na):
            for n, sj in ((4, sx), (5, sy), (6, sd)):
                cp(a, n, part(a, sj, 1 - c), part(a, sj, 1 - c), sib).wait_recv()
        for copy in sends:
            copy.wait_send()
        for mine in started:
            mine.wait()

    return pl.pallas_call(
        body, name="allgather_weights", in_specs=[ANY] * na, out_specs=[ANY] * na,
        out_shape=[jax.ShapeDtypeStruct((N_SHARD,) + t.shape, t.dtype) for t in shards],
        scratch_shapes=[pltpu.SemaphoreType.DMA((ns * na,)), pltpu.SemaphoreType.DMA((ns * na,)),
                        pltpu.SemaphoreType.DMA((na,))])(*shards)


def _sibling_send_halves(gs):
    na = len(gs)

    def body(*refs):
        g_refs, o_refs = refs[:na], refs[na:2 * na]
        send_sems, recv_sems = refs[2 * na:]
        x, y, c = _place()
        cps = []
        for a in range(na):
            for t in range(N_SHARD):
                cp = _rcopy(send_sems, recv_sems, N_SHARD * a + t, g_refs[a].at[t, 1 - c], o_refs[a].at[t],
                            (x, y, 1 - c))
                cp.start()
                cps.append(cp)
        for cp in cps:
            cp.wait()

    return pl.pallas_call(
        body, name="grad_sibling_exchange", in_specs=[ANY] * na, out_specs=[ANY] * na,
        out_shape=[jax.ShapeDtypeStruct((N_SHARD,) + g.shape[2:], g.dtype) for g in gs],
        scratch_shapes=[pltpu.SemaphoreType.DMA((N_SHARD * na,)), pltpu.SemaphoreType.DMA((N_SHARD * na,))])(*gs)


def _chip_scatter(parts):
    na = len(parts)

    def body(*refs):
        a_refs, o_refs = refs[:na], refs[na:2 * na]
        send_sems, recv_sems, local_sems = refs[2 * na:]
        x, y, c = _place()
        s = 2 * x + y
        chips = [(1 - x, y), (x, 1 - y), (1 - x, 1 - y)]
        started, cps = [], []
        for a in range(na):
            mine = pltpu.make_async_copy(a_refs[a].at[s], o_refs[a].at[s], local_sems.at[a])
            mine.start()
            started.append(mine)
            for n, (cx, cy) in enumerate(chips):
                cp = _rcopy(send_sems, recv_sems, 3 * a + n, a_refs[a].at[2 * cx + cy], o_refs[a].at[s], (cx, cy, c))
                cp.start()
                cps.append(cp)
        for a in range(na):
            for n, (cx, cy) in enumerate(chips):
                sj = 2 * cx + cy
                _rcopy(send_sems, recv_sems, 3 * a + n, a_refs[a].at[sj], o_refs[a].at[sj], (cx, cy, c)).wait_recv()
        for cp in cps:
            cp.wait_send()
        for mine in started:
            mine.wait()

    return pl.pallas_call(
        body, name="grad_chip_scatter", in_specs=[ANY] * na, out_specs=[ANY] * na,
        out_shape=[jax.ShapeDtypeStruct(t.shape, t.dtype) for t in parts],
        scratch_shapes=[pltpu.SemaphoreType.DMA((3 * na,)), pltpu.SemaphoreType.DMA((3 * na,)),
                        pltpu.SemaphoreType.DMA((na,))])(*parts)


def _sibling_swap(rs):
    na = len(rs)

    def body(*refs):
        r_refs, o_refs = refs[:na], refs[na:2 * na]
        send_sems, recv_sems = refs[2 * na:]
        x, y, c = _place()
        cps = []
        for a in range(na):
            cp = _rcopy(send_sems, recv_sems, a, r_refs[a], o_refs[a], (x, y, 1 - c))
            cp.start()
            cps.append(cp)
        for cp in cps:
            cp.wait()

    return pl.pallas_call(
        body, name="grad_sibling_swap", in_specs=[ANY] * na, out_specs=[ANY] * na,
        out_shape=[jax.ShapeDtypeStruct(t.shape, t.dtype) for t in rs],
        scratch_shapes=[pltpu.SemaphoreType.DMA((na,)), pltpu.SemaphoreType.DMA((na,))])(*rs)


def _pack_small(norm_g, q_norm_g, kv_norm_g, final_g):
    flat = jnp.concatenate([norm_g.reshape(-1), q_norm_g.reshape(-1), kv_norm_g.reshape(-1), final_g.reshape(-1),
                            jnp.zeros((SMALL_ROWS * LANES - N_SMALL,), F32)])
    return flat.reshape(SMALL_ROWS, LANES)


def _split_small(s):
    s = s.reshape(-1)
    o = 0
    out = []
    for n, shape in ((DEPTH * D_MODEL, (DEPTH, D_MODEL)), (DEPTH * Q_LORA, (DEPTH, Q_LORA)),
                     (DEPTH * KV_LORA, (DEPTH, KV_LORA)), (D_MODEL, (D_MODEL,))):
        out.append(s[o:o + n].reshape(shape))
        o += n
    return out


def _assemble_w_in(sh):
    z = lambda n: jnp.zeros(sh.shape[1:3] + (n,), sh.dtype)
    s0, s1, s2, s3 = sh[0], sh[1], sh[2], sh[3]
    return jnp.concatenate([s0[..., 0:640], z(64), s0[..., 640:672], z(32), z(256), s0[..., 672:1184],
                            s3[..., 1064:1576], s0[..., 1184:1576], s1, s2, s3[..., 0:1064]], axis=-1)


def _split_w_in_grad(parts):
    def shard(s, a, b):
        if s == 0:
            return jnp.concatenate([a[:, 0:640], a[:, 704:736], a[:, 1024:1536], b[:, 0:392]], axis=1)
        if s == 3:
            return jnp.concatenate([b[:, 3544:4608], a[:, 1536:2048]], axis=1)
        return b[:, 392 + (s - 1) * SHARD_COLS_IN:392 + s * SHARD_COLS_IN]

    rows = jnp.concatenate([shard(s, a, b) for s in range(N_SHARD) for a, b in parts], axis=0)
    return rows.reshape(N_SHARD, DEPTH * D_MODEL, SHARD_COLS_IN)


def _col_shards(w):
    dl, r, cc = w.shape
    return w.reshape(dl, r, N_SHARD, cc // N_SHARD).transpose(2, 0, 1, 3).reshape(N_SHARD, dl * r, cc // N_SHARD)


def _from_col_shards(g, rows):
    cc = g.shape[-1]
    return g.reshape(N_SHARD, DEPTH, rows, cc).transpose(1, 2, 0, 3).reshape(DEPTH, rows, N_SHARD * cc)


def _pad_w_in(w):
    z = lambda n: jnp.zeros(w.shape[:-1] + (n,), w.dtype)
    return jnp.concatenate([w[..., 0:640], z(64), w[..., 640:672], z(32), z(256), w[..., 672:1184],
                            w[..., 5792:6304], w[..., 1184:5792]], axis=-1)


def _unpad_w_in(w):
    return jnp.concatenate([w[..., 0:640], w[..., 704:736], w[..., 1024:1536], w[..., 2048:6656],
                            w[..., 1536:2048]], axis=-1)


def _pad_w_uq(w):
    s = w.shape[:-1]
    w = w.reshape(s + (MLA_HEADS, 96))
    return jnp.pad(w, [(0, 0)] * (w.ndim - 1) + [(0, 32)]).reshape(s + (1024,))


def _unpad_w_uq(w):
    s = w.shape[:-1]
    return w.reshape(s + (MLA_HEADS, LANES))[..., :96].reshape(s + (768,))


def _pad_w_ukv(w):
    s = w.shape[:-1]
    w = w.reshape(s + (MLA_HEADS, 128))
    kpart = jnp.pad(w[..., :64], [(0, 0)] * (w.ndim - 1) + [(0, 64)]).reshape(s + (1024,))
    vpart = w[..., 64:].reshape(s + (512,))
    return jnp.concatenate([kpart, vpart], axis=-1)


def _unpad_w_ukv(w):
    s = w.shape[:-1]
    kpart = w[..., :1024].reshape(s + (MLA_HEADS, LANES))[..., :64]
    vpart = w[..., 1024:].reshape(s + (MLA_HEADS, 64))
    return jnp.concatenate([kpart, vpart], axis=-1).reshape(s + (1024,))


def _rope_tables(L, dim, lane_lo, period):
    half = dim // 2
    inv = 1.0 / (ROPE_THETA ** (jnp.arange(0, dim, 2, dtype=F32) / dim))
    ang = jnp.arange(L, dtype=F32)[:, None] * inv[None, :]
    cos, sin = jnp.cos(ang), jnp.sin(ang)
    one = lambda n: jnp.ones((L, n), F32)
    zero = lambda n: jnp.zeros((L, n), F32)
    rest = period - lane_lo - dim
    rep = LANES // period
    c = jnp.tile(jnp.concatenate([one(lane_lo), cos, cos, one(rest)], axis=1), (1, rep))
    a = jnp.tile(jnp.concatenate([zero(lane_lo), -sin, zero(half), zero(rest)], axis=1), (1, rep))
    b = jnp.tile(jnp.concatenate([zero(lane_lo + half), sin, zero(rest)], axis=1), (1, rep))
    return c, a, b


def _to_strided(t, d):
    L, w = t.shape
    return t.reshape(L // d, d, w).transpose(1, 0, 2)


def _from_strided(t):
    d, ld, w = t.shape
    return t.transpose(1, 0, 2).reshape(d * ld, w)


def _head_rows(t):
    return t.T.reshape(MLA_HEADS // 2, 2, t.shape[0])


def _head_rows_strided(t, d):
    s = _to_strided(t, d)
    return s.transpose(0, 2, 1).reshape(d, MLA_HEADS // 2, 2, s.shape[1])

def _local_grads(x, target, norm_g, w_in_p, q_norm_g, kv_norm_g, w_uq_p, w_ukv_p, w_out, final_g):
    L = x.shape[0]
    tabs_m = _rope_tables(L, MLA_ROPE, MLA_NOPE, LANES)
    tabs_d = _rope_tables(L, ROT_DIM, 0, DIL_HD)
    tabs_m_t = (tabs_m[0], -tabs_m[1], -tabs_m[2])
    tabs_d_t = (tabs_d[0], -tabs_d[1], -tabs_d[2])
    w_in_t = jnp.swapaxes(w_in_p, 1, 2)
    w_uq_t = jnp.swapaxes(w_uq_p, 1, 2)
    w_ukv_t = jnp.swapaxes(w_ukv_p, 1, 2)
    w_out_t = jnp.swapaxes(w_out, 1, 2)

    saved = []
    for l in range(DEPTH):
        h = _rms_fwd(x, norm_g[l:l + 1], "rms_fwd")
        p = _mm(h, w_in_p[l], tm=1024, tn=3328, tk=1024, out_dtype=BF16, name="in_proj")
        q, k, v, cqn, ckvn = _mla_prep(p, q_norm_g[l:l + 1], kv_norm_g[l:l + 1], w_uq_p[l], w_ukv_p[l], tabs_m,
                                       "mla_prep")
        oa, lse_a = _mla_fwd(q, k, v.T, "mla_fwd")
        dil = _dil_prep(p, tabs_d, "dil_prep")
        dil_s, o_g, lse_g = [], [], []
        for g, (_, dd) in enumerate(DIL_PAIRS):
            qs, ks, vs = (_to_strided(t, dd) for t in dil[3 * g:3 * g + 3])
            pad = ((0, 0), (DIL_HALF, DIL_HALF), (0, 0))
            vp = jnp.pad(vs, pad)
            v_t4 = vp.reshape(dd, vp.shape[1] // LANES, LANES, HEAD_W).transpose(0, 1, 3, 2)
            og, lg = _dil_fwd(qs, jnp.pad(ks, pad), v_t4, "dil_fwd_%d" % dd)
            dil_s.append((qs, ks, vs))
            o_g.append(_from_strided(og))
            lse_g.append(_from_strided(lg))
        ab, bm, lt = _merge_gate(oa, p, o_g, lse_g, "merge_gate")
        x_next = _mm(ab, w_out[l], tm=1024, tn=1024, tk=1024, out_dtype=F32, name="out_proj", add=x)
        saved.append((x, h, p, q, k, v, cqn, ckvn, oa, lse_a, dil_s, bm, lt, ab))
        x = x_next

    loss_b, dx, d_final = _loss_head(x, final_g[None, :], target, "loss_head")
    loss = loss_b[0, 0]

    d_norm, d_qn, d_kvn, d_win, d_wuq, d_wukv, d_wout = [], [], [], [], [], [], []
    for l in reversed(range(DEPTH)):
        x_l, h, p, q, k, v, cqn, ckvn, oa, lse_a, dil_s, bm, lt, ab = saved[l]
        dab = _mm(dx, w_out_t[l], tm=1024, tn=1024, tk=1024, out_dtype=F32, name="out_proj_dgrad")
        d_wout.append(_mm(ab, dx, tm=1024, tn=1024, tk=1024, out_dtype=F32, name="out_proj_wgrad", a_is_kxm=True))
        doa, dbm, D_a, D_b, dgates = _gate_bwd(dab, p, oa, bm, "gate_bwd")
        dq, dk_t, dv_t = _mla_bwd(q, k, v, q.T, doa, doa.T, lse_a.transpose(0, 2, 1),
                                  D_a.reshape(L, MLA_HEADS // 2, 2).transpose(1, 0, 2), "mla_bwd")
        dk, dv = dk_t.T, dv_t.T
        dp_mla, dq_pre, dkv, dqg, dkvg = _mla_prep_bwd(dq, dk, dv, p, q_norm_g[l:l + 1], kv_norm_g[l:l + 1],
                                                       w_uq_t[l], w_ukv_t[l], tabs_m_t, "mla_prep_bwd")
        d_wuq.append(_mm(cqn, dq_pre, tm=Q_LORA, tn=1024, tk=2048, out_dtype=F32, name="w_uq_wgrad", a_is_kxm=True))
        d_wukv.append(_mm(ckvn, dkv, tm=KV_LORA, tn=1536, tk=2048, out_dtype=F32, name="w_ukv_wgrad", a_is_kxm=True))
        dgr = []
        for g, (_, dd) in enumerate(DIL_PAIRS):
            qs, ks, vs = dil_s[g]
            dqs, dks, dvs = _dil_bwd(qs, ks, vs, _to_strided(dbm, dd), _head_rows_strided(lt, dd),
                                     _head_rows_strided(D_b, dd), "dil_bwd_%d" % dd)
            dgr += [_from_strided(dqs), _from_strided(dks), _from_strided(dvs)]
        dp_dil = _dil_prep_bwd(dgr, tabs_d_t, "dil_prep_bwd")
        dp_a = jnp.concatenate([dp_mla, dgates], axis=1)
        dh = _mm(dp_a, w_in_t[l][0:P_DIL0], tm=1024, tn=1024, tk=2048, out_dtype=F32, name="in_proj_dgrad_a")
        dh = _mm(dp_dil, w_in_t[l][P_DIL0:], tm=512, tn=1024, tk=2304, out_dtype=F32, name="in_proj_dgrad_b",
                 add=dh)
        d_win.append((_mm(h, dp_a, tm=512, tn=2048, tk=1024, out_dtype=F32, name="in_proj_wgrad_a", a_is_kxm=True),
                      _mm(h, dp_dil, tm=512, tn=1536, tk=2048, out_dtype=F32, name="in_proj_wgrad_b",
                          a_is_kxm=True)))
        dx, dng = _rms_bwd(dh, x_l, norm_g[l:l + 1], dx, "rms_bwd")
        d_norm.append(dng[0])
        d_qn.append(dqg[0])
        d_kvn.append(dkvg[0])

    rev = lambda xs: jnp.stack(xs[::-1])
    return (loss, dx, rev(d_norm), d_win[::-1], rev(d_qn), rev(d_kvn), rev(d_wuq), rev(d_wukv), rev(d_wout),
            d_final[0])


def kernel(x, norm_g, w_in, q_norm_g, kv_norm_g, w_uq, w_ukv, w_out, final_g, loss_target, m_norm_g, m_w_in, m_q_norm_g, m_kv_norm_g, m_w_uq, m_w_ukv, m_w_out, m_final_g, v_norm_g, v_w_in, v_q_norm_g, v_kv_norm_g, v_w_uq, v_w_ukv, v_w_out, v_final_g):
    c = lax.axis_index("c")

    def families(a_in, a_uq, a_ukv, a_out):
        return [t.reshape(shape) for t, shape in zip((a_in, a_uq, a_ukv, a_out), FAM_SHAPES)]

    g_in, g_uq, g_ukv, g_out = _allgather_weights([t.astype(BF16) for t in families(w_in, w_uq, w_ukv, w_out)])
    w_in_p = _assemble_w_in(g_in.reshape(N_SHARD, DEPTH, D_MODEL, SHARD_COLS_IN))
    w_uq_p = _pad_w_uq(_from_col_shards(g_uq, Q_LORA))
    w_ukv_p = _pad_w_ukv(_from_col_shards(g_ukv, KV_LORA))
    w_out_f = g_out.reshape(N_SHARD, DEPTH, 1024 // N_SHARD, D_MODEL).transpose(1, 0, 2, 3).reshape(DEPTH, 1024, D_MODEL)

    (loss, dx, d_norm, d_win_p, d_qn, d_kvn, d_wuq_p, d_wukv_p, d_wout, d_final) = _local_grads(
        x[0], loss_target[0], norm_g, w_in_p, q_norm_g, kv_norm_g, w_uq_p, w_ukv_p, w_out_f, final_g)
    loss = lax.psum(loss, ("x", "y", "c"))

    small = _pack_small(d_norm, d_qn, d_kvn, d_final)
    grads = [_split_w_in_grad(d_win_p), _col_shards(_unpad_w_uq(d_wuq_p)), _col_shards(_unpad_w_ukv(d_wukv_p)),
             d_wout.reshape(DEPTH, N_SHARD, 1024 // N_SHARD, D_MODEL).transpose(1, 0, 2, 3).reshape(
                 N_SHARD, DEPTH * (1024 // N_SHARD), D_MODEL),
             jnp.broadcast_to(small[None], (N_SHARD, SMALL_ROWS, LANES))]
    halves = [g.reshape(N_SHARD, 2, g.shape[1] // 2, g.shape[2]) for g in grads]
    from_sib = _sibling_send_halves(halves)
    chip_sum = [_add2(lax.dynamic_index_in_dim(h, c, axis=1, keepdims=False), f, "grad_add_pair", BF16)
                for h, f in zip(halves, from_sib)]
    red_half = [_add4_ordered(t, "grad_add_chips") for t in _chip_scatter(chip_sum)]
    other_half = _sibling_swap(red_half)
    gred = []
    for mine, other in zip(red_half, other_half):
        both = jnp.stack([mine, other])
        gred.append(jnp.concatenate([lax.dynamic_index_in_dim(both, c, axis=0, keepdims=False),
                                     lax.dynamic_index_in_dim(both, 1 - c, axis=0, keepdims=False)], axis=0))

    wf = families(w_in, w_uq, w_ukv, w_out) + [_pack_small(norm_g, q_norm_g, kv_norm_g, final_g)]
    mf = families(m_w_in, m_w_uq, m_w_ukv, m_w_out) + [_pack_small(m_norm_g, m_q_norm_g, m_kv_norm_g, m_final_g)]
    vf = families(v_w_in, v_w_uq, v_w_ukv, v_w_out) + [_pack_small(v_norm_g, v_q_norm_g, v_kv_norm_g, v_final_g)]
    upd = [_adamw(w, g, m, v, "adamw") for w, g, m, v in zip(wf, gred, mf, vf)]

    def leaves(fams):
        a_in, a_uq, a_ukv, a_out, s = fams
        s_norm, s_qn, s_kvn, s_final = _split_small(s)
        return [s_norm, a_in.reshape(w_in.shape), s_qn, s_kvn, a_uq.reshape(w_uq.shape), a_ukv.reshape(w_ukv.shape),
                a_out.reshape(w_out.shape), s_final]

    return (loss, dx[None], *leaves(gred), *leaves([u[0] for u in upd]), *leaves([u[1] for u in upd]),
            *leaves([u[2] for u in upd]))
```

```python
import functools

import jax
import jax.numpy as jnp
from jax import lax
from jax.experimental import pallas as pl
from jax.experimental.pallas import tpu as pltpu

F32 = jnp.float32
BF16 = jnp.bfloat16
MESH = pl.DeviceIdType.MESH

D_MODEL = 1024
DEPTH = 4
MLA_HEADS = 8
MLA_NOPE = 64
MLA_ROPE = 32
Q_LORA = 384
KV_LORA = 256
DIL_PAIRS = ((128, 1), (512, 4), (2048, 16))
DIL_HD = 64
DIL_HALF = 64
ROT_DIM = 16
ROPE_THETA = 500000.0
EPS = 1e-6
IN_WIDTH = 6304
N_SHARD = 4

P_WIDTH = 6656
P_MLA = 1024
P_GATE = 1024
P_DIL0 = 2048
LANES = 128
HEAD_W = 512

ADAM_LR = 0.001
ADAM_B1 = 0.9
ADAM_B2 = 0.999
ADAM_EPS = 1e-08
ADAM_WD = 0.01
ADAM_STEP = 10

SHARD_COLS_IN = IN_WIDTH // N_SHARD
FAM_SHAPES = ((DEPTH * D_MODEL, SHARD_COLS_IN), (DEPTH * Q_LORA, 768 // N_SHARD), (DEPTH * KV_LORA, 1024 // N_SHARD),
              (DEPTH * (1024 // N_SHARD), D_MODEL))
N_SMALL = DEPTH * (D_MODEL + Q_LORA + KV_LORA) + D_MODEL
SMALL_ROWS = 64
VMEM_BIG_MB = 48


def _cparams(vmem_mb=None):
    if vmem_mb is None:
        return None
    return pltpu.CompilerParams(vmem_limit_bytes=vmem_mb << 20)


def _sigmoid(x):
    return 1.0 / (1.0 + jnp.exp(-x))


def _rope(x, c, a, b, sh):
    return x * c + pltpu.roll(x, LANES - sh, 1) * a + pltpu.roll(x, sh, 1) * b


def _per_head8(x, pick_first):
    r = lax.broadcasted_iota(jnp.int32, (HEAD_W, MLA_HEADS), 0)
    c = lax.broadcasted_iota(jnp.int32, (HEAD_W, MLA_HEADS), 1)
    sel = (r == c * DIL_HD) if pick_first else (r // DIL_HD == c)
    mat = jnp.where(sel, 1.0, 0.0).astype(BF16)
    out = jnp.zeros((x.shape[0], MLA_HEADS), F32)
    for _ in range(3):
        part = x.astype(BF16)
        out = out + jnp.dot(part, mat, preferred_element_type=F32)
        x = x - part.astype(F32)
    return out


def _mm(a, b, *, tm, tn, tk, out_dtype, name, add=None, a_is_kxm=False):
    K, M = a.shape if a_is_kxm else a.shape[::-1]
    N = b.shape[1]
    tm, tn, tk = min(tm, M), min(tn, N), min(tk, K)
    assert M % tm == 0 and N % tn == 0 and K % tk == 0, (a.shape, b.shape)
    nk = K // tk
    has_add = add is not None

    def body(*refs):
        a_ref, b_ref = refs[0], refs[1]
        add_ref = refs[2] if has_add else None
        o_ref = refs[3] if has_add else refs[2]
        k = pl.program_id(2)
        if a_is_kxm:
            part = lax.dot_general(a_ref[...].astype(BF16), b_ref[...].astype(BF16), (((0,), (0,)), ((), ())),
                                   preferred_element_type=F32)
        else:
            part = jnp.dot(a_ref[...].astype(BF16), b_ref[...].astype(BF16), preferred_element_type=F32)
        if nk == 1:
            o_ref[...] = (part + add_ref[...] if has_add else part).astype(out_dtype)
            return
        acc = refs[-1]

        @pl.when(k == 0)
        def _():
            acc[...] = part

        @pl.when(k > 0)
        def _():
            acc[...] += part

        @pl.when(k == nk - 1)
        def _():
            r = acc[...]
            if has_add:
                r = r + add_ref[...]
            o_ref[...] = r.astype(out_dtype)

    a_spec = pl.BlockSpec((tk, tm), lambda i, j, k: (k, i)) if a_is_kxm else pl.BlockSpec((tm, tk), lambda i, j, k: (i, k))
    in_specs = [a_spec, pl.BlockSpec((tk, tn), lambda i, j, k: (k, j))]
    args = [a, b]
    if has_add:
        in_specs.append(pl.BlockSpec((tm, tn), lambda i, j, k: (i, j)))
        args.append(add)
    return pl.pallas_call(
        body, name=name, grid=(M // tm, N // tn, nk), in_specs=in_specs,
        out_specs=pl.BlockSpec((tm, tn), lambda i, j, k: (i, j)),
        out_shape=jax.ShapeDtypeStruct((M, N), out_dtype),
        scratch_shapes=[pltpu.VMEM((tm, tn), F32)] if nk > 1 else [],
        compiler_params=_cparams(VMEM_BIG_MB))(*args)


def _row_spec(tm, w, cb=0):
    return pl.BlockSpec((tm, w), lambda i: (i, cb))


def _const_spec(arr):
    nd = arr.ndim
    return pl.BlockSpec(arr.shape, lambda i: (0,) * nd)


def _rms_fwd(x, g, name):
    L, D = x.shape
    tm = min(512, L)

    def body(x_ref, g_ref, o_ref):
        xv = x_ref[...]
        r = lax.rsqrt(jnp.mean(xv * xv, axis=-1, keepdims=True) + EPS)
        o_ref[...] = (xv * r * g_ref[...]).astype(BF16)

    return pl.pallas_call(
        body, name=name, grid=(L // tm,), in_specs=[_row_spec(tm, D), _const_spec(g)],
        out_specs=_row_spec(tm, D), out_shape=jax.ShapeDtypeStruct((L, D), BF16))(x, g)


def _rms_bwd(dh, x, g, dres, name):
    L, D = x.shape
    tm = min(512, L)

    def body(dh_ref, x_ref, g_ref, dres_ref, dx_ref, dg_ref):
        xv = x_ref[...]
        dy = dh_ref[...]
        r = lax.rsqrt(jnp.mean(xv * xv, axis=-1, keepdims=True) + EPS)
        dyg = dy * g_ref[...]
        dx_ref[...] = dres_ref[...] + r * dyg - xv * (r * r * r) * jnp.mean(dyg * xv, axis=-1, keepdims=True)
        part = jnp.sum(dy * xv * r, axis=0, keepdims=True)

        @pl.when(pl.program_id(0) == 0)
        def _():
            dg_ref[...] = part

        @pl.when(pl.program_id(0) > 0)
        def _():
            dg_ref[...] += part

    return pl.pallas_call(
        body, name=name, grid=(L // tm,),
        in_specs=[_row_spec(tm, D), _row_spec(tm, D), _const_spec(g), _row_spec(tm, D)],
        out_specs=[_row_spec(tm, D), pl.BlockSpec((1, D), lambda i: (0, 0))],
        out_shape=[jax.ShapeDtypeStruct((L, D), F32), jax.ShapeDtypeStruct((1, D), F32)])(dh, x, g, dres)


def _loss_head(x, g, target, name):
    L, D = x.shape
    tm = min(512, L)

    def body(x_ref, g_ref, t_ref, loss_ref, dx_ref, dg_ref):
        xv = x_ref[...]
        gv = g_ref[...]
        r = lax.rsqrt(jnp.mean(xv * xv, axis=-1, keepdims=True) + EPS)
        xr = xv * r
        err = xr * gv - t_ref[...]
        lp = 0.5 * jnp.sum(jnp.mean(err * err, axis=-1, keepdims=True))
        dy = err * (1.0 / D)
        dyg = dy * gv
        dx_ref[...] = r * dyg - xv * (r * r * r) * jnp.mean(dyg * xv, axis=-1, keepdims=True)
        part = jnp.sum(dy * xr, axis=0, keepdims=True)

        @pl.when(pl.program_id(0) == 0)
        def _():
            dg_ref[...] = part
            loss_ref[...] = jnp.zeros(loss_ref.shape, F32) + lp

        @pl.when(pl.program_id(0) > 0)
        def _():
            dg_ref[...] += part
            loss_ref[...] += lp

    return pl.pallas_call(
        body, name=name, grid=(L // tm,),
        in_specs=[_row_spec(tm, D), _const_spec(g), _row_spec(tm, D)],
        out_specs=[pl.BlockSpec((8, LANES), lambda i: (0, 0)), _row_spec(tm, D), pl.BlockSpec((1, D), lambda i: (0, 0))],
        out_shape=[jax.ShapeDtypeStruct((8, LANES), F32), jax.ShapeDtypeStruct((L, D), F32),
                   jax.ShapeDtypeStruct((1, D), F32)])(x, g, target)


def _mla_prep(p, qg, kvg, wuq, wukv, tabs, name):
    L = p.shape[0]
    tm = min(512, L)
    scale = (MLA_NOPE + MLA_ROPE) ** -0.5
    tc, ta, tb = tabs

    def body(p_ref, qg_ref, kvg_ref, wuq_ref, wukv_ref, c_ref, a_ref, b_ref, q_ref, k_ref, v_ref, cqn_ref, ckvn_ref):
        c, a, b = c_ref[...], a_ref[...], b_ref[...]
        cq = p_ref[:, 0:Q_LORA].astype(F32)
        ckv = p_ref[:, Q_LORA:Q_LORA + KV_LORA].astype(F32)
        kr = p_ref[:, 640:768].astype(F32)
        cqn = (cq * lax.rsqrt(jnp.mean(cq * cq, axis=-1, keepdims=True) + EPS) * qg_ref[...]).astype(BF16)
        ckvn = (ckv * lax.rsqrt(jnp.mean(ckv * ckv, axis=-1, keepdims=True) + EPS) * kvg_ref[...]).astype(BF16)
        cqn_ref[...] = cqn
        ckvn_ref[...] = ckvn
        q = jnp.dot(cqn, wuq_ref[...], preferred_element_type=F32)
        kv = jnp.dot(ckvn, wukv_ref[...], preferred_element_type=F32)
        krr = _rope(kr, c, a, b, MLA_ROPE // 2)
        for h in range(MLA_HEADS):
            sl = slice(h * LANES, (h + 1) * LANES)
            q_ref[:, sl] = (_rope(q[:, sl], c, a, b, MLA_ROPE // 2) * (scale * LOG2E)).astype(BF16)
            k_ref[:, sl] = (kv[:, sl] + krr).astype(BF16)
        v_ref[...] = kv[:, 1024:1536].astype(BF16)

    return pl.pallas_call(
        body, name=name, grid=(L // tm,),
        in_specs=[_row_spec(tm, P_MLA, 0), _const_spec(qg), _const_spec(kvg), _const_spec(wuq), _const_spec(wukv),
                  _row_spec(tm, LANES), _row_spec(tm, LANES), _row_spec(tm, LANES)],
        out_specs=[_row_spec(tm, 1024), _row_spec(tm, 1024), _row_spec(tm, HEAD_W), _row_spec(tm, Q_LORA),
                   _row_spec(tm, KV_LORA)],
        out_shape=[jax.ShapeDtypeStruct((L, 1024), BF16), jax.ShapeDtypeStruct((L, 1024), BF16),
                   jax.ShapeDtypeStruct((L, HEAD_W), BF16), jax.ShapeDtypeStruct((L, Q_LORA), BF16),
                   jax.ShapeDtypeStruct((L, KV_LORA), BF16)],
        compiler_params=_cparams(VMEM_BIG_MB))(p, qg, kvg, wuq, wukv, tc, ta, tb)


def _mla_prep_bwd(dq, dk, dv, p, qg, kvg, wuq_t, wukv_t, tabs_t, name):
    L = p.shape[0]
    tm = min(512, L)
    scale = (MLA_NOPE + MLA_ROPE) ** -0.5
    tc, ta, tb = tabs_t

    def body(dq_ref, dk_ref, dv_ref, p_ref, qg_ref, kvg_ref, wuqt_ref, wukvt_ref, c_ref, a_ref, b_ref,
             dp_ref, dqp_ref, dkv_ref, dqg_ref, dkvg_ref):
        c, a, b = c_ref[...], a_ref[...], b_ref[...]
        dkr = jnp.zeros((tm, LANES), F32)
        for h in range(MLA_HEADS):
            sl = slice(h * LANES, (h + 1) * LANES)
            dqp_ref[:, sl] = (_rope(dq_ref[:, sl].astype(F32), c, a, b, MLA_ROPE // 2) * scale).astype(BF16)
            dkh = dk_ref[:, sl].astype(F32) * LN2
            dkv_ref[:, sl] = dkh.astype(BF16)
            dkr = dkr + dkh
        dkv_ref[:, 1024:1536] = dv_ref[...].astype(BF16)
        lane = lax.broadcasted_iota(jnp.int32, (tm, LANES), 1)
        dkr = jnp.where((lane >= MLA_NOPE) & (lane < MLA_NOPE + MLA_ROPE), _rope(dkr, c, a, b, MLA_ROPE // 2), 0.0)

        d_cqn = jnp.dot(dqp_ref[...], wuqt_ref[...], preferred_element_type=F32)
        d_ckvn = jnp.dot(dkv_ref[...], wukvt_ref[...], preferred_element_type=F32)

        def norm_bwd(xv, gv, dy):
            r = lax.rsqrt(jnp.mean(xv * xv, axis=-1, keepdims=True) + EPS)
            dyg = dy * gv
            dx = r * dyg - xv * (r * r * r) * jnp.mean(dyg * xv, axis=-1, keepdims=True)
            return dx, jnp.sum(dy * xv * r, axis=0, keepdims=True)

        d_cq, dqg = norm_bwd(p_ref[:, 0:Q_LORA].astype(F32), qg_ref[...], d_cqn)
        d_ckv, dkvg = norm_bwd(p_ref[:, Q_LORA:Q_LORA + KV_LORA].astype(F32), kvg_ref[...], d_ckvn)
        dp_ref[:, 0:Q_LORA] = d_cq.astype(BF16)
        dp_ref[:, Q_LORA:Q_LORA + KV_LORA] = d_ckv.astype(BF16)
        dp_ref[:, 640:768] = dkr.astype(BF16)
        dp_ref[:, 768:1024] = jnp.zeros((tm, 256), BF16)

        @pl.when(pl.program_id(0) == 0)
        def _():
            dqg_ref[...] = dqg
            dkvg_ref[...] = dkvg

        @pl.when(pl.program_id(0) > 0)
        def _():
            dqg_ref[...] += dqg
            dkvg_ref[...] += dkvg

    return pl.pallas_call(
        body, name=name, grid=(L // tm,),
        in_specs=[_row_spec(tm, 1024), _row_spec(tm, 1024), _row_spec(tm, HEAD_W), _row_spec(tm, P_MLA, 0),
                  _const_spec(qg), _const_spec(kvg), _const_spec(wuq_t), _const_spec(wukv_t),
                  _row_spec(tm, LANES), _row_spec(tm, LANES), _row_spec(tm, LANES)],
        out_specs=[_row_spec(tm, P_MLA), _row_spec(tm, 1024), _row_spec(tm, 1536),
                   pl.BlockSpec((1, Q_LORA), lambda i: (0, 0)), pl.BlockSpec((1, KV_LORA), lambda i: (0, 0))],
        out_shape=[jax.ShapeDtypeStruct((L, P_MLA), BF16), jax.ShapeDtypeStruct((L, 1024), BF16),
                   jax.ShapeDtypeStruct((L, 1536), BF16), jax.ShapeDtypeStruct((1, Q_LORA), F32),
                   jax.ShapeDtypeStruct((1, KV_LORA), F32)],
        compiler_params=_cparams(VMEM_BIG_MB))(dq, dk, dv, p, qg, kvg, wuq_t, wukv_t, tc, ta, tb)


def _dil_prep(p, tabs, name):
    L = p.shape[0]
    tm = min(512, L)
    tc, ta, tb = tabs
    scale = DIL_HD ** -0.5

    def body(*refs):
        ins, (c_ref, a_ref, b_ref), outs = refs[:9], refs[9:12], refs[12:]
        c, a, b = c_ref[...], a_ref[...], b_ref[...]
        for n in range(9):
            t = n % 3
            for cb in range(HEAD_W // LANES):
                sl = slice(cb * LANES, (cb + 1) * LANES)
                xv = ins[n][:, sl].astype(F32)
                if t == 0:
                    xv = _rope(xv, c, a, b, ROT_DIM // 2) * (scale * LOG2E)
                elif t == 1:
                    xv = _rope(xv, c, a, b, ROT_DIM // 2)
                outs[n][:, sl] = xv.astype(BF16)

    in_specs = [_row_spec(tm, HEAD_W, P_DIL0 // HEAD_W + n) for n in range(9)] + [_row_spec(tm, LANES)] * 3
    return pl.pallas_call(
        body, name=name, grid=(L // tm,), in_specs=in_specs,
        out_specs=[_row_spec(tm, HEAD_W)] * 9,
        out_shape=[jax.ShapeDtypeStruct((L, HEAD_W), BF16)] * 9)(*([p] * 9), tc, ta, tb)


def _dil_prep_bwd(grads, tabs_t, name):
    L = grads[0].shape[0]
    tm = min(512, L)
    tc, ta, tb = tabs_t
    scale = DIL_HD ** -0.5

    def body(*refs):
        ins, (c_ref, a_ref, b_ref), o_ref = refs[:9], refs[9:12], refs[12]
        c, a, b = c_ref[...], a_ref[...], b_ref[...]
        for n in range(9):
            t = n % 3
            for cb in range(HEAD_W // LANES):
                sl = slice(cb * LANES, (cb + 1) * LANES)
                xv = ins[n][:, sl].astype(F32)
                if t == 0:
                    xv = _rope(xv, c, a, b, ROT_DIM // 2) * scale
                elif t == 1:
                    xv = _rope(xv, c, a, b, ROT_DIM // 2) * LN2
                o_ref[:, n * HEAD_W + cb * LANES:n * HEAD_W + (cb + 1) * LANES] = xv.astype(BF16)

    return pl.pallas_call(
        body, name=name, grid=(L // tm,), in_specs=[_row_spec(tm, HEAD_W)] * 9 + [_row_spec(tm, LANES)] * 3,
        out_specs=_row_spec(tm, 9 * HEAD_W), out_shape=jax.ShapeDtypeStruct((L, 9 * HEAD_W), BF16),
        compiler_params=_cparams(VMEM_BIG_MB))(*grads, tc, ta, tb)


def _merge_gate(oa, p, o_g, lse_g, name):
    L = oa.shape[0]
    tm = min(512, L)

    def body(oa_ref, ga_ref, gb_ref, o1, o2, o3, l1, l2, l3, ab_ref, bm_ref, lt_ref):
        la, lb, lc = l1[...], l2[...], l3[...]
        m = jnp.maximum(jnp.maximum(la, lb), lc)
        ea, eb, ec = jnp.exp2(la - m), jnp.exp2(lb - m), jnp.exp2(lc - m)
        den = ea + eb + ec
        bm = (ea * o1[...] + eb * o2[...] + ec * o3[...]) / den
        bm_ref[...] = bm
        lt_ref[...] = _per_head8(m + jnp.log2(den), True)
        ga, gb = ga_ref[...].astype(F32), gb_ref[...].astype(F32)
        ab_ref[:, 0:HEAD_W] = (oa_ref[...] * (ga * _sigmoid(ga))).astype(BF16)
        ab_ref[:, HEAD_W:2 * HEAD_W] = (bm * (gb * _sigmoid(gb))).astype(BF16)

    w = _row_spec(tm, HEAD_W)
    return pl.pallas_call(
        body, name=name, grid=(L // tm,),
        in_specs=[w, _row_spec(tm, HEAD_W, 2), _row_spec(tm, HEAD_W, 3), w, w, w, w, w, w],
        out_specs=[_row_spec(tm, 2 * HEAD_W), w, _row_spec(tm, MLA_HEADS)],
        out_shape=[jax.ShapeDtypeStruct((L, 2 * HEAD_W), BF16), jax.ShapeDtypeStruct((L, HEAD_W), F32),
                   jax.ShapeDtypeStruct((L, MLA_HEADS), F32)])(oa, p, p, *o_g, *lse_g)


def _gate_bwd(dab, p, oa, bm, name):
    L = oa.shape[0]
    tm = min(512, L)

    def body(da_ref, db_ref, ga_ref, gb_ref, oa_ref, bm_ref, doa_ref, dbm_ref, Da_ref, Db_ref, dg_ref):
        def one(d, g, o, do_ref, D_ref, col):
            sg = _sigmoid(g)
            do = d * (g * sg)
            do_ref[...] = do.astype(BF16)
            dg_ref[:, col:col + HEAD_W] = (d * o * (sg * (1.0 + g * (1.0 - sg)))).astype(BF16)
            D_ref[...] = _per_head8(do * o, False)

        one(da_ref[...], ga_ref[...].astype(F32), oa_ref[...], doa_ref, Da_ref, 0)
        one(db_ref[...], gb_ref[...].astype(F32), bm_ref[...], dbm_ref, Db_ref, HEAD_W)

    w = _row_spec(tm, HEAD_W)
    w8 = _row_spec(tm, MLA_HEADS)
    return pl.pallas_call(
        body, name=name, grid=(L // tm,),
        in_specs=[_row_spec(tm, HEAD_W, 0), _row_spec(tm, HEAD_W, 1), _row_spec(tm, HEAD_W, 2),
                  _row_spec(tm, HEAD_W, 3), w, w],
        out_specs=[w, w, w8, w8, _row_spec(tm, 2 * HEAD_W)],
        out_shape=[jax.ShapeDtypeStruct((L, HEAD_W), BF16), jax.ShapeDtypeStruct((L, HEAD_W), BF16),
                   jax.ShapeDtypeStruct((L, MLA_HEADS), F32), jax.ShapeDtypeStruct((L, MLA_HEADS), F32),
                   jax.ShapeDtypeStruct((L, 2 * HEAD_W), BF16)])(dab, dab, p, p, oa, bm)


NT = (((1,), (1,)), ((), ()))
TN = (((0,), (0,)), ((), ()))
NEG = -1e30


MLA_TQ = 512
MLA_TK = 4096
MLA_BWD_TQ = 1024
MLA_BWD_TK = 2048
LOG2E = 1.4426950408889634
LN2 = 0.6931471805599453


def _mla_fwd(q, k, v_t, name):
    L = q.shape[0]
    tq, tk = min(MLA_TQ, L), min(MLA_TK, L)
    nq, nk = L // tq, L // tk
    npair = MLA_HEADS // 2

    def body(q_ref, k_ref, vt_ref, o_ref, lse_ref, m0, l0, a0, m1, l1, a1):
        j = pl.program_id(2)
        stats = ((m0, l0, a0), (m1, l1, a1))

        @pl.when(j == 0)
        def _():
            for m_sc, l_sc, acc_sc in stats:
                m_sc[...] = jnp.full(m_sc.shape, NEG, F32)
                l_sc[...] = jnp.zeros(l_sc.shape, F32)
                acc_sc[...] = jnp.zeros(acc_sc.shape, F32)

        s_ts = [lax.dot_general(k_ref[:, hh * LANES:(hh + 1) * LANES], q_ref[:, hh * LANES:(hh + 1) * LANES], NT,
                                preferred_element_type=F32) for hh in range(2)]
        for hh in range(2):
            m_sc, l_sc, acc_sc = stats[hh]
            s_t = s_ts[hh]
            m_prev = m_sc[...]
            m_new = jnp.maximum(m_prev, jnp.max(s_t, axis=0, keepdims=True))
            alpha = jnp.exp2(m_prev - m_new)
            p_t = jnp.exp2(s_t - m_new)
            l_sc[...] = alpha * l_sc[...] + jnp.sum(p_t, axis=0, keepdims=True)
            m_sc[...] = m_new
            pv = jnp.dot(vt_ref[hh * DIL_HD:(hh + 1) * DIL_HD, :], p_t.astype(BF16),
                         preferred_element_type=F32)
            acc_sc[...] = alpha * acc_sc[...] + pv

        @pl.when(j == nk - 1)
        def _():
            o_ref[...] = jnp.concatenate([a0[...] / l0[...], a1[...] / l1[...]], axis=0).T
            lse_ref[...] = jnp.concatenate([m0[...] + jnp.log2(l0[...]), m1[...] + jnp.log2(l1[...])], axis=0)

    stat = [pltpu.VMEM((1, tq), F32), pltpu.VMEM((1, tq), F32), pltpu.VMEM((DIL_HD, tq), F32)]
    return pl.pallas_call(
        body, name=name, grid=(npair, nq, nk),
        in_specs=[pl.BlockSpec((tq, 2 * LANES), lambda pr, i, j: (i, pr)),
                  pl.BlockSpec((tk, 2 * LANES), lambda pr, i, j: (j, pr)),
                  pl.BlockSpec((LANES, tk), lambda pr, i, j: (pr, j))],
        out_specs=[pl.BlockSpec((tq, LANES), lambda pr, i, j: (i, pr)),
                   pl.BlockSpec((None, 2, tq), lambda pr, i, j: (pr, 0, i))],
        out_shape=[jax.ShapeDtypeStruct((L, HEAD_W), F32), jax.ShapeDtypeStruct((npair, 2, L), F32)],
        scratch_shapes=stat + stat, compiler_params=_cparams(VMEM_BIG_MB))(q, k, v_t)


def _mla_bwd(q, k, v, q_t, do, do_t, lse_cols, d_cols, name):
    L = q.shape[0]
    tq, tk = min(MLA_BWD_TQ, L), min(MLA_BWD_TK, L)
    nq, nk = L // tq, L // tk
    npair = MLA_HEADS // 2

    def body(q_ref, k_ref, v_ref, qt_ref, do_ref, dot_ref, lse_ref, d_ref, dq_out, dkt_out, dvt_out,
             dq_ref, dkt_ref, dvt_ref):
        j, i = pl.program_id(1), pl.program_id(2)

        @pl.when((j == 0) & (i == 0))
        def _():
            dq_ref[...] = jnp.zeros(dq_ref.shape, F32)

        @pl.when(i == 0)
        def _():
            dkt_ref[...] = jnp.zeros(dkt_ref.shape, F32)
            dvt_ref[...] = jnp.zeros(dvt_ref.shape, F32)

        first = lax.broadcasted_iota(jnp.int32, (tq, LANES), 1) < DIL_HD
        dov = do_ref[...]
        vv = v_ref[...]
        rows = pl.ds(pl.multiple_of(i * tq, tq), tq)
        for hh in range(2):
            sl = slice(hh * LANES, (hh + 1) * LANES)
            hrows = slice(hh * DIL_HD, (hh + 1) * DIL_HD)
            qh, kh = q_ref[:, sl], k_ref[:, sl]
            do_h = jnp.where(first if hh == 0 else ~first, dov, jnp.zeros_like(dov))
            s = lax.dot_general(qh, kh, NT, preferred_element_type=F32)
            p = jnp.exp2(s - lse_ref[:, hh:hh + 1])
            dvt_ref[hrows, :] += jnp.dot(dot_ref[hrows, :], p.astype(BF16), preferred_element_type=F32)
            dp = lax.dot_general(do_h, vv, NT, preferred_element_type=F32)
            ds = (p * (dp - d_ref[:, hh:hh + 1])).astype(BF16)
            dq_ref[rows, sl] += jnp.dot(ds, kh, preferred_element_type=F32)
            dkt_ref[sl, :] += jnp.dot(qt_ref[sl, :], ds, preferred_element_type=F32)

        @pl.when(i == nq - 1)
        def _():
            dkt_out[...] = dkt_ref[...].astype(BF16)
            dvt_out[...] = dvt_ref[...].astype(BF16)

        @pl.when((j == nk - 1) & (i == nq - 1))
        def _():
            dq_out[...] = dq_ref[...].astype(BF16)

    colspec = pl.BlockSpec((None, tq, 2), lambda pr, j, i: (pr, i, 0))
    return pl.pallas_call(
        body, name=name, grid=(npair, nk, nq),
        in_specs=[pl.BlockSpec((tq, 2 * LANES), lambda pr, j, i: (i, pr)),
                  pl.BlockSpec((tk, 2 * LANES), lambda pr, j, i: (j, pr)),
                  pl.BlockSpec((tk, LANES), lambda pr, j, i: (j, pr)),
                  pl.BlockSpec((2 * LANES, tq), lambda pr, j, i: (pr, i)),
                  pl.BlockSpec((tq, LANES), lambda pr, j, i: (i, pr)),
                  pl.BlockSpec((LANES, tq), lambda pr, j, i: (pr, i)),
                  colspec, colspec],
        out_specs=[pl.BlockSpec((L, 2 * LANES), lambda pr, j, i: (0, pr)),
                   pl.BlockSpec((2 * LANES, tk), lambda pr, j, i: (pr, j)),
                   pl.BlockSpec((LANES, tk), lambda pr, j, i: (pr, j))],
        out_shape=[jax.ShapeDtypeStruct((L, 1024), BF16), jax.ShapeDtypeStruct((1024, L), BF16),
                   jax.ShapeDtypeStruct((HEAD_W, L), BF16)],
        scratch_shapes=[pltpu.VMEM((L, 2 * LANES), F32), pltpu.VMEM((2 * LANES, tk), F32),
                        pltpu.VMEM((LANES, tk), F32)],
        compiler_params=_cparams(VMEM_BIG_MB))(q, k, v, q_t, do, do_t, lse_cols, d_cols)


DIL_TQ = 1024
DIL_SQ = 128
DIL_SW = DIL_SQ + 2 * DIL_HALF


def _dil_window(a_sub, ld):
    return pl.multiple_of(jnp.clip(a_sub - DIL_HALF, 0, ld - DIL_SW), DIL_HALF)


def _dil_band_mask(shift, heads):
    kidx = lax.broadcasted_iota(jnp.int32, (DIL_SW, heads * DIL_SQ), 0)
    qidx = lax.broadcasted_iota(jnp.int32, (DIL_SW, heads * DIL_SQ), 1) % DIL_SQ
    return jnp.abs(shift + kidx - qidx) <= DIL_HALF


def _pair_rows(x, first):
    zero = jnp.zeros_like(x)
    return jnp.concatenate([jnp.where(first, x, zero), jnp.where(first, zero, x)], axis=0)


def _dil_fwd(q, kp, vp_t4, name):
    d, ld, _ = q.shape
    assert ld % DIL_SQ == 0
    tq = min(DIL_TQ, ld)
    nq = ld // tq
    npair = HEAD_W // LANES
    nb = (ld + 2 * DIL_HALF) // LANES

    def body(q_ref, k_ref, vt_ref, o_ref, lse_ref):
        i = pl.program_id(2)
        first = lax.broadcasted_iota(jnp.int32, (DIL_SQ, LANES), 1) < DIL_HD
        kidx = lax.broadcasted_iota(jnp.int32, (DIL_SW, DIL_SQ), 0)
        for u in range(tq // DIL_SQ):
            a_sub = pl.multiple_of(i * tq + u * DIL_SQ, DIL_SQ)
            kk = a_sub // LANES
            rows = slice(u * DIL_SQ, (u + 1) * DIL_SQ)
            kwin = k_ref[pl.ds(a_sub, DIL_SW), :]
            valid = _dil_band_mask(-DIL_HALF, 1) & (kidx >= DIL_HALF - a_sub) & (kidx < ld + DIL_HALF - a_sub)
            qv = q_ref[rows, :]
            outs, lses = [], []
            for hh in range(2):
                qh = jnp.where(first if hh == 0 else ~first, qv, jnp.zeros_like(qv))
                s_t = jnp.where(valid, lax.dot_general(kwin, qh, NT, preferred_element_type=F32), NEG)
                m = jnp.max(s_t, axis=0, keepdims=True)
                p32 = jnp.exp2(s_t - m)
                l = jnp.sum(p32, axis=0, keepdims=True)
                p_t = p32.astype(BF16)
                hrows = slice(hh * DIL_HD, (hh + 1) * DIL_HD)
                pv = (jnp.dot(vt_ref[kk, hrows, :], p_t[0:LANES, :], preferred_element_type=F32)
                      + jnp.dot(vt_ref[kk + 1, hrows, :], p_t[LANES:DIL_SW, :], preferred_element_type=F32))
                outs.append(pv / l)
                lses.append(jnp.broadcast_to(m + jnp.log2(l), (DIL_HD, DIL_SQ)))
            o_ref[rows, :] = jnp.concatenate(outs, axis=0).T
            lse_ref[rows, :] = jnp.concatenate(lses, axis=0).T

    blk = pl.BlockSpec((None, tq, LANES), lambda r, pr, i: (r, i, pr))
    full = pl.BlockSpec((None, ld + 2 * DIL_HALF, LANES), lambda r, pr, i: (r, 0, pr))
    vspec = pl.BlockSpec((None, nb, LANES, LANES), lambda r, pr, i: (r, 0, pr, 0))
    return pl.pallas_call(
        body, name=name, grid=(d, npair, nq), in_specs=[blk, full, vspec], out_specs=[blk, blk],
        out_shape=[jax.ShapeDtypeStruct((d, ld, HEAD_W), F32), jax.ShapeDtypeStruct((d, ld, HEAD_W), F32)],
        compiler_params=_cparams(VMEM_BIG_MB))(q, kp, vp_t4)


def _dil_bwd(q, k, v, do, lse_rows, d_rows, name):
    d, ld, _ = q.shape
    assert ld % DIL_SQ == 0 and ld >= DIL_SW
    tq = min(DIL_TQ, ld)
    nq = ld // tq
    npair = HEAD_W // LANES
    span = min(tq + 2 * DIL_HALF, ld)

    def body(q_ref, k_ref, v_ref, do_ref, lse_ref, d_ref, dq_ref, dk_ref, dv_ref, dk_sc, dv_sc, dk_acc, dv_acc):
        i = pl.program_id(2)

        @pl.when(i == 0)
        def _():
            dk_acc[...] = jnp.zeros(dk_acc.shape, F32)
            dv_acc[...] = jnp.zeros(dv_acc.shape, F32)

        dk_sc[...] = jnp.zeros(dk_sc.shape, F32)
        dv_sc[...] = jnp.zeros(dv_sc.shape, F32)
        first = lax.broadcasted_iota(jnp.int32, (DIL_SQ, LANES), 1) < DIL_HD
        base = pl.multiple_of(jnp.clip(i * tq - DIL_HALF, 0, ld - span), DIL_HALF)
        for u in range(tq // DIL_SQ):
            a_sub = i * tq + u * DIL_SQ
            ws = _dil_window(a_sub, ld)
            rows = slice(u * DIL_SQ, (u + 1) * DIL_SQ)
            win = pl.ds(pl.multiple_of(ws - base, DIL_HALF), DIL_SW)
            kwin = k_ref[pl.ds(ws, DIL_SW), :]
            vwin = v_ref[pl.ds(ws, DIL_SW), :]
            q2 = _pair_rows(q_ref[rows, :], first)
            do2 = _pair_rows(do_ref[rows, :], first)
            lse2 = jnp.concatenate([lse_ref[0:1, rows], lse_ref[1:2, rows]], axis=1)
            dd2 = jnp.concatenate([d_ref[0:1, rows], d_ref[1:2, rows]], axis=1)
            s_t = lax.dot_general(kwin, q2, NT, preferred_element_type=F32)
            p_t = jnp.exp2(jnp.where(_dil_band_mask(ws - a_sub, 2), s_t, NEG) - lse2)
            dv_sc[win, :] += jnp.dot(p_t.astype(BF16), do2, preferred_element_type=F32)
            dp_t = lax.dot_general(vwin, do2, NT, preferred_element_type=F32)
            ds_t = (p_t * (dp_t - dd2)).astype(BF16)
            dk_sc[win, :] += jnp.dot(ds_t, q2, preferred_element_type=F32)
            dq2 = lax.dot_general(ds_t, kwin, TN, preferred_element_type=F32)
            dq_ref[rows, :] = jnp.where(first, dq2[0:DIL_SQ, :], dq2[DIL_SQ:2 * DIL_SQ, :]).astype(BF16)
        dk_acc[pl.ds(base, span), :] += dk_sc[...]
        dv_acc[pl.ds(base, span), :] += dv_sc[...]

        @pl.when(i == nq - 1)
        def _():
            dk_ref[...] = dk_acc[...].astype(BF16)
            dv_ref[...] = dv_acc[...].astype(BF16)

    blk = pl.BlockSpec((None, tq, LANES), lambda r, pr, i: (r, i, pr))
    full = pl.BlockSpec((None, ld, LANES), lambda r, pr, i: (r, 0, pr))
    rowspec = pl.BlockSpec((None, None, 2, tq), lambda r, pr, i: (r, pr, 0, i))
    return pl.pallas_call(
        body, name=name, grid=(d, npair, nq), in_specs=[blk, full, full, blk, rowspec, rowspec],
        out_specs=[blk, full, full],
        out_shape=[jax.ShapeDtypeStruct((d, ld, HEAD_W), BF16)] * 3,
        scratch_shapes=[pltpu.VMEM((span, LANES), F32), pltpu.VMEM((span, LANES), F32),
                        pltpu.VMEM((ld, LANES), F32), pltpu.VMEM((ld, LANES), F32)],
        compiler_params=_cparams(VMEM_BIG_MB))(q, k, v, do, lse_rows, d_rows)


TILE_BYTES = 1 << 21


def _row_tile(rows, cols, budget=TILE_BYTES):
    for parts in range(1, rows + 1):
        tr = rows // parts
        if rows % parts == 0 and tr % 8 == 0 and tr * cols * 4 <= budget:
            return tr
    return rows


def _add2(a, b, name, out_dtype):
    n, rows, cols = a.shape
    tr = _row_tile(rows, cols)

    def body(a_ref, b_ref, o_ref):
        o_ref[...] = (a_ref[...] + b_ref[...]).astype(out_dtype)

    spec = pl.BlockSpec((None, tr, cols), lambda t, i: (t, i, 0))
    return pl.pallas_call(body, name=name, grid=(n, rows // tr), in_specs=[spec, spec], out_specs=spec,
                          out_shape=jax.ShapeDtypeStruct(a.shape, out_dtype))(a, b)


def _add4_ordered(a, name):
    _, rows, cols = a.shape
    tr = _row_tile(rows, cols, TILE_BYTES // 4)

    def body(a_ref, o_ref):
        o_ref[...] = ((a_ref[0].astype(F32) + a_ref[1].astype(F32)) + a_ref[2].astype(F32)) + a_ref[3].astype(F32)

    return pl.pallas_call(
        body, name=name, grid=(rows // tr,), in_specs=[pl.BlockSpec((4, tr, cols), lambda i: (0, i, 0))],
        out_specs=pl.BlockSpec((tr, cols), lambda i: (i, 0)),
        out_shape=jax.ShapeDtypeStruct((rows, cols), F32))(a)


def _adamw(w, g, m, v, name):
    rows, cols = w.shape
    tr = _row_tile(rows, cols)
    bc1 = 1.0 - ADAM_B1 ** ADAM_STEP
    bc2 = 1.0 - ADAM_B2 ** ADAM_STEP

    def body(w_ref, g_ref, m_ref, v_ref, d_ref, nm_ref, nv_ref):
        gv = g_ref[...]
        nm = ADAM_B1 * m_ref[...] + (1.0 - ADAM_B1) * gv
        nv = ADAM_B2 * v_ref[...] + (1.0 - ADAM_B2) * (gv * gv)
        d_ref[...] = -ADAM_LR * ((nm / bc1) / (jnp.sqrt(nv / bc2) + ADAM_EPS) + ADAM_WD * w_ref[...])
        nm_ref[...] = nm
        nv_ref[...] = nv

    spec = pl.BlockSpec((tr, cols), lambda i: (i, 0))
    return pl.pallas_call(body, name=name, grid=(rows // tr,), in_specs=[spec] * 4, out_specs=[spec] * 3,
                          out_shape=[jax.ShapeDtypeStruct(w.shape, F32)] * 3,
                          compiler_params=_cparams(VMEM_BIG_MB))(w, g, m, v)


ANY = pl.BlockSpec(memory_space=pl.ANY)


def _place():
    return lax.axis_index("x"), lax.axis_index("y"), lax.axis_index("c")


def _rcopy(send_sems, recv_sems, n, src, dst, to):
    return pltpu.make_async_remote_copy(src_ref=src, dst_ref=dst, send_sem=send_sems.at[n], recv_sem=recv_sems.at[n],
                                        device_id=to, device_id_type=MESH)


def _allgather_weights(shards):
    na = len(shards)
    ns = 7

    def body(*refs):
        w_refs, g_refs = refs[:na], refs[na:2 * na]
        send_sems, recv_sems, local_sems = refs[2 * na:]
        x, y, c = _place()
        s, sx, sy, sd = 2 * x + y, 2 * (1 - x) + y, 2 * x + (1 - y), 2 * (1 - x) + (1 - y)
        to_x, to_y, sib = (1 - x, y, c), (x, 1 - y, c), (x, y, 1 - c)

        def part(a, shard, h, k=None):
            hr = shards[a].shape[0] // 2
            if k is None:
                return g_refs[a].at[shard, pl.ds(h * hr, hr), :]
            return g_refs[a].at[shard, pl.ds(h * hr + k * (hr // 2), hr // 2), :]

        def cp(a, n, src, dst, to):
            return _rcopy(send_sems, recv_sems, ns * a + n, src, dst, to)

        started, sends = [], []

        def go(copy):
            copy.start()
            sends.append(copy)

        for a in range(na):
            hr = shards[a].shape[0] // 2
            mine = pltpu.make_async_copy(w_refs[a], g_refs[a].at[s], local_sems.at[a])
            mine.start()
            started.append(mine)
            own = w_refs[a].at[pl.ds(c * hr, hr), :]
            go(cp(a, 0, own, part(a, s, c), to_x))
            go(cp(a, 1, own, part(a, s, c), to_y))
        for a in range(na):
            cp(a, 0, part(a, sx, c), part(a, sx, c), to_x).wait_recv()
            go(cp(a, 2, part(a, sx, c, 0), part(a, sx, c, 0), to_y))
            go(cp(a, 4, part(a, sx, c), part(a, sx, c), sib))
            cp(a, 1, part(a, sy, c), part(a, sy, c), to_y).wait_recv()
            go(cp(a, 3, part(a, sy, c, 1), part(a, sy, c, 1), to_x))
            go(cp(a, 5, part(a, sy, c), part(a, sy, c), sib))
        for a in range(na):
            cp(a, 2, part(a, sd, c, 0), part(a, sd, c, 0), to_y).wait_recv()
            cp(a, 3, part(a, sd, c, 1), part(a, sd, c, 1), to_x).wait_recv()
            go(cp(a, 6, part(a, sd, c), part(a, sd, c), sib))
        for a in range(na):
            for n, sj in ((4, sx), (5, sy), (6, sd)):
                cp(a, n, part(a, sj, 1 - c), part(a, sj, 1 - c), sib).wait_recv()
        for copy in sends:
            copy.wait_send()
        for mine in started:
            mine.wait()

    return pl.pallas_call(
        body, name="allgather_weights", in_specs=[ANY] * na, out_specs=[ANY] * na,
        out_shape=[jax.ShapeDtypeStruct((N_SHARD,) + t.shape, t.dtype) for t in shards],
        scratch_shapes=[pltpu.SemaphoreType.DMA((ns * na,)), pltpu.SemaphoreType.DMA((ns * na,)),
                        pltpu.SemaphoreType.DMA((na,))])(*shards)


def _sibling_send_halves(gs):
    na = len(gs)

    def body(*refs):
        g_refs, o_refs = refs[:na], refs[na:2 * na]
        send_sems, recv_sems = refs[2 * na:]
        x, y, c = _place()
        cps = []
        for a in range(na):
            for t in range(N_SHARD):
                cp = _rcopy(send_sems, recv_sems, N_SHARD * a + t, g_refs[a].at[t, 1 - c], o_refs[a].at[t],
                            (x, y, 1 - c))
                cp.start()
                cps.append(cp)
        for cp in cps:
            cp.wait()

    return pl.pallas_call(
        body, name="grad_sibling_exchange", in_specs=[ANY] * na, out_specs=[ANY] * na,
        out_shape=[jax.ShapeDtypeStruct((N_SHARD,) + g.shape[2:], g.dtype) for g in gs],
        scratch_shapes=[pltpu.SemaphoreType.DMA((N_SHARD * na,)), pltpu.SemaphoreType.DMA((N_SHARD * na,))])(*gs)


def _chip_scatter(parts):
    na = len(parts)

    def body(*refs):
        a_refs, o_refs = refs[:na], refs[na:2 * na]
        send_sems, recv_sems, local_sems = refs[2 * na:]
        x, y, c = _place()
        s = 2 * x + y
        chips = [(1 - x, y), (x, 1 - y), (1 - x, 1 - y)]
        started, cps = [], []
        for a in range(na):
            mine = pltpu.make_async_copy(a_refs[a].at[s], o_refs[a].at[s], local_sems.at[a])
            mine.start()
            started.append(mine)
            for n, (cx, cy) in enumerate(chips):
                cp = _rcopy(send_sems, recv_sems, 3 * a + n, a_refs[a].at[2 * cx + cy], o_refs[a].at[s], (cx, cy, c))
                cp.start()
                cps.append(cp)
        for a in range(na):
            for n, (cx, cy) in enumerate(chips):
                sj = 2 * cx + cy
                _rcopy(send_sems, recv_sems, 3 * a + n, a_refs[a].at[sj], o_refs[a].at[sj], (cx, cy, c)).wait_recv()
        for cp in cps:
            cp.wait_send()
        for mine in started:
            mine.wait()

    return pl.pallas_call(
        body, name="grad_chip_scatter", in_specs=[ANY] * na, out_specs=[ANY] * na,
        out_shape=[jax.ShapeDtypeStruct(t.shape, t.dtype) for t in parts],
        scratch_shapes=[pltpu.SemaphoreType.DMA((3 * na,)), pltpu.SemaphoreType.DMA((3 * na,)),
                        pltpu.SemaphoreType.DMA((na,))])(*parts)


def _sibling_swap(rs):
    na = len(rs)

    def body(*refs):
        r_refs, o_refs = refs[:na], refs[na:2 * na]
        send_sems, recv_sems = refs[2 * na:]
        x, y, c = _place()
        cps = []
        for a in range(na):
            cp = _rcopy(send_sems, recv_sems, a, r_refs[a], o_refs[a], (x, y, 1 - c))
            cp.start()
            cps.append(cp)
        for cp in cps:
            cp.wait()

    return pl.pallas_call(
        body, name="grad_sibling_swap", in_specs=[ANY] * na, out_specs=[ANY] * na,
        out_shape=[jax.ShapeDtypeStruct(t.shape, t.dtype) for t in rs],
        scratch_shapes=[pltpu.SemaphoreType.DMA((na,)), pltpu.SemaphoreType.DMA((na,))])(*rs)


def _pack_small(norm_g, q_norm_g, kv_norm_g, final_g):
    flat = jnp.concatenate([norm_g.reshape(-1), q_norm_g.reshape(-1), kv_norm_g.reshape(-1), final_g.reshape(-1),
                            jnp.zeros((SMALL_ROWS * LANES - N_SMALL,), F32)])
    return flat.reshape(SMALL_ROWS, LANES)


def _split_small(s):
    s = s.reshape(-1)
    o = 0
    out = []
    for n, shape in ((DEPTH * D_MODEL, (DEPTH, D_MODEL)), (DEPTH * Q_LORA, (DEPTH, Q_LORA)),
                     (DEPTH * KV_LORA, (DEPTH, KV_LORA)), (D_MODEL, (D_MODEL,))):
        out.append(s[o:o + n].reshape(shape))
        o += n
    return out


def _assemble_w_in(sh):
    z = lambda n: jnp.zeros(sh.shape[1:3] + (n,), sh.dtype)
    s0, s1, s2, s3 = sh[0], sh[1], sh[2], sh[3]
    return jnp.concatenate([s0[..., 0:640], z(64), s0[..., 640:672], z(32), z(256), s0[..., 672:1184],
                            s3[..., 1064:1576], s0[..., 1184:1576], s1, s2, s3[..., 0:1064]], axis=-1)


def _split_w_in_grad(parts):
    def shard(s, a, b):
        if s == 0:
            return jnp.concatenate([a[:, 0:640], a[:, 704:736], a[:, 1024:1536], b[:, 0:392]], axis=1)
        if s == 3:
            return jnp.concatenate([b[:, 3544:4608], a[:, 1536:2048]], axis=1)
        return b[:, 392 + (s - 1) * SHARD_COLS_IN:392 + s * SHARD_COLS_IN]

    rows = jnp.concatenate([shard(s, a, b) for s in range(N_SHARD) for a, b in parts], axis=0)
    return rows.reshape(N_SHARD, DEPTH * D_MODEL, SHARD_COLS_IN)


def _col_shards(w):
    dl, r, cc = w.shape
    return w.reshape(dl, r, N_SHARD, cc // N_SHARD).transpose(2, 0, 1, 3).reshape(N_SHARD, dl * r, cc // N_SHARD)


def _from_col_shards(g, rows):
    cc = g.shape[-1]
    return g.reshape(N_SHARD, DEPTH, rows, cc).transpose(1, 2, 0, 3).reshape(DEPTH, rows, N_SHARD * cc)


def _pad_w_in(w):
    z = lambda n: jnp.zeros(w.shape[:-1] + (n,), w.dtype)
    return jnp.concatenate([w[..., 0:640], z(64), w[..., 640:672], z(32), z(256), w[..., 672:1184],
                            w[..., 5792:6304], w[..., 1184:5792]], axis=-1)


def _unpad_w_in(w):
    return jnp.concatenate([w[..., 0:640], w[..., 704:736], w[..., 1024:1536], w[..., 2048:6656],
                            w[..., 1536:2048]], axis=-1)


def _pad_w_uq(w):
    s = w.shape[:-1]
    w = w.reshape(s + (MLA_HEADS, 96))
    return jnp.pad(w, [(0, 0)] * (w.ndim - 1) + [(0, 32)]).reshape(s + (1024,))


def _unpad_w_uq(w):
    s = w.shape[:-1]
    return w.reshape(s + (MLA_HEADS, LANES))[..., :96].reshape(s + (768,))


def _pad_w_ukv(w):
    s = w.shape[:-1]
    w = w.reshape(s + (MLA_HEADS, 128))
    kpart = jnp.pad(w[..., :64], [(0, 0)] * (w.ndim - 1) + [(0, 64)]).reshape(s + (1024,))
    vpart = w[..., 64:].reshape(s + (512,))
    return jnp.concatenate([kpart, vpart], axis=-1)


def _unpad_w_ukv(w):
    s = w.shape[:-1]
    kpart = w[..., :1024].reshape(s + (MLA_HEADS, LANES))[..., :64]
    vpart = w[..., 1024:].reshape(s + (MLA_HEADS, 64))
    return jnp.concatenate([kpart, vpart], axis=-1).reshape(s + (1024,))


def _rope_tables(L, dim, lane_lo, period):
    half = dim // 2
    inv = 1.0 / (ROPE_THETA ** (jnp.arange(0, dim, 2, dtype=F32) / dim))
    ang = jnp.arange(L, dtype=F32)[:, None] * inv[None, :]
    cos, sin = jnp.cos(ang), jnp.sin(ang)
    one = lambda n: jnp.ones((L, n), F32)
    zero = lambda n: jnp.zeros((L, n), F32)
    rest = period - lane_lo - dim
    rep = LANES // period
    c = jnp.tile(jnp.concatenate([one(lane_lo), cos, cos, one(rest)], axis=1), (1, rep))
    a = jnp.tile(jnp.concatenate([zero(lane_lo), -sin, zero(half), zero(rest)], axis=1), (1, rep))
    b = jnp.tile(jnp.concatenate([zero(lane_lo + half), sin, zero(rest)], axis=1), (1, rep))
    return c, a, b


def _to_strided(t, d):
    L, w = t.shape
    return t.reshape(L // d, d, w).transpose(1, 0, 2)


def _from_strided(t):
    d, ld, w = t.shape
    return t.transpose(1, 0, 2).reshape(d * ld, w)


def _head_rows(t):
    return t.T.reshape(MLA_HEADS // 2, 2, t.shape[0])


def _head_rows_strided(t, d):
    s = _to_strided(t, d)
    return s.transpose(0, 2, 1).reshape(d, MLA_HEADS // 2, 2, s.shape[1])

def _local_grads(x, target, norm_g, w_in_p, q_norm_g, kv_norm_g, w_uq_p, w_ukv_p, w_out, final_g):
    L = x.shape[0]
    tabs_m = _rope_tables(L, MLA_ROPE, MLA_NOPE, LANES)
    tabs_d = _rope_tables(L, ROT_DIM, 0, DIL_HD)
    tabs_m_t = (tabs_m[0], -tabs_m[1], -tabs_m[2])
    tabs_d_t = (tabs_d[0], -tabs_d[1], -tabs_d[2])
    w_in_t = jnp.swapaxes(w_in_p, 1, 2)
    w_uq_t = jnp.swapaxes(w_uq_p, 1, 2)
    w_ukv_t = jnp.swapaxes(w_ukv_p, 1, 2)
    w_out_t = jnp.swapaxes(w_out, 1, 2)

    saved = []
    for l in range(DEPTH):
        h = _rms_fwd(x, norm_g[l:l + 1], "rms_fwd")
        p = _mm(h, w_in_p[l], tm=1024, tn=3328, tk=1024, out_dtype=BF16, name="in_proj")
        q, k, v, cqn, ckvn = _mla_prep(p, q_norm_g[l:l + 1], kv_norm_g[l:l + 1], w_uq_p[l], w_ukv_p[l], tabs_m,
                                       "mla_prep")
        oa, lse_a = _mla_fwd(q, k, v.T, "mla_fwd")
        dil = _dil_prep(p, tabs_d, "dil_prep")
        dil_s, o_g, lse_g = [], [], []
        for g, (_, dd) in enumerate(DIL_PAIRS):
            qs, ks, vs = (_to_strided(t, dd) for t in dil[3 * g:3 * g + 3])
            pad = ((0, 0), (DIL_HALF, DIL_HALF), (0, 0))
            vp = jnp.pad(vs, pad)
            v_t4 = vp.reshape(dd, vp.shape[1] // LANES, LANES, HEAD_W).transpose(0, 1, 3, 2)
            og, lg = _dil_fwd(qs, jnp.pad(ks, pad), v_t4, "dil_fwd_%d" % dd)
            dil_s.append((qs, ks, vs))
            o_g.append(_from_strided(og))
            lse_g.append(_from_strided(lg))
        ab, bm, lt = _merge_gate(oa, p, o_g, lse_g, "merge_gate")
        x_next = _mm(ab, w_out[l], tm=1024, tn=1024, tk=1024, out_dtype=F32, name="out_proj", add=x)
        saved.append((x, h, p, q, k, v, cqn, ckvn, oa, lse_a, dil_s, bm, lt, ab))
        x = x_next

    loss_b, dx, d_final = _loss_head(x, final_g[None, :], target, "loss_head")
    loss = loss_b[0, 0]

    d_norm, d_qn, d_kvn, d_win, d_wuq, d_wukv, d_wout = [], [], [], [], [], [], []
    for l in reversed(range(DEPTH)):
        x_l, h, p, q, k, v, cqn, ckvn, oa, lse_a, dil_s, bm, lt, ab = saved[l]
        dab = _mm(dx, w_out_t[l], tm=1024, tn=1024, tk=1024, out_dtype=F32, name="out_proj_dgrad")
        d_wout.append(_mm(ab, dx, tm=1024, tn=1024, tk=1024, out_dtype=F32, name="out_proj_wgrad", a_is_kxm=True))
        doa, dbm, D_a, D_b, dgates = _gate_bwd(dab, p, oa, bm, "gate_bwd")
        dq, dk_t, dv_t = _mla_bwd(q, k, v, q.T, doa, doa.T, lse_a.transpose(0, 2, 1),
                                  D_a.reshape(L, MLA_HEADS // 2, 2).transpose(1, 0, 2), "mla_bwd")
        dk, dv = dk_t.T, dv_t.T
        dp_mla, dq_pre, dkv, dqg, dkvg = _mla_prep_bwd(dq, dk, dv, p, q_norm_g[l:l + 1], kv_norm_g[l:l + 1],
                                                       w_uq_t[l], w_ukv_t[l], tabs_m_t, "mla_prep_bwd")
        d_wuq.append(_mm(cqn, dq_pre, tm=Q_LORA, tn=1024, tk=2048, out_dtype=F32, name="w_uq_wgrad", a_is_kxm=True))
        d_wukv.append(_mm(ckvn, dkv, tm=KV_LORA, tn=1536, tk=2048, out_dtype=F32, name="w_ukv_wgrad", a_is_kxm=True))
        dgr = []
        for g, (_, dd) in enumerate(DIL_PAIRS):
            qs, ks, vs = dil_s[g]
            dqs, dks, dvs = _dil_bwd(qs, ks, vs, _to_strided(dbm, dd), _head_rows_strided(lt, dd),
                                     _head_rows_strided(D_b, dd), "dil_bwd_%d" % dd)
            dgr += [_from_strided(dqs), _from_strided(dks), _from_strided(dvs)]
        dp_dil = _dil_prep_bwd(dgr, tabs_d_t, "dil_prep_bwd")
        dp_a = jnp.concatenate([dp_mla, dgates], axis=1)
        dh = _mm(dp_a, w_in_t[l][0:P_DIL0], tm=1024, tn=1024, tk=2048, out_dtype=F32, name="in_proj_dgrad_a")
        dh = _mm(dp_dil, w_in_t[l][P_DIL0:], tm=512, tn=1024, tk=2304, out_dtype=F32, name="in_proj_dgrad_b",
                 add=dh)
        d_win.append((_mm(h, dp_a, tm=512, tn=2048, tk=1024, out_dtype=F32, name="in_proj_wgrad_a", a_is_kxm=True),
                      _mm(h, dp_dil, tm=512, tn=1536, tk=2048, out_dtype=F32, name="in_proj_wgrad_b",
                          a_is_kxm=True)))
        dx, dng = _rms_bwd(dh, x_l, norm_g[l:l + 1], dx, "rms_bwd")
        d_norm.append(dng[0])
        d_qn.append(dqg[0])
        d_kvn.append(dkvg[0])

    rev = lambda xs: jnp.stack(xs[::-1])
    return (loss, dx, rev(d_norm), d_win[::-1], rev(d_qn), rev(d_kvn), rev(d_wuq), rev(d_wukv), rev(d_wout),
            d_final[0])


def kernel(x, norm_g, w_in, q_norm_g, kv_norm_g, w_uq, w_ukv, w_out, final_g, loss_target, m_norm_g, m_w_in, m_q_norm_g, m_kv_norm_g, m_w_uq, m_w_ukv, m_w_out, m_final_g, v_norm_g, v_w_in, v_q_norm_g, v_kv_norm_g, v_w_uq, v_w_ukv, v_w_out, v_final_g):
    c = lax.axis_index("c")

    def families(a_in, a_uq, a_ukv, a_out):
        return [t.reshape(shape) for t, shape in zip((a_in, a_uq, a_ukv, a_out), FAM_SHAPES)]

    g_in, g_uq, g_ukv, g_out = _allgather_weights([t.astype(BF16) for t in families(w_in, w_uq, w_ukv, w_out)])
    w_in_p = _assemble_w_in(g_in.reshape(N_SHARD, DEPTH, D_MODEL, SHARD_COLS_IN))
    w_uq_p = _pad_w_uq(_from_col_shards(g_uq, Q_LORA))
    w_ukv_p = _pad_w_ukv(_from_col_shards(g_ukv, KV_LORA))
    w_out_f = g_out.reshape(N_SHARD, DEPTH, 1024 // N_SHARD, D_MODEL).transpose(1, 0, 2, 3).reshape(DEPTH, 1024, D_MODEL)

    (loss, dx, d_norm, d_win_p, d_qn, d_kvn, d_wuq_p, d_wukv_p, d_wout, d_final) = _local_grads(
        x[0], loss_target[0], norm_g, w_in_p, q_norm_g, kv_norm_g, w_uq_p, w_ukv_p, w_out_f, final_g)
    loss = lax.psum(loss, ("x", "y", "c"))

    small = _pack_small(d_norm, d_qn, d_kvn, d_final)
    grads = [_split_w_in_grad(d_win_p), _col_shards(_unpad_w_uq(d_wuq_p)), _col_shards(_unpad_w_ukv(d_wukv_p)),
             d_wout.reshape(DEPTH, N_SHARD, 1024 // N_SHARD, D_MODEL).transpose(1, 0, 2, 3).reshape(
                 N_SHARD, DEPTH * (1024 // N_SHARD), D_MODEL),
             jnp.broadcast_to(small[None], (N_SHARD, SMALL_ROWS, LANES))]
    halves = [g.reshape(N_SHARD, 2, g.shape[1] // 2, g.shape[2]) for g in grads]
    from_sib = _sibling_send_halves(halves)
    chip_sum = [_add2(lax.dynamic_index_in_dim(h, c, axis=1, keepdims=False), f, "grad_add_pair", BF16)
                for h, f in zip(halves, from_sib)]
    red_half = [_add4_ordered(t, "grad_add_chips") for t in _chip_scatter(chip_sum)]
    other_half = _sibling_swap(red_half)
    gred = []
    for mine, other in zip(red_half, other_half):
        both = jnp.stack([mine, other])
        gred.append(jnp.concatenate([lax.dynamic_index_in_dim(both, c, axis=0, keepdims=False),
                                     lax.dynamic_index_in_dim(both, 1 - c, axis=0, keepdims=False)], axis=0))

    wf = families(w_in, w_uq, w_ukv, w_out) + [_pack_small(norm_g, q_norm_g, kv_norm_g, final_g)]
    mf = families(m_w_in, m_w_uq, m_w_ukv, m_w_out) + [_pack_small(m_norm_g, m_q_norm_g, m_kv_norm_g, m_final_g)]
    vf = families(v_w_in, v_w_uq, v_w_ukv, v_w_out) + [_pack_small(v_norm_g, v_q_norm_g, v_kv_norm_g, v_final_g)]
    upd = [_adamw(w, g, m, v, "adamw") for w, g, m, v in zip(wf, gred, mf, vf)]

    def leaves(fams):
        a_in, a_uq, a_ukv, a_out, s = fams
        s_norm, s_qn, s_kvn, s_final = _split_small(s)
        return [s_norm, a_in.reshape(w_in.shape), s_qn, s_kvn, a_uq.reshape(w_uq.shape), a_ukv.reshape(w_ukv.shape),
                a_out.reshape(w_out.shape), s_final]

    return (loss, dx[None], *leaves(gred), *leaves([u[0] for u in upd]), *leaves([u[1] for u in upd]),
            *leaves([u[2] for u in upd]))
```

```python
import functools

import jax
import jax.numpy as jnp
from jax import lax
from jax.experimental import pallas as pl
from jax.experimental.pallas import tpu as pltpu

F32 = jnp.float32
BF16 = jnp.bfloat16
MESH = pl.DeviceIdType.MESH

D_MODEL = 1024
DEPTH = 4
MLA_HEADS = 8
MLA_NOPE = 64
MLA_ROPE = 32
Q_LORA = 384
KV_LORA = 256
DIL_PAIRS = ((128, 1), (512, 4), (2048, 16))
DIL_HD = 64
DIL_HALF = 64
ROT_DIM = 16
ROPE_THETA = 500000.0
EPS = 1e-6
IN_WIDTH = 6304
N_SHARD = 4

P_WIDTH = 6656
P_MLA = 1024
P_GATE = 1024
P_DIL0 = 2048
LANES = 128
HEAD_W = 512

ADAM_LR = 0.001
ADAM_B1 = 0.9
ADAM_B2 = 0.999
ADAM_EPS = 1e-08
ADAM_WD = 0.01
ADAM_STEP = 10

SHARD_COLS_IN = IN_WIDTH // N_SHARD
FAM_SHAPES = ((DEPTH * D_MODEL, SHARD_COLS_IN), (DEPTH * Q_LORA, 768 // N_SHARD), (DEPTH * KV_LORA, 1024 // N_SHARD),
              (DEPTH * (1024 // N_SHARD), D_MODEL))
N_SMALL = DEPTH * (D_MODEL + Q_LORA + KV_LORA) + D_MODEL
SMALL_ROWS = 64
VMEM_BIG_MB = 48


def _cparams(vmem_mb=None):
    if vmem_mb is None:
        return None
    return pltpu.CompilerParams(vmem_limit_bytes=vmem_mb << 20)


def _sigmoid(x):
    return 1.0 / (1.0 + jnp.exp(-x))


def _rope(x, c, a, b, sh):
    return x * c + pltpu.roll(x, LANES - sh, 1) * a + pltpu.roll(x, sh, 1) * b


def _per_head8(x, pick_first):
    r = lax.broadcasted_iota(jnp.int32, (HEAD_W, MLA_HEADS), 0)
    c = lax.broadcasted_iota(jnp.int32, (HEAD_W, MLA_HEADS), 1)
    sel = (r == c * DIL_HD) if pick_first else (r // DIL_HD == c)
    mat = jnp.where(sel, 1.0, 0.0).astype(BF16)
    out = jnp.zeros((x.shape[0], MLA_HEADS), F32)
    for _ in range(3):
        part = x.astype(BF16)
        out = out + jnp.dot(part, mat, preferred_element_type=F32)
        x = x - part.astype(F32)
    return out


def _mm(a, b, *, tm, tn, tk, out_dtype, name, add=None, a_is_kxm=False):
    K, M = a.shape if a_is_kxm else a.shape[::-1]
    N = b.shape[1]
    tm, tn, tk = min(tm, M), min(tn, N), min(tk, K)
    assert M % tm == 0 and N % tn == 0 and K % tk == 0, (a.shape, b.shape)
    nk = K // tk
    has_add = add is not None

    def body(*refs):
        a_ref, b_ref = refs[0], refs[1]
        add_ref = refs[2] if has_add else None
        o_ref = refs[3] if has_add else refs[2]
        k = pl.program_id(2)
        if a_is_kxm:
            part = lax.dot_general(a_ref[...].astype(BF16), b_ref[...].astype(BF16), (((0,), (0,)), ((), ())),
                                   preferred_element_type=F32)
        else:
            part = jnp.dot(a_ref[...].astype(BF16), b_ref[...].astype(BF16), preferred_element_type=F32)
        if nk == 1:
            o_ref[...] = (part + add_ref[...] if has_add else part).astype(out_dtype)
            return
        acc = refs[-1]

        @pl.when(k == 0)
        def _():
            acc[...] = part

        @pl.when(k > 0)
        def _():
            acc[...] += part

        @pl.when(k == nk - 1)
        def _():
            r = acc[...]
            if has_add:
                r = r + add_ref[...]
            o_ref[...] = r.astype(out_dtype)

    a_spec = pl.BlockSpec((tk, tm), lambda i, j, k: (k, i)) if a_is_kxm else pl.BlockSpec((tm, tk), lambda i, j, k: (i, k))
    in_specs = [a_spec, pl.BlockSpec((tk, tn), lambda i, j, k: (k, j))]
    args = [a, b]
    if has_add:
        in_specs.append(pl.BlockSpec((tm, tn), lambda i, j, k: (i, j)))
        args.append(add)
    return pl.pallas_call(
        body, name=name, grid=(M // tm, N // tn, nk), in_specs=in_specs,
        out_specs=pl.BlockSpec((tm, tn), lambda i, j, k: (i, j)),
        out_shape=jax.ShapeDtypeStruct((M, N), out_dtype),
        scratch_shapes=[pltpu.VMEM((tm, tn), F32)] if nk > 1 else [],
        compiler_params=_cparams(VMEM_BIG_MB))(*args)


def _row_spec(tm, w, cb=0):
    return pl.BlockSpec((tm, w), lambda i: (i, cb))


def _const_spec(arr):
    nd = arr.ndim
    return pl.BlockSpec(arr.shape, lambda i: (0,) * nd)


def _rms_fwd(x, g, name):
    L, D = x.shape
    tm = min(512, L)

    def body(x_ref, g_ref, o_ref):
        xv = x_ref[...]
        r = lax.rsqrt(jnp.mean(xv * xv, axis=-1, keepdims=True) + EPS)
        o_ref[...] = (xv * r * g_ref[...]).astype(BF16)

    return pl.pallas_call(
        body, name=name, grid=(L // tm,), in_specs=[_row_spec(tm, D), _const_spec(g)],
        out_specs=_row_spec(tm, D), out_shape=jax.ShapeDtypeStruct((L, D), BF16))(x, g)


def _rms_bwd(dh, x, g, dres, name):
    L, D = x.shape
    tm = min(512, L)

    def body(dh_ref, x_ref, g_ref, dres_ref, dx_ref, dg_ref):
        xv = x_ref[...]
        dy = dh_ref[...]
        r = lax.rsqrt(jnp.mean(xv * xv, axis=-1, keepdims=True) + EPS)
        dyg = dy * g_ref[...]
        dx_ref[...] = dres_ref[...] + r * dyg - xv * (r * r * r) * jnp.mean(dyg * xv, axis=-1, keepdims=True)
        part = jnp.sum(dy * xv * r, axis=0, keepdims=True)

        @pl.when(pl.program_id(0) == 0)
        def _():
            dg_ref[...] = part

        @pl.when(pl.program_id(0) > 0)
        def _():
            dg_ref[...] += part

    return pl.pallas_call(
        body, name=name, grid=(L // tm,),
        in_specs=[_row_spec(tm, D), _row_spec(tm, D), _const_spec(g), _row_spec(tm, D)],
        out_specs=[_row_spec(tm, D), pl.BlockSpec((1, D), lambda i: (0, 0))],
        out_shape=[jax.ShapeDtypeStruct((L, D), F32), jax.ShapeDtypeStruct((1, D), F32)])(dh, x, g, dres)


def _loss_head(x, g, target, name):
    L, D = x.shape
    tm = min(512, L)

    def body(x_ref, g_ref, t_ref, loss_ref, dx_ref, dg_ref):
        xv = x_ref[...]
        gv = g_ref[...]
        r = lax.rsqrt(jnp.mean(xv * xv, axis=-1, keepdims=True) + EPS)
        xr = xv * r
        err = xr * gv - t_ref[...]
        lp = 0.5 * jnp.sum(jnp.mean(err * err, axis=-1, keepdims=True))
        dy = err * (1.0 / D)
        dyg = dy * gv
        dx_ref[...] = r * dyg - xv * (r * r * r) * jnp.mean(dyg * xv, axis=-1, keepdims=True)
        part = jnp.sum(dy * xr, axis=0, keepdims=True)

        @pl.when(pl.program_id(0) == 0)
        def _():
            dg_ref[...] = part
            loss_ref[...] = jnp.zeros(loss_ref.shape, F32) + lp

        @pl.when(pl.program_id(0) > 0)
        def _():
            dg_ref[...] += part
            loss_ref[...] += lp

    return pl.pallas_call(
        body, name=name, grid=(L // tm,),
        in_specs=[_row_spec(tm, D), _const_spec(g), _row_spec(tm, D)],
        out_specs=[pl.BlockSpec((8, LANES), lambda i: (0, 0)), _row_spec(tm, D), pl.BlockSpec((1, D), lambda i: (0, 0))],
        out_shape=[jax.ShapeDtypeStruct((8, LANES), F32), jax.ShapeDtypeStruct((L, D), F32),
                   jax.ShapeDtypeStruct((1, D), F32)])(x, g, target)


def _mla_prep(p, qg, kvg, wuq, wukv, tabs, name):
    L = p.shape[0]
    tm = min(512, L)
    scale = (MLA_NOPE + MLA_ROPE) ** -0.5
    tc, ta, tb = tabs

    def body(p_ref, qg_ref, kvg_ref, wuq_ref, wukv_ref, c_ref, a_ref, b_ref, q_ref, k_ref, v_ref, cqn_ref, ckvn_ref):
        c, a, b = c_ref[...], a_ref[...], b_ref[...]
        cq = p_ref[:, 0:Q_LORA].astype(F32)
        ckv = p_ref[:, Q_LORA:Q_LORA + KV_LORA].astype(F32)
        kr = p_ref[:, 640:768].astype(F32)
        cqn = (cq * lax.rsqrt(jnp.mean(cq * cq, axis=-1, keepdims=True) + EPS) * qg_ref[...]).astype(BF16)
        ckvn = (ckv * lax.rsqrt(jnp.mean(ckv * ckv, axis=-1, keepdims=True) + EPS) * kvg_ref[...]).astype(BF16)
        cqn_ref[...] = cqn
        ckvn_ref[...] = ckvn
        q = jnp.dot(cqn, wuq_ref[...], preferred_element_type=F32)
        kv = jnp.dot(ckvn, wukv_ref[...], preferred_element_type=F32)
        krr = _rope(kr, c, a, b, MLA_ROPE // 2)
        for h in range(MLA_HEADS):
            sl = slice(h * LANES, (h + 1) * LANES)
            q_ref[:, sl] = (_rope(q[:, sl], c, a, b, MLA_ROPE // 2) * (scale * LOG2E)).astype(BF16)
            k_ref[:, sl] = (kv[:, sl] + krr).astype(BF16)
        v_ref[...] = kv[:, 1024:1536].astype(BF16)

    return pl.pallas_call(
        body, name=name, grid=(L // tm,),
        in_specs=[_row_spec(tm, P_MLA, 0), _const_spec(qg), _const_spec(kvg), _const_spec(wuq), _const_spec(wukv),
                  _row_spec(tm, LANES), _row_spec(tm, LANES), _row_spec(tm, LANES)],
        out_specs=[_row_spec(tm, 1024), _row_spec(tm, 1024), _row_spec(tm, HEAD_W), _row_spec(tm, Q_LORA),
                   _row_spec(tm, KV_LORA)],
        out_shape=[jax.ShapeDtypeStruct((L, 1024), BF16), jax.ShapeDtypeStruct((L, 1024), BF16),
                   jax.ShapeDtypeStruct((L, HEAD_W), BF16), jax.ShapeDtypeStruct((L, Q_LORA), BF16),
                   jax.ShapeDtypeStruct((L, KV_LORA), BF16)],
        compiler_params=_cparams(VMEM_BIG_MB))(p, qg, kvg, wuq, wukv, tc, ta, tb)


def _mla_prep_bwd(dq, dk, dv, p, qg, kvg, wuq_t, wukv_t, tabs_t, name):
    L = p.shape[0]
    tm = min(512, L)
    scale = (MLA_NOPE + MLA_ROPE) ** -0.5
    tc, ta, tb = tabs_t

    def body(dq_ref, dk_ref, dv_ref, p_ref, qg_ref, kvg_ref, wuqt_ref, wukvt_ref, c_ref, a_ref, b_ref,
             dp_ref, dqp_ref, dkv_ref, dqg_ref, dkvg_ref):
        c, a, b = c_ref[...], a_ref[...], b_ref[...]
        dkr = jnp.zeros((tm, LANES), F32)
        for h in range(MLA_HEADS):
            sl = slice(h * LANES, (h + 1) * LANES)
            dqp_ref[:, sl] = (_rope(dq_ref[:, sl].astype(F32), c, a, b, MLA_ROPE // 2) * scale).astype(BF16)
            dkh = dk_ref[:, sl].astype(F32) * LN2
            dkv_ref[:, sl] = dkh.astype(BF16)
            dkr = dkr + dkh
        dkv_ref[:, 1024:1536] = dv_ref[...].astype(BF16)
        lane = lax.broadcasted_iota(jnp.int32, (tm, LANES), 1)
        dkr = jnp.where((lane >= MLA_NOPE) & (lane < MLA_NOPE + MLA_ROPE), _rope(dkr, c, a, b, MLA_ROPE // 2), 0.0)

        d_cqn = jnp.dot(dqp_ref[...], wuqt_ref[...], preferred_element_type=F32)
        d_ckvn = jnp.dot(dkv_ref[...], wukvt_ref[...], preferred_element_type=F32)

        def norm_bwd(xv, gv, dy):
            r = lax.rsqrt(jnp.mean(xv * xv, axis=-1, keepdims=True) + EPS)
            dyg = dy * gv
            dx = r * dyg - xv * (r * r * r) * jnp.mean(dyg * xv, axis=-1, keepdims=True)
            return dx, jnp.sum(dy * xv * r, axis=0, keepdims=True)

        d_cq, dqg = norm_bwd(p_ref[:, 0:Q_LORA].astype(F32), qg_ref[...], d_cqn)
        d_ckv, dkvg = norm_bwd(p_ref[:, Q_LORA:Q_LORA + KV_LORA].astype(F32), kvg_ref[...], d_ckvn)
        dp_ref[:, 0:Q_LORA] = d_cq.astype(BF16)
        dp_ref[:, Q_LORA:Q_LORA + KV_LORA] = d_ckv.astype(BF16)
        dp_ref[:, 640:768] = dkr.astype(BF16)
        dp_ref[:, 768:1024] = jnp.zeros((tm, 256), BF16)

        @pl.when(pl.program_id(0) == 0)
        def _():
            dqg_ref[...] = dqg
            dkvg_ref[...] = dkvg

        @pl.when(pl.program_id(0) > 0)
        def _():
            dqg_ref[...] += dqg
            dkvg_ref[...] += dkvg

    return pl.pallas_call(
        body, name=name, grid=(L // tm,),
        in_specs=[_row_spec(tm, 1024), _row_spec(tm, 1024), _row_spec(tm, HEAD_W), _row_spec(tm, P_MLA, 0),
                  _const_spec(qg), _const_spec(kvg), _const_spec(wuq_t), _const_spec(wukv_t),
                  _row_spec(tm, LANES), _row_spec(tm, LANES), _row_spec(tm, LANES)],
        out_specs=[_row_spec(tm, P_MLA), _row_spec(tm, 1024), _row_spec(tm, 1536),
                   pl.BlockSpec((1, Q_LORA), lambda i: (0, 0)), pl.BlockSpec((1, KV_LORA), lambda i: (0, 0))],
        out_shape=[jax.ShapeDtypeStruct((L, P_MLA), BF16), jax.ShapeDtypeStruct((L, 1024), BF16),
                   jax.ShapeDtypeStruct((L, 1536), BF16), jax.ShapeDtypeStruct((1, Q_LORA), F32),
                   jax.ShapeDtypeStruct((1, KV_LORA), F32)],
        compiler_params=_cparams(VMEM_BIG_MB))(dq, dk, dv, p, qg, kvg, wuq_t, wukv_t, tc, ta, tb)


def _dil_prep(p, tabs, name):
    L = p.shape[0]
    tm = min(512, L)
    tc, ta, tb = tabs
    scale = DIL_HD ** -0.5

    def body(*refs):
        ins, (c_ref, a_ref, b_ref), outs = refs[:9], refs[9:12], refs[12:]
        c, a, b = c_ref[...], a_ref[...], b_ref[...]
        for n in range(9):
            t = n % 3
            for cb in range(HEAD_W // LANES):
                sl = slice(cb * LANES, (cb + 1) * LANES)
                xv = ins[n][:, sl].astype(F32)
                if t == 0:
                    xv = _rope(xv, c, a, b, ROT_DIM // 2) * (scale * LOG2E)
                elif t == 1:
                    xv = _rope(xv, c, a, b, ROT_DIM // 2)
                outs[n][:, sl] = xv.astype(BF16)

    in_specs = [_row_spec(tm, HEAD_W, P_DIL0 // HEAD_W + n) for n in range(9)] + [_row_spec(tm, LANES)] * 3
    return pl.pallas_call(
        body, name=name, grid=(L // tm,), in_specs=in_specs,
        out_specs=[_row_spec(tm, HEAD_W)] * 9,
        out_shape=[jax.ShapeDtypeStruct((L, HEAD_W), BF16)] * 9)(*([p] * 9), tc, ta, tb)


def _dil_prep_bwd(grads, tabs_t, name):
    L = grads[0].shape[0]
    tm = min(512, L)
    tc, ta, tb = tabs_t
    scale = DIL_HD ** -0.5

    def body(*refs):
        ins, (c_ref, a_ref, b_ref), o_ref = refs[:9], refs[9:12], refs[12]
        c, a, b = c_ref[...], a_ref[...], b_ref[...]
        for n in range(9):
            t = n % 3
            for cb in range(HEAD_W // LANES):
                sl = slice(cb * LANES, (cb + 1) * LANES)
                xv = ins[n][:, sl].astype(F32)
                if t == 0:
                    xv = _rope(xv, c, a, b, ROT_DIM // 2) * scale
                elif t == 1:
                    xv = _rope(xv, c, a, b, ROT_DIM // 2) * LN2
                o_ref[:, n * HEAD_W + cb * LANES:n * HEAD_W + (cb + 1) * LANES] = xv.astype(BF16)

    return pl.pallas_call(
        body, name=name, grid=(L // tm,), in_specs=[_row_spec(tm, HEAD_W)] * 9 + [_row_spec(tm, LANES)] * 3,
        out_specs=_row_spec(tm, 9 * HEAD_W), out_shape=jax.ShapeDtypeStruct((L, 9 * HEAD_W), BF16),
        compiler_params=_cparams(VMEM_BIG_MB))(*grads, tc, ta, tb)


def _merge_gate(oa, p, o_g, lse_g, name):
    L = oa.shape[0]
    tm = min(512, L)

    def body(oa_ref, ga_ref, gb_ref, o1, o2, o3, l1, l2, l3, ab_ref, bm_ref, lt_ref):
        la, lb, lc = l1[...], l2[...], l3[...]
        m = jnp.maximum(jnp.maximum(la, lb), lc)
        ea, eb, ec = jnp.exp2(la - m), jnp.exp2(lb - m), jnp.exp2(lc - m)
        den = ea + eb + ec
        bm = (ea * o1[...] + eb * o2[...] + ec * o3[...]) / den
        bm_ref[...] = bm
        lt_ref[...] = _per_head8(m + jnp.log2(den), True)
        ga, gb = ga_ref[...].astype(F32), gb_ref[...].astype(F32)
        ab_ref[:, 0:HEAD_W] = (oa_ref[...] * (ga * _sigmoid(ga))).astype(BF16)
        ab_ref[:, HEAD_W:2 * HEAD_W] = (bm * (gb * _sigmoid(gb))).astype(BF16)

    w = _row_spec(tm, HEAD_W)
    return pl.pallas_call(
        body, name=name, grid=(L // tm,),
        in_specs=[w, _row_spec(tm, HEAD_W, 2), _row_spec(tm, HEAD_W, 3), w, w, w, w, w, w],
        out_specs=[_row_spec(tm, 2 * HEAD_W), w, _row_spec(tm, MLA_HEADS)],
        out_shape=[jax.ShapeDtypeStruct((L, 2 * HEAD_W), BF16), jax.ShapeDtypeStruct((L, HEAD_W), F32),
                   jax.ShapeDtypeStruct((L, MLA_HEADS), F32)])(oa, p, p, *o_g, *lse_g)


def _gate_bwd(dab, p, oa, bm, name):
    L = oa.shape[0]
    tm = min(512, L)

    def body(da_ref, db_ref, ga_ref, gb_ref, oa_ref, bm_ref, doa_ref, dbm_ref, Da_ref, Db_ref, dg_ref):
        def one(d, g, o, do_ref, D_ref, col):
            sg = _sigmoid(g)
            do = d * (g * sg)
            do_ref[...] = do.astype(BF16)
            dg_ref[:, col:col + HEAD_W] = (d * o * (sg * (1.0 + g * (1.0 - sg)))).astype(BF16)
            D_ref[...] = _per_head8(do * o, False)

        one(da_ref[...], ga_ref[...].astype(F32), oa_ref[...], doa_ref, Da_ref, 0)
        one(db_ref[...], gb_ref[...].astype(F32), bm_ref[...], dbm_ref, Db_ref, HEAD_W)

    w = _row_spec(tm, HEAD_W)
    w8 = _row_spec(tm, MLA_HEADS)
    return pl.pallas_call(
        body, name=name, grid=(L // tm,),
        in_specs=[_row_spec(tm, HEAD_W, 0), _row_spec(tm, HEAD_W, 1), _row_spec(tm, HEAD_W, 2),
                  _row_spec(tm, HEAD_W, 3), w, w],
        out_specs=[w, w, w8, w8, _row_spec(tm, 2 * HEAD_W)],
        out_shape=[jax.ShapeDtypeStruct((L, HEAD_W), BF16), jax.ShapeDtypeStruct((L, HEAD_W), BF16),
                   jax.ShapeDtypeStruct((L, MLA_HEADS), F32), jax.ShapeDtypeStruct((L, MLA_HEADS), F32),
                   jax.ShapeDtypeStruct((L, 2 * HEAD_W), BF16)])(dab, dab, p, p, oa, bm)


NT = (((1,), (1,)), ((), ()))
TN = (((0,), (0,)), ((), ()))
NEG = -1e30


MLA_TQ = 512
MLA_TK = 4096
MLA_BWD_TQ = 1024
MLA_BWD_TK = 2048
LOG2E = 1.4426950408889634
LN2 = 0.6931471805599453


def _mla_fwd(q, k, v_t, name):
    L = q.shape[0]
    tq, tk = min(MLA_TQ, L), min(MLA_TK, L)
    nq, nk = L // tq, L // tk
    npair = MLA_HEADS // 2

    def body(q_ref, k_ref, vt_ref, o_ref, lse_ref, m0, l0, a0, m1, l1, a1):
        j = pl.program_id(2)
        stats = ((m0, l0, a0), (m1, l1, a1))

        @pl.when(j == 0)
        def _():
            for m_sc, l_sc, acc_sc in stats:
                m_sc[...] = jnp.full(m_sc.shape, NEG, F32)
                l_sc[...] = jnp.zeros(l_sc.shape, F32)
                acc_sc[...] = jnp.zeros(acc_sc.shape, F32)

        s_ts = [lax.dot_general(k_ref[:, hh * LANES:(hh + 1) * LANES], q_ref[:, hh * LANES:(hh + 1) * LANES], NT,
                                preferred_element_type=F32) for hh in range(2)]
        for hh in range(2):
            m_sc, l_sc, acc_sc = stats[hh]
            s_t = s_ts[hh]
            m_prev = m_sc[...]
            m_new = jnp.maximum(m_prev, jnp.max(s_t, axis=0, keepdims=True))
            alpha = jnp.exp2(m_prev - m_new)
            p_t = jnp.exp2(s_t - m_new)
            l_sc[...] = alpha * l_sc[...] + jnp.sum(p_t, axis=0, keepdims=True)
            m_sc[...] = m_new
            pv = jnp.dot(vt_ref[hh * DIL_HD:(hh + 1) * DIL_HD, :], p_t.astype(BF16),
                         preferred_element_type=F32)
            acc_sc[...] = alpha * acc_sc[...] + pv

        @pl.when(j == nk - 1)
        def _():
            o_ref[...] = jnp.concatenate([a0[...] / l0[...], a1[...] / l1[...]], axis=0).T
            lse_ref[...] = jnp.concatenate([m0[...] + jnp.log2(l0[...]), m1[...] + jnp.log2(l1[...])], axis=0)

    stat = [pltpu.VMEM((1, tq), F32), pltpu.VMEM((1, tq), F32), pltpu.VMEM((DIL_HD, tq), F32)]
    return pl.pallas_call(
        body, name=name, grid=(npair, nq, nk),
        in_specs=[pl.BlockSpec((tq, 2 * LANES), lambda pr, i, j: (i, pr)),
                  pl.BlockSpec((tk, 2 * LANES), lambda pr, i, j: (j, pr)),
                  pl.BlockSpec((LANES, tk), lambda pr, i, j: (pr, j))],
        out_specs=[pl.BlockSpec((tq, LANES), lambda pr, i, j: (i, pr)),
                   pl.BlockSpec((None, 2, tq), lambda pr, i, j: (pr, 0, i))],
        out_shape=[jax.ShapeDtypeStruct((L, HEAD_W), F32), jax.ShapeDtypeStruct((npair, 2, L), F32)],
        scratch_shapes=stat + stat, compiler_params=_cparams(VMEM_BIG_MB))(q, k, v_t)


def _mla_bwd(q, k, v, q_t, do, do_t, lse_cols, d_cols, name):
    L = q.shape[0]
    tq, tk = min(MLA_BWD_TQ, L), min(MLA_BWD_TK, L)
    nq, nk = L // tq, L // tk
    npair = MLA_HEADS // 2

    def body(q_ref, k_ref, v_ref, qt_ref, do_ref, dot_ref, lse_ref, d_ref, dq_out, dkt_out, dvt_out,
             dq_ref, dkt_ref, dvt_ref):
        j, i = pl.program_id(1), pl.program_id(2)

        @pl.when((j == 0) & (i == 0))
        def _():
            dq_ref[...] = jnp.zeros(dq_ref.shape, F32)

        @pl.when(i == 0)
        def _():
            dkt_ref[...] = jnp.zeros(dkt_ref.shape, F32)
            dvt_ref[...] = jnp.zeros(dvt_ref.shape, F32)

        first = lax.broadcasted_iota(jnp.int32, (tq, LANES), 1) < DIL_HD
        dov = do_ref[...]
        vv = v_ref[...]
        rows = pl.ds(pl.multiple_of(i * tq, tq), tq)
        for hh in range(2):
            sl = slice(hh * LANES, (hh + 1) * LANES)
            hrows = slice(hh * DIL_HD, (hh + 1) * DIL_HD)
            qh, kh = q_ref[:, sl], k_ref[:, sl]
            do_h = jnp.where(first if hh == 0 else ~first, dov, jnp.zeros_like(dov))
            s = lax.dot_general(qh, kh, NT, preferred_element_type=F32)
            p = jnp.exp2(s - lse_ref[:, hh:hh + 1])
            dvt_ref[hrows, :] += jnp.dot(dot_ref[hrows, :], p.astype(BF16), preferred_element_type=F32)
            dp = lax.dot_general(do_h, vv, NT, preferred_element_type=F32)
            ds = (p * (dp - d_ref[:, hh:hh + 1])).astype(BF16)
            dq_ref[rows, sl] += jnp.dot(ds, kh, preferred_element_type=F32)
            dkt_ref[sl, :] += jnp.dot(qt_ref[sl, :], ds, preferred_element_type=F32)

        @pl.when(i == nq - 1)
        def _():
            dkt_out[...] = dkt_ref[...].astype(BF16)
            dvt_out[...] = dvt_ref[...].astype(BF16)

        @pl.when((j == nk - 1) & (i == nq - 1))
        def _():
            dq_out[...] = dq_ref[...].astype(BF16)

    colspec = pl.BlockSpec((None, tq, 2), lambda pr, j, i: (pr, i, 0))
    return pl.pallas_call(
        body, name=name, grid=(npair, nk, nq),
        in_specs=[pl.BlockSpec((tq, 2 * LANES), lambda pr, j, i: (i, pr)),
                  pl.BlockSpec((tk, 2 * LANES), lambda pr, j, i: (j, pr)),
                  pl.BlockSpec((tk, LANES), lambda pr, j, i: (j, pr)),
                  pl.BlockSpec((2 * LANES, tq), lambda pr, j, i: (pr, i)),
                  pl.BlockSpec((tq, LANES), lambda pr, j, i: (i, pr)),
                  pl.BlockSpec((LANES, tq), lambda pr, j, i: (pr, i)),
                  colspec, colspec],
        out_specs=[pl.BlockSpec((L, 2 * LANES), lambda pr, j, i: (0, pr)),
                   pl.BlockSpec((2 * LANES, tk), lambda pr, j, i: (pr, j)),
                   pl.BlockSpec((LANES, tk), lambda pr, j, i: (pr, j))],
        out_shape=[jax.ShapeDtypeStruct((L, 1024), BF16), jax.ShapeDtypeStruct((1024, L), BF16),
                   jax.ShapeDtypeStruct((HEAD_W, L), BF16)],
        scratch_shapes=[pltpu.VMEM((L, 2 * LANES), F32), pltpu.VMEM((2 * LANES, tk), F32),
                        pltpu.VMEM((LANES, tk), F32)],
        compiler_params=_cparams(VMEM_BIG_MB))(q, k, v, q_t, do, do_t, lse_cols, d_cols)


DIL_TQ = 2048
DIL_SQ = 128
DIL_SW = DIL_SQ + 2 * DIL_HALF


def _dil_window(a_sub, ld):
    return pl.multiple_of(jnp.clip(a_sub - DIL_HALF, 0, ld - DIL_SW), DIL_HALF)


def _dil_band_mask(shift, heads):
    kidx = lax.broadcasted_iota(jnp.int32, (DIL_SW, heads * DIL_SQ), 0)
    qidx = lax.broadcasted_iota(jnp.int32, (DIL_SW, heads * DIL_SQ), 1) % DIL_SQ
    return jnp.abs(shift + kidx - qidx) <= DIL_HALF


def _pair_rows(x, first):
    zero = jnp.zeros_like(x)
    return jnp.concatenate([jnp.where(first, x, zero), jnp.where(first, zero, x)], axis=0)


def _dil_fwd(q, kp, vp_t4, name):
    d, ld, _ = q.shape
    assert ld % DIL_SQ == 0
    tq = min(DIL_TQ, ld)
    nq = ld // tq
    npair = HEAD_W // LANES
    nb = (ld + 2 * DIL_HALF) // LANES

    def body(q_ref, k_ref, vt_ref, o_ref, lse_ref):
        i = pl.program_id(2)
        first = lax.broadcasted_iota(jnp.int32, (DIL_SQ, LANES), 1) < DIL_HD
        kidx = lax.broadcasted_iota(jnp.int32, (DIL_SW, DIL_SQ), 0)
        for u in range(tq // DIL_SQ):
            a_sub = pl.multiple_of(i * tq + u * DIL_SQ, DIL_SQ)
            kk = a_sub // LANES
            rows = slice(u * DIL_SQ, (u + 1) * DIL_SQ)
            kwin = k_ref[pl.ds(a_sub, DIL_SW), :]
            valid = _dil_band_mask(-DIL_HALF, 1) & (kidx >= DIL_HALF - a_sub) & (kidx < ld + DIL_HALF - a_sub)
            qv = q_ref[rows, :]
            outs, lses = [], []
            for hh in range(2):
                qh = jnp.where(first if hh == 0 else ~first, qv, jnp.zeros_like(qv))
                s_t = jnp.where(valid, lax.dot_general(kwin, qh, NT, preferred_element_type=F32), NEG)
                m = jnp.max(s_t, axis=0, keepdims=True)
                p32 = jnp.exp2(s_t - m)
                l = jnp.sum(p32, axis=0, keepdims=True)
                p_t = p32.astype(BF16)
                hrows = slice(hh * DIL_HD, (hh + 1) * DIL_HD)
                pv = (jnp.dot(vt_ref[kk, hrows, :], p_t[0:LANES, :], preferred_element_type=F32)
                      + jnp.dot(vt_ref[kk + 1, hrows, :], p_t[LANES:DIL_SW, :], preferred_element_type=F32))
                outs.append(pv / l)
                lses.append(jnp.broadcast_to(m + jnp.log2(l), (DIL_HD, DIL_SQ)))
            o_ref[rows, :] = jnp.concatenate(outs, axis=0).T
            lse_ref[rows, :] = jnp.concatenate(lses, axis=0).T

    blk = pl.BlockSpec((None, tq, LANES), lambda r, pr, i: (r, i, pr))
    full = pl.BlockSpec((None, ld + 2 * DIL_HALF, LANES), lambda r, pr, i: (r, 0, pr))
    vspec = pl.BlockSpec((None, nb, LANES, LANES), lambda r, pr, i: (r, 0, pr, 0))
    return pl.pallas_call(
        body, name=name, grid=(d, npair, nq), in_specs=[blk, full, vspec], out_specs=[blk, blk],
        out_shape=[jax.ShapeDtypeStruct((d, ld, HEAD_W), F32), jax.ShapeDtypeStruct((d, ld, HEAD_W), F32)],
        compiler_params=_cparams(VMEM_BIG_MB))(q, kp, vp_t4)


def _dil_bwd(q, k, v, do, lse_rows, d_rows, name):
    d, ld, _ = q.shape
    assert ld % DIL_SQ == 0 and ld >= DIL_SW
    tq = min(DIL_TQ, ld)
    nq = ld // tq
    npair = HEAD_W // LANES
    span = min(tq + 2 * DIL_HALF, ld)

    def body(q_ref, k_ref, v_ref, do_ref, lse_ref, d_ref, dq_ref, dk_ref, dv_ref, dk_sc, dv_sc, dk_acc, dv_acc):
        i = pl.program_id(2)

        @pl.when(i == 0)
        def _():
            dk_acc[...] = jnp.zeros(dk_acc.shape, F32)
            dv_acc[...] = jnp.zeros(dv_acc.shape, F32)

        dk_sc[...] = jnp.zeros(dk_sc.shape, F32)
        dv_sc[...] = jnp.zeros(dv_sc.shape, F32)
        first = lax.broadcasted_iota(jnp.int32, (DIL_SQ, LANES), 1) < DIL_HD
        base = pl.multiple_of(jnp.clip(i * tq - DIL_HALF, 0, ld - span), DIL_HALF)
        for u in range(tq // DIL_SQ):
            a_sub = i * tq + u * DIL_SQ
            ws = _dil_window(a_sub, ld)
            rows = slice(u * DIL_SQ, (u + 1) * DIL_SQ)
            win = pl.ds(pl.multiple_of(ws - base, DIL_HALF), DIL_SW)
            kwin = k_ref[pl.ds(ws, DIL_SW), :]
            vwin = v_ref[pl.ds(ws, DIL_SW), :]
            q2 = _pair_rows(q_ref[rows, :], first)
            do2 = _pair_rows(do_ref[rows, :], first)
            lse2 = jnp.concatenate([lse_ref[0:1, rows], lse_ref[1:2, rows]], axis=1)
            dd2 = jnp.concatenate([d_ref[0:1, rows], d_ref[1:2, rows]], axis=1)
            s_t = lax.dot_general(kwin, q2, NT, preferred_element_type=F32)
            p_t = jnp.exp2(jnp.where(_dil_band_mask(ws - a_sub, 2), s_t, NEG) - lse2)
            dv_sc[win, :] += jnp.dot(p_t.astype(BF16), do2, preferred_element_type=F32)
            dp_t = lax.dot_general(vwin, do2, NT, preferred_element_type=F32)
            ds_t = (p_t * (dp_t - dd2)).astype(BF16)
            dk_sc[win, :] += jnp.dot(ds_t, q2, preferred_element_type=F32)
            dq2 = lax.dot_general(ds_t, kwin, TN, preferred_element_type=F32)
            dq_ref[rows, :] = jnp.where(first, dq2[0:DIL_SQ, :], dq2[DIL_SQ:2 * DIL_SQ, :]).astype(BF16)
        dk_acc[pl.ds(base, span), :] += dk_sc[...]
        dv_acc[pl.ds(base, span), :] += dv_sc[...]

        @pl.when(i == nq - 1)
        def _():
            dk_ref[...] = dk_acc[...].astype(BF16)
            dv_ref[...] = dv_acc[...].astype(BF16)

    blk = pl.BlockSpec((None, tq, LANES), lambda r, pr, i: (r, i, pr))
    full = pl.BlockSpec((None, ld, LANES), lambda r, pr, i: (r, 0, pr))
    rowspec = pl.BlockSpec((None, None, 2, tq), lambda r, pr, i: (r, pr, 0, i))
    return pl.pallas_call(
        body, name=name, grid=(d, npair, nq), in_specs=[blk, full, full, blk, rowspec, rowspec],
        out_specs=[blk, full, full],
        out_shape=[jax.ShapeDtypeStruct((d, ld, HEAD_W), BF16)] * 3,
        scratch_shapes=[pltpu.VMEM((span, LANES), F32), pltpu.VMEM((span, LANES), F32),
                        pltpu.VMEM((ld, LANES), F32), pltpu.VMEM((ld, LANES), F32)],
        compiler_params=_cparams(VMEM_BIG_MB))(q, k, v, do, lse_rows, d_rows)


TILE_BYTES = 1 << 21


def _row_tile(rows, cols, budget=TILE_BYTES):
    for parts in range(1, rows + 1):
        tr = rows // parts
        if rows % parts == 0 and tr % 8 == 0 and tr * cols * 4 <= budget:
            return tr
    return rows


def _add2(a, b, name, out_dtype):
    n, rows, cols = a.shape
    tr = _row_tile(rows, cols)

    def body(a_ref, b_ref, o_ref):
        o_ref[...] = (a_ref[...] + b_ref[...]).astype(out_dtype)

    spec = pl.BlockSpec((None, tr, cols), lambda t, i: (t, i, 0))
    return pl.pallas_call(body, name=name, grid=(n, rows // tr), in_specs=[spec, spec], out_specs=spec,
                          out_shape=jax.ShapeDtypeStruct(a.shape, out_dtype))(a, b)


def _add4_ordered(a, name):
    _, rows, cols = a.shape
    tr = _row_tile(rows, cols, TILE_BYTES // 4)

    def body(a_ref, o_ref):
        o_ref[...] = ((a_ref[0].astype(F32) + a_ref[1].astype(F32)) + a_ref[2].astype(F32)) + a_ref[3].astype(F32)

    return pl.pallas_call(
        body, name=name, grid=(rows // tr,), in_specs=[pl.BlockSpec((4, tr, cols), lambda i: (0, i, 0))],
        out_specs=pl.BlockSpec((tr, cols), lambda i: (i, 0)),
        out_shape=jax.ShapeDtypeStruct((rows, cols), F32))(a)


def _adamw(w, g, m, v, name):
    rows, cols = w.shape
    tr = _row_tile(rows, cols)
    bc1 = 1.0 - ADAM_B1 ** ADAM_STEP
    bc2 = 1.0 - ADAM_B2 ** ADAM_STEP

    def body(w_ref, g_ref, m_ref, v_ref, d_ref, nm_ref, nv_ref):
        gv = g_ref[...]
        nm = ADAM_B1 * m_ref[...] + (1.0 - ADAM_B1) * gv
        nv = ADAM_B2 * v_ref[...] + (1.0 - ADAM_B2) * (gv * gv)
        d_ref[...] = -ADAM_LR * ((nm / bc1) / (jnp.sqrt(nv / bc2) + ADAM_EPS) + ADAM_WD * w_ref[...])
        nm_ref[...] = nm
        nv_ref[...] = nv

    spec = pl.BlockSpec((tr, cols), lambda i: (i, 0))
    return pl.pallas_call(body, name=name, grid=(rows // tr,), in_specs=[spec] * 4, out_specs=[spec] * 3,
                          out_shape=[jax.ShapeDtypeStruct(w.shape, F32)] * 3,
                          compiler_params=_cparams(VMEM_BIG_MB))(w, g, m, v)


ANY = pl.BlockSpec(memory_space=pl.ANY)


def _place():
    return lax.axis_index("x"), lax.axis_index("y"), lax.axis_index("c")


def _rcopy(send_sems, recv_sems, n, src, dst, to):
    return pltpu.make_async_remote_copy(src_ref=src, dst_ref=dst, send_sem=send_sems.at[n], recv_sem=recv_sems.at[n],
                                        device_id=to, device_id_type=MESH)


def _allgather_weights(shards):
    na = len(shards)
    ns = 7

    def body(*refs):
        w_refs, g_refs = refs[:na], refs[na:2 * na]
        send_sems, recv_sems, local_sems = refs[2 * na:]
        x, y, c = _place()
        s, sx, sy, sd = 2 * x + y, 2 * (1 - x) + y, 2 * x + (1 - y), 2 * (1 - x) + (1 - y)
        to_x, to_y, sib = (1 - x, y, c), (x, 1 - y, c), (x, y, 1 - c)

        def part(a, shard, h, k=None):
            hr = shards[a].shape[0] // 2
            if k is None:
                return g_refs[a].at[shard, pl.ds(h * hr, hr), :]
            return g_refs[a].at[shard, pl.ds(h * hr + k * (hr // 2), hr // 2), :]

        def cp(a, n, src, dst, to):
            return _rcopy(send_sems, recv_sems, ns * a + n, src, dst, to)

        started, sends = [], []

        def go(copy):
            copy.start()
            sends.append(copy)

        for a in range(na):
            hr = shards[a].shape[0] // 2
            mine = pltpu.make_async_copy(w_refs[a], g_refs[a].at[s], local_sems.at[a])
            mine.start()
            started.append(mine)
            own = w_refs[a].at[pl.ds(c * hr, hr), :]
            go(cp(a, 0, own, part(a, s, c), to_x))
            go(cp(a, 1, own, part(a, s, c), to_y))
        for a in range(na):
            cp(a, 0, part(a, sx, c), part(a, sx, c), to_x).wait_recv()
            go(cp(a, 2, part(a, sx, c, 0), part(a, sx, c, 0), to_y))
            go(cp(a, 4, part(a, sx, c), part(a, sx, c), sib))
            cp(a, 1, part(a, sy, c), part(a, sy, c), to_y).wait_recv()
            go(cp(a, 3, part(a, sy, c, 1), part(a, sy, c, 1), to_x))
            go(cp(a, 5, part(a, sy, c), part(a, sy, c), sib))
        for a in range(na):
            cp(a, 2, part(a, sd, c, 0), part(a, sd, c, 0), to_y).wait_recv()
            cp(a, 3, part(a, sd, c, 1), part(a, sd, c, 1), to_x).wait_recv()
            go(cp(a, 6, part(a, sd, c), part(a, sd, c), sib))
        for a in range(na):
            for n, sj in ((4, sx), (5, sy), (6, sd)):
                cp(a, n, part(a, sj, 1 - c), part(a, sj, 1 - c), sib).wait_recv()
        for copy in sends:
            copy.wait_send()
        for mine in started:
            mine.wait()

    return pl.pallas_call(
        body, name="allgather_weights", in_specs=[ANY] * na, out_specs=[ANY] * na,
        out_shape=[jax.ShapeDtypeStruct((N_SHARD,) + t.shape, t.dtype) for t in shards],
        scratch_shapes=[pltpu.SemaphoreType.DMA((ns * na,)), pltpu.SemaphoreType.DMA((ns * na,)),
                        pltpu.SemaphoreType.DMA((na,))])(*shards)


def _sibling_send_halves(gs):
    na = len(gs)

    def body(*refs):
        g_refs, o_refs = refs[:na], refs[na:2 * na]
        send_sems, recv_sems = refs[2 * na:]
        x, y, c = _place()
        cps = []
        for a in range(na):
            for t in range(N_SHARD):
                cp = _rcopy(send_sems, recv_sems, N_SHARD * a + t, g_refs[a].at[t, 1 - c], o_refs[a].at[t],
                            (x, y, 1 - c))
                cp.start()
                cps.append(cp)
        for cp in cps:
            cp.wait()

    return pl.pallas_call(
        body, name="grad_sibling_exchange", in_specs=[ANY] * na, out_specs=[ANY] * na,
        out_shape=[jax.ShapeDtypeStruct((N_SHARD,) + g.shape[2:], g.dtype) for g in gs],
        scratch_shapes=[pltpu.SemaphoreType.DMA((N_SHARD * na,)), pltpu.SemaphoreType.DMA((N_SHARD * na,))])(*gs)


def _chip_scatter(parts):
    na = len(parts)

    def body(*refs):
        a_refs, o_refs = refs[:na], refs[na:2 * na]
        send_sems, recv_sems, local_sems = refs[2 * na:]
        x, y, c = _place()
        s = 2 * x + y
        chips = [(1 - x, y), (x, 1 - y), (1 - x, 1 - y)]
        started, cps = [], []
        for a in range(na):
            mine = pltpu.make_async_copy(a_refs[a].at[s], o_refs[a].at[s], local_sems.at[a])
            mine.start()
            started.append(mine)
            for n, (cx, cy) in enumerate(chips):
                cp = _rcopy(send_sems, recv_sems, 3 * a + n, a_refs[a].at[2 * cx + cy], o_refs[a].at[s], (cx, cy, c))
                cp.start()
                cps.append(cp)
        for a in range(na):
            for n, (cx, cy) in enumerate(chips):
                sj = 2 * cx + cy
                _rcopy(send_sems, recv_sems, 3 * a + n, a_refs[a].at[sj], o_refs[a].at[sj], (cx, cy, c)).wait_recv()
        for cp in cps:
            cp.wait_send()
        for mine in started:
            mine.wait()

    return pl.pallas_call(
        body, name="grad_chip_scatter", in_specs=[ANY] * na, out_specs=[ANY] * na,
        out_shape=[jax.ShapeDtypeStruct(t.shape, t.dtype) for t in parts],
        scratch_shapes=[pltpu.SemaphoreType.DMA((3 * na,)), pltpu.SemaphoreType.DMA((3 * na,)),
                        pltpu.SemaphoreType.DMA((na,))])(*parts)


def _sibling_swap(rs):
    na = len(rs)

    def body(*refs):
        r_refs, o_refs = refs[:na], refs[na:2 * na]
        send_sems, recv_sems = refs[2 * na:]
        x, y, c = _place()
        cps = []
        for a in range(na):
            cp = _rcopy(send_sems, recv_sems, a, r_refs[a], o_refs[a], (x, y, 1 - c))
            cp.start()
            cps.append(cp)
        for cp in cps:
            cp.wait()

    return pl.pallas_call(
        body, name="grad_sibling_swap", in_specs=[ANY] * na, out_specs=[ANY] * na,
        out_shape=[jax.ShapeDtypeStruct(t.shape, t.dtype) for t in rs],
        scratch_shapes=[pltpu.SemaphoreType.DMA((na,)), pltpu.SemaphoreType.DMA((na,))])(*rs)


def _pack_small(norm_g, q_norm_g, kv_norm_g, final_g):
    flat = jnp.concatenate([norm_g.reshape(-1), q_norm_g.reshape(-1), kv_norm_g.reshape(-1), final_g.reshape(-1),
                            jnp.zeros((SMALL_ROWS * LANES - N_SMALL,), F32)])
    return flat.reshape(SMALL_ROWS, LANES)


def _split_small(s):
    s = s.reshape(-1)
    o = 0
    out = []
    for n, shape in ((DEPTH * D_MODEL, (DEPTH, D_MODEL)), (DEPTH * Q_LORA, (DEPTH, Q_LORA)),
                     (DEPTH * KV_LORA, (DEPTH, KV_LORA)), (D_MODEL, (D_MODEL,))):
        out.append(s[o:o + n].reshape(shape))
        o += n
    return out


def _assemble_w_in(sh):
    z = lambda n: jnp.zeros(sh.shape[1:3] + (n,), sh.dtype)
    s0, s1, s2, s3 = sh[0], sh[1], sh[2], sh[3]
    return jnp.concatenate([s0[..., 0:640], z(64), s0[..., 640:672], z(32), z(256), s0[..., 672:1184],
                            s3[..., 1064:1576], s0[..., 1184:1576], s1, s2, s3[..., 0:1064]], axis=-1)


def _split_w_in_grad(parts):
    def shard(s, a, b):
        if s == 0:
            return jnp.concatenate([a[:, 0:640], a[:, 704:736], a[:, 1024:1536], b[:, 0:392]], axis=1)
        if s == 3:
            return jnp.concatenate([b[:, 3544:4608], a[:, 1536:2048]], axis=1)
        return b[:, 392 + (s - 1) * SHARD_COLS_IN:392 + s * SHARD_COLS_IN]

    rows = jnp.concatenate([shard(s, a, b) for s in range(N_SHARD) for a, b in parts], axis=0)
    return rows.reshape(N_SHARD, DEPTH * D_MODEL, SHARD_COLS_IN)


def _col_shards(w):
    dl, r, cc = w.shape
    return w.reshape(dl, r, N_SHARD, cc // N_SHARD).transpose(2, 0, 1, 3).reshape(N_SHARD, dl * r, cc // N_SHARD)


def _from_col_shards(g, rows):
    cc = g.shape[-1]
    return g.reshape(N_SHARD, DEPTH, rows, cc).transpose(1, 2, 0, 3).reshape(DEPTH, rows, N_SHARD * cc)


def _pad_w_in(w):
    z = lambda n: jnp.zeros(w.shape[:-1] + (n,), w.dtype)
    return jnp.concatenate([w[..., 0:640], z(64), w[..., 640:672], z(32), z(256), w[..., 672:1184],
                            w[..., 5792:6304], w[..., 1184:5792]], axis=-1)


def _unpad_w_in(w):
    return jnp.concatenate([w[..., 0:640], w[..., 704:736], w[..., 1024:1536], w[..., 2048:6656],
                            w[..., 1536:2048]], axis=-1)


def _pad_w_uq(w):
    s = w.shape[:-1]
    w = w.reshape(s + (MLA_HEADS, 96))
    return jnp.pad(w, [(0, 0)] * (w.ndim - 1) + [(0, 32)]).reshape(s + (1024,))


def _unpad_w_uq(w):
    s = w.shape[:-1]
    return w.reshape(s + (MLA_HEADS, LANES))[..., :96].reshape(s + (768,))


def _pad_w_ukv(w):
    s = w.shape[:-1]
    w = w.reshape(s + (MLA_HEADS, 128))
    kpart = jnp.pad(w[..., :64], [(0, 0)] * (w.ndim - 1) + [(0, 64)]).reshape(s + (1024,))
    vpart = w[..., 64:].reshape(s + (512,))
    return jnp.concatenate([kpart, vpart], axis=-1)


def _unpad_w_ukv(w):
    s = w.shape[:-1]
    kpart = w[..., :1024].reshape(s + (MLA_HEADS, LANES))[..., :64]
    vpart = w[..., 1024:].reshape(s + (MLA_HEADS, 64))
    return jnp.concatenate([kpart, vpart], axis=-1).reshape(s + (1024,))


def _rope_tables(L, dim, lane_lo, period):
    half = dim // 2
    inv = 1.0 / (ROPE_THETA ** (jnp.arange(0, dim, 2, dtype=F32) / dim))
    ang = jnp.arange(L, dtype=F32)[:, None] * inv[None, :]
    cos, sin = jnp.cos(ang), jnp.sin(ang)
    one = lambda n: jnp.ones((L, n), F32)
    zero = lambda n: jnp.zeros((L, n), F32)
    rest = period - lane_lo - dim
    rep = LANES // period
    c = jnp.tile(jnp.concatenate([one(lane_lo), cos, cos, one(rest)], axis=1), (1, rep))
    a = jnp.tile(jnp.concatenate([zero(lane_lo), -sin, zero(half), zero(rest)], axis=1), (1, rep))
    b = jnp.tile(jnp.concatenate([zero(lane_lo + half), sin, zero(rest)], axis=1), (1, rep))
    return c, a, b


def _to_strided(t, d):
    L, w = t.shape
    return t.reshape(L // d, d, w).transpose(1, 0, 2)


def _from_strided(t):
    d, ld, w = t.shape
    return t.transpose(1, 0, 2).reshape(d * ld, w)


def _head_rows(t):
    return t.T.reshape(MLA_HEADS // 2, 2, t.shape[0])


def _head_rows_strided(t, d):
    s = _to_strided(t, d)
    return s.transpose(0, 2, 1).reshape(d, MLA_HEADS // 2, 2, s.shape[1])

def _local_grads(x, target, norm_g, w_in_p, q_norm_g, kv_norm_g, w_uq_p, w_ukv_p, w_out, final_g):
    L = x.shape[0]
    tabs_m = _rope_tables(L, MLA_ROPE, MLA_NOPE, LANES)
    tabs_d = _rope_tables(L, ROT_DIM, 0, DIL_HD)
    tabs_m_t = (tabs_m[0], -tabs_m[1], -tabs_m[2])
    tabs_d_t = (tabs_d[0], -tabs_d[1], -tabs_d[2])
    w_in_t = jnp.swapaxes(w_in_p, 1, 2)
    w_uq_t = jnp.swapaxes(w_uq_p, 1, 2)
    w_ukv_t = jnp.swapaxes(w_ukv_p, 1, 2)
    w_out_t = jnp.swapaxes(w_out, 1, 2)

    saved = []
    for l in range(DEPTH):
        h = _rms_fwd(x, norm_g[l:l + 1], "rms_fwd")
        p = _mm(h, w_in_p[l], tm=1024, tn=3328, tk=1024, out_dtype=BF16, name="in_proj")
        q, k, v, cqn, ckvn = _mla_prep(p, q_norm_g[l:l + 1], kv_norm_g[l:l + 1], w_uq_p[l], w_ukv_p[l], tabs_m,
                                       "mla_prep")
        oa, lse_a = _mla_fwd(q, k, v.T, "mla_fwd")
        dil = _dil_prep(p, tabs_d, "dil_prep")
        dil_s, o_g, lse_g = [], [], []
        for g, (_, dd) in enumerate(DIL_PAIRS):
            qs, ks, vs = (_to_strided(t, dd) for t in dil[3 * g:3 * g + 3])
            pad = ((0, 0), (DIL_HALF, DIL_HALF), (0, 0))
            vp = jnp.pad(vs, pad)
            v_t4 = vp.reshape(dd, vp.shape[1] // LANES, LANES, HEAD_W).transpose(0, 1, 3, 2)
            og, lg = _dil_fwd(qs, jnp.pad(ks, pad), v_t4, "dil_fwd_%d" % dd)
            dil_s.append((qs, ks, vs))
            o_g.append(_from_strided(og))
            lse_g.append(_from_strided(lg))
        ab, bm, lt = _merge_gate(oa, p, o_g, lse_g, "merge_gate")
        x_next = _mm(ab, w_out[l], tm=1024, tn=1024, tk=1024, out_dtype=F32, name="out_proj", add=x)
        saved.append((x, h, p, q, k, v, cqn, ckvn, oa, lse_a, dil_s, bm, lt, ab))
        x = x_next

    loss_b, dx, d_final = _loss_head(x, final_g[None, :], target, "loss_head")
    loss = loss_b[0, 0]

    d_norm, d_qn, d_kvn, d_win, d_wuq, d_wukv, d_wout = [], [], [], [], [], [], []
    for l in reversed(range(DEPTH)):
        x_l, h, p, q, k, v, cqn, ckvn, oa, lse_a, dil_s, bm, lt, ab = saved[l]
        dab = _mm(dx, w_out_t[l], tm=1024, tn=1024, tk=1024, out_dtype=F32, name="out_proj_dgrad")
        d_wout.append(_mm(ab, dx, tm=1024, tn=1024, tk=1024, out_dtype=F32, name="out_proj_wgrad", a_is_kxm=True))
        doa, dbm, D_a, D_b, dgates = _gate_bwd(dab, p, oa, bm, "gate_bwd")
        dq, dk_t, dv_t = _mla_bwd(q, k, v, q.T, doa, doa.T, lse_a.transpose(0, 2, 1),
                                  D_a.reshape(L, MLA_HEADS // 2, 2).transpose(1, 0, 2), "mla_bwd")
        dk, dv = dk_t.T, dv_t.T
        dp_mla, dq_pre, dkv, dqg, dkvg = _mla_prep_bwd(dq, dk, dv, p, q_norm_g[l:l + 1], kv_norm_g[l:l + 1],
                                                       w_uq_t[l], w_ukv_t[l], tabs_m_t, "mla_prep_bwd")
        d_wuq.append(_mm(cqn, dq_pre, tm=Q_LORA, tn=1024, tk=2048, out_dtype=F32, name="w_uq_wgrad", a_is_kxm=True))
        d_wukv.append(_mm(ckvn, dkv, tm=KV_LORA, tn=1536, tk=2048, out_dtype=F32, name="w_ukv_wgrad", a_is_kxm=True))
        dgr = []
        for g, (_, dd) in enumerate(DIL_PAIRS):
            qs, ks, vs = dil_s[g]
            dqs, dks, dvs = _dil_bwd(qs, ks, vs, _to_strided(dbm, dd), _head_rows_strided(lt, dd),
                                     _head_rows_strided(D_b, dd), "dil_bwd_%d" % dd)
            dgr += [_from_strided(dqs), _from_strided(dks), _from_strided(dvs)]
        dp_dil = _dil_prep_bwd(dgr, tabs_d_t, "dil_prep_bwd")
        dp_a = jnp.concatenate([dp_mla, dgates], axis=1)
        dh = _mm(dp_a, w_in_t[l][0:P_DIL0], tm=1024, tn=1024, tk=2048, out_dtype=F32, name="in_proj_dgrad_a")
        dh = _mm(dp_dil, w_in_t[l][P_DIL0:], tm=512, tn=1024, tk=2304, out_dtype=F32, name="in_proj_dgrad_b",
                 add=dh)
        d_win.append((_mm(h, dp_a, tm=512, tn=2048, tk=1024, out_dtype=F32, name="in_proj_wgrad_a", a_is_kxm=True),
                      _mm(h, dp_dil, tm=512, tn=1536, tk=2048, out_dtype=F32, name="in_proj_wgrad_b",
                          a_is_kxm=True)))
        dx, dng = _rms_bwd(dh, x_l, norm_g[l:l + 1], dx, "rms_bwd")
        d_norm.append(dng[0])
        d_qn.append(dqg[0])
        d_kvn.append(dkvg[0])

    rev = lambda xs: jnp.stack(xs[::-1])
    return (loss, dx, rev(d_norm), d_win[::-1], rev(d_qn), rev(d_kvn), rev(d_wuq), rev(d_wukv), rev(d_wout),
            d_final[0])


def kernel(x, norm_g, w_in, q_norm_g, kv_norm_g, w_uq, w_ukv, w_out, final_g, loss_target, m_norm_g, m_w_in, m_q_norm_g, m_kv_norm_g, m_w_uq, m_w_ukv, m_w_out, m_final_g, v_norm_g, v_w_in, v_q_norm_g, v_kv_norm_g, v_w_uq, v_w_ukv, v_w_out, v_final_g):
    c = lax.axis_index("c")

    def families(a_in, a_uq, a_ukv, a_out):
        return [t.reshape(shape) for t, shape in zip((a_in, a_uq, a_ukv, a_out), FAM_SHAPES)]

    g_in, g_uq, g_ukv, g_out = _allgather_weights([t.astype(BF16) for t in families(w_in, w_uq, w_ukv, w_out)])
    w_in_p = _assemble_w_in(g_in.reshape(N_SHARD, DEPTH, D_MODEL, SHARD_COLS_IN))
    w_uq_p = _pad_w_uq(_from_col_shards(g_uq, Q_LORA))
    w_ukv_p = _pad_w_ukv(_from_col_shards(g_ukv, KV_LORA))
    w_out_f = g_out.reshape(N_SHARD, DEPTH, 1024 // N_SHARD, D_MODEL).transpose(1, 0, 2, 3).reshape(DEPTH, 1024, D_MODEL)

    (loss, dx, d_norm, d_win_p, d_qn, d_kvn, d_wuq_p, d_wukv_p, d_wout, d_final) = _local_grads(
        x[0], loss_target[0], norm_g, w_in_p, q_norm_g, kv_norm_g, w_uq_p, w_ukv_p, w_out_f, final_g)
    loss = lax.psum(loss, ("x", "y", "c"))

    small = _pack_small(d_norm, d_qn, d_kvn, d_final)
    grads = [_split_w_in_grad(d_win_p), _col_shards(_unpad_w_uq(d_wuq_p)), _col_shards(_unpad_w_ukv(d_wukv_p)),
             d_wout.reshape(DEPTH, N_SHARD, 1024 // N_SHARD, D_MODEL).transpose(1, 0, 2, 3).reshape(
                 N_SHARD, DEPTH * (1024 // N_SHARD), D_MODEL),
             jnp.broadcast_to(small[None], (N_SHARD, SMALL_ROWS, LANES))]
    halves = [g.reshape(N_SHARD, 2, g.shape[1] // 2, g.shape[2]) for g in grads]
    from_sib = _sibling_send_halves(halves)
    chip_sum = [_add2(lax.dynamic_index_in_dim(h, c, axis=1, keepdims=False), f, "grad_add_pair", BF16)
                for h, f in zip(halves, from_sib)]
    red_half = [_add4_ordered(t, "grad_add_chips") for t in _chip_scatter(chip_sum)]
    other_half = _sibling_swap(red_half)
    gred = []
    for mine, other in zip(red_half, other_half):
        both = jnp.stack([mine, other])
        gred.append(jnp.concatenate([lax.dynamic_index_in_dim(both, c, axis=0, keepdims=False),
                                     lax.dynamic_index_in_dim(both, 1 - c, axis=0, keepdims=False)], axis=0))

    wf = families(w_in, w_uq, w_ukv, w_out) + [_pack_small(norm_g, q_norm_g, kv_norm_g, final_g)]
    mf = families(m_w_in, m_w_uq, m_w_ukv, m_w_out) + [_pack_small(m_norm_g, m_q_norm_g, m_kv_norm_g, m_final_g)]
    vf = families(v_w_in, v_w_uq, v_w_ukv, v_w_out) + [_pack_small(v_norm_g, v_q_norm_g, v_kv_norm_g, v_final_g)]
    upd = [_adamw(w, g, m, v, "adamw") for w, g, m, v in zip(wf, gred, mf, vf)]

    def leaves(fams):
        a_in, a_uq, a_ukv, a_out, s = fams
        s_norm, s_qn, s_kvn, s_final = _split_small(s)
        return [s_norm, a_in.reshape(w_in.shape), s_qn, s_kvn, a_uq.reshape(w_uq.shape), a_ukv.reshape(w_ukv.shape),
                a_out.reshape(w_out.shape), s_final]

    return (loss, dx[None], *leaves(gred), *leaves([u[0] for u in upd]), *leaves([u[1] for u in upd]),
            *leaves([u[2] for u in upd]))
```

```python
import functools

import jax
import jax.numpy as jnp
from jax import lax
from jax.experimental import pallas as pl
from jax.experimental.pallas import tpu as pltpu

F32 = jnp.float32
BF16 = jnp.bfloat16
MESH = pl.DeviceIdType.MESH

D_MODEL = 1024
DEPTH = 4
MLA_HEADS = 8
MLA_NOPE = 64
MLA_ROPE = 32
Q_LORA = 384
KV_LORA = 256
DIL_PAIRS = ((128, 1), (512, 4), (2048, 16))
DIL_HD = 64
DIL_HALF = 64
ROT_DIM = 16
ROPE_THETA = 500000.0
EPS = 1e-6
IN_WIDTH = 6304
N_SHARD = 4

P_WIDTH = 6656
P_MLA = 1024
P_GATE = 1024
P_DIL0 = 2048
LANES = 128
HEAD_W = 512

ADAM_LR = 0.001
ADAM_B1 = 0.9
ADAM_B2 = 0.999
ADAM_EPS = 1e-08
ADAM_WD = 0.01
ADAM_STEP = 10

SHARD_COLS_IN = IN_WIDTH // N_SHARD
FAM_SHAPES = ((DEPTH * D_MODEL, SHARD_COLS_IN), (DEPTH * Q_LORA, 768 // N_SHARD), (DEPTH * KV_LORA, 1024 // N_SHARD),
              (DEPTH * (1024 // N_SHARD), D_MODEL))
N_SMALL = DEPTH * (D_MODEL + Q_LORA + KV_LORA) + D_MODEL
SMALL_ROWS = 64
VMEM_BIG_MB = 48


def _cparams(vmem_mb=None):
    if vmem_mb is None:
        return None
    return pltpu.CompilerParams(vmem_limit_bytes=vmem_mb << 20)


def _sigmoid(x):
    return 1.0 / (1.0 + jnp.exp(-x))


def _rope(x, c, a, b, sh):
    return x * c + pltpu.roll(x, LANES - sh, 1) * a + pltpu.roll(x, sh, 1) * b


def _per_head8(x, pick_first):
    r = lax.broadcasted_iota(jnp.int32, (HEAD_W, MLA_HEADS), 0)
    c = lax.broadcasted_iota(jnp.int32, (HEAD_W, MLA_HEADS), 1)
    sel = (r == c * DIL_HD) if pick_first else (r // DIL_HD == c)
    mat = jnp.where(sel, 1.0, 0.0).astype(BF16)
    out = jnp.zeros((x.shape[0], MLA_HEADS), F32)
    for _ in range(3):
        part = x.astype(BF16)
        out = out + jnp.dot(part, mat, preferred_element_type=F32)
        x = x - part.astype(F32)
    return out


def _mm(a, b, *, tm, tn, tk, out_dtype, name, add=None, a_is_kxm=False):
    K, M = a.shape if a_is_kxm else a.shape[::-1]
    N = b.shape[1]
    tm, tn, tk = min(tm, M), min(tn, N), min(tk, K)
    assert M % tm == 0 and N % tn == 0 and K % tk == 0, (a.shape, b.shape)
    nk = K // tk
    has_add = add is not None

    def body(*refs):
        a_ref, b_ref = refs[0], refs[1]
        add_ref = refs[2] if has_add else None
        o_ref = refs[3] if has_add else refs[2]
        k = pl.program_id(2)
        if a_is_kxm:
            part = lax.dot_general(a_ref[...].astype(BF16), b_ref[...].astype(BF16), (((0,), (0,)), ((), ())),
                                   preferred_element_type=F32)
        else:
            part = jnp.dot(a_ref[...].astype(BF16), b_ref[...].astype(BF16), preferred_element_type=F32)
        if nk == 1:
            o_ref[...] = (part + add_ref[...] if has_add else part).astype(out_dtype)
            return
        acc = refs[-1]

        @pl.when(k == 0)
        def _():
            acc[...] = part

        @pl.when(k > 0)
        def _():
            acc[...] += part

        @pl.when(k == nk - 1)
        def _():
            r = acc[...]
            if has_add:
                r = r + add_ref[...]
            o_ref[...] = r.astype(out_dtype)

    a_spec = pl.BlockSpec((tk, tm), lambda i, j, k: (k, i)) if a_is_kxm else pl.BlockSpec((tm, tk), lambda i, j, k: (i, k))
    in_specs = [a_spec, pl.BlockSpec((tk, tn), lambda i, j, k: (k, j))]
    args = [a, b]
    if has_add:
        in_specs.append(pl.BlockSpec((tm, tn), lambda i, j, k: (i, j)))
        args.append(add)
    return pl.pallas_call(
        body, name=name, grid=(M // tm, N // tn, nk), in_specs=in_specs,
        out_specs=pl.BlockSpec((tm, tn), lambda i, j, k: (i, j)),
        out_shape=jax.ShapeDtypeStruct((M, N), out_dtype),
        scratch_shapes=[pltpu.VMEM((tm, tn), F32)] if nk > 1 else [],
        compiler_params=_cparams(VMEM_BIG_MB))(*args)


def _row_spec(tm, w, cb=0):
    return pl.BlockSpec((tm, w), lambda i: (i, cb))


def _const_spec(arr):
    nd = arr.ndim
    return pl.BlockSpec(arr.shape, lambda i: (0,) * nd)


def _rms_fwd(x, g, name):
    L, D = x.shape
    tm = min(512, L)

    def body(x_ref, g_ref, o_ref):
        xv = x_ref[...]
        r = lax.rsqrt(jnp.mean(xv * xv, axis=-1, keepdims=True) + EPS)
        o_ref[...] = (xv * r * g_ref[...]).astype(BF16)

    return pl.pallas_call(
        body, name=name, grid=(L // tm,), in_specs=[_row_spec(tm, D), _const_spec(g)],
        out_specs=_row_spec(tm, D), out_shape=jax.ShapeDtypeStruct((L, D), BF16))(x, g)


def _rms_bwd(dh, x, g, dres, name):
    L, D = x.shape
    tm = min(512, L)

    def body(dh_ref, x_ref, g_ref, dres_ref, dx_ref, dg_ref):
        xv = x_ref[...]
        dy = dh_ref[...]
        r = lax.rsqrt(jnp.mean(xv * xv, axis=-1, keepdims=True) + EPS)
        dyg = dy * g_ref[...]
        dx_ref[...] = dres_ref[...] + r * dyg - xv * (r * r * r) * jnp.mean(dyg * xv, axis=-1, keepdims=True)
        part = jnp.sum(dy * xv * r, axis=0, keepdims=True)

        @pl.when(pl.program_id(0) == 0)
        def _():
            dg_ref[...] = part

        @pl.when(pl.program_id(0) > 0)
        def _():
            dg_ref[...] += part

    return pl.pallas_call(
        body, name=name, grid=(L // tm,),
        in_specs=[_row_spec(tm, D), _row_spec(tm, D), _const_spec(g), _row_spec(tm, D)],
        out_specs=[_row_spec(tm, D), pl.BlockSpec((1, D), lambda i: (0, 0))],
        out_shape=[jax.ShapeDtypeStruct((L, D), F32), jax.ShapeDtypeStruct((1, D), F32)])(dh, x, g, dres)


def _loss_head(x, g, target, name):
    L, D = x.shape
    tm = min(512, L)

    def body(x_ref, g_ref, t_ref, loss_ref, dx_ref, dg_ref):
        xv = x_ref[...]
        gv = g_ref[...]
        r = lax.rsqrt(jnp.mean(xv * xv, axis=-1, keepdims=True) + EPS)
        xr = xv * r
        err = xr * gv - t_ref[...]
        lp = 0.5 * jnp.sum(jnp.mean(err * err, axis=-1, keepdims=True))
        dy = err * (1.0 / D)
        dyg = dy * gv
        dx_ref[...] = r * dyg - xv * (r * r * r) * jnp.mean(dyg * xv, axis=-1, keepdims=True)
        part = jnp.sum(dy * xr, axis=0, keepdims=True)

        @pl.when(pl.program_id(0) == 0)
        def _():
            dg_ref[...] = part
            loss_ref[...] = jnp.zeros(loss_ref.shape, F32) + lp

        @pl.when(pl.program_id(0) > 0)
        def _():
            dg_ref[...] += part
            loss_ref[...] += lp

    return pl.pallas_call(
        body, name=name, grid=(L // tm,),
        in_specs=[_row_spec(tm, D), _const_spec(g), _row_spec(tm, D)],
        out_specs=[pl.BlockSpec((8, LANES), lambda i: (0, 0)), _row_spec(tm, D), pl.BlockSpec((1, D), lambda i: (0, 0))],
        out_shape=[jax.ShapeDtypeStruct((8, LANES), F32), jax.ShapeDtypeStruct((L, D), F32),
                   jax.ShapeDtypeStruct((1, D), F32)])(x, g, target)


def _mla_prep(p, qg, kvg, wuq, wukv, tabs, name):
    L = p.shape[0]
    tm = min(512, L)
    scale = (MLA_NOPE + MLA_ROPE) ** -0.5
    tc, ta, tb = tabs

    def body(p_ref, qg_ref, kvg_ref, wuq_ref, wukv_ref, c_ref, a_ref, b_ref, q_ref, k_ref, v_ref, cqn_ref, ckvn_ref):
        c, a, b = c_ref[...], a_ref[...], b_ref[...]
        cq = p_ref[:, 0:Q_LORA].astype(F32)
        ckv = p_ref[:, Q_LORA:Q_LORA + KV_LORA].astype(F32)
        kr = p_ref[:, 640:768].astype(F32)
        cqn = (cq * lax.rsqrt(jnp.mean(cq * cq, axis=-1, keepdims=True) + EPS) * qg_ref[...]).astype(BF16)
        ckvn = (ckv * lax.rsqrt(jnp.mean(ckv * ckv, axis=-1, keepdims=True) + EPS) * kvg_ref[...]).astype(BF16)
        cqn_ref[...] = cqn
        ckvn_ref[...] = ckvn
        q = jnp.dot(cqn, wuq_ref[...], preferred_element_type=F32)
        kv = jnp.dot(ckvn, wukv_ref[...], preferred_element_type=F32)
        krr = _rope(kr, c, a, b, MLA_ROPE // 2)
        for h in range(MLA_HEADS):
            sl = slice(h * LANES, (h + 1) * LANES)
            q_ref[:, sl] = (_rope(q[:, sl], c, a, b, MLA_ROPE // 2) * (scale * LOG2E)).astype(BF16)
            k_ref[:, sl] = (kv[:, sl] + krr).astype(BF16)
        v_ref[...] = kv[:, 1024:1536].astype(BF16)

    return pl.pallas_call(
        body, name=name, grid=(L // tm,),
        in_specs=[_row_spec(tm, P_MLA, 0), _const_spec(qg), _const_spec(kvg), _const_spec(wuq), _const_spec(wukv),
                  _row_spec(tm, LANES), _row_spec(tm, LANES), _row_spec(tm, LANES)],
        out_specs=[_row_spec(tm, 1024), _row_spec(tm, 1024), _row_spec(tm, HEAD_W), _row_spec(tm, Q_LORA),
                   _row_spec(tm, KV_LORA)],
        out_shape=[jax.ShapeDtypeStruct((L, 1024), BF16), jax.ShapeDtypeStruct((L, 1024), BF16),
                   jax.ShapeDtypeStruct((L, HEAD_W), BF16), jax.ShapeDtypeStruct((L, Q_LORA), BF16),
                   jax.ShapeDtypeStruct((L, KV_LORA), BF16)],
        compiler_params=_cparams(VMEM_BIG_MB))(p, qg, kvg, wuq, wukv, tc, ta, tb)


def _mla_prep_bwd(dq, dk, dv, p, qg, kvg, wuq_t, wukv_t, tabs_t, name):
    L = p.shape[0]
    tm = min(512, L)
    scale = (MLA_NOPE + MLA_ROPE) ** -0.5
    tc, ta, tb = tabs_t

    def body(dq_ref, dk_ref, dv_ref, p_ref, qg_ref, kvg_ref, wuqt_ref, wukvt_ref, c_ref, a_ref, b_ref,
             dp_ref, dqp_ref, dkv_ref, dqg_ref, dkvg_ref):
        c, a, b = c_ref[...], a_ref[...], b_ref[...]
        dkr = jnp.zeros((tm, LANES), F32)
        for h in range(MLA_HEADS):
            sl = slice(h * LANES, (h + 1) * LANES)
            dqp_ref[:, sl] = (_rope(dq_ref[:, sl].astype(F32), c, a, b, MLA_ROPE // 2) * scale).astype(BF16)
            dkh = dk_ref[:, sl].astype(F32) * LN2
            dkv_ref[:, sl] = dkh.astype(BF16)
            dkr = dkr + dkh
        dkv_ref[:, 1024:1536] = dv_ref[...].astype(BF16)
        lane = lax.broadcasted_iota(jnp.int32, (tm, LANES), 1)
        dkr = jnp.where((lane >= MLA_NOPE) & (lane < MLA_NOPE + MLA_ROPE), _rope(dkr, c, a, b, MLA_ROPE // 2), 0.0)

        d_cqn = jnp.dot(dqp_ref[...], wuqt_ref[...], preferred_element_type=F32)
        d_ckvn = jnp.dot(dkv_ref[...], wukvt_ref[...], preferred_element_type=F32)

        def norm_bwd(xv, gv, dy):
            r = lax.rsqrt(jnp.mean(xv * xv, axis=-1, keepdims=True) + EPS)
            dyg = dy * gv
            dx = r * dyg - xv * (r * r * r) * jnp.mean(dyg * xv, axis=-1, keepdims=True)
            return dx, jnp.sum(dy * xv * r, axis=0, keepdims=True)

        d_cq, dqg = norm_bwd(p_ref[:, 0:Q_LORA].astype(F32), qg_ref[...], d_cqn)
        d_ckv, dkvg = norm_bwd(p_ref[:, Q_LORA:Q_LORA + KV_LORA].astype(F32), kvg_ref[...], d_ckvn)
        dp_ref[:, 0:Q_LORA] = d_cq.astype(BF16)
        dp_ref[:, Q_LORA:Q_LORA + KV_LORA] = d_ckv.astype(BF16)
        dp_ref[:, 640:768] = dkr.astype(BF16)
        dp_ref[:, 768:1024] = jnp.zeros((tm, 256), BF16)

        @pl.when(pl.program_id(0) == 0)
        def _():
            dqg_ref[...] = dqg
            dkvg_ref[...] = dkvg

        @pl.when(pl.program_id(0) > 0)
        def _():
            dqg_ref[...] += dqg
            dkvg_ref[...] += dkvg

    return pl.pallas_call(
        body, name=name, grid=(L // tm,),
        in_specs=[_row_spec(tm, 1024), _row_spec(tm, 1024), _row_spec(tm, HEAD_W), _row_spec(tm, P_MLA, 0),
                  _const_spec(qg), _const_spec(kvg), _const_spec(wuq_t), _const_spec(wukv_t),
                  _row_spec(tm, LANES), _row_spec(tm, LANES), _row_spec(tm, LANES)],
        out_specs=[_row_spec(tm, P_MLA), _row_spec(tm, 1024), _row_spec(tm, 1536),
                   pl.BlockSpec((1, Q_LORA), lambda i: (0, 0)), pl.BlockSpec((1, KV_LORA), lambda i: (0, 0))],
        out_shape=[jax.ShapeDtypeStruct((L, P_MLA), BF16), jax.ShapeDtypeStruct((L, 1024), BF16),
                   jax.ShapeDtypeStruct((L, 1536), BF16), jax.ShapeDtypeStruct((1, Q_LORA), F32),
                   jax.ShapeDtypeStruct((1, KV_LORA), F32)],
        compiler_params=_cparams(VMEM_BIG_MB))(dq, dk, dv, p, qg, kvg, wuq_t, wukv_t, tc, ta, tb)


def _dil_prep(p, tabs, name):
    L = p.shape[0]
    tm = min(512, L)
    tc, ta, tb = tabs
    scale = DIL_HD ** -0.5

    def body(*refs):
        ins, (c_ref, a_ref, b_ref), outs = refs[:9], refs[9:12], refs[12:]
        c, a, b = c_ref[...], a_ref[...], b_ref[...]
        for n in range(9):
            t = n % 3
            for cb in range(HEAD_W // LANES):
                sl = slice(cb * LANES, (cb + 1) * LANES)
                xv = ins[n][:, sl].astype(F32)
                if t == 0:
                    xv = _rope(xv, c, a, b, ROT_DIM // 2) * (scale * LOG2E)
                elif t == 1:
                    xv = _rope(xv, c, a, b, ROT_DIM // 2)
                outs[n][:, sl] = xv.astype(BF16)

    in_specs = [_row_spec(tm, HEAD_W, P_DIL0 // HEAD_W + n) for n in range(9)] + [_row_spec(tm, LANES)] * 3
    return pl.pallas_call(
        body, name=name, grid=(L // tm,), in_specs=in_specs,
        out_specs=[_row_spec(tm, HEAD_W)] * 9,
        out_shape=[jax.ShapeDtypeStruct((L, HEAD_W), BF16)] * 9)(*([p] * 9), tc, ta, tb)


def _dil_prep_bwd(grads, tabs_t, name):
    L = grads[0].shape[0]
    tm = min(512, L)
    tc, ta, tb = tabs_t
    scale = DIL_HD ** -0.5

    def body(*refs):
        ins, (c_ref, a_ref, b_ref), o_ref = refs[:9], refs[9:12], refs[12]
        c, a, b = c_ref[...], a_ref[...], b_ref[...]
        for n in range(9):
            t = n % 3
            for cb in range(HEAD_W // LANES):
                sl = slice(cb * LANES, (cb + 1) * LANES)
                xv = ins[n][:, sl].astype(F32)
                if t == 0:
                    xv = _rope(xv, c, a, b, ROT_DIM // 2) * scale
                elif t == 1:
                    xv = _rope(xv, c, a, b, ROT_DIM // 2) * LN2
                o_ref[:, n * HEAD_W + cb * LANES:n * HEAD_W + (cb + 1) * LANES] = xv.astype(BF16)

    return pl.pallas_call(
        body, name=name, grid=(L // tm,), in_specs=[_row_spec(tm, HEAD_W)] * 9 + [_row_spec(tm, LANES)] * 3,
        out_specs=_row_spec(tm, 9 * HEAD_W), out_shape=jax.ShapeDtypeStruct((L, 9 * HEAD_W), BF16),
        compiler_params=_cparams(VMEM_BIG_MB))(*grads, tc, ta, tb)


def _merge_gate(oa, p, o_g, lse_g, name):
    L = oa.shape[0]
    tm = min(512, L)

    def body(oa_ref, ga_ref, gb_ref, o1, o2, o3, l1, l2, l3, ab_ref, bm_ref, lt_ref):
        la, lb, lc = l1[...], l2[...], l3[...]
        m = jnp.maximum(jnp.maximum(la, lb), lc)
        ea, eb, ec = jnp.exp2(la - m), jnp.exp2(lb - m), jnp.exp2(lc - m)
        den = ea + eb + ec
        bm = (ea * o1[...] + eb * o2[...] + ec * o3[...]) / den
        bm_ref[...] = bm
        lt_ref[...] = _per_head8(m + jnp.log2(den), True)
        ga, gb = ga_ref[...].astype(F32), gb_ref[...].astype(F32)
        ab_ref[:, 0:HEAD_W] = (oa_ref[...] * (ga * _sigmoid(ga))).astype(BF16)
        ab_ref[:, HEAD_W:2 * HEAD_W] = (bm * (gb * _sigmoid(gb))).astype(BF16)

    w = _row_spec(tm, HEAD_W)
    return pl.pallas_call(
        body, name=name, grid=(L // tm,),
        in_specs=[w, _row_spec(tm, HEAD_W, 2), _row_spec(tm, HEAD_W, 3), w, w, w, w, w, w],
        out_specs=[_row_spec(tm, 2 * HEAD_W), w, _row_spec(tm, MLA_HEADS)],
        out_shape=[jax.ShapeDtypeStruct((L, 2 * HEAD_W), BF16), jax.ShapeDtypeStruct((L, HEAD_W), F32),
                   jax.ShapeDtypeStruct((L, MLA_HEADS), F32)])(oa, p, p, *o_g, *lse_g)


def _gate_bwd(dab, p, oa, bm, name):
    L = oa.shape[0]
    tm = min(512, L)

    def body(da_ref, db_ref, ga_ref, gb_ref, oa_ref, bm_ref, doa_ref, dbm_ref, Da_ref, Db_ref, dg_ref):
        def one(d, g, o, do_ref, D_ref, col):
            sg = _sigmoid(g)
            do = d * (g * sg)
            do_ref[...] = do.astype(BF16)
            dg_ref[:, col:col + HEAD_W] = (d * o * (sg * (1.0 + g * (1.0 - sg)))).astype(BF16)
            D_ref[...] = _per_head8(do * o, False)

        one(da_ref[...], ga_ref[...].astype(F32), oa_ref[...], doa_ref, Da_ref, 0)
        one(db_ref[...], gb_ref[...].astype(F32), bm_ref[...], dbm_ref, Db_ref, HEAD_W)

    w = _row_spec(tm, HEAD_W)
    w8 = _row_spec(tm, MLA_HEADS)
    return pl.pallas_call(
        body, name=name, grid=(L // tm,),
        in_specs=[_row_spec(tm, HEAD_W, 0), _row_spec(tm, HEAD_W, 1), _row_spec(tm, HEAD_W, 2),
                  _row_spec(tm, HEAD_W, 3), w, w],
        out_specs=[w, w, w8, w8, _row_spec(tm, 2 * HEAD_W)],
        out_shape=[jax.ShapeDtypeStruct((L, HEAD_W), BF16), jax.ShapeDtypeStruct((L, HEAD_W), BF16),
                   jax.ShapeDtypeStruct((L, MLA_HEADS), F32), jax.ShapeDtypeStruct((L, MLA_HEADS), F32),
                   jax.ShapeDtypeStruct((L, 2 * HEAD_W), BF16)])(dab, dab, p, p, oa, bm)


NT = (((1,), (1,)), ((), ()))
TN = (((0,), (0,)), ((), ()))
NEG = -1e30


MLA_TQ = 512
MLA_TK = 4096
MLA_BWD_TQ = 1024
MLA_BWD_TK = 2048
LOG2E = 1.4426950408889634
LN2 = 0.6931471805599453


def _mla_fwd(q, k, v_t, name):
    L = q.shape[0]
    tq, tk = min(MLA_TQ, L), min(MLA_TK, L)
    nq, nk = L // tq, L // tk
    npair = MLA_HEADS // 2

    def body(q_ref, k_ref, vt_ref, o_ref, lse_ref, m0, l0, a0, m1, l1, a1):
        j = pl.program_id(2)
        stats = ((m0, l0, a0), (m1, l1, a1))

        @pl.when(j == 0)
        def _():
            for m_sc, l_sc, acc_sc in stats:
                m_sc[...] = jnp.full(m_sc.shape, NEG, F32)
                l_sc[...] = jnp.zeros(l_sc.shape, F32)
                acc_sc[...] = jnp.zeros(acc_sc.shape, F32)

        s_ts = [lax.dot_general(k_ref[:, hh * LANES:(hh + 1) * LANES], q_ref[:, hh * LANES:(hh + 1) * LANES], NT,
                                preferred_element_type=F32) for hh in range(2)]
        for hh in range(2):
            m_sc, l_sc, acc_sc = stats[hh]
            s_t = s_ts[hh]
            m_prev = m_sc[...]
            m_new = jnp.maximum(m_prev, jnp.max(s_t, axis=0, keepdims=True))
            alpha = jnp.exp2(m_prev - m_new)
            p_t = jnp.exp2(s_t - m_new)
            l_sc[...] = alpha * l_sc[...] + jnp.sum(p_t, axis=0, keepdims=True)
            m_sc[...] = m_new
            pv = jnp.dot(vt_ref[hh * DIL_HD:(hh + 1) * DIL_HD, :], p_t.astype(BF16),
                         preferred_element_type=F32)
            acc_sc[...] = alpha * acc_sc[...] + pv

        @pl.when(j == nk - 1)
        def _():
            o_ref[...] = jnp.concatenate([a0[...] / l0[...], a1[...] / l1[...]], axis=0).T
            lse_ref[...] = jnp.concatenate([m0[...] + jnp.log2(l0[...]), m1[...] + jnp.log2(l1[...])], axis=0)

    stat = [pltpu.VMEM((1, tq), F32), pltpu.VMEM((1, tq), F32), pltpu.VMEM((DIL_HD, tq), F32)]
    return pl.pallas_call(
        body, name=name, grid=(npair, nq, nk),
        in_specs=[pl.BlockSpec((tq, 2 * LANES), lambda pr, i, j: (i, pr)),
                  pl.BlockSpec((tk, 2 * LANES), lambda pr, i, j: (j, pr)),
                  pl.BlockSpec((LANES, tk), lambda pr, i, j: (pr, j))],
        out_specs=[pl.BlockSpec((tq, LANES), lambda pr, i, j: (i, pr)),
                   pl.BlockSpec((None, 2, tq), lambda pr, i, j: (pr, 0, i))],
        out_shape=[jax.ShapeDtypeStruct((L, HEAD_W), F32), jax.ShapeDtypeStruct((npair, 2, L), F32)],
        scratch_shapes=stat + stat, compiler_params=_cparams(VMEM_BIG_MB))(q, k, v_t)


def _mla_bwd(q, k, v, q_t, do, do_t, lse_cols, d_cols, name):
    L = q.shape[0]
    tq, tk = min(MLA_BWD_TQ, L), min(MLA_BWD_TK, L)
    nq, nk = L // tq, L // tk
    npair = MLA_HEADS // 2

    def body(q_ref, k_ref, v_ref, qt_ref, do_ref, dot_ref, lse_ref, d_ref, dq_out, dkt_out, dvt_out,
             dq_ref, dkt_ref, dvt_ref):
        j, i = pl.program_id(1), pl.program_id(2)

        @pl.when((j == 0) & (i == 0))
        def _():
            dq_ref[...] = jnp.zeros(dq_ref.shape, F32)

        @pl.when(i == 0)
        def _():
            dkt_ref[...] = jnp.zeros(dkt_ref.shape, F32)
            dvt_ref[...] = jnp.zeros(dvt_ref.shape, F32)

        first = lax.broadcasted_iota(jnp.int32, (tq, LANES), 1) < DIL_HD
        dov = do_ref[...]
        vv = v_ref[...]
        rows = pl.ds(pl.multiple_of(i * tq, tq), tq)
        for hh in range(2):
            sl = slice(hh * LANES, (hh + 1) * LANES)
            hrows = slice(hh * DIL_HD, (hh + 1) * DIL_HD)
            qh, kh = q_ref[:, sl], k_ref[:, sl]
            do_h = jnp.where(first if hh == 0 else ~first, dov, jnp.zeros_like(dov))
            s = lax.dot_general(qh, kh, NT, preferred_element_type=F32)
            p = jnp.exp2(s - lse_ref[:, hh:hh + 1])
            dvt_ref[hrows, :] += jnp.dot(dot_ref[hrows, :], p.astype(BF16), preferred_element_type=F32)
            dp = lax.dot_general(do_h, vv, NT, preferred_element_type=F32)
            ds = (p * (dp - d_ref[:, hh:hh + 1])).astype(BF16)
            dq_ref[rows, sl] += jnp.dot(ds, kh, preferred_element_type=F32)
            dkt_ref[sl, :] += jnp.dot(qt_ref[sl, :], ds, preferred_element_type=F32)

        @pl.when(i == nq - 1)
        def _():
            dkt_out[...] = dkt_ref[...].astype(BF16)
            dvt_out[...] = dvt_ref[...].astype(BF16)

        @pl.when((j == nk - 1) & (i == nq - 1))
        def _():
            dq_out[...] = dq_ref[...].astype(BF16)

    colspec = pl.BlockSpec((None, tq, 2), lambda pr, j, i: (pr, i, 0))
    return pl.pallas_call(
        body, name=name, grid=(npair, nk, nq),
        in_specs=[pl.BlockSpec((tq, 2 * LANES), lambda pr, j, i: (i, pr)),
                  pl.BlockSpec((tk, 2 * LANES), lambda pr, j, i: (j, pr)),
                  pl.BlockSpec((tk, LANES), lambda pr, j, i: (j, pr)),
                  pl.BlockSpec((2 * LANES, tq), lambda pr, j, i: (pr, i)),
                  pl.BlockSpec((tq, LANES), lambda pr, j, i: (i, pr)),
                  pl.BlockSpec((LANES, tq), lambda pr, j, i: (pr, i)),
                  colspec, colspec],
        out_specs=[pl.BlockSpec((L, 2 * LANES), lambda pr, j, i: (0, pr)),
                   pl.BlockSpec((2 * LANES, tk), lambda pr, j, i: (pr, j)),
                   pl.BlockSpec((LANES, tk), lambda pr, j, i: (pr, j))],
        out_shape=[jax.ShapeDtypeStruct((L, 1024), BF16), jax.ShapeDtypeStruct((1024, L), BF16),
                   jax.ShapeDtypeStruct((HEAD_W, L), BF16)],
        scratch_shapes=[pltpu.VMEM((L, 2 * LANES), F32), pltpu.VMEM((2 * LANES, tk), F32),
                        pltpu.VMEM((LANES, tk), F32)],
        compiler_params=_cparams(VMEM_BIG_MB))(q, k, v, q_t, do, do_t, lse_cols, d_cols)


DIL_TQ = 4096
DIL_SQ = 128
DIL_SW = DIL_SQ + 2 * DIL_HALF


def _dil_window(a_sub, ld):
    return pl.multiple_of(jnp.clip(a_sub - DIL_HALF, 0, ld - DIL_SW), DIL_HALF)


def _dil_band_mask(shift, heads):
    kidx = lax.broadcasted_iota(jnp.int32, (DIL_SW, heads * DIL_SQ), 0)
    qidx = lax.broadcasted_iota(jnp.int32, (DIL_SW, heads * DIL_SQ), 1) % DIL_SQ
    return jnp.abs(shift + kidx - qidx) <= DIL_HALF


def _pair_rows(x, first):
    zero = jnp.zeros_like(x)
    return jnp.concatenate([jnp.where(first, x, zero), jnp.where(first, zero, x)], axis=0)


def _dil_fwd(q, kp, vp_t4, name):
    d, ld, _ = q.shape
    assert ld % DIL_SQ == 0
    tq = min(DIL_TQ, ld)
    nq = ld // tq
    npair = HEAD_W // LANES
    nb = (ld + 2 * DIL_HALF) // LANES

    def body(q_ref, k_ref, vt_ref, o_ref, lse_ref):
        i = pl.program_id(2)
        first = lax.broadcasted_iota(jnp.int32, (DIL_SQ, LANES), 1) < DIL_HD
        kidx = lax.broadcasted_iota(jnp.int32, (DIL_SW, DIL_SQ), 0)
        for u in range(tq // DIL_SQ):
            a_sub = pl.multiple_of(i * tq + u * DIL_SQ, DIL_SQ)
            kk = a_sub // LANES
            rows = slice(u * DIL_SQ, (u + 1) * DIL_SQ)
            kwin = k_ref[pl.ds(a_sub, DIL_SW), :]
            valid = _dil_band_mask(-DIL_HALF, 1) & (kidx >= DIL_HALF - a_sub) & (kidx < ld + DIL_HALF - a_sub)
            qv = q_ref[rows, :]
            outs, lses = [], []
            for hh in range(2):
                qh = jnp.where(first if hh == 0 else ~first, qv, jnp.zeros_like(qv))
                s_t = jnp.where(valid, lax.dot_general(kwin, qh, NT, preferred_element_type=F32), NEG)
                m = jnp.max(s_t, axis=0, keepdims=True)
                p32 = jnp.exp2(s_t - m)
                l = jnp.sum(p32, axis=0, keepdims=True)
                p_t = p32.astype(BF16)
                hrows = slice(hh * DIL_HD, (hh + 1) * DIL_HD)
                pv = (jnp.dot(vt_ref[kk, hrows, :], p_t[0:LANES, :], preferred_element_type=F32)
                      + jnp.dot(vt_ref[kk + 1, hrows, :], p_t[LANES:DIL_SW, :], preferred_element_type=F32))
                outs.append(pv / l)
                lses.append(jnp.broadcast_to(m + jnp.log2(l), (DIL_HD, DIL_SQ)))
            o_ref[rows, :] = jnp.concatenate(outs, axis=0).T
            lse_ref[rows, :] = jnp.concatenate(lses, axis=0).T

    blk = pl.BlockSpec((None, tq, LANES), lambda r, pr, i: (r, i, pr))
    full = pl.BlockSpec((None, ld + 2 * DIL_HALF, LANES), lambda r, pr, i: (r, 0, pr))
    vspec = pl.BlockSpec((None, nb, LANES, LANES), lambda r, pr, i: (r, 0, pr, 0))
    return pl.pallas_call(
        body, name=name, grid=(d, npair, nq), in_specs=[blk, full, vspec], out_specs=[blk, blk],
        out_shape=[jax.ShapeDtypeStruct((d, ld, HEAD_W), F32), jax.ShapeDtypeStruct((d, ld, HEAD_W), F32)],
        compiler_params=_cparams(VMEM_BIG_MB))(q, kp, vp_t4)


def _dil_bwd(q, k, v, do, lse_rows, d_rows, name):
    d, ld, _ = q.shape
    assert ld % DIL_SQ == 0 and ld >= DIL_SW
    tq = min(DIL_TQ, ld)
    nq = ld // tq
    npair = HEAD_W // LANES
    span = min(tq + 2 * DIL_HALF, ld)

    def body(q_ref, k_ref, v_ref, do_ref, lse_ref, d_ref, dq_ref, dk_ref, dv_ref, dk_sc, dv_sc, dk_acc, dv_acc):
        i = pl.program_id(2)

        @pl.when(i == 0)
        def _():
            dk_acc[...] = jnp.zeros(dk_acc.shape, F32)
            dv_acc[...] = jnp.zeros(dv_acc.shape, F32)

        dk_sc[...] = jnp.zeros(dk_sc.shape, F32)
        dv_sc[...] = jnp.zeros(dv_sc.shape, F32)
        first = lax.broadcasted_iota(jnp.int32, (DIL_SQ, LANES), 1) < DIL_HD
        base = pl.multiple_of(jnp.clip(i * tq - DIL_HALF, 0, ld - span), DIL_HALF)
        for u in range(tq // DIL_SQ):
            a_sub = i * tq + u * DIL_SQ
            ws = _dil_window(a_sub, ld)
            rows = slice(u * DIL_SQ, (u + 1) * DIL_SQ)
            win = pl.ds(pl.multiple_of(ws - base, DIL_HALF), DIL_SW)
            kwin = k_ref[pl.ds(ws, DIL_SW), :]
            vwin = v_ref[pl.ds(ws, DIL_SW), :]
            q2 = _pair_rows(q_ref[rows, :], first)
            do2 = _pair_rows(do_ref[rows, :], first)
            lse2 = jnp.concatenate([lse_ref[0:1, rows], lse_ref[1:2, rows]], axis=1)
            dd2 = jnp.concatenate([d_ref[0:1, rows], d_ref[1:2, rows]], axis=1)
            s_t = lax.dot_general(kwin, q2, NT, preferred_element_type=F32)
            p_t = jnp.exp2(jnp.where(_dil_band_mask(ws - a_sub, 2), s_t, NEG) - lse2)
            dv_sc[win, :] += jnp.dot(p_t.astype(BF16), do2, preferred_element_type=F32)
            dp_t = lax.dot_general(vwin, do2, NT, preferred_element_type=F32)
            ds_t = (p_t * (dp_t - dd2)).astype(BF16)
            dk_sc[win, :] += jnp.dot(ds_t, q2, preferred_element_type=F32)
            dq2 = lax.dot_general(ds_t, kwin, TN, preferred_element_type=F32)
            dq_ref[rows, :] = jnp.where(first, dq2[0:DIL_SQ, :], dq2[DIL_SQ:2 * DIL_SQ, :]).astype(BF16)
        dk_acc[pl.ds(base, span), :] += dk_sc[...]
        dv_acc[pl.ds(base, span), :] += dv_sc[...]

        @pl.when(i == nq - 1)
        def _():
            dk_ref[...] = dk_acc[...].astype(BF16)
            dv_ref[...] = dv_acc[...].astype(BF16)

    blk = pl.BlockSpec((None, tq, LANES), lambda r, pr, i: (r, i, pr))
    full = pl.BlockSpec((None, ld, LANES), lambda r, pr, i: (r, 0, pr))
    rowspec = pl.BlockSpec((None, None, 2, tq), lambda r, pr, i: (r, pr, 0, i))
    return pl.pallas_call(
        body, name=name, grid=(d, npair, nq), in_specs=[blk, full, full, blk, rowspec, rowspec],
        out_specs=[blk, full, full],
        out_shape=[jax.ShapeDtypeStruct((d, ld, HEAD_W), BF16)] * 3,
        scratch_shapes=[pltpu.VMEM((span, LANES), F32), pltpu.VMEM((span, LANES), F32),
                        pltpu.VMEM((ld, LANES), F32), pltpu.VMEM((ld, LANES), F32)],
        compiler_params=_cparams(VMEM_BIG_MB))(q, k, v, do, lse_rows, d_rows)


TILE_BYTES = 1 << 21


def _row_tile(rows, cols, budget=TILE_BYTES):
    for parts in range(1, rows + 1):
        tr = rows // parts
        if rows % parts == 0 and tr % 8 == 0 and tr * cols * 4 <= budget:
            return tr
    return rows


def _add2(a, b, name, out_dtype):
    n, rows, cols = a.shape
    tr = _row_tile(rows, cols)

    def body(a_ref, b_ref, o_ref):
        o_ref[...] = (a_ref[...] + b_ref[...]).astype(out_dtype)

    spec = pl.BlockSpec((None, tr, cols), lambda t, i: (t, i, 0))
    return pl.pallas_call(body, name=name, grid=(n, rows // tr), in_specs=[spec, spec], out_specs=spec,
                          out_shape=jax.ShapeDtypeStruct(a.shape, out_dtype))(a, b)


def _add4_ordered(a, name):
    _, rows, cols = a.shape
    tr = _row_tile(rows, cols, TILE_BYTES // 4)

    def body(a_ref, o_ref):
        o_ref[...] = ((a_ref[0].astype(F32) + a_ref[1].astype(F32)) + a_ref[2].astype(F32)) + a_ref[3].astype(F32)

    return pl.pallas_call(
        body, name=name, grid=(rows // tr,), in_specs=[pl.BlockSpec((4, tr, cols), lambda i: (0, i, 0))],
        out_specs=pl.BlockSpec((tr, cols), lambda i: (i, 0)),
        out_shape=jax.ShapeDtypeStruct((rows, cols), F32))(a)


def _adamw(w, g, m, v, name):
    rows, cols = w.shape
    tr = _row_tile(rows, cols)
    bc1 = 1.0 - ADAM_B1 ** ADAM_STEP
    bc2 = 1.0 - ADAM_B2 ** ADAM_STEP

    def body(w_ref, g_ref, m_ref, v_ref, d_ref, nm_ref, nv_ref):
        gv = g_ref[...]
        nm = ADAM_B1 * m_ref[...] + (1.0 - ADAM_B1) * gv
        nv = ADAM_B2 * v_ref[...] + (1.0 - ADAM_B2) * (gv * gv)
        d_ref[...] = -ADAM_LR * ((nm / bc1) / (jnp.sqrt(nv / bc2) + ADAM_EPS) + ADAM_WD * w_ref[...])
        nm_ref[...] = nm
        nv_ref[...] = nv

    spec = pl.BlockSpec((tr, cols), lambda i: (i, 0))
    return pl.pallas_call(body, name=name, grid=(rows // tr,), in_specs=[spec] * 4, out_specs=[spec] * 3,
                          out_shape=[jax.ShapeDtypeStruct(w.shape, F32)] * 3,
                          compiler_params=_cparams(VMEM_BIG_MB))(w, g, m, v)


ANY = pl.BlockSpec(memory_space=pl.ANY)


def _place():
    return lax.axis_index("x"), lax.axis_index("y"), lax.axis_index("c")


def _rcopy(send_sems, recv_sems, n, src, dst, to):
    return pltpu.make_async_remote_copy(src_ref=src, dst_ref=dst, send_sem=send_sems.at[n], recv_sem=recv_sems.at[n],
                                        device_id=to, device_id_type=MESH)


def _allgather_weights(shards):
    na = len(shards)
    ns = 7

    def body(*refs):
        w_refs, g_refs = refs[:na], refs[na:2 * na]
        send_sems, recv_sems, local_sems = refs[2 * na:]
        x, y, c = _place()
        s, sx, sy, sd = 2 * x + y, 2 * (1 - x) + y, 2 * x + (1 - y), 2 * (1 - x) + (1 - y)
        to_x, to_y, sib = (1 - x, y, c), (x, 1 - y, c), (x, y, 1 - c)

        def part(a, shard, h, k=None):
            hr = shards[a].shape[0] // 2
            if k is None:
                return g_refs[a].at[shard, pl.ds(h * hr, hr), :]
            return g_refs[a].at[shard, pl.ds(h * hr + k * (hr // 2), hr // 2), :]

        def cp(a, n, src, dst, to):
            return _rcopy(send_sems, recv_sems, ns * a + n, src, dst, to)

        started, sends = [], []

        def go(copy):
            copy.start()
            sends.append(copy)

        for a in range(na):
            hr = shards[a].shape[0] // 2
            mine = pltpu.make_async_copy(w_refs[a], g_refs[a].at[s], local_sems.at[a])
            mine.start()
            started.append(mine)
            own = w_refs[a].at[pl.ds(c * hr, hr), :]
            go(cp(a, 0, own, part(a, s, c), to_x))
            go(cp(a, 1, own, part(a, s, c), to_y))
        for a in range(na):
            cp(a, 0, part(a, sx, c), part(a, sx, c), to_x).wait_recv()
            go(cp(a, 2, part(a, sx, c, 0), part(a, sx, c, 0), to_y))
            go(cp(a, 4, part(a, sx, c), part(a, sx, c), sib))
            cp(a, 1, part(a, sy, c), part(a, sy, c), to_y).wait_recv()
            go(cp(a, 3, part(a, sy, c, 1), part(a, sy, c, 1), to_x))
            go(cp(a, 5, part(a, sy, c), part(a, sy, c), sib))
        for a in range(na):
            cp(a, 2, part(a, sd, c, 0), part(a, sd, c, 0), to_y).wait_recv()
            cp(a, 3, part(a, sd, c, 1), part(a, sd, c, 1), to_x).wait_recv()
            go(cp(a, 6, part(a, sd, c), part(a, sd, c), sib))
        for a in range(na):
            for n, sj in ((4, sx), (5, sy), (6, sd)):
                cp(a, n, part(a, sj, 1 - c), part(a, sj, 1 - c), sib).wait_recv()
        for copy in sends:
            copy.wait_send()
        for mine in started:
            mine.wait()

    return pl.pallas_call(
        body, name="allgather_weights", in_specs=[ANY] * na, out_specs=[ANY] * na,
        out_shape=[jax.ShapeDtypeStruct((N_SHARD,) + t.shape, t.dtype) for t in shards],
        scratch_shapes=[pltpu.SemaphoreType.DMA((ns * na,)), pltpu.SemaphoreType.DMA((ns * na,)),
                        pltpu.SemaphoreType.DMA((na,))])(*shards)


def _sibling_send_halves(gs):
    na = len(gs)

    def body(*refs):
        g_refs, o_refs = refs[:na], refs[na:2 * na]
        send_sems, recv_sems = refs[2 * na:]
        x, y, c = _place()
        cps = []
        for a in range(na):
            for t in range(N_SHARD):
                cp = _rcopy(send_sems, recv_sems, N_SHARD * a + t, g_refs[a].at[t, 1 - c], o_refs[a].at[t],
                            (x, y, 1 - c))
                cp.start()
                cps.append(cp)
        for cp in cps:
            cp.wait()

    return pl.pallas_call(
        body, name="grad_sibling_exchange", in_specs=[ANY] * na, out_specs=[ANY] * na,
        out_shape=[jax.ShapeDtypeStruct((N_SHARD,) + g.shape[2:], g.dtype) for g in gs],
        scratch_shapes=[pltpu.SemaphoreType.DMA((N_SHARD * na,)), pltpu.SemaphoreType.DMA((N_SHARD * na,))])(*gs)


def _chip_scatter(parts):
    na = len(parts)

    def body(*refs):
        a_refs, o_refs = refs[:na], refs[na:2 * na]
        send_sems, recv_sems, local_sems = refs[2 * na:]
        x, y, c = _place()
        s = 2 * x + y
        chips = [(1 - x, y), (x, 1 - y), (1 - x, 1 - y)]
        started, cps = [], []
        for a in range(na):
            mine = pltpu.make_async_copy(a_refs[a].at[s], o_refs[a].at[s], local_sems.at[a])
            mine.start()
            started.append(mine)
            for n, (cx, cy) in enumerate(chips):
                cp = _rcopy(send_sems, recv_sems, 3 * a + n, a_refs[a].at[2 * cx + cy], o_refs[a].at[s], (cx, cy, c))
                cp.start()
                cps.append(cp)
        for a in range(na):
            for n, (cx, cy) in enumerate(chips):
                sj = 2 * cx + cy
                _rcopy(send_sems, recv_sems, 3 * a + n, a_refs[a].at[sj], o_refs[a].at[sj], (cx, cy, c)).wait_recv()
        for cp in cps:
            cp.wait_send()
        for mine in started:
            mine.wait()

    return pl.pallas_call(
        body, name="grad_chip_scatter", in_specs=[ANY] * na, out_specs=[ANY] * na,
        out_shape=[jax.ShapeDtypeStruct(t.shape, t.dtype) for t in parts],
        scratch_shapes=[pltpu.SemaphoreType.DMA((3 * na,)), pltpu.SemaphoreType.DMA((3 * na,)),
                        pltpu.SemaphoreType.DMA((na,))])(*parts)


def _sibling_swap(rs):
    na = len(rs)

    def body(*refs):
        r_refs, o_refs = refs[:na], refs[na:2 * na]
        send_sems, recv_sems = refs[2 * na:]
        x, y, c = _place()
        cps = []
        for a in range(na):
            cp = _rcopy(send_sems, recv_sems, a, r_refs[a], o_refs[a], (x, y, 1 - c))
            cp.start()
            cps.append(cp)
        for cp in cps:
            cp.wait()

    return pl.pallas_call(
        body, name="grad_sibling_swap", in_specs=[ANY] * na, out_specs=[ANY] * na,
        out_shape=[jax.ShapeDtypeStruct(t.shape, t.dtype) for t in rs],
        scratch_shapes=[pltpu.SemaphoreType.DMA((na,)), pltpu.SemaphoreType.DMA((na,))])(*rs)


def _pack_small(norm_g, q_norm_g, kv_norm_g, final_g):
    flat = jnp.concatenate([norm_g.reshape(-1), q_norm_g.reshape(-1), kv_norm_g.reshape(-1), final_g.reshape(-1),
                            jnp.zeros((SMALL_ROWS * LANES - N_SMALL,), F32)])
    return flat.reshape(SMALL_ROWS, LANES)


def _split_small(s):
    s = s.reshape(-1)
    o = 0
    out = []
    for n, shape in ((DEPTH * D_MODEL, (DEPTH, D_MODEL)), (DEPTH * Q_LORA, (DEPTH, Q_LORA)),
                     (DEPTH * KV_LORA, (DEPTH, KV_LORA)), (D_MODEL, (D_MODEL,))):
        out.append(s[o:o + n].reshape(shape))
        o += n
    return out


def _assemble_w_in(sh):
    z = lambda n: jnp.zeros(sh.shape[1:3] + (n,), sh.dtype)
    s0, s1, s2, s3 = sh[0], sh[1], sh[2], sh[3]
    return jnp.concatenate([s0[..., 0:640], z(64), s0[..., 640:672], z(32), z(256), s0[..., 672:1184],
                            s3[..., 1064:1576], s0[..., 1184:1576], s1, s2, s3[..., 0:1064]], axis=-1)


def _split_w_in_grad(parts):
    def shard(s, a, b):
        if s == 0:
            return jnp.concatenate([a[:, 0:640], a[:, 704:736], a[:, 1024:1536], b[:, 0:392]], axis=1)
        if s == 3:
            return jnp.concatenate([b[:, 3544:4608], a[:, 1536:2048]], axis=1)
        return b[:, 392 + (s - 1) * SHARD_COLS_IN:392 + s * SHARD_COLS_IN]

    rows = jnp.concatenate([shard(s, a, b) for s in range(N_SHARD) for a, b in parts], axis=0)
    return rows.reshape(N_SHARD, DEPTH * D_MODEL, SHARD_COLS_IN)


def _col_shards(w):
    dl, r, cc = w.shape
    return w.reshape(dl, r, N_SHARD, cc // N_SHARD).transpose(2, 0, 1, 3).reshape(N_SHARD, dl * r, cc // N_SHARD)


def _from_col_shards(g, rows):
    cc = g.shape[-1]
    return g.reshape(N_SHARD, DEPTH, rows, cc).transpose(1, 2, 0, 3).reshape(DEPTH, rows, N_SHARD * cc)


def _pad_w_in(w):
    z = lambda n: jnp.zeros(w.shape[:-1] + (n,), w.dtype)
    return jnp.concatenate([w[..., 0:640], z(64), w[..., 640:672], z(32), z(256), w[..., 672:1184],
                            w[..., 5792:6304], w[..., 1184:5792]], axis=-1)


def _unpad_w_in(w):
    return jnp.concatenate([w[..., 0:640], w[..., 704:736], w[..., 1024:1536], w[..., 2048:6656],
                            w[..., 1536:2048]], axis=-1)


def _pad_w_uq(w):
    s = w.shape[:-1]
    w = w.reshape(s + (MLA_HEADS, 96))
    return jnp.pad(w, [(0, 0)] * (w.ndim - 1) + [(0, 32)]).reshape(s + (1024,))


def _unpad_w_uq(w):
    s = w.shape[:-1]
    return w.reshape(s + (MLA_HEADS, LANES))[..., :96].reshape(s + (768,))


def _pad_w_ukv(w):
    s = w.shape[:-1]
    w = w.reshape(s + (MLA_HEADS, 128))
    kpart = jnp.pad(w[..., :64], [(0, 0)] * (w.ndim - 1) + [(0, 64)]).reshape(s + (1024,))
    vpart = w[..., 64:].reshape(s + (512,))
    return jnp.concatenate([kpart, vpart], axis=-1)


def _unpad_w_ukv(w):
    s = w.shape[:-1]
    kpart = w[..., :1024].reshape(s + (MLA_HEADS, LANES))[..., :64]
    vpart = w[..., 1024:].reshape(s + (MLA_HEADS, 64))
    return jnp.concatenate([kpart, vpart], axis=-1).reshape(s + (1024,))


def _rope_tables(L, dim, lane_lo, period):
    half = dim // 2
    inv = 1.0 / (ROPE_THETA ** (jnp.arange(0, dim, 2, dtype=F32) / dim))
    ang = jnp.arange(L, dtype=F32)[:, None] * inv[None, :]
    cos, sin = jnp.cos(ang), jnp.sin(ang)
    one = lambda n: jnp.ones((L, n), F32)
    zero = lambda n: jnp.zeros((L, n), F32)
    rest = period - lane_lo - dim
    rep = LANES // period
    c = jnp.tile(jnp.concatenate([one(lane_lo), cos, cos, one(rest)], axis=1), (1, rep))
    a = jnp.tile(jnp.concatenate([zero(lane_lo), -sin, zero(half), zero(rest)], axis=1), (1, rep))
    b = jnp.tile(jnp.concatenate([zero(lane_lo + half), sin, zero(rest)], axis=1), (1, rep))
    return c, a, b


def _to_strided(t, d):
    L, w = t.shape
    return t.reshape(L // d, d, w).transpose(1, 0, 2)


def _from_strided(t):
    d, ld, w = t.shape
    return t.transpose(1, 0, 2).reshape(d * ld, w)


def _head_rows(t):
    return t.T.reshape(MLA_HEADS // 2, 2, t.shape[0])


def _head_rows_strided(t, d):
    s = _to_strided(t, d)
    return s.transpose(0, 2, 1).reshape(d, MLA_HEADS // 2, 2, s.shape[1])

def _local_grads(x, target, norm_g, w_in_p, q_norm_g, kv_norm_g, w_uq_p, w_ukv_p, w_out, final_g):
    L = x.shape[0]
    tabs_m = _rope_tables(L, MLA_ROPE, MLA_NOPE, LANES)
    tabs_d = _rope_tables(L, ROT_DIM, 0, DIL_HD)
    tabs_m_t = (tabs_m[0], -tabs_m[1], -tabs_m[2])
    tabs_d_t = (tabs_d[0], -tabs_d[1], -tabs_d[2])
    w_in_t = jnp.swapaxes(w_in_p, 1, 2)
    w_uq_t = jnp.swapaxes(w_uq_p, 1, 2)
    w_ukv_t = jnp.swapaxes(w_ukv_p, 1, 2)
    w_out_t = jnp.swapaxes(w_out, 1, 2)

    saved = []
    for l in range(DEPTH):
        h = _rms_fwd(x, norm_g[l:l + 1], "rms_fwd")
        p = _mm(h, w_in_p[l], tm=1024, tn=3328, tk=1024, out_dtype=BF16, name="in_proj")
        q, k, v, cqn, ckvn = _mla_prep(p, q_norm_g[l:l + 1], kv_norm_g[l:l + 1], w_uq_p[l], w_ukv_p[l], tabs_m,
                                       "mla_prep")
        oa, lse_a = _mla_fwd(q, k, v.T, "mla_fwd")
        dil = _dil_prep(p, tabs_d, "dil_prep")
        dil_s, o_g, lse_g = [], [], []
        for g, (_, dd) in enumerate(DIL_PAIRS):
            qs, ks, vs = (_to_strided(t, dd) for t in dil[3 * g:3 * g + 3])
            pad = ((0, 0), (DIL_HALF, DIL_HALF), (0, 0))
            vp = jnp.pad(vs, pad)
            v_t4 = vp.reshape(dd, vp.shape[1] // LANES, LANES, HEAD_W).transpose(0, 1, 3, 2)
            og, lg = _dil_fwd(qs, jnp.pad(ks, pad), v_t4, "dil_fwd_%d" % dd)
            dil_s.append((qs, ks, vs))
            o_g.append(_from_strided(og))
            lse_g.append(_from_strided(lg))
        ab, bm, lt = _merge_gate(oa, p, o_g, lse_g, "merge_gate")
        x_next = _mm(ab, w_out[l], tm=1024, tn=1024, tk=1024, out_dtype=F32, name="out_proj", add=x)
        saved.append((x, h, p, q, k, v, cqn, ckvn, oa, lse_a, dil_s, bm, lt, ab))
        x = x_next

    loss_b, dx, d_final = _loss_head(x, final_g[None, :], target, "loss_head")
    loss = loss_b[0, 0]

    d_norm, d_qn, d_kvn, d_win, d_wuq, d_wukv, d_wout = [], [], [], [], [], [], []
    for l in reversed(range(DEPTH)):
        x_l, h, p, q, k, v, cqn, ckvn, oa, lse_a, dil_s, bm, lt, ab = saved[l]
        dab = _mm(dx, w_out_t[l], tm=1024, tn=1024, tk=1024, out_dtype=F32, name="out_proj_dgrad")
        d_wout.append(_mm(ab, dx, tm=1024, tn=1024, tk=1024, out_dtype=F32, name="out_proj_wgrad", a_is_kxm=True))
        doa, dbm, D_a, D_b, dgates = _gate_bwd(dab, p, oa, bm, "gate_bwd")
        dq, dk_t, dv_t = _mla_bwd(q, k, v, q.T, doa, doa.T, lse_a.transpose(0, 2, 1),
                                  D_a.reshape(L, MLA_HEADS // 2, 2).transpose(1, 0, 2), "mla_bwd")
        dk, dv = dk_t.T, dv_t.T
        dp_mla, dq_pre, dkv, dqg, dkvg = _mla_prep_bwd(dq, dk, dv, p, q_norm_g[l:l + 1], kv_norm_g[l:l + 1],
                                                       w_uq_t[l], w_ukv_t[l], tabs_m_t, "mla_prep_bwd")
        d_wuq.append(_mm(cqn, dq_pre, tm=Q_LORA, tn=1024, tk=2048, out_dtype=F32, name="w_uq_wgrad", a_is_kxm=True))
        d_wukv.append(_mm(ckvn, dkv, tm=KV_LORA, tn=1536, tk=2048, out_dtype=F32, name="w_ukv_wgrad", a_is_kxm=True))
        dgr = []
        for g, (_, dd) in enumerate(DIL_PAIRS):
            qs, ks, vs = dil_s[g]
            dqs, dks, dvs = _dil_bwd(qs, ks, vs, _to_strided(dbm, dd), _head_rows_strided(lt, dd),
                                     _head_rows_strided(D_b, dd), "dil_bwd_%d" % dd)
            dgr += [_from_strided(dqs), _from_strided(dks), _from_strided(dvs)]
        dp_dil = _dil_prep_bwd(dgr, tabs_d_t, "dil_prep_bwd")
        dp_a = jnp.concatenate([dp_mla, dgates], axis=1)
        dh = _mm(dp_a, w_in_t[l][0:P_DIL0], tm=1024, tn=1024, tk=2048, out_dtype=F32, name="in_proj_dgrad_a")
        dh = _mm(dp_dil, w_in_t[l][P_DIL0:], tm=512, tn=1024, tk=2304, out_dtype=F32, name="in_proj_dgrad_b",
                 add=dh)
        d_win.append((_mm(h, dp_a, tm=512, tn=2048, tk=1024, out_dtype=F32, name="in_proj_wgrad_a", a_is_kxm=True),
                      _mm(h, dp_dil, tm=512, tn=1536, tk=2048, out_dtype=F32, name="in_proj_wgrad_b",
                          a_is_kxm=True)))
        dx, dng = _rms_bwd(dh, x_l, norm_g[l:l + 1], dx, "rms_bwd")
        d_norm.append(dng[0])
        d_qn.append(dqg[0])
        d_kvn.append(dkvg[0])

    rev = lambda xs: jnp.stack(xs[::-1])
    return (loss, dx, rev(d_norm), d_win[::-1], rev(d_qn), rev(d_kvn), rev(d_wuq), rev(d_wukv), rev(d_wout),
            d_final[0])


def kernel(x, norm_g, w_in, q_norm_g, kv_norm_g, w_uq, w_ukv, w_out, final_g, loss_target, m_norm_g, m_w_in, m_q_norm_g, m_kv_norm_g, m_w_uq, m_w_ukv, m_w_out, m_final_g, v_norm_g, v_w_in, v_q_norm_g, v_kv_norm_g, v_w_uq, v_w_ukv, v_w_out, v_final_g):
    c = lax.axis_index("c")

    def families(a_in, a_uq, a_ukv, a_out):
        return [t.reshape(shape) for t, shape in zip((a_in, a_uq, a_ukv, a_out), FAM_SHAPES)]

    g_in, g_uq, g_ukv, g_out = _allgather_weights([t.astype(BF16) for t in families(w_in, w_uq, w_ukv, w_out)])
    w_in_p = _assemble_w_in(g_in.reshape(N_SHARD, DEPTH, D_MODEL, SHARD_COLS_IN))
    w_uq_p = _pad_w_uq(_from_col_shards(g_uq, Q_LORA))
    w_ukv_p = _pad_w_ukv(_from_col_shards(g_ukv, KV_LORA))
    w_out_f = g_out.reshape(N_SHARD, DEPTH, 1024 // N_SHARD, D_MODEL).transpose(1, 0, 2, 3).reshape(DEPTH, 1024, D_MODEL)

    (loss, dx, d_norm, d_win_p, d_qn, d_kvn, d_wuq_p, d_wukv_p, d_wout, d_final) = _local_grads(
        x[0], loss_target[0], norm_g, w_in_p, q_norm_g, kv_norm_g, w_uq_p, w_ukv_p, w_out_f, final_g)
    loss = lax.psum(loss, ("x", "y", "c"))

    small = _pack_small(d_norm, d_qn, d_kvn, d_final)
    grads = [_split_w_in_grad(d_win_p), _col_shards(_unpad_w_uq(d_wuq_p)), _col_shards(_unpad_w_ukv(d_wukv_p)),
             d_wout.reshape(DEPTH, N_SHARD, 1024 // N_SHARD, D_MODEL).transpose(1, 0, 2, 3).reshape(
                 N_SHARD, DEPTH * (1024 // N_SHARD), D_MODEL),
             jnp.broadcast_to(small[None], (N_SHARD, SMALL_ROWS, LANES))]
    halves = [g.reshape(N_SHARD, 2, g.shape[1] // 2, g.shape[2]) for g in grads]
    from_sib = _sibling_send_halves(halves)
    chip_sum = [_add2(lax.dynamic_index_in_dim(h, c, axis=1, keepdims=False), f, "grad_add_pair", BF16)
                for h, f in zip(halves, from_sib)]
    red_half = [_add4_ordered(t, "grad_add_chips") for t in _chip_scatter(chip_sum)]
    other_half = _sibling_swap(red_half)
    gred = []
    for mine, other in zip(red_half, other_half):
        both = jnp.stack([mine, other])
        gred.append(jnp.concatenate([lax.dynamic_index_in_dim(both, c, axis=0, keepdims=False),
                                     lax.dynamic_index_in_dim(both, 1 - c, axis=0, keepdims=False)], axis=0))

    wf = families(w_in, w_uq, w_ukv, w_out) + [_pack_small(norm_g, q_norm_g, kv_norm_g, final_g)]
    mf = families(m_w_in, m_w_uq, m_w_ukv, m_w_out) + [_pack_small(m_norm_g, m_q_norm_g, m_kv_norm_g, m_final_g)]
    vf = families(v_w_in, v_w_uq, v_w_ukv, v_w_out) + [_pack_small(v_norm_g, v_q_norm_g, v_kv_norm_g, v_final_g)]
    upd = [_adamw(w, g, m, v, "adamw") for w, g, m, v in zip(wf, gred, mf, vf)]

    def leaves(fams):
        a_in, a_uq, a_ukv, a_out, s = fams
        s_norm, s_qn, s_kvn, s_final = _split_small(s)
        return [s_norm, a_in.reshape(w_in.shape), s_qn, s_kvn, a_uq.reshape(w_uq.shape), a_ukv.reshape(w_ukv.shape),
                a_out.reshape(w_out.shape), s_final]

    return (loss, dx[None], *leaves(gred), *leaves([u[0] for u in upd]), *leaves([u[1] for u in upd]),
            *leaves([u[2] for u in upd]))
```

```python
import functools

import jax
import jax.numpy as jnp
from jax import lax
from jax.experimental import pallas as pl
from jax.experimental.pallas import tpu as pltpu

F32 = jnp.float32
BF16 = jnp.bfloat16
MESH = pl.DeviceIdType.MESH

D_MODEL = 1024
DEPTH = 4
MLA_HEADS = 8
MLA_NOPE = 64
MLA_ROPE = 32
Q_LORA = 384
KV_LORA = 256
DIL_PAIRS = ((128, 1), (512, 4), (2048, 16))
DIL_HD = 64
DIL_HALF = 64
ROT_DIM = 16
ROPE_THETA = 500000.0
EPS = 1e-6
IN_WIDTH = 6304
N_SHARD = 4

P_WIDTH = 6656
P_MLA = 1024
P_GATE = 1024
P_DIL0 = 2048
LANES = 128
HEAD_W = 512

ADAM_LR = 0.001
ADAM_B1 = 0.9
ADAM_B2 = 0.999
ADAM_EPS = 1e-08
ADAM_WD = 0.01
ADAM_STEP = 10

SHARD_COLS_IN = IN_WIDTH // N_SHARD
FAM_SHAPES = ((DEPTH * D_MODEL, SHARD_COLS_IN), (DEPTH * Q_LORA, 768 // N_SHARD), (DEPTH * KV_LORA, 1024 // N_SHARD),
              (DEPTH * (1024 // N_SHARD), D_MODEL))
N_SMALL = DEPTH * (D_MODEL + Q_LORA + KV_LORA) + D_MODEL
SMALL_ROWS = 64
VMEM_BIG_MB = 48


def _cparams(vmem_mb=None):
    if vmem_mb is None:
        return None
    return pltpu.CompilerParams(vmem_limit_bytes=vmem_mb << 20)


def _sigmoid(x):
    return 1.0 / (1.0 + jnp.exp(-x))


def _rope(x, c, a, b, sh):
    return x * c + pltpu.roll(x, LANES - sh, 1) * a + pltpu.roll(x, sh, 1) * b


def _per_head8(x, pick_first):
    r = lax.broadcasted_iota(jnp.int32, (HEAD_W, MLA_HEADS), 0)
    c = lax.broadcasted_iota(jnp.int32, (HEAD_W, MLA_HEADS), 1)
    sel = (r == c * DIL_HD) if pick_first else (r // DIL_HD == c)
    mat = jnp.where(sel, 1.0, 0.0).astype(BF16)
    out = jnp.zeros((x.shape[0], MLA_HEADS), F32)
    for _ in range(3):
        part = x.astype(BF16)
        out = out + jnp.dot(part, mat, preferred_element_type=F32)
        x = x - part.astype(F32)
    return out


def _mm(a, b, *, tm, tn, tk, out_dtype, name, add=None, a_is_kxm=False):
    K, M = a.shape if a_is_kxm else a.shape[::-1]
    N = b.shape[1]
    tm, tn, tk = min(tm, M), min(tn, N), min(tk, K)
    assert M % tm == 0 and N % tn == 0 and K % tk == 0, (a.shape, b.shape)
    nk = K // tk
    has_add = add is not None

    def body(*refs):
        a_ref, b_ref = refs[0], refs[1]
        add_ref = refs[2] if has_add else None
        o_ref = refs[3] if has_add else refs[2]
        k = pl.program_id(2)
        if a_is_kxm:
            part = lax.dot_general(a_ref[...].astype(BF16), b_ref[...].astype(BF16), (((0,), (0,)), ((), ())),
                                   preferred_element_type=F32)
        else:
            part = jnp.dot(a_ref[...].astype(BF16), b_ref[...].astype(BF16), preferred_element_type=F32)
        if nk == 1:
            o_ref[...] = (part + add_ref[...] if has_add else part).astype(out_dtype)
            return
        acc = refs[-1]

        @pl.when(k == 0)
        def _():
            acc[...] = part

        @pl.when(k > 0)
        def _():
            acc[...] += part

        @pl.when(k == nk - 1)
        def _():
            r = acc[...]
            if has_add:
                r = r + add_ref[...]
            o_ref[...] = r.astype(out_dtype)

    a_spec = pl.BlockSpec((tk, tm), lambda i, j, k: (k, i)) if a_is_kxm else pl.BlockSpec((tm, tk), lambda i, j, k: (i, k))
    in_specs = [a_spec, pl.BlockSpec((tk, tn), lambda i, j, k: (k, j))]
    args = [a, b]
    if has_add:
        in_specs.append(pl.BlockSpec((tm, tn), lambda i, j, k: (i, j)))
        args.append(add)
    return pl.pallas_call(
        body, name=name, grid=(M // tm, N // tn, nk), in_specs=in_specs,
        out_specs=pl.BlockSpec((tm, tn), lambda i, j, k: (i, j)),
        out_shape=jax.ShapeDtypeStruct((M, N), out_dtype),
        scratch_shapes=[pltpu.VMEM((tm, tn), F32)] if nk > 1 else [],
        compiler_params=_cparams(VMEM_BIG_MB))(*args)


def _dgrad_rms_bwd(dp, w, dh_add, x, g, dres, name):
    M, K = dp.shape
    N = w.shape[1]
    tm, tk = min(512, M), min(2304, K)
    assert M % tm == 0 and K % tk == 0
    nk = K // tk

    def body(a_ref, b_ref, add_ref, x_ref, g_ref, dres_ref, dx_ref, dg_ref, acc):
        i, k = pl.program_id(0), pl.program_id(1)
        part = jnp.dot(a_ref[...], b_ref[...], preferred_element_type=F32)

        @pl.when((i == 0) & (k == 0))
        def _():
            dg_ref[...] = jnp.zeros(dg_ref.shape, F32)

        @pl.when(k == 0)
        def _():
            acc[...] = part + add_ref[...]

        @pl.when(k > 0)
        def _():
            acc[...] += part

        @pl.when(k == nk - 1)
        def _():
            dy = acc[...]
            xv = x_ref[...]
            r = lax.rsqrt(jnp.mean(xv * xv, axis=-1, keepdims=True) + EPS)
            dyg = dy * g_ref[...]
            dx_ref[...] = dres_ref[...] + r * dyg - xv * (r * r * r) * jnp.mean(dyg * xv, axis=-1, keepdims=True)
            dg_ref[...] += jnp.sum(dy * xv * r, axis=0, keepdims=True)

    row = pl.BlockSpec((tm, N), lambda i, k: (i, 0))
    vec = pl.BlockSpec((1, N), lambda i, k: (0, 0))
    return pl.pallas_call(
        body, name=name, grid=(M // tm, nk),
        in_specs=[pl.BlockSpec((tm, tk), lambda i, k: (i, k)), pl.BlockSpec((tk, N), lambda i, k: (k, 0)),
                  row, row, vec, row],
        out_specs=[row, vec],
        out_shape=[jax.ShapeDtypeStruct((M, N), F32), jax.ShapeDtypeStruct((1, N), F32)],
        scratch_shapes=[pltpu.VMEM((tm, N), F32)], compiler_params=_cparams(VMEM_BIG_MB))(dp, w, dh_add, x, g, dres)


def _row_spec(tm, w, cb=0):
    return pl.BlockSpec((tm, w), lambda i: (i, cb))


def _const_spec(arr):
    nd = arr.ndim
    return pl.BlockSpec(arr.shape, lambda i: (0,) * nd)


def _rms_fwd(x, g, name):
    L, D = x.shape
    tm = min(512, L)

    def body(x_ref, g_ref, o_ref):
        xv = x_ref[...]
        r = lax.rsqrt(jnp.mean(xv * xv, axis=-1, keepdims=True) + EPS)
        o_ref[...] = (xv * r * g_ref[...]).astype(BF16)

    return pl.pallas_call(
        body, name=name, grid=(L // tm,), in_specs=[_row_spec(tm, D), _const_spec(g)],
        out_specs=_row_spec(tm, D), out_shape=jax.ShapeDtypeStruct((L, D), BF16))(x, g)


def _rms_bwd(dh, x, g, dres, name):
    L, D = x.shape
    tm = min(512, L)

    def body(dh_ref, x_ref, g_ref, dres_ref, dx_ref, dg_ref):
        xv = x_ref[...]
        dy = dh_ref[...]
        r = lax.rsqrt(jnp.mean(xv * xv, axis=-1, keepdims=True) + EPS)
        dyg = dy * g_ref[...]
        dx_ref[...] = dres_ref[...] + r * dyg - xv * (r * r * r) * jnp.mean(dyg * xv, axis=-1, keepdims=True)
        part = jnp.sum(dy * xv * r, axis=0, keepdims=True)

        @pl.when(pl.program_id(0) == 0)
        def _():
            dg_ref[...] = part

        @pl.when(pl.program_id(0) > 0)
        def _():
            dg_ref[...] += part

    return pl.pallas_call(
        body, name=name, grid=(L // tm,),
        in_specs=[_row_spec(tm, D), _row_spec(tm, D), _const_spec(g), _row_spec(tm, D)],
        out_specs=[_row_spec(tm, D), pl.BlockSpec((1, D), lambda i: (0, 0))],
        out_shape=[jax.ShapeDtypeStruct((L, D), F32), jax.ShapeDtypeStruct((1, D), F32)])(dh, x, g, dres)


def _loss_head(x, g, target, name):
    L, D = x.shape
    tm = min(512, L)

    def body(x_ref, g_ref, t_ref, loss_ref, dx_ref, dg_ref):
        xv = x_ref[...]
        gv = g_ref[...]
        r = lax.rsqrt(jnp.mean(xv * xv, axis=-1, keepdims=True) + EPS)
        xr = xv * r
        err = xr * gv - t_ref[...]
        lp = 0.5 * jnp.sum(jnp.mean(err * err, axis=-1, keepdims=True))
        dy = err * (1.0 / D)
        dyg = dy * gv
        dx_ref[...] = r * dyg - xv * (r * r * r) * jnp.mean(dyg * xv, axis=-1, keepdims=True)
        part = jnp.sum(dy * xr, axis=0, keepdims=True)

        @pl.when(pl.program_id(0) == 0)
        def _():
            dg_ref[...] = part
            loss_ref[...] = jnp.zeros(loss_ref.shape, F32) + lp

        @pl.when(pl.program_id(0) > 0)
        def _():
            dg_ref[...] += part
            loss_ref[...] += lp

    return pl.pallas_call(
        body, name=name, grid=(L // tm,),
        in_specs=[_row_spec(tm, D), _const_spec(g), _row_spec(tm, D)],
        out_specs=[pl.BlockSpec((8, LANES), lambda i: (0, 0)), _row_spec(tm, D), pl.BlockSpec((1, D), lambda i: (0, 0))],
        out_shape=[jax.ShapeDtypeStruct((8, LANES), F32), jax.ShapeDtypeStruct((L, D), F32),
                   jax.ShapeDtypeStruct((1, D), F32)])(x, g, target)


def _mla_prep(p, qg, kvg, wuq, wukv, tabs, name):
    L = p.shape[0]
    tm = min(512, L)
    scale = (MLA_NOPE + MLA_ROPE) ** -0.5
    tc, ta, tb = tabs

    def body(p_ref, qg_ref, kvg_ref, wuq_ref, wukv_ref, c_ref, a_ref, b_ref, q_ref, k_ref, v_ref, cqn_ref, ckvn_ref):
        c, a, b = c_ref[...], a_ref[...], b_ref[...]
        cq = p_ref[:, 0:Q_LORA].astype(F32)
        ckv = p_ref[:, Q_LORA:Q_LORA + KV_LORA].astype(F32)
        kr = p_ref[:, 640:768].astype(F32)
        cqn = (cq * lax.rsqrt(jnp.mean(cq * cq, axis=-1, keepdims=True) + EPS) * qg_ref[...]).astype(BF16)
        ckvn = (ckv * lax.rsqrt(jnp.mean(ckv * ckv, axis=-1, keepdims=True) + EPS) * kvg_ref[...]).astype(BF16)
        cqn_ref[...] = cqn
        ckvn_ref[...] = ckvn
        q = jnp.dot(cqn, wuq_ref[...], preferred_element_type=F32)
        kv = jnp.dot(ckvn, wukv_ref[...], preferred_element_type=F32)
        krr = _rope(kr, c, a, b, MLA_ROPE // 2)
        for h in range(MLA_HEADS):
            sl = slice(h * LANES, (h + 1) * LANES)
            q_ref[:, sl] = (_rope(q[:, sl], c, a, b, MLA_ROPE // 2) * (scale * LOG2E)).astype(BF16)
            k_ref[:, sl] = (kv[:, sl] + krr).astype(BF16)
        v_ref[...] = kv[:, 1024:1536].astype(BF16)

    return pl.pallas_call(
        body, name=name, grid=(L // tm,),
        in_specs=[_row_spec(tm, P_MLA, 0), _const_spec(qg), _const_spec(kvg), _const_spec(wuq), _const_spec(wukv),
                  _row_spec(tm, LANES), _row_spec(tm, LANES), _row_spec(tm, LANES)],
        out_specs=[_row_spec(tm, 1024), _row_spec(tm, 1024), _row_spec(tm, HEAD_W), _row_spec(tm, Q_LORA),
                   _row_spec(tm, KV_LORA)],
        out_shape=[jax.ShapeDtypeStruct((L, 1024), BF16), jax.ShapeDtypeStruct((L, 1024), BF16),
                   jax.ShapeDtypeStruct((L, HEAD_W), BF16), jax.ShapeDtypeStruct((L, Q_LORA), BF16),
                   jax.ShapeDtypeStruct((L, KV_LORA), BF16)],
        compiler_params=_cparams(VMEM_BIG_MB))(p, qg, kvg, wuq, wukv, tc, ta, tb)


def _mla_prep_bwd(dq, dk, dv, p, qg, kvg, wuq_t, wukv_t, tabs_t, name):
    L = p.shape[0]
    tm = min(512, L)
    scale = (MLA_NOPE + MLA_ROPE) ** -0.5
    tc, ta, tb = tabs_t

    def body(dq_ref, dk_ref, dv_ref, p_ref, qg_ref, kvg_ref, wuqt_ref, wukvt_ref, c_ref, a_ref, b_ref,
             dp_ref, dqp_ref, dkv_ref, dqg_ref, dkvg_ref):
        c, a, b = c_ref[...], a_ref[...], b_ref[...]
        dkr = jnp.zeros((tm, LANES), F32)
        for h in range(MLA_HEADS):
            sl = slice(h * LANES, (h + 1) * LANES)
            dqp_ref[:, sl] = (_rope(dq_ref[:, sl].astype(F32), c, a, b, MLA_ROPE // 2) * scale).astype(BF16)
            dkh = dk_ref[:, sl].astype(F32) * LN2
            dkv_ref[:, sl] = dkh.astype(BF16)
            dkr = dkr + dkh
        dkv_ref[:, 1024:1536] = dv_ref[...].astype(BF16)
        lane = lax.broadcasted_iota(jnp.int32, (tm, LANES), 1)
        dkr = jnp.where((lane >= MLA_NOPE) & (lane < MLA_NOPE + MLA_ROPE), _rope(dkr, c, a, b, MLA_ROPE // 2), 0.0)

        d_cqn = jnp.dot(dqp_ref[...], wuqt_ref[...], preferred_element_type=F32)
        d_ckvn = jnp.dot(dkv_ref[...], wukvt_ref[...], preferred_element_type=F32)

        def norm_bwd(xv, gv, dy):
            r = lax.rsqrt(jnp.mean(xv * xv, axis=-1, keepdims=True) + EPS)
            dyg = dy * gv
            dx = r * dyg - xv * (r * r * r) * jnp.mean(dyg * xv, axis=-1, keepdims=True)
            return dx, jnp.sum(dy * xv * r, axis=0, keepdims=True)

        d_cq, dqg = norm_bwd(p_ref[:, 0:Q_LORA].astype(F32), qg_ref[...], d_cqn)
        d_ckv, dkvg = norm_bwd(p_ref[:, Q_LORA:Q_LORA + KV_LORA].astype(F32), kvg_ref[...], d_ckvn)
        dp_ref[:, 0:Q_LORA] = d_cq.astype(BF16)
        dp_ref[:, Q_LORA:Q_LORA + KV_LORA] = d_ckv.astype(BF16)
        dp_ref[:, 640:768] = dkr.astype(BF16)
        dp_ref[:, 768:1024] = jnp.zeros((tm, 256), BF16)

        @pl.when(pl.program_id(0) == 0)
        def _():
            dqg_ref[...] = dqg
            dkvg_ref[...] = dkvg

        @pl.when(pl.program_id(0) > 0)
        def _():
            dqg_ref[...] += dqg
            dkvg_ref[...] += dkvg

    return pl.pallas_call(
        body, name=name, grid=(L // tm,),
        in_specs=[_row_spec(tm, 1024), _row_spec(tm, 1024), _row_spec(tm, HEAD_W), _row_spec(tm, P_MLA, 0),
                  _const_spec(qg), _const_spec(kvg), _const_spec(wuq_t), _const_spec(wukv_t),
                  _row_spec(tm, LANES), _row_spec(tm, LANES), _row_spec(tm, LANES)],
        out_specs=[_row_spec(tm, P_MLA), _row_spec(tm, 1024), _row_spec(tm, 1536),
                   pl.BlockSpec((1, Q_LORA), lambda i: (0, 0)), pl.BlockSpec((1, KV_LORA), lambda i: (0, 0))],
        out_shape=[jax.ShapeDtypeStruct((L, P_MLA), BF16), jax.ShapeDtypeStruct((L, 1024), BF16),
                   jax.ShapeDtypeStruct((L, 1536), BF16), jax.ShapeDtypeStruct((1, Q_LORA), F32),
                   jax.ShapeDtypeStruct((1, KV_LORA), F32)],
        compiler_params=_cparams(VMEM_BIG_MB))(dq, dk, dv, p, qg, kvg, wuq_t, wukv_t, tc, ta, tb)


def _dil_prep(p, tabs, name):
    L = p.shape[0]
    tm = min(512, L)
    tc, ta, tb = tabs
    scale = DIL_HD ** -0.5

    def body(*refs):
        ins, (c_ref, a_ref, b_ref), outs = refs[:9], refs[9:12], refs[12:]
        c, a, b = c_ref[...], a_ref[...], b_ref[...]
        for n in range(9):
            t = n % 3
            for cb in range(HEAD_W // LANES):
                sl = slice(cb * LANES, (cb + 1) * LANES)
                xv = ins[n][:, sl].astype(F32)
                if t == 0:
                    xv = _rope(xv, c, a, b, ROT_DIM // 2) * (scale * LOG2E)
                elif t == 1:
                    xv = _rope(xv, c, a, b, ROT_DIM // 2)
                outs[n][:, sl] = xv.astype(BF16)

    in_specs = [_row_spec(tm, HEAD_W, P_DIL0 // HEAD_W + n) for n in range(9)] + [_row_spec(tm, LANES)] * 3
    return pl.pallas_call(
        body, name=name, grid=(L // tm,), in_specs=in_specs,
        out_specs=[_row_spec(tm, HEAD_W)] * 9,
        out_shape=[jax.ShapeDtypeStruct((L, HEAD_W), BF16)] * 9)(*([p] * 9), tc, ta, tb)


def _dil_prep_bwd(grads, tabs_t, name):
    L = grads[0].shape[0]
    tm = min(512, L)
    tc, ta, tb = tabs_t
    scale = DIL_HD ** -0.5

    def body(*refs):
        ins, (c_ref, a_ref, b_ref), o_ref = refs[:9], refs[9:12], refs[12]
        c, a, b = c_ref[...], a_ref[...], b_ref[...]
        for n in range(9):
            t = n % 3
            for cb in range(HEAD_W // LANES):
                sl = slice(cb * LANES, (cb + 1) * LANES)
                xv = ins[n][:, sl].astype(F32)
                if t == 0:
                    xv = _rope(xv, c, a, b, ROT_DIM // 2) * scale
                elif t == 1:
                    xv = _rope(xv, c, a, b, ROT_DIM // 2) * LN2
                o_ref[:, n * HEAD_W + cb * LANES:n * HEAD_W + (cb + 1) * LANES] = xv.astype(BF16)

    return pl.pallas_call(
        body, name=name, grid=(L // tm,), in_specs=[_row_spec(tm, HEAD_W)] * 9 + [_row_spec(tm, LANES)] * 3,
        out_specs=_row_spec(tm, 9 * HEAD_W), out_shape=jax.ShapeDtypeStruct((L, 9 * HEAD_W), BF16),
        compiler_params=_cparams(VMEM_BIG_MB))(*grads, tc, ta, tb)


def _merge_gate(oa, p, o_g, lse_g, name):
    L = oa.shape[0]
    tm = min(512, L)

    def body(oa_ref, ga_ref, gb_ref, o1, o2, o3, l1, l2, l3, ab_ref, bm_ref, lt_ref):
        la, lb, lc = l1[...], l2[...], l3[...]
        m = jnp.maximum(jnp.maximum(la, lb), lc)
        ea, eb, ec = jnp.exp2(la - m), jnp.exp2(lb - m), jnp.exp2(lc - m)
        den = ea + eb + ec
        bm = (ea * o1[...] + eb * o2[...] + ec * o3[...]) / den
        bm_ref[...] = bm
        lt_ref[...] = _per_head8(m + jnp.log2(den), True)
        ga, gb = ga_ref[...].astype(F32), gb_ref[...].astype(F32)
        ab_ref[:, 0:HEAD_W] = (oa_ref[...] * (ga * _sigmoid(ga))).astype(BF16)
        ab_ref[:, HEAD_W:2 * HEAD_W] = (bm * (gb * _sigmoid(gb))).astype(BF16)

    w = _row_spec(tm, HEAD_W)
    return pl.pallas_call(
        body, name=name, grid=(L // tm,),
        in_specs=[w, _row_spec(tm, HEAD_W, 2), _row_spec(tm, HEAD_W, 3), w, w, w, w, w, w],
        out_specs=[_row_spec(tm, 2 * HEAD_W), w, _row_spec(tm, MLA_HEADS)],
        out_shape=[jax.ShapeDtypeStruct((L, 2 * HEAD_W), BF16), jax.ShapeDtypeStruct((L, HEAD_W), F32),
                   jax.ShapeDtypeStruct((L, MLA_HEADS), F32)])(oa, p, p, *o_g, *lse_g)


def _gate_bwd(dab, p, oa, bm, name):
    L = oa.shape[0]
    tm = min(512, L)

    def body(da_ref, db_ref, ga_ref, gb_ref, oa_ref, bm_ref, doa_ref, dbm_ref, Da_ref, Db_ref, dg_ref):
        def one(d, g, o, do_ref, D_ref, col):
            sg = _sigmoid(g)
            do = d * (g * sg)
            do_ref[...] = do.astype(BF16)
            dg_ref[:, col:col + HEAD_W] = (d * o * (sg * (1.0 + g * (1.0 - sg)))).astype(BF16)
            D_ref[...] = _per_head8(do * o, False)

        one(da_ref[...], ga_ref[...].astype(F32), oa_ref[...], doa_ref, Da_ref, 0)
        one(db_ref[...], gb_ref[...].astype(F32), bm_ref[...], dbm_ref, Db_ref, HEAD_W)

    w = _row_spec(tm, HEAD_W)
    w8 = _row_spec(tm, MLA_HEADS)
    return pl.pallas_call(
        body, name=name, grid=(L // tm,),
        in_specs=[_row_spec(tm, HEAD_W, 0), _row_spec(tm, HEAD_W, 1), _row_spec(tm, HEAD_W, 2),
                  _row_spec(tm, HEAD_W, 3), w, w],
        out_specs=[w, w, w8, w8, _row_spec(tm, 2 * HEAD_W)],
        out_shape=[jax.ShapeDtypeStruct((L, HEAD_W), BF16), jax.ShapeDtypeStruct((L, HEAD_W), BF16),
                   jax.ShapeDtypeStruct((L, MLA_HEADS), F32), jax.ShapeDtypeStruct((L, MLA_HEADS), F32),
                   jax.ShapeDtypeStruct((L, 2 * HEAD_W), BF16)])(dab, dab, p, p, oa, bm)


NT = (((1,), (1,)), ((), ()))
TN = (((0,), (0,)), ((), ()))
NEG = -1e30


MLA_TQ = 512
MLA_TK = 4096
MLA_BWD_TQ = 1024
MLA_BWD_TK = 2048
LOG2E = 1.4426950408889634
LN2 = 0.6931471805599453


def _mla_fwd(q, k, v_t, name):
    L = q.shape[0]
    tq, tk = min(MLA_TQ, L), min(MLA_TK, L)
    nq, nk = L // tq, L // tk
    npair = MLA_HEADS // 2

    def body(q_ref, k_ref, vt_ref, o_ref, lse_ref, m0, l0, a0, m1, l1, a1):
        j = pl.program_id(2)
        stats = ((m0, l0, a0), (m1, l1, a1))

        @pl.when(j == 0)
        def _():
            for m_sc, l_sc, acc_sc in stats:
                m_sc[...] = jnp.full(m_sc.shape, NEG, F32)
                l_sc[...] = jnp.zeros(l_sc.shape, F32)
                acc_sc[...] = jnp.zeros(acc_sc.shape, F32)

        s_ts = [lax.dot_general(k_ref[:, hh * LANES:(hh + 1) * LANES], q_ref[:, hh * LANES:(hh + 1) * LANES], NT,
                                preferred_element_type=F32) for hh in range(2)]
        for hh in range(2):
            m_sc, l_sc, acc_sc = stats[hh]
            s_t = s_ts[hh]
            m_prev = m_sc[...]
            m_new = jnp.maximum(m_prev, jnp.max(s_t, axis=0, keepdims=True))
            alpha = jnp.exp2(m_prev - m_new)
            p_t = jnp.exp2(s_t - m_new)
            l_sc[...] = alpha * l_sc[...] + jnp.sum(p_t, axis=0, keepdims=True)
            m_sc[...] = m_new
            pv = jnp.dot(vt_ref[hh * DIL_HD:(hh + 1) * DIL_HD, :], p_t.astype(BF16),
                         preferred_element_type=F32)
            acc_sc[...] = alpha * acc_sc[...] + pv

        @pl.when(j == nk - 1)
        def _():
            o_ref[...] = jnp.concatenate([a0[...] / l0[...], a1[...] / l1[...]], axis=0).T
            lse_ref[...] = jnp.concatenate([m0[...] + jnp.log2(l0[...]), m1[...] + jnp.log2(l1[...])], axis=0)

    stat = [pltpu.VMEM((1, tq), F32), pltpu.VMEM((1, tq), F32), pltpu.VMEM((DIL_HD, tq), F32)]
    return pl.pallas_call(
        body, name=name, grid=(npair, nq, nk),
        in_specs=[pl.BlockSpec((tq, 2 * LANES), lambda pr, i, j: (i, pr)),
                  pl.BlockSpec((tk, 2 * LANES), lambda pr, i, j: (j, pr)),
                  pl.BlockSpec((LANES, tk), lambda pr, i, j: (pr, j))],
        out_specs=[pl.BlockSpec((tq, LANES), lambda pr, i, j: (i, pr)),
                   pl.BlockSpec((None, 2, tq), lambda pr, i, j: (pr, 0, i))],
        out_shape=[jax.ShapeDtypeStruct((L, HEAD_W), F32), jax.ShapeDtypeStruct((npair, 2, L), F32)],
        scratch_shapes=stat + stat, compiler_params=_cparams(VMEM_BIG_MB))(q, k, v_t)


def _mla_bwd(q, k, v, q_t, do, do_t, lse_cols, d_cols, name):
    L = q.shape[0]
    tq, tk = min(MLA_BWD_TQ, L), min(MLA_BWD_TK, L)
    nq, nk = L // tq, L // tk
    npair = MLA_HEADS // 2

    def body(q_ref, k_ref, v_ref, qt_ref, do_ref, dot_ref, lse_ref, d_ref, dq_out, dkt_out, dvt_out,
             dq_ref, dkt_ref, dvt_ref):
        j, i = pl.program_id(1), pl.program_id(2)

        @pl.when((j == 0) & (i == 0))
        def _():
            dq_ref[...] = jnp.zeros(dq_ref.shape, F32)

        @pl.when(i == 0)
        def _():
            dkt_ref[...] = jnp.zeros(dkt_ref.shape, F32)
            dvt_ref[...] = jnp.zeros(dvt_ref.shape, F32)

        first = lax.broadcasted_iota(jnp.int32, (tq, LANES), 1) < DIL_HD
        dov = do_ref[...]
        vv = v_ref[...]
        rows = pl.ds(pl.multiple_of(i * tq, tq), tq)
        for hh in range(2):
            sl = slice(hh * LANES, (hh + 1) * LANES)
            hrows = slice(hh * DIL_HD, (hh + 1) * DIL_HD)
            qh, kh = q_ref[:, sl], k_ref[:, sl]
            do_h = jnp.where(first if hh == 0 else ~first, dov, jnp.zeros_like(dov))
            s = lax.dot_general(qh, kh, NT, preferred_element_type=F32)
            p = jnp.exp2(s - lse_ref[:, hh:hh + 1])
            dvt_ref[hrows, :] += jnp.dot(dot_ref[hrows, :], p.astype(BF16), preferred_element_type=F32)
            dp = lax.dot_general(do_h, vv, NT, preferred_element_type=F32)
            ds = (p * (dp - d_ref[:, hh:hh + 1])).astype(BF16)
            dq_ref[rows, sl] += jnp.dot(ds, kh, preferred_element_type=F32)
            dkt_ref[sl, :] += jnp.dot(qt_ref[sl, :], ds, preferred_element_type=F32)

        @pl.when(i == nq - 1)
        def _():
            dkt_out[...] = dkt_ref[...].astype(BF16)
            dvt_out[...] = dvt_ref[...].astype(BF16)

        @pl.when((j == nk - 1) & (i == nq - 1))
        def _():
            dq_out[...] = dq_ref[...].astype(BF16)

    colspec = pl.BlockSpec((None, tq, 2), lambda pr, j, i: (pr, i, 0))
    return pl.pallas_call(
        body, name=name, grid=(npair, nk, nq),
        in_specs=[pl.BlockSpec((tq, 2 * LANES), lambda pr, j, i: (i, pr)),
                  pl.BlockSpec((tk, 2 * LANES), lambda pr, j, i: (j, pr)),
                  pl.BlockSpec((tk, LANES), lambda pr, j, i: (j, pr)),
                  pl.BlockSpec((2 * LANES, tq), lambda pr, j, i: (pr, i)),
                  pl.BlockSpec((tq, LANES), lambda pr, j, i: (i, pr)),
                  pl.BlockSpec((LANES, tq), lambda pr, j, i: (pr, i)),
                  colspec, colspec],
        out_specs=[pl.BlockSpec((L, 2 * LANES), lambda pr, j, i: (0, pr)),
                   pl.BlockSpec((2 * LANES, tk), lambda pr, j, i: (pr, j)),
                   pl.BlockSpec((LANES, tk), lambda pr, j, i: (pr, j))],
        out_shape=[jax.ShapeDtypeStruct((L, 1024), BF16), jax.ShapeDtypeStruct((1024, L), BF16),
                   jax.ShapeDtypeStruct((HEAD_W, L), BF16)],
        scratch_shapes=[pltpu.VMEM((L, 2 * LANES), F32), pltpu.VMEM((2 * LANES, tk), F32),
                        pltpu.VMEM((LANES, tk), F32)],
        compiler_params=_cparams(VMEM_BIG_MB))(q, k, v, q_t, do, do_t, lse_cols, d_cols)


DIL_TQ = 4096
DIL_SQ = 128
DIL_SW = DIL_SQ + 2 * DIL_HALF


def _dil_window(a_sub, ld):
    return pl.multiple_of(jnp.clip(a_sub - DIL_HALF, 0, ld - DIL_SW), DIL_HALF)


def _dil_band_mask(shift, heads):
    kidx = lax.broadcasted_iota(jnp.int32, (DIL_SW, heads * DIL_SQ), 0)
    qidx = lax.broadcasted_iota(jnp.int32, (DIL_SW, heads * DIL_SQ), 1) % DIL_SQ
    return jnp.abs(shift + kidx - qidx) <= DIL_HALF


def _pair_rows(x, first):
    zero = jnp.zeros_like(x)
    return jnp.concatenate([jnp.where(first, x, zero), jnp.where(first, zero, x)], axis=0)


def _dil_fwd(q, kp, vp_t4, name):
    d, ld, _ = q.shape
    assert ld % DIL_SQ == 0
    tq = min(DIL_TQ, ld)
    nq = ld // tq
    npair = HEAD_W // LANES
    nb = (ld + 2 * DIL_HALF) // LANES

    def body(q_ref, k_ref, vt_ref, o_ref, lse_ref):
        i = pl.program_id(2)
        first = lax.broadcasted_iota(jnp.int32, (DIL_SQ, LANES), 1) < DIL_HD
        kidx = lax.broadcasted_iota(jnp.int32, (DIL_SW, DIL_SQ), 0)
        for u in range(tq // DIL_SQ):
            a_sub = pl.multiple_of(i * tq + u * DIL_SQ, DIL_SQ)
            kk = a_sub // LANES
            rows = slice(u * DIL_SQ, (u + 1) * DIL_SQ)
            kwin = k_ref[pl.ds(a_sub, DIL_SW), :]
            valid = _dil_band_mask(-DIL_HALF, 1) & (kidx >= DIL_HALF - a_sub) & (kidx < ld + DIL_HALF - a_sub)
            qv = q_ref[rows, :]
            outs, lses = [], []
            for hh in range(2):
                qh = jnp.where(first if hh == 0 else ~first, qv, jnp.zeros_like(qv))
                s_t = jnp.where(valid, lax.dot_general(kwin, qh, NT, preferred_element_type=F32), NEG)
                m = jnp.max(s_t, axis=0, keepdims=True)
                p32 = jnp.exp2(s_t - m)
                l = jnp.sum(p32, axis=0, keepdims=True)
                p_t = p32.astype(BF16)
                hrows = slice(hh * DIL_HD, (hh + 1) * DIL_HD)
                pv = (jnp.dot(vt_ref[kk, hrows, :], p_t[0:LANES, :], preferred_element_type=F32)
                      + jnp.dot(vt_ref[kk + 1, hrows, :], p_t[LANES:DIL_SW, :], preferred_element_type=F32))
                outs.append(pv / l)
                lses.append(jnp.broadcast_to(m + jnp.log2(l), (DIL_HD, DIL_SQ)))
            o_ref[rows, :] = jnp.concatenate(outs, axis=0).T
            lse_ref[rows, :] = jnp.concatenate(lses, axis=0).T

    blk = pl.BlockSpec((None, tq, LANES), lambda r, pr, i: (r, i, pr))
    full = pl.BlockSpec((None, ld + 2 * DIL_HALF, LANES), lambda r, pr, i: (r, 0, pr))
    vspec = pl.BlockSpec((None, nb, LANES, LANES), lambda r, pr, i: (r, 0, pr, 0))
    return pl.pallas_call(
        body, name=name, grid=(d, npair, nq), in_specs=[blk, full, vspec], out_specs=[blk, blk],
        out_shape=[jax.ShapeDtypeStruct((d, ld, HEAD_W), F32), jax.ShapeDtypeStruct((d, ld, HEAD_W), F32)],
        compiler_params=_cparams(VMEM_BIG_MB))(q, kp, vp_t4)


def _dil_bwd(q, k, v, do, lse_rows, d_rows, name):
    d, ld, _ = q.shape
    assert ld % DIL_SQ == 0 and ld >= DIL_SW
    tq = min(DIL_TQ, ld)
    nq = ld // tq
    npair = HEAD_W // LANES
    span = min(tq + 2 * DIL_HALF, ld)

    def body(q_ref, k_ref, v_ref, do_ref, lse_ref, d_ref, dq_ref, dk_ref, dv_ref, dk_sc, dv_sc, dk_acc, dv_acc):
        i = pl.program_id(2)

        @pl.when(i == 0)
        def _():
            dk_acc[...] = jnp.zeros(dk_acc.shape, F32)
            dv_acc[...] = jnp.zeros(dv_acc.shape, F32)

        dk_sc[...] = jnp.zeros(dk_sc.shape, F32)
        dv_sc[...] = jnp.zeros(dv_sc.shape, F32)
        first = lax.broadcasted_iota(jnp.int32, (DIL_SQ, LANES), 1) < DIL_HD
        base = pl.multiple_of(jnp.clip(i * tq - DIL_HALF, 0, ld - span), DIL_HALF)
        for u in range(tq // DIL_SQ):
            a_sub = i * tq + u * DIL_SQ
            ws = _dil_window(a_sub, ld)
            rows = slice(u * DIL_SQ, (u + 1) * DIL_SQ)
            win = pl.ds(pl.multiple_of(ws - base, DIL_HALF), DIL_SW)
            kwin = k_ref[pl.ds(ws, DIL_SW), :]
            vwin = v_ref[pl.ds(ws, DIL_SW), :]
            q2 = _pair_rows(q_ref[rows, :], first)
            do2 = _pair_rows(do_ref[rows, :], first)
            lse2 = jnp.concatenate([lse_ref[0:1, rows], lse_ref[1:2, rows]], axis=1)
            dd2 = jnp.concatenate([d_ref[0:1, rows], d_ref[1:2, rows]], axis=1)
            s_t = lax.dot_general(kwin, q2, NT, preferred_element_type=F32)
            p_t = jnp.exp2(jnp.where(_dil_band_mask(ws - a_sub, 2), s_t, NEG) - lse2)
            dv_sc[win, :] += jnp.dot(p_t.astype(BF16), do2, preferred_element_type=F32)
            dp_t = lax.dot_general(vwin, do2, NT, preferred_element_type=F32)
            ds_t = (p_t * (dp_t - dd2)).astype(BF16)
            dk_sc[win, :] += jnp.dot(ds_t, q2, preferred_element_type=F32)
            dq2 = lax.dot_general(ds_t, kwin, TN, preferred_element_type=F32)
            dq_ref[rows, :] = jnp.where(first, dq2[0:DIL_SQ, :], dq2[DIL_SQ:2 * DIL_SQ, :]).astype(BF16)
        dk_acc[pl.ds(base, span), :] += dk_sc[...]
        dv_acc[pl.ds(base, span), :] += dv_sc[...]

        @pl.when(i == nq - 1)
        def _():
            dk_ref[...] = dk_acc[...].astype(BF16)
            dv_ref[...] = dv_acc[...].astype(BF16)

    blk = pl.BlockSpec((None, tq, LANES), lambda r, pr, i: (r, i, pr))
    full = pl.BlockSpec((None, ld, LANES), lambda r, pr, i: (r, 0, pr))
    rowspec = pl.BlockSpec((None, None, 2, tq), lambda r, pr, i: (r, pr, 0, i))
    return pl.pallas_call(
        body, name=name, grid=(d, npair, nq), in_specs=[blk, full, full, blk, rowspec, rowspec],
        out_specs=[blk, full, full],
        out_shape=[jax.ShapeDtypeStruct((d, ld, HEAD_W), BF16)] * 3,
        scratch_shapes=[pltpu.VMEM((span, LANES), F32), pltpu.VMEM((span, LANES), F32),
                        pltpu.VMEM((ld, LANES), F32), pltpu.VMEM((ld, LANES), F32)],
        compiler_params=_cparams(VMEM_BIG_MB))(q, k, v, do, lse_rows, d_rows)


TILE_BYTES = 1 << 21


def _row_tile(rows, cols, budget=TILE_BYTES):
    for parts in range(1, rows + 1):
        tr = rows // parts
        if rows % parts == 0 and tr % 8 == 0 and tr * cols * 4 <= budget:
            return tr
    return rows


def _add2(a, b, name, out_dtype):
    n, rows, cols = a.shape
    tr = _row_tile(rows, cols)

    def body(a_ref, b_ref, o_ref):
        o_ref[...] = (a_ref[...] + b_ref[...]).astype(out_dtype)

    spec = pl.BlockSpec((None, tr, cols), lambda t, i: (t, i, 0))
    return pl.pallas_call(body, name=name, grid=(n, rows // tr), in_specs=[spec, spec], out_specs=spec,
                          out_shape=jax.ShapeDtypeStruct(a.shape, out_dtype))(a, b)


def _add4_ordered(a, name):
    _, rows, cols = a.shape
    tr = _row_tile(rows, cols, TILE_BYTES // 4)

    def body(a_ref, o_ref):
        o_ref[...] = ((a_ref[0].astype(F32) + a_ref[1].astype(F32)) + a_ref[2].astype(F32)) + a_ref[3].astype(F32)

    return pl.pallas_call(
        body, name=name, grid=(rows // tr,), in_specs=[pl.BlockSpec((4, tr, cols), lambda i: (0, i, 0))],
        out_specs=pl.BlockSpec((tr, cols), lambda i: (i, 0)),
        out_shape=jax.ShapeDtypeStruct((rows, cols), F32))(a)


def _adamw(w, g, m, v, name):
    rows, cols = w.shape
    tr = _row_tile(rows, cols)
    bc1 = 1.0 - ADAM_B1 ** ADAM_STEP
    bc2 = 1.0 - ADAM_B2 ** ADAM_STEP

    def body(w_ref, g_ref, m_ref, v_ref, d_ref, nm_ref, nv_ref):
        gv = g_ref[...]
        nm = ADAM_B1 * m_ref[...] + (1.0 - ADAM_B1) * gv
        nv = ADAM_B2 * v_ref[...] + (1.0 - ADAM_B2) * (gv * gv)
        d_ref[...] = -ADAM_LR * ((nm / bc1) / (jnp.sqrt(nv / bc2) + ADAM_EPS) + ADAM_WD * w_ref[...])
        nm_ref[...] = nm
        nv_ref[...] = nv

    spec = pl.BlockSpec((tr, cols), lambda i: (i, 0))
    return pl.pallas_call(body, name=name, grid=(rows // tr,), in_specs=[spec] * 4, out_specs=[spec] * 3,
                          out_shape=[jax.ShapeDtypeStruct(w.shape, F32)] * 3,
                          compiler_params=_cparams(VMEM_BIG_MB))(w, g, m, v)


ANY = pl.BlockSpec(memory_space=pl.ANY)


def _place():
    return lax.axis_index("x"), lax.axis_index("y"), lax.axis_index("c")


def _rcopy(send_sems, recv_sems, n, src, dst, to):
    return pltpu.make_async_remote_copy(src_ref=src, dst_ref=dst, send_sem=send_sems.at[n], recv_sem=recv_sems.at[n],
                                        device_id=to, device_id_type=MESH)


def _allgather_weights(shards):
    na = len(shards)
    ns = 7

    def body(*refs):
        w_refs, g_refs = refs[:na], refs[na:2 * na]
        send_sems, recv_sems, local_sems = refs[2 * na:]
        x, y, c = _place()
        s, sx, sy, sd = 2 * x + y, 2 * (1 - x) + y, 2 * x + (1 - y), 2 * (1 - x) + (1 - y)
        to_x, to_y, sib = (1 - x, y, c), (x, 1 - y, c), (x, y, 1 - c)

        def part(a, shard, h, k=None):
            hr = shards[a].shape[0] // 2
            if k is None:
                return g_refs[a].at[shard, pl.ds(h * hr, hr), :]
            return g_refs[a].at[shard, pl.ds(h * hr + k * (hr // 2), hr // 2), :]

        def cp(a, n, src, dst, to):
            return _rcopy(send_sems, recv_sems, ns * a + n, src, dst, to)

        started, sends = [], []

        def go(copy):
            copy.start()
            sends.append(copy)

        for a in range(na):
            hr = shards[a].shape[0] // 2
            mine = pltpu.make_async_copy(w_refs[a], g_refs[a].at[s], local_sems.at[a])
            mine.start()
            started.append(mine)
            own = w_refs[a].at[pl.ds(c * hr, hr), :]
            go(cp(a, 0, own, part(a, s, c), to_x))
            go(cp(a, 1, own, part(a, s, c), to_y))
        for a in range(na):
            cp(a, 0, part(a, sx, c), part(a, sx, c), to_x).wait_recv()
            go(cp(a, 2, part(a, sx, c, 0), part(a, sx, c, 0), to_y))
            go(cp(a, 4, part(a, sx, c), part(a, sx, c), sib))
            cp(a, 1, part(a, sy, c), part(a, sy, c), to_y).wait_recv()
            go(cp(a, 3, part(a, sy, c, 1), part(a, sy, c, 1), to_x))
            go(cp(a, 5, part(a, sy, c), part(a, sy, c), sib))
        for a in range(na):
            cp(a, 2, part(a, sd, c, 0), part(a, sd, c, 0), to_y).wait_recv()
            cp(a, 3, part(a, sd, c, 1), part(a, sd, c, 1), to_x).wait_recv()
            go(cp(a, 6, part(a, sd, c), part(a, sd, c), sib))
        for a in range(na):
            for n, sj in ((4, sx), (5, sy), (6, sd)):
                cp(a, n, part(a, sj, 1 - c), part(a, sj, 1 - c), sib).wait_recv()
        for copy in sends:
            copy.wait_send()
        for mine in started:
            mine.wait()

    return pl.pallas_call(
        body, name="allgather_weights", in_specs=[ANY] * na, out_specs=[ANY] * na,
        out_shape=[jax.ShapeDtypeStruct((N_SHARD,) + t.shape, t.dtype) for t in shards],
        scratch_shapes=[pltpu.SemaphoreType.DMA((ns * na,)), pltpu.SemaphoreType.DMA((ns * na,)),
                        pltpu.SemaphoreType.DMA((na,))])(*shards)


def _sibling_send_halves(gs):
    na = len(gs)

    def body(*refs):
        g_refs, o_refs = refs[:na], refs[na:2 * na]
        send_sems, recv_sems = refs[2 * na:]
        x, y, c = _place()
        cps = []
        for a in range(na):
            for t in range(N_SHARD):
                cp = _rcopy(send_sems, recv_sems, N_SHARD * a + t, g_refs[a].at[t, 1 - c], o_refs[a].at[t],
                            (x, y, 1 - c))
                cp.start()
                cps.append(cp)
        for cp in cps:
            cp.wait()

    return pl.pallas_call(
        body, name="grad_sibling_exchange", in_specs=[ANY] * na, out_specs=[ANY] * na,
        out_shape=[jax.ShapeDtypeStruct((N_SHARD,) + g.shape[2:], g.dtype) for g in gs],
        scratch_shapes=[pltpu.SemaphoreType.DMA((N_SHARD * na,)), pltpu.SemaphoreType.DMA((N_SHARD * na,))])(*gs)


def _chip_scatter(parts):
    na = len(parts)

    def body(*refs):
        a_refs, o_refs = refs[:na], refs[na:2 * na]
        send_sems, recv_sems, local_sems = refs[2 * na:]
        x, y, c = _place()
        s = 2 * x + y
        chips = [(1 - x, y), (x, 1 - y), (1 - x, 1 - y)]
        started, cps = [], []
        for a in range(na):
            mine = pltpu.make_async_copy(a_refs[a].at[s], o_refs[a].at[s], local_sems.at[a])
            mine.start()
            started.append(mine)
            for n, (cx, cy) in enumerate(chips):
                cp = _rcopy(send_sems, recv_sems, 3 * a + n, a_refs[a].at[2 * cx + cy], o_refs[a].at[s], (cx, cy, c))
                cp.start()
                cps.append(cp)
        for a in range(na):
            for n, (cx, cy) in enumerate(chips):
                sj = 2 * cx + cy
                _rcopy(send_sems, recv_sems, 3 * a + n, a_refs[a].at[sj], o_refs[a].at[sj], (cx, cy, c)).wait_recv()
        for cp in cps:
            cp.wait_send()
        for mine in started:
            mine.wait()

    return pl.pallas_call(
        body, name="grad_chip_scatter", in_specs=[ANY] * na, out_specs=[ANY] * na,
        out_shape=[jax.ShapeDtypeStruct(t.shape, t.dtype) for t in parts],
        scratch_shapes=[pltpu.SemaphoreType.DMA((3 * na,)), pltpu.SemaphoreType.DMA((3 * na,)),
                        pltpu.SemaphoreType.DMA((na,))])(*parts)


def _sibling_swap(rs):
    na = len(rs)

    def body(*refs):
        r_refs, o_refs = refs[:na], refs[na:2 * na]
        send_sems, recv_sems = refs[2 * na:]
        x, y, c = _place()
        cps = []
        for a in range(na):
            cp = _rcopy(send_sems, recv_sems, a, r_refs[a], o_refs[a], (x, y, 1 - c))
            cp.start()
            cps.append(cp)
        for cp in cps:
            cp.wait()

    return pl.pallas_call(
        body, name="grad_sibling_swap", in_specs=[ANY] * na, out_specs=[ANY] * na,
        out_shape=[jax.ShapeDtypeStruct(t.shape, t.dtype) for t in rs],
        scratch_shapes=[pltpu.SemaphoreType.DMA((na,)), pltpu.SemaphoreType.DMA((na,))])(*rs)


def _pack_small(norm_g, q_norm_g, kv_norm_g, final_g):
    flat = jnp.concatenate([norm_g.reshape(-1), q_norm_g.reshape(-1), kv_norm_g.reshape(-1), final_g.reshape(-1),
                            jnp.zeros((SMALL_ROWS * LANES - N_SMALL,), F32)])
    return flat.reshape(SMALL_ROWS, LANES)


def _split_small(s):
    s = s.reshape(-1)
    o = 0
    out = []
    for n, shape in ((DEPTH * D_MODEL, (DEPTH, D_MODEL)), (DEPTH * Q_LORA, (DEPTH, Q_LORA)),
                     (DEPTH * KV_LORA, (DEPTH, KV_LORA)), (D_MODEL, (D_MODEL,))):
        out.append(s[o:o + n].reshape(shape))
        o += n
    return out


def _assemble_w_in(sh):
    z = lambda n: jnp.zeros(sh.shape[1:3] + (n,), sh.dtype)
    s0, s1, s2, s3 = sh[0], sh[1], sh[2], sh[3]
    return jnp.concatenate([s0[..., 0:640], z(64), s0[..., 640:672], z(32), z(256), s0[..., 672:1184],
                            s3[..., 1064:1576], s0[..., 1184:1576], s1, s2, s3[..., 0:1064]], axis=-1)


def _split_w_in_grad(parts):
    def shard(s, a, b):
        if s == 0:
            return jnp.concatenate([a[:, 0:640], a[:, 704:736], a[:, 1024:1536], b[:, 0:392]], axis=1)
        if s == 3:
            return jnp.concatenate([b[:, 3544:4608], a[:, 1536:2048]], axis=1)
        return b[:, 392 + (s - 1) * SHARD_COLS_IN:392 + s * SHARD_COLS_IN]

    rows = jnp.concatenate([shard(s, a, b) for s in range(N_SHARD) for a, b in parts], axis=0)
    return rows.reshape(N_SHARD, DEPTH * D_MODEL, SHARD_COLS_IN)


def _col_shards(w):
    dl, r, cc = w.shape
    return w.reshape(dl, r, N_SHARD, cc // N_SHARD).transpose(2, 0, 1, 3).reshape(N_SHARD, dl * r, cc // N_SHARD)


def _from_col_shards(g, rows):
    cc = g.shape[-1]
    return g.reshape(N_SHARD, DEPTH, rows, cc).transpose(1, 2, 0, 3).reshape(DEPTH, rows, N_SHARD * cc)


def _pad_w_in(w):
    z = lambda n: jnp.zeros(w.shape[:-1] + (n,), w.dtype)
    return jnp.concatenate([w[..., 0:640], z(64), w[..., 640:672], z(32), z(256), w[..., 672:1184],
                            w[..., 5792:6304], w[..., 1184:5792]], axis=-1)


def _unpad_w_in(w):
    return jnp.concatenate([w[..., 0:640], w[..., 704:736], w[..., 1024:1536], w[..., 2048:6656],
                            w[..., 1536:2048]], axis=-1)


def _pad_w_uq(w):
    s = w.shape[:-1]
    w = w.reshape(s + (MLA_HEADS, 96))
    return jnp.pad(w, [(0, 0)] * (w.ndim - 1) + [(0, 32)]).reshape(s + (1024,))


def _unpad_w_uq(w):
    s = w.shape[:-1]
    return w.reshape(s + (MLA_HEADS, LANES))[..., :96].reshape(s + (768,))


def _pad_w_ukv(w):
    s = w.shape[:-1]
    w = w.reshape(s + (MLA_HEADS, 128))
    kpart = jnp.pad(w[..., :64], [(0, 0)] * (w.ndim - 1) + [(0, 64)]).reshape(s + (1024,))
    vpart = w[..., 64:].reshape(s + (512,))
    return jnp.concatenate([kpart, vpart], axis=-1)


def _unpad_w_ukv(w):
    s = w.shape[:-1]
    kpart = w[..., :1024].reshape(s + (MLA_HEADS, LANES))[..., :64]
    vpart = w[..., 1024:].reshape(s + (MLA_HEADS, 64))
    return jnp.concatenate([kpart, vpart], axis=-1).reshape(s + (1024,))


def _rope_tables(L, dim, lane_lo, period):
    half = dim // 2
    inv = 1.0 / (ROPE_THETA ** (jnp.arange(0, dim, 2, dtype=F32) / dim))
    ang = jnp.arange(L, dtype=F32)[:, None] * inv[None, :]
    cos, sin = jnp.cos(ang), jnp.sin(ang)
    one = lambda n: jnp.ones((L, n), F32)
    zero = lambda n: jnp.zeros((L, n), F32)
    rest = period - lane_lo - dim
    rep = LANES // period
    c = jnp.tile(jnp.concatenate([one(lane_lo), cos, cos, one(rest)], axis=1), (1, rep))
    a = jnp.tile(jnp.concatenate([zero(lane_lo), -sin, zero(half), zero(rest)], axis=1), (1, rep))
    b = jnp.tile(jnp.concatenate([zero(lane_lo + half), sin, zero(rest)], axis=1), (1, rep))
    return c, a, b


def _to_strided(t, d):
    L, w = t.shape
    return t.reshape(L // d, d, w).transpose(1, 0, 2)


def _from_strided(t):
    d, ld, w = t.shape
    return t.transpose(1, 0, 2).reshape(d * ld, w)


def _head_rows(t):
    return t.T.reshape(MLA_HEADS // 2, 2, t.shape[0])


def _head_rows_strided(t, d):
    s = _to_strided(t, d)
    return s.transpose(0, 2, 1).reshape(d, MLA_HEADS // 2, 2, s.shape[1])

def _local_grads(x, target, norm_g, w_in_p, q_norm_g, kv_norm_g, w_uq_p, w_ukv_p, w_out, final_g):
    L = x.shape[0]
    tabs_m = _rope_tables(L, MLA_ROPE, MLA_NOPE, LANES)
    tabs_d = _rope_tables(L, ROT_DIM, 0, DIL_HD)
    tabs_m_t = (tabs_m[0], -tabs_m[1], -tabs_m[2])
    tabs_d_t = (tabs_d[0], -tabs_d[1], -tabs_d[2])
    w_in_t = jnp.swapaxes(w_in_p, 1, 2)
    w_uq_t = jnp.swapaxes(w_uq_p, 1, 2)
    w_ukv_t = jnp.swapaxes(w_ukv_p, 1, 2)
    w_out_t = jnp.swapaxes(w_out, 1, 2)

    saved = []
    for l in range(DEPTH):
        h = _rms_fwd(x, norm_g[l:l + 1], "rms_fwd")
        p = _mm(h, w_in_p[l], tm=1024, tn=3328, tk=1024, out_dtype=BF16, name="in_proj")
        q, k, v, cqn, ckvn = _mla_prep(p, q_norm_g[l:l + 1], kv_norm_g[l:l + 1], w_uq_p[l], w_ukv_p[l], tabs_m,
                                       "mla_prep")
        oa, lse_a = _mla_fwd(q, k, v.T, "mla_fwd")
        dil = _dil_prep(p, tabs_d, "dil_prep")
        dil_s, o_g, lse_g = [], [], []
        for g, (_, dd) in enumerate(DIL_PAIRS):
            qs, ks, vs = (_to_strided(t, dd) for t in dil[3 * g:3 * g + 3])
            pad = ((0, 0), (DIL_HALF, DIL_HALF), (0, 0))
            vp = jnp.pad(vs, pad)
            v_t4 = vp.reshape(dd, vp.shape[1] // LANES, LANES, HEAD_W).transpose(0, 1, 3, 2)
            og, lg = _dil_fwd(qs, jnp.pad(ks, pad), v_t4, "dil_fwd_%d" % dd)
            dil_s.append((qs, ks, vs))
            o_g.append(_from_strided(og))
            lse_g.append(_from_strided(lg))
        ab, bm, lt = _merge_gate(oa, p, o_g, lse_g, "merge_gate")
        x_next = _mm(ab, w_out[l], tm=1024, tn=1024, tk=1024, out_dtype=F32, name="out_proj", add=x)
        saved.append((x, h, p, q, k, v, cqn, ckvn, oa, lse_a, dil_s, bm, lt, ab))
        x = x_next

    loss_b, dx, d_final = _loss_head(x, final_g[None, :], target, "loss_head")
    loss = loss_b[0, 0]

    d_norm, d_qn, d_kvn, d_win, d_wuq, d_wukv, d_wout = [], [], [], [], [], [], []
    for l in reversed(range(DEPTH)):
        x_l, h, p, q, k, v, cqn, ckvn, oa, lse_a, dil_s, bm, lt, ab = saved[l]
        dab = _mm(dx, w_out_t[l], tm=1024, tn=1024, tk=1024, out_dtype=F32, name="out_proj_dgrad")
        d_wout.append(_mm(ab, dx, tm=1024, tn=1024, tk=1024, out_dtype=F32, name="out_proj_wgrad", a_is_kxm=True))
        doa, dbm, D_a, D_b, dgates = _gate_bwd(dab, p, oa, bm, "gate_bwd")
        dq, dk_t, dv_t = _mla_bwd(q, k, v, q.T, doa, doa.T, lse_a.transpose(0, 2, 1),
                                  D_a.reshape(L, MLA_HEADS // 2, 2).transpose(1, 0, 2), "mla_bwd")
        dk, dv = dk_t.T, dv_t.T
        dp_mla, dq_pre, dkv, dqg, dkvg = _mla_prep_bwd(dq, dk, dv, p, q_norm_g[l:l + 1], kv_norm_g[l:l + 1],
                                                       w_uq_t[l], w_ukv_t[l], tabs_m_t, "mla_prep_bwd")
        d_wuq.append(_mm(cqn, dq_pre, tm=Q_LORA, tn=1024, tk=2048, out_dtype=F32, name="w_uq_wgrad", a_is_kxm=True))
        d_wukv.append(_mm(ckvn, dkv, tm=KV_LORA, tn=1536, tk=2048, out_dtype=F32, name="w_ukv_wgrad", a_is_kxm=True))
        dgr = []
        for g, (_, dd) in enumerate(DIL_PAIRS):
            qs, ks, vs = dil_s[g]
            dqs, dks, dvs = _dil_bwd(qs, ks, vs, _to_strided(dbm, dd), _head_rows_strided(lt, dd),
                                     _head_rows_strided(D_b, dd), "dil_bwd_%d" % dd)
            dgr += [_from_strided(dqs), _from_strided(dks), _from_strided(dvs)]
        dp_dil = _dil_prep_bwd(dgr, tabs_d_t, "dil_prep_bwd")
        dp_a = jnp.concatenate([dp_mla, dgates], axis=1)
        dh = _mm(dp_a, w_in_t[l][0:P_DIL0], tm=1024, tn=1024, tk=2048, out_dtype=F32, name="in_proj_dgrad_a")
        d_win.append((_mm(h, dp_a, tm=512, tn=2048, tk=1024, out_dtype=F32, name="in_proj_wgrad_a", a_is_kxm=True),
                      _mm(h, dp_dil, tm=512, tn=1536, tk=2048, out_dtype=F32, name="in_proj_wgrad_b",
                          a_is_kxm=True)))
        dx, dng = _dgrad_rms_bwd(dp_dil, w_in_t[l][P_DIL0:], dh, x_l, norm_g[l:l + 1], dx, "in_proj_dgrad_b_rms_bwd")
        d_norm.append(dng[0])
        d_qn.append(dqg[0])
        d_kvn.append(dkvg[0])

    rev = lambda xs: jnp.stack(xs[::-1])
    return (loss, dx, rev(d_norm), d_win[::-1], rev(d_qn), rev(d_kvn), rev(d_wuq), rev(d_wukv), rev(d_wout),
            d_final[0])


def kernel(x, norm_g, w_in, q_norm_g, kv_norm_g, w_uq, w_ukv, w_out, final_g, loss_target, m_norm_g, m_w_in, m_q_norm_g, m_kv_norm_g, m_w_uq, m_w_ukv, m_w_out, m_final_g, v_norm_g, v_w_in, v_q_norm_g, v_kv_norm_g, v_w_uq, v_w_ukv, v_w_out, v_final_g):
    c = lax.axis_index("c")

    def families(a_in, a_uq, a_ukv, a_out):
        return [t.reshape(shape) for t, shape in zip((a_in, a_uq, a_ukv, a_out), FAM_SHAPES)]

    g_in, g_uq, g_ukv, g_out = _allgather_weights([t.astype(BF16) for t in families(w_in, w_uq, w_ukv, w_out)])
    w_in_p = _assemble_w_in(g_in.reshape(N_SHARD, DEPTH, D_MODEL, SHARD_COLS_IN))
    w_uq_p = _pad_w_uq(_from_col_shards(g_uq, Q_LORA))
    w_ukv_p = _pad_w_ukv(_from_col_shards(g_ukv, KV_LORA))
    w_out_f = g_out.reshape(N_SHARD, DEPTH, 1024 // N_SHARD, D_MODEL).transpose(1, 0, 2, 3).reshape(DEPTH, 1024, D_MODEL)

    (loss, dx, d_norm, d_win_p, d_qn, d_kvn, d_wuq_p, d_wukv_p, d_wout, d_final) = _local_grads(
        x[0], loss_target[0], norm_g, w_in_p, q_norm_g, kv_norm_g, w_uq_p, w_ukv_p, w_out_f, final_g)
    loss = lax.psum(loss, ("x", "y", "c"))

    small = _pack_small(d_norm, d_qn, d_kvn, d_final)
    grads = [_split_w_in_grad(d_win_p), _col_shards(_unpad_w_uq(d_wuq_p)), _col_shards(_unpad_w_ukv(d_wukv_p)),
             d_wout.reshape(DEPTH, N_SHARD, 1024 // N_SHARD, D_MODEL).transpose(1, 0, 2, 3).reshape(
                 N_SHARD, DEPTH * (1024 // N_SHARD), D_MODEL),
             jnp.broadcast_to(small[None], (N_SHARD, SMALL_ROWS, LANES))]
    halves = [g.reshape(N_SHARD, 2, g.shape[1] // 2, g.shape[2]) for g in grads]
    from_sib = _sibling_send_halves(halves)
    chip_sum = [_add2(lax.dynamic_index_in_dim(h, c, axis=1, keepdims=False), f, "grad_add_pair", BF16)
                for h, f in zip(halves, from_sib)]
    red_half = [_add4_ordered(t, "grad_add_chips") for t in _chip_scatter(chip_sum)]
    other_half = _sibling_swap(red_half)
    gred = []
    for mine, other in zip(red_half, other_half):
        both = jnp.stack([mine, other])
        gred.append(jnp.concatenate([lax.dynamic_index_in_dim(both, c, axis=0, keepdims=False),
                                     lax.dynamic_index_in_dim(both, 1 - c, axis=0, keepdims=False)], axis=0))

    wf = families(w_in, w_uq, w_ukv, w_out) + [_pack_small(norm_g, q_norm_g, kv_norm_g, final_g)]
    mf = families(m_w_in, m_w_uq, m_w_ukv, m_w_out) + [_pack_small(m_norm_g, m_q_norm_g, m_kv_norm_g, m_final_g)]
    vf = families(v_w_in, v_w_uq, v_w_ukv, v_w_out) + [_pack_small(v_norm_g, v_q_norm_g, v_kv_norm_g, v_final_g)]
    upd = [_adamw(w, g, m, v, "adamw") for w, g, m, v in zip(wf, gred, mf, vf)]

    def leaves(fams):
        a_in, a_uq, a_ukv, a_out, s = fams
        s_norm, s_qn, s_kvn, s_final = _split_small(s)
        return [s_norm, a_in.reshape(w_in.shape), s_qn, s_kvn, a_uq.reshape(w_uq.shape), a_ukv.reshape(w_ukv.shape),
                a_out.reshape(w_out.shape), s_final]

    return (loss, dx[None], *leaves(gred), *leaves([u[0] for u in upd]), *leaves([u[1] for u in upd]),
            *leaves([u[2] for u in upd]))
```
